```python
import jax, jax.numpy as jnp
from jax import lax
import numpy as np

D_MODEL = 2048
BATCH = 2
SEQ = 4096
DEPTH = 1

GRID_W = 64
CTX_LEN = 256
D_A = 2048
D_B = 2048
D_MIX = D_A + D_B
N_BLOCKS_A = 16
BLOCK_A = D_A // N_BLOCKS_A
CONV_W = 4
CONV_PAD_L = 2
CONV_PAD_R = CONV_W - 1 - CONV_PAD_L
LRU_C = 8.0
N_HEADS_B = 16
HEAD_B = D_B // N_HEADS_B
CHUNK = 64
EPS = 1e-6
SPLITS = (D_A, 2 * D_A, 2 * D_A + D_B, 2 * D_A + 2 * D_B, 2 * D_A + 3 * D_B, 2 * D_A + 4 * D_B)
IN_COLS = 2 * D_A + 5 * D_B

kernel_name = "hybrid_rglru_hgrn2_dit_layer"


def rmsnorm(x, w):
    xf = x.astype(jnp.float32)
    y = xf * lax.rsqrt(jnp.mean(xf * xf, axis=-1, keepdims=True) + EPS)
    return (y * w.astype(jnp.float32)).astype(x.dtype)


def flip(z):
    return jnp.flip(z, axis=1)


def to_colmajor(z, rows):
    b, t, ch = z.shape
    return z.reshape(b, rows, GRID_W, ch).swapaxes(1, 2).reshape(b, t, ch)


def from_colmajor(z, rows):
    b, t, ch = z.shape
    return z.reshape(b, GRID_W, rows, ch).swapaxes(1, 2).reshape(b, t, ch)


def dwconv_centred(u, w, b):
    t = u.shape[1]
    up = jnp.pad(u, ((0, 0), (CONV_PAD_L, CONV_PAD_R), (0, 0)))
    return b + sum(up[:, k:k + t] * w[k] for k in range(CONV_W))


def linear_scan(a, b, h0):
    b = b.at[:, 0].add(a[:, 0] * h0)

    def comb(left, right):
        al, bl = left
        ar, br = right
        return ar * al, ar * bl + br

    _, h = lax.associative_scan(comb, (a, b), axis=1)
    return h


def rglru_scan(u, w_r, b_r, w_i, b_i, lam, h0):
    uf = u.astype(jnp.float32)
    ub = uf.reshape(uf.shape[0], uf.shape[1], N_BLOCKS_A, BLOCK_A)
    r = jax.nn.sigmoid(jnp.einsum('btnc,ncd->btnd', ub, w_r.astype(jnp.float32)).reshape(uf.shape) + b_r.astype(jnp.float32))
    i = jax.nn.sigmoid(jnp.einsum('btnc,ncd->btnd', ub, w_i.astype(jnp.float32)).reshape(uf.shape) + b_i.astype(jnp.float32))
    log_a = -LRU_C * r * jax.nn.softplus(-lam.astype(jnp.float32))
    a = jnp.exp(log_a)
    bx = jnp.sqrt(-jnp.expm1(2.0 * log_a)) * (i * uf)
    h = linear_scan(a, bx, h0)
    return h, h[:, -1]


def hgrn2_chunked(q, k, v, log_f, s0):
    bsz, t = q.shape[0], q.shape[1]
    n = t // CHUNK

    def chunks(z):
        return z.reshape(bsz, n, CHUNK, N_HEADS_B, HEAD_B).transpose(1, 0, 3, 2, 4)

    causal = jnp.tril(jnp.ones((CHUNK, CHUNK), dtype=bool))

    def step(s, inp):
        qc, kc, vc, gc = inp
        F = jnp.cumsum(gc, axis=2)
        diff = F[:, :, :, None, :] - F[:, :, None, :, :]
        decay = jnp.exp(jnp.where(causal[:, :, None], diff, -jnp.inf))
        scores = jnp.einsum('bhtk,bhsk,bhtsk->bhts', qc, kc, decay)
        o = jnp.einsum('bhts,bhsv->bhtv', scores, vc) + jnp.einsum('bhtk,bhkv->bhtv', qc * jnp.exp(F), s)
        f_last = F[:, :, -1]
        s_new = jnp.exp(f_last)[..., None] * s + jnp.einsum('bhsk,bhsv->bhkv', kc * jnp.exp(f_last[:, :, None] - F), vc)
        return s_new, o

    s_final, o = lax.scan(step, s0, (chunks(q), chunks(k), chunks(v), chunks(log_f)))
    o = o.transpose(1, 0, 3, 2, 4).reshape(bsz, t, N_HEADS_B, HEAD_B)
    return o, s_final


def hgrn2_dir(q, f_pre, v, lb, s0):
    f = lb + (1.0 - lb) * jax.nn.sigmoid(f_pre)
    return hgrn2_chunked(q, 1.0 - f, v, jnp.log(f), s0)


def split_heads(z):
    return z.astype(jnp.float32).reshape(z.shape[0], z.shape[1], N_HEADS_B, HEAD_B)


def hybrid_layer(x, ctx, c, c_ctx, rows, ada_w, ada_b, norm_w, w_in, conv_w, conv_b,
                 lru_wr, lru_br, lru_wi, lru_bi, lru_lambda, lb_f, lb_b, hgrn_norm_w, w_out, last):
    bsz, t = x.shape[0], x.shape[1]
    shift, scale, gate = jnp.split(jax.nn.silu(c) @ ada_w + ada_b, 3, axis=-1)
    shift_c, scale_c, gate_c = jnp.split(jax.nn.silu(c_ctx) @ ada_w + ada_b, 3, axis=-1)
    h_lat = rmsnorm(x, norm_w) * (1.0 + scale[:, None]) + shift[:, None]
    h_ctx = rmsnorm(ctx, norm_w) * (1.0 + scale_c) + shift_c
    xa_l, ga_l, q_l, ff_l, fb_l, v_l, gb_l = jnp.split(h_lat @ w_in, SPLITS, axis=-1)
    xa_c, ga_c, q_c, ff_c, fb_c, v_c, gb_c = jnp.split(h_ctx @ w_in, SPLITS, axis=-1)

    ua_l = dwconv_centred(xa_l, conv_w, conv_b)
    ua_c = dwconv_centred(xa_c, conv_w, conv_b)
    p_fwd = (lru_wr[0], lru_br[0], lru_wi[0], lru_bi[0], lru_lambda[0])
    p_bwd = (lru_wr[1], lru_br[1], lru_wi[1], lru_bi[1], lru_lambda[1])
    h0 = jnp.zeros((bsz, D_A), jnp.float32)
    hc_f, sc_f = rglru_scan(ua_c, *p_fwd, h0)
    hl_f, _ = rglru_scan(ua_l, *p_fwd, sc_f)
    hc_b, sc_b = rglru_scan(flip(ua_c), *p_bwd, h0)
    hl_b, _ = rglru_scan(flip(ua_l), *p_bwd, sc_b)
    ya_l = (hl_f + flip(hl_b)).astype(x.dtype) * jax.nn.silu(ga_l)

    qL = jax.nn.silu(split_heads(to_colmajor(q_l, rows)))
    ffL, fbL, vL = [split_heads(to_colmajor(z, rows)) for z in (ff_l, fb_l, v_l)]
    qC = jax.nn.silu(split_heads(q_c))
    ffC, fbC, vC = [split_heads(z) for z in (ff_c, fb_c, v_c)]
    s0 = jnp.zeros((bsz, N_HEADS_B, HEAD_B, HEAD_B), jnp.float32)
    oc_f, Sc_f = hgrn2_dir(qC, ffC, vC, lb_f, s0)
    ol_f, _ = hgrn2_dir(qL, ffL, vL, lb_f, Sc_f)
    oc_b, Sc_b = hgrn2_dir(flip(qC), flip(fbC), flip(vC), lb_b, s0)
    ol_b, _ = hgrn2_dir(flip(qL), flip(fbL), flip(vL), lb_b, Sc_b)
    ol = from_colmajor((ol_f + flip(ol_b)).reshape(bsz, t, D_B), rows)
    ol = rmsnorm(ol.reshape(bsz, t, N_HEADS_B, HEAD_B), hgrn_norm_w).reshape(bsz, t, D_B)
    yb_l = ol.astype(x.dtype) * jax.nn.silu(gb_l)

    x = x + gate[:, None] * (jnp.concatenate([ya_l, yb_l], axis=-1) @ w_out)

    if not last:
        ya_c = (hc_f + flip(hc_b)).astype(ctx.dtype) * jax.nn.silu(ga_c)
        oc = rmsnorm(oc_f + flip(oc_b), hgrn_norm_w).reshape(bsz, ctx.shape[1], D_B)
        yb_c = oc.astype(ctx.dtype) * jax.nn.silu(gb_c)
        ctx = ctx + gate_c * (jnp.concatenate([ya_c, yb_c], axis=-1) @ w_out)
    return x, ctx


def setup_inputs(seed: int = 0) -> dict:
    key = jax.random.key(seed)
    ks = jax.random.split(key, 20)
    f32 = jnp.float32
    nrm = lambda k, shape, s: jax.random.normal(k, shape, f32) * s
    u = jax.random.uniform(ks[14], (DEPTH, 2, D_A), f32, 0.9, 0.999)
    a = u ** (1.0 / LRU_C)
    lru_lambda = jnp.log(a) - jnp.log1p(-a)
    return {
        "x": nrm(ks[0], (BATCH, SEQ, D_MODEL), 1.0),
        "c": nrm(ks[1], (BATCH, D_MODEL), 1.0),
        "ctx": nrm(ks[2], (BATCH, CTX_LEN, D_MODEL), 1.0),
        "c_ctx": nrm(ks[3], (D_MODEL,), 1.0),
        "ada_w": nrm(ks[4], (DEPTH, D_MODEL, 3 * D_MODEL), 0.5 * D_MODEL ** -0.5),
        "ada_b": nrm(ks[5], (DEPTH, 3 * D_MODEL), 0.01),
        "norm_w": 1.0 + nrm(ks[6], (DEPTH, D_MODEL), 0.02),
        "w_in": nrm(ks[7], (DEPTH, D_MODEL, IN_COLS), D_MODEL ** -0.5),
        "conv_w": nrm(ks[8], (DEPTH, CONV_W, D_A), CONV_W ** -0.5),
        "conv_b": nrm(ks[9], (DEPTH, D_A), 0.01),
        "lru_wr": nrm(ks[10], (DEPTH, 2, N_BLOCKS_A, BLOCK_A, BLOCK_A), BLOCK_A ** -0.5),
        "lru_br": nrm(ks[11], (DEPTH, 2, D_A), 0.01),
        "lru_wi": nrm(ks[12], (DEPTH, 2, N_BLOCKS_A, BLOCK_A, BLOCK_A), BLOCK_A ** -0.5),
        "lru_bi": nrm(ks[13], (DEPTH, 2, D_A), 0.01),
        "lru_lambda": lru_lambda,
        "hgrn_lb_logits": nrm(ks[15], (2, DEPTH + 1, D_B), 0.5),
        "hgrn_norm_w": 1.0 + nrm(ks[16], (DEPTH, HEAD_B), 0.02),
        "w_out": nrm(ks[17], (DEPTH, D_MIX, D_MODEL), D_MIX ** -0.5),
        "final_norm_w": 1.0 + nrm(ks[18], (D_MODEL,), 0.02),
    }


def reference(x, c, ctx, c_ctx, ada_w, ada_b, norm_w, w_in, conv_w, conv_b,
              lru_wr, lru_br, lru_wi, lru_bi, lru_lambda, hgrn_lb_logits, hgrn_norm_w,
              w_out, final_norm_w):
    rows = x.shape[1] // GRID_W
    lb_all = jnp.cumsum(jax.nn.softmax(hgrn_lb_logits.astype(jnp.float32), axis=1), axis=1)
    for l in range(DEPTH):
        x, ctx = hybrid_layer(
            x, ctx, c, c_ctx, rows, ada_w[l], ada_b[l], norm_w[l], w_in[l], conv_w[l], conv_b[l],
            lru_wr[l], lru_br[l], lru_wi[l], lru_bi[l], lru_lambda[l],
            lb_all[0, l].reshape(N_HEADS_B, HEAD_B), lb_all[1, l].reshape(N_HEADS_B, HEAD_B),
            hgrn_norm_w[l], w_out[l], l == DEPTH - 1)
    return rmsnorm(x, final_norm_w)
```

```python
import functools

import jax
import jax.numpy as jnp
from jax import lax
from jax.experimental import pallas as pl
from jax.experimental.pallas import tpu as pltpu

GRID_W = 64
CHUNK = 64
LRU_C = 8.0
EPS = 1e-6
CONV_PAD_L = 2
LANES = 128
SUBLANES = 8
EXP_CLAMP = 80.0
VMEM_LIMIT = 56 * 1024 * 1024

F32 = jnp.float32
BF16 = jnp.bfloat16


def _sigmoid(z):
    return 1.0 / (1.0 + jnp.exp(-z))


def _silu(z):
    return z * _sigmoid(z)


def _softplus(z):
    return jnp.maximum(z, 0.0) + jnp.log1p(jnp.exp(-jnp.abs(z)))


def _mod_kernel(c_ref, w_ref, b_ref, o_ref):
    s = _silu(c_ref[...])
    o_ref[...] = jnp.dot(s.astype(BF16), w_ref[...].astype(BF16),
                         preferred_element_type=F32) + b_ref[...]


def _modulation(cc, w, b):
    rows, d = cc.shape
    n = w.shape[1]
    tn = 512 if n % 512 == 0 else n
    return pl.pallas_call(
        _mod_kernel,
        grid=(n // tn,),
        in_specs=[pl.BlockSpec((rows, d), lambda j: (0, 0)),
                  pl.BlockSpec((d, tn), lambda j: (0, j)),
                  pl.BlockSpec((1, tn), lambda j: (0, j))],
        out_specs=pl.BlockSpec((rows, tn), lambda j: (0, j)),
        out_shape=jax.ShapeDtypeStruct((rows, n), F32),
        compiler_params=pltpu.CompilerParams(
            dimension_semantics=("arbitrary",), vmem_limit_bytes=VMEM_LIMIT),
        name="adaln_modulation",
    )(cc, w, b.reshape(1, n))


def _inproj_kernel(x_ref, shift_ref, scale_ref, nw_ref, w_ref, o_ref, h_ref):
    @pl.when(pl.program_id(1) == 0)
    def _():
        x = x_ref[...]
        ms = jnp.mean(x * x, axis=-1, keepdims=True)
        y = x * lax.rsqrt(ms + EPS) * nw_ref[...]
        h_ref[...] = (y * (1.0 + scale_ref[0]) + shift_ref[0]).astype(BF16)

    o_ref[...] = jnp.dot(h_ref[...], w_ref[...],
                         preferred_element_type=F32).astype(o_ref.dtype)


def _in_projection(x2d, mod3, norm_w, w_bf16, row_of_tile, tm, tn):
    m, d = x2d.shape
    n = w_bf16.shape[1]
    return pl.pallas_call(
        _inproj_kernel,
        grid=(m // tm, n // tn),
        in_specs=[pl.BlockSpec((tm, d), lambda i, j: (i, 0)),
                  pl.BlockSpec((1, 1, d), lambda i, j: (row_of_tile(i), 0, 0)),
                  pl.BlockSpec((1, 1, d), lambda i, j: (row_of_tile(i), 0, 1)),
                  pl.BlockSpec((1, d), lambda i, j: (0, 0)),
                  pl.BlockSpec((d, tn), lambda i, j: (0, j))],
        out_specs=pl.BlockSpec((tm, tn), lambda i, j: (i, j)),
        out_shape=jax.ShapeDtypeStruct((m, n), BF16),
        scratch_shapes=[pltpu.VMEM((tm, d), BF16)],
        compiler_params=pltpu.CompilerParams(
            dimension_semantics=("parallel", "arbitrary"),
            vmem_limit_bytes=VMEM_LIMIT),
        name="in_projection",
    )(x2d, mod3, mod3, norm_w.reshape(1, d), w_bf16)


def _group_scan(a, b, reverse):
    row = lax.broadcasted_iota(jnp.int32, a.shape, 1)
    for k in (1, 2, 4):
        if reverse:
            a_sh = pltpu.roll(a, SUBLANES - k, axis=1)
            b_sh = pltpu.roll(b, SUBLANES - k, axis=1)
            m = row < SUBLANES - k
        else:
            a_sh = pltpu.roll(a, k, axis=1)
            b_sh = pltpu.roll(b, k, axis=1)
            m = row >= k
        b = jnp.where(m, a * b_sh + b, b)
        a = jnp.where(m, a * a_sh, a)
    return a, b


def _lru_tile(xw, cw_ref, cb_ref, wg_ref, bg_ref, sp, carry, reverse, tt):
    cdim = xw.shape[1]
    u = cb_ref[...]
    for k in range(cw_ref.shape[0]):
        off = SUBLANES - CONV_PAD_L + k
        u = u + xw[off:off + tt, :] * cw_ref[k:k + 1, :]
    a_parts, b_parts = [], []
    for blk in range(cdim // LANES):
        sl = slice(blk * LANES, (blk + 1) * LANES)
        ub = u[:, sl]
        g = jnp.dot(ub.astype(BF16), wg_ref[blk], preferred_element_type=F32) + bg_ref[blk]
        r = _sigmoid(g[:, :LANES])
        i = _sigmoid(g[:, LANES:])
        log_a = (-LRU_C) * r * sp[:, sl]
        a_blk = jnp.exp(log_a)
        a_parts.append(a_blk)
        b_parts.append(jnp.sqrt(1.0 - a_blk * a_blk) * (i * ub))
    a = a_parts[0] if len(a_parts) == 1 else jnp.concatenate(a_parts, axis=1)
    b = b_parts[0] if len(b_parts) == 1 else jnp.concatenate(b_parts, axis=1)
    groups = tt // SUBLANES
    a3, b3 = _group_scan(a.reshape(groups, SUBLANES, cdim),
                         b.reshape(groups, SUBLANES, cdim), reverse)
    hs = [None] * groups
    order = range(groups - 1, -1, -1) if reverse else range(groups)
    for g_i in order:
        h = b3[g_i] + a3[g_i] * carry
        carry = h[0:1, :] if reverse else h[SUBLANES - 1:SUBLANES, :]
        hs[g_i] = h
    return jnp.concatenate(hs, axis=0), carry


def _rglru_kernel(xa_ref, ga_ref, xc_ref, cw_ref, cb_ref, wgf_ref, bgf_ref, wgb_ref, bgb_ref,
                  lam_ref, o_ref, xf_ref, xcf_ref, hf_ref, *, tt):
    t_len = xa_ref.shape[1]
    tc_len = xc_ref.shape[1]
    cdim = xa_ref.shape[2]
    zeros = jnp.zeros((SUBLANES, cdim), F32)
    xf_ref[0:SUBLANES, :] = zeros
    xf_ref[SUBLANES:SUBLANES + t_len, :] = xa_ref[0].astype(F32)
    xf_ref[SUBLANES + t_len:, :] = zeros
    xcf_ref[0:SUBLANES, :] = zeros
    xcf_ref[SUBLANES:SUBLANES + tc_len, :] = xc_ref[0].astype(F32)
    xcf_ref[SUBLANES + tc_len:, :] = zeros

    sp_f = _softplus(-lam_ref[0:1, :])
    sp_b = _softplus(-lam_ref[1:2, :])
    win = tt + 2 * SUBLANES

    def tile(src_ref, idx, carry, reverse):
        t0 = pl.multiple_of(idx * tt, tt)
        xw = src_ref[pl.ds(t0, win), :]
        if reverse:
            return _lru_tile(xw, cw_ref, cb_ref, wgb_ref, bgb_ref, sp_b, carry, True, tt)
        return _lru_tile(xw, cw_ref, cb_ref, wgf_ref, bgf_ref, sp_f, carry, False, tt)

    n_lat = t_len // tt
    n_ctx = tc_len // tt
    carry0 = jnp.zeros((1, cdim), F32)

    def ctx_f(i, carry):
        return tile(xcf_ref, i, carry, False)[1]

    carry = lax.fori_loop(0, n_ctx, ctx_f, carry0)

    def lat_f(i, carry):
        h, carry = tile(xf_ref, i, carry, False)
        hf_ref[pl.ds(pl.multiple_of(i * tt, tt), tt), :] = h
        return carry

    lax.fori_loop(0, n_lat, lat_f, carry)

    def ctx_b(i, carry):
        return tile(xcf_ref, n_ctx - 1 - i, carry, True)[1]

    carry = lax.fori_loop(0, n_ctx, ctx_b, carry0)

    def lat_b(i, carry):
        idx = n_lat - 1 - i
        h, carry = tile(xf_ref, idx, carry, True)
        rows = pl.ds(pl.multiple_of(idx * tt, tt), tt)
        gate = ga_ref[0, rows, :].astype(F32)
        o_ref[0, rows, :] = ((hf_ref[rows, :] + h) * _silu(gate)).astype(o_ref.dtype)
        return carry

    lax.fori_loop(0, n_lat, lat_b, carry)


def _rglru(p_lat, p_ctx, conv_w, conv_b, wg_f, bg_f, wg_b, bg_b, lam, d_a, cb, tt):
    bsz, t_len, _ = p_lat.shape
    tc_len = p_ctx.shape[1]
    nblk = cb // LANES
    ncb = d_a // cb
    kern = functools.partial(_rglru_kernel, tt=tt)
    return pl.pallas_call(
        kern,
        grid=(bsz, ncb),
        in_specs=[pl.BlockSpec((1, t_len, cb), lambda b, c: (b, 0, c)),
                  pl.BlockSpec((1, t_len, cb), lambda b, c: (b, 0, ncb + c)),
                  pl.BlockSpec((1, tc_len, cb), lambda b, c: (b, 0, c)),
                  pl.BlockSpec((conv_w.shape[0], cb), lambda b, c: (0, c)),
                  pl.BlockSpec((1, cb), lambda b, c: (0, c)),
                  pl.BlockSpec((nblk, LANES, 2 * LANES), lambda b, c: (c, 0, 0)),
                  pl.BlockSpec((nblk, 1, 2 * LANES), lambda b, c: (c, 0, 0)),
                  pl.BlockSpec((nblk, LANES, 2 * LANES), lambda b, c: (c, 0, 0)),
                  pl.BlockSpec((nblk, 1, 2 * LANES), lambda b, c: (c, 0, 0)),
                  pl.BlockSpec((2, cb), lambda b, c: (0, c))],
        out_specs=pl.BlockSpec((1, t_len, cb), lambda b, c: (b, 0, c)),
        out_shape=jax.ShapeDtypeStruct((bsz, t_len, d_a), BF16),
        scratch_shapes=[pltpu.VMEM((t_len + 2 * SUBLANES, cb), F32),
                        pltpu.VMEM((tc_len + 2 * SUBLANES, cb), F32),
                        pltpu.VMEM((t_len, cb), F32)],
        compiler_params=pltpu.CompilerParams(
            dimension_semantics=("parallel", "parallel"),
            vmem_limit_bytes=VMEM_LIMIT),
        name="rglru",
    )(p_lat, p_lat, p_ctx, conv_w, conv_b.reshape(1, d_a), wg_f, bg_f, wg_b, bg_b, lam)


def _hgrn_head(q, fpre, v, lb, st, tri, reverse):
    c_len = q.shape[0]
    f = lb + (1.0 - lb) * _sigmoid(fpre)
    logf = jnp.log(f)
    kk = 1.0 - f
    hi = logf.astype(BF16)
    lo = (logf - hi.astype(F32)).astype(BF16)
    tri_b = tri.astype(BF16)
    cum = (jnp.dot(tri_b, hi, preferred_element_type=F32)
           + jnp.dot(tri_b, lo, preferred_element_type=F32))
    half = c_len // 2
    if reverse:
        last = cum[0:1, :]
        ref = cum[half:half + 1, :]
    else:
        last = cum[c_len - 1:c_len, :]
        ref = cum[half - 1:half, :]
    qs = _silu(q)
    qd = qs * jnp.exp(jnp.minimum(cum - ref, EXP_CLAMP))
    kd = kk * jnp.exp(jnp.minimum(ref - cum, EXP_CLAMP))
    scores = lax.dot_general(qd.astype(BF16), kd.astype(BF16), (((1,), (1,)), ((), ())),
                             preferred_element_type=F32)
    scores = jnp.where(tri, scores, 0.0)
    vb = v.astype(BF16)
    q_in = (qs * jnp.exp(cum)).astype(BF16)
    k_out = (kk * jnp.exp(last - cum)).astype(BF16)
    o = (jnp.dot(scores.astype(BF16), vb, preferred_element_type=F32)
         + lax.dot_general(q_in, st.astype(BF16), (((1,), (1,)), ((), ())),
                           preferred_element_type=F32))
    st_new = st * jnp.exp(last) + lax.dot_general(vb, k_out, (((0,), (0,)), ((), ())),
                                                  preferred_element_type=F32)
    return o, st_new


def _lower_bounds(logits_ref, layer):
    out = []
    for d in range(2):
        rows = [logits_ref[d, l:l + 1, :] for l in range(logits_ref.shape[1])]
        m = functools.reduce(jnp.maximum, rows)
        e = [jnp.exp(r - m) for r in rows]
        out.append(sum(e[:layer + 1]) / sum(e))
    return out


def _hgrn_kernel(*refs, n_heads, hd, has_init, emit_o, emit_state):
    qf_ref, ff_ref, vf_ref, qb_ref, fb_ref, vb_ref, lg_ref = refs[:7]
    pos = 7
    if has_init:
        s0f_ref, s0b_ref = refs[pos:pos + 2]
        pos += 2
    if emit_o:
        of_ref, ob_ref = refs[pos:pos + 2]
        pos += 2
    if emit_state:
        sof_ref, sob_ref = refs[pos:pos + 2]
        pos += 2
    sf_ref, sb_ref = refs[pos:pos + 2]

    j = pl.program_id(1)
    c_len = qf_ref.shape[1]

    @pl.when(j == 0)
    def _():
        if has_init:
            sf_ref[...] = s0f_ref[0]
            sb_ref[...] = s0b_ref[0]
        else:
            sf_ref[...] = jnp.zeros_like(sf_ref)
            sb_ref[...] = jnp.zeros_like(sb_ref)

    lb_f, lb_b = _lower_bounds(lg_ref, 0)
    r_i = lax.broadcasted_iota(jnp.int32, (c_len, c_len), 0)
    c_i = lax.broadcasted_iota(jnp.int32, (c_len, c_len), 1)
    tri_f = r_i >= c_i
    tri_b = r_i <= c_i

    for h in range(n_heads):
        sl = slice(h * hd, (h + 1) * hd)
        o, st = _hgrn_head(qf_ref[0, :, sl].astype(F32), ff_ref[0, :, sl].astype(F32),
                           vf_ref[0, :, sl].astype(F32), lb_f[:, sl], sf_ref[h], tri_f, False)
        sf_ref[h] = st
        if emit_o:
            of_ref[0, :, sl] = o.astype(of_ref.dtype)
        o, st = _hgrn_head(qb_ref[0, :, sl].astype(F32), fb_ref[0, :, sl].astype(F32),
                           vb_ref[0, :, sl].astype(F32), lb_b[:, sl], sb_ref[h], tri_b, True)
        sb_ref[h] = st
        if emit_o:
            ob_ref[0, :, sl] = o.astype(ob_ref.dtype)

    if emit_state:
        @pl.when(j == pl.num_programs(1) - 1)
        def _():
            sof_ref[0] = sf_ref[...]
            sob_ref[0] = sb_ref[...]


def _hgrn(p_view, logits, d_b, n_heads, col_block, n_chunks, chunk_idx, init_states, emit_o):
    bsz = p_view.shape[0]
    hd = d_b // n_heads
    has_init = init_states is not None
    emit_state = not emit_o

    def spec(group, reverse):
        def imap(b, j):
            ch = n_chunks - 1 - j if reverse else j
            return (b, chunk_idx(ch), col_block(ch, group))
        return pl.BlockSpec((1, CHUNK, d_b), imap)

    in_specs = [spec(0, False), spec(1, False), spec(3, False),
                spec(0, True), spec(2, True), spec(3, True),
                pl.BlockSpec(logits.shape, lambda b, j: (0, 0, 0))]
    args = [p_view] * 6 + [logits]
    state_spec = pl.BlockSpec((1, n_heads, hd, hd), lambda b, j: (b, 0, 0, 0))
    state_shape = jax.ShapeDtypeStruct((bsz, n_heads, hd, hd), F32)
    if has_init:
        in_specs += [state_spec, state_spec]
        args += list(init_states)
    out_specs, out_shape = [], []
    if emit_o:
        o_shape = jax.ShapeDtypeStruct((bsz, CHUNK, n_chunks * d_b), BF16)
        out_specs += [pl.BlockSpec((1, CHUNK, d_b), lambda b, j: (b, 0, j)),
                      pl.BlockSpec((1, CHUNK, d_b), lambda b, j: (b, 0, n_chunks - 1 - j))]
        out_shape += [o_shape, o_shape]
    if emit_state:
        out_specs += [state_spec, state_spec]
        out_shape += [state_shape, state_shape]
    kern = functools.partial(_hgrn_kernel, n_heads=n_heads, hd=hd, has_init=has_init,
                             emit_o=emit_o, emit_state=emit_state)
    return pl.pallas_call(
        kern,
        grid=(bsz, n_chunks),
        in_specs=in_specs,
        out_specs=out_specs,
        out_shape=out_shape,
        scratch_shapes=[pltpu.VMEM((n_heads, hd, hd), F32), pltpu.VMEM((n_heads, hd, hd), F32)],
        compiler_params=pltpu.CompilerParams(
            dimension_semantics=("parallel", "arbitrary"),
            vmem_limit_bytes=VMEM_LIMIT),
        name="hgrn2_latent" if emit_o else "hgrn2_context",
    )(*args)


def _outproj_kernel(ya_ref, of_ref, ob_ref, gb_ref, x_ref, gate_ref, hnw_ref, fnw_ref, w_ref,
                    o_ref, y_ref, *, n_heads, hd):
    d_a = ya_ref.shape[1]
    y_ref[:, 0:d_a] = ya_ref[...]
    hnw = hnw_ref[...]
    for h in range(n_heads):
        sl = slice(h * hd, (h + 1) * hd)
        o = of_ref[:, sl].astype(F32) + ob_ref[:, sl].astype(F32)
        ms = jnp.mean(o * o, axis=-1, keepdims=True)
        on = o * lax.rsqrt(ms + EPS) * hnw
        y_ref[:, d_a + h * hd:d_a + (h + 1) * hd] = (on * _silu(gb_ref[:, sl].astype(F32))).astype(BF16)
    acc = jnp.dot(y_ref[...], w_ref[...], preferred_element_type=F32)
    z = x_ref[...] + gate_ref[0] * acc
    ms = jnp.mean(z * z, axis=-1, keepdims=True)
    o_ref[...] = z * lax.rsqrt(ms + EPS) * fnw_ref[...]


def _out_projection(ya2d, of2d, ob2d, p2d, gb_block, x2d, mod3, hnw, fnw, w_bf16, n_heads, tm,
                    tiles_per_batch):
    m, d = x2d.shape
    d_a = ya2d.shape[1]
    d_b = of2d.shape[1]
    hd = d_b // n_heads
    kern = functools.partial(_outproj_kernel, n_heads=n_heads, hd=hd)
    return pl.pallas_call(
        kern,
        grid=(m // tm,),
        in_specs=[pl.BlockSpec((tm, d_a), lambda i: (i, 0)),
                  pl.BlockSpec((tm, d_b), lambda i: (i, 0)),
                  pl.BlockSpec((tm, d_b), lambda i: (i, 0)),
                  pl.BlockSpec((tm, d_b), lambda i: (i, gb_block)),
                  pl.BlockSpec((tm, d), lambda i: (i, 0)),
                  pl.BlockSpec((1, 1, d), lambda i: (i // tiles_per_batch, 0, 2)),
                  pl.BlockSpec((1, hd), lambda i: (0, 0)),
                  pl.BlockSpec((1, d), lambda i: (0, 0)),
                  pl.BlockSpec((d_a + d_b, d), lambda i: (0, 0), pipeline_mode=pl.Buffered(1))],
        out_specs=pl.BlockSpec((tm, d), lambda i: (i, 0)),
        out_shape=jax.ShapeDtypeStruct((m, d), F32),
        scratch_shapes=[pltpu.VMEM((tm, d_a + d_b), BF16)],
        compiler_params=pltpu.CompilerParams(
            dimension_semantics=("parallel",), vmem_limit_bytes=VMEM_LIMIT),
        name="out_projection",
    )(ya2d, of2d, ob2d, p2d, x2d, mod3, hnw.reshape(1, hd), fnw.reshape(1, d), w_bf16)


def kernel(x, c, ctx, c_ctx, ada_w, ada_b, norm_w, w_in, conv_w, conv_b, lru_wr, lru_br, lru_wi,
           lru_bi, lru_lambda, hgrn_lb_logits, hgrn_norm_w, w_out, final_norm_w):
    bsz, t_len, d = x.shape
    tc_len = ctx.shape[1]
    assert ada_w.shape[0] == 1, "single-layer stack only"
    d_a = conv_w.shape[2]
    d_b = hgrn_lb_logits.shape[2]
    hd = hgrn_norm_w.shape[1]
    n_heads = d_b // hd
    n_blocks_a = lru_wr.shape[2]
    n_cols = w_in.shape[2]
    assert t_len == GRID_W * CHUNK and tc_len % CHUNK == 0
    assert d_a // n_blocks_a == LANES and hd == LANES
    assert (2 * d_a) % d_b == 0 and n_cols == 2 * d_a + 5 * d_b
    groups_per_row = n_cols // d_b
    first_b_group = (2 * d_a) // d_b

    n_rows = -(-(bsz + 1) // SUBLANES) * SUBLANES
    cc = jnp.zeros((n_rows, d), F32).at[:bsz].set(c).at[bsz].set(c_ctx)
    mod3 = _modulation(cc, ada_w[0], ada_b[0]).reshape(n_rows, 1, 3 * d)

    w_in_b = w_in[0].astype(BF16)
    w_out_b = w_out[0].astype(BF16)

    tm = 1024 if t_len % 1024 == 0 else t_len
    tn = 1024 if n_cols % 1024 == 0 else d_b
    tpb = t_len // tm
    p_lat = _in_projection(x.reshape(bsz * t_len, d), mod3, norm_w[0], w_in_b,
                           lambda i: i // tpb, tm, tn).reshape(bsz, t_len, n_cols)
    p_ctx = _in_projection(ctx.reshape(bsz * tc_len, d), mod3, norm_w[0], w_in_b,
                           lambda i: bsz, bsz * tc_len, tn).reshape(bsz, tc_len, n_cols)

    def gate_w(dirn):
        return jnp.concatenate([lru_wr[0, dirn], lru_wi[0, dirn]], axis=-1).astype(BF16)

    def gate_b(dirn):
        return jnp.concatenate([lru_br[0, dirn].reshape(n_blocks_a, 1, LANES),
                                lru_bi[0, dirn].reshape(n_blocks_a, 1, LANES)], axis=-1)

    cb = 256 if d_a % 256 == 0 else LANES
    ya = _rglru(p_lat, p_ctx, conv_w[0], conv_b[0], gate_w(0), gate_b(0), gate_w(1), gate_b(1),
                lru_lambda[0], d_a, cb, tt=128)

    n_ctx_chunks = tc_len // CHUNK
    states = _hgrn(p_ctx, hgrn_lb_logits, d_b, n_heads,
                   lambda ch, g: first_b_group + g, n_ctx_chunks, lambda ch: ch, None, False)
    p_cols = p_lat.reshape(bsz, CHUNK, GRID_W * n_cols)
    o_f, o_b = _hgrn(p_cols, hgrn_lb_logits, d_b, n_heads,
                     lambda ch, g: ch * groups_per_row + first_b_group + g, GRID_W,
                     lambda ch: 0, states, True)

    tm_o = 256 if t_len % 256 == 0 else t_len
    out = _out_projection(ya.reshape(bsz * t_len, d_a), o_f.reshape(bsz * t_len, d_b),
                          o_b.reshape(bsz * t_len, d_b), p_lat.reshape(bsz * t_len, n_cols),
                          first_b_group + 4, x.reshape(bsz * t_len, d), mod3, hgrn_norm_w[0],
                          final_norm_w, w_out_b, n_heads, tm_o, t_len // tm_o)
    return out.reshape(bsz, t_len, d)
```

```python
import functools

import jax
import jax.numpy as jnp
from jax import lax
from jax.experimental import pallas as pl
from jax.experimental.pallas import tpu as pltpu

GRID_W = 64
CHUNK = 64
LRU_C = 8.0
EPS = 1e-6
CONV_PAD_L = 2
LANES = 128
SUBLANES = 8
EXP_CLAMP = 80.0
VMEM_LIMIT = 56 * 1024 * 1024

F32 = jnp.float32
BF16 = jnp.bfloat16


def _sigmoid(z):
    return 1.0 / (1.0 + jnp.exp(-z))


def _silu(z):
    return z * _sigmoid(z)


def _softplus(z):
    return jnp.maximum(z, 0.0) + jnp.log1p(jnp.exp(-jnp.abs(z)))


def _mod_kernel(c_ref, w_ref, b_ref, o_ref):
    s = _silu(c_ref[...])
    o_ref[...] = jnp.dot(s.astype(BF16), w_ref[...].astype(BF16),
                         preferred_element_type=F32) + b_ref[...]


def _modulation(cc, w, b):
    rows, d = cc.shape
    n = w.shape[1]
    tn = 512 if n % 512 == 0 else n
    return pl.pallas_call(
        _mod_kernel,
        grid=(n // tn,),
        in_specs=[pl.BlockSpec((rows, d), lambda j: (0, 0)),
                  pl.BlockSpec((d, tn), lambda j: (0, j)),
                  pl.BlockSpec((1, tn), lambda j: (0, j))],
        out_specs=pl.BlockSpec((rows, tn), lambda j: (0, j)),
        out_shape=jax.ShapeDtypeStruct((rows, n), F32),
        compiler_params=pltpu.CompilerParams(
            dimension_semantics=("arbitrary",), vmem_limit_bytes=VMEM_LIMIT),
        name="adaln_modulation",
    )(cc, w, b.reshape(1, n))


def _inproj_kernel(x_ref, shift_ref, scale_ref, nw_ref, w_ref, o_ref, h_ref):
    @pl.when(pl.program_id(1) == 0)
    def _():
        x = x_ref[...]
        ms = jnp.mean(x * x, axis=-1, keepdims=True)
        y = x * lax.rsqrt(ms + EPS) * nw_ref[...]
        h_ref[...] = (y * (1.0 + scale_ref[0]) + shift_ref[0]).astype(BF16)

    o_ref[...] = jnp.dot(h_ref[...], w_ref[...],
                         preferred_element_type=F32).astype(o_ref.dtype)


def _in_projection(x2d, mod3, norm_w, w_bf16, row_of_tile, wcol_of_tile, n_out, tm, tn):
    m, d = x2d.shape
    return pl.pallas_call(
        _inproj_kernel,
        grid=(m // tm, n_out // tn),
        in_specs=[pl.BlockSpec((tm, d), lambda i, j: (i, 0)),
                  pl.BlockSpec((1, 1, d), lambda i, j: (row_of_tile(i), 0, 0)),
                  pl.BlockSpec((1, 1, d), lambda i, j: (row_of_tile(i), 0, 1)),
                  pl.BlockSpec((1, d), lambda i, j: (0, 0)),
                  pl.BlockSpec((d, tn), lambda i, j: (0, wcol_of_tile(j)))],
        out_specs=pl.BlockSpec((tm, tn), lambda i, j: (i, j)),
        out_shape=jax.ShapeDtypeStruct((m, n_out), BF16),
        scratch_shapes=[pltpu.VMEM((tm, d), BF16)],
        compiler_params=pltpu.CompilerParams(
            dimension_semantics=("parallel", "arbitrary"),
            vmem_limit_bytes=VMEM_LIMIT),
        name="in_projection",
    )(x2d, mod3, mod3, norm_w.reshape(1, d), w_bf16)


def _inproj_colmajor_kernel(x_ref, shift_ref, scale_ref, nw_ref, w_ref, o_ref, h_ref, *, slab):
    _, n_r, n_w, d = x_ref.shape

    @pl.when(pl.program_id(2) == 0)
    def _():
        x = x_ref[0]
        rs = lax.rsqrt(jnp.mean(x * x, axis=-1, keepdims=True) + EPS)
        for s in range(d // slab):
            sl = slice(s * slab, (s + 1) * slab)
            y = x_ref[0, :, :, sl] * rs * nw_ref[:, sl]
            y = y * (1.0 + scale_ref[0][:, sl]) + shift_ref[0][:, sl]
            yt = jnp.transpose(y, (1, 0, 2))
            h_ref[:, sl] = yt.reshape(n_w * n_r, slab).astype(BF16)

    res = jnp.dot(h_ref[...], w_ref[...], preferred_element_type=F32)
    o_ref[0] = res.reshape(n_w, n_r, res.shape[1]).astype(o_ref.dtype)


def _in_projection_colmajor(x4d, mod3, norm_w, w_bf16, wcol_of_tile, n_out, wb, tn):
    bsz, n_r, n_w, d = x4d.shape
    kern = functools.partial(_inproj_colmajor_kernel, slab=2 * LANES)
    return pl.pallas_call(
        kern,
        grid=(bsz, n_w // wb, n_out // tn),
        in_specs=[pl.BlockSpec((1, n_r, wb, d), lambda b, w, j: (b, 0, w, 0)),
                  pl.BlockSpec((1, 1, d), lambda b, w, j: (b, 0, 0)),
                  pl.BlockSpec((1, 1, d), lambda b, w, j: (b, 0, 1)),
                  pl.BlockSpec((1, d), lambda b, w, j: (0, 0)),
                  pl.BlockSpec((d, tn), lambda b, w, j: (0, wcol_of_tile(j)))],
        out_specs=pl.BlockSpec((1, wb, n_r, tn), lambda b, w, j: (b, w, 0, j)),
        out_shape=jax.ShapeDtypeStruct((bsz, n_w, n_r, n_out), BF16),
        scratch_shapes=[pltpu.VMEM((wb * n_r, d), BF16)],
        compiler_params=pltpu.CompilerParams(
            dimension_semantics=("parallel", "parallel", "arbitrary"),
            vmem_limit_bytes=VMEM_LIMIT),
        name="in_projection_colmajor",
    )(x4d, mod3, mod3, norm_w.reshape(1, d), w_bf16)


def _group_scan(a, b, reverse):
    row = lax.broadcasted_iota(jnp.int32, a.shape, 1)
    for k in (1, 2, 4):
        if reverse:
            a_sh = pltpu.roll(a, SUBLANES - k, axis=1)
            b_sh = pltpu.roll(b, SUBLANES - k, axis=1)
            m = row < SUBLANES - k
        else:
            a_sh = pltpu.roll(a, k, axis=1)
            b_sh = pltpu.roll(b, k, axis=1)
            m = row >= k
        b = jnp.where(m, a * b_sh + b, b)
        a = jnp.where(m, a * a_sh, a)
    return a, b


def _lru_tile(xw, cw_ref, cb_ref, wg_ref, bg_ref, sp, carry, reverse, tt):
    cdim = xw.shape[1]
    u = cb_ref[...]
    for k in range(cw_ref.shape[0]):
        off = SUBLANES - CONV_PAD_L + k
        u = u + xw[off:off + tt, :] * cw_ref[k:k + 1, :]
    a_parts, b_parts = [], []
    for blk in range(cdim // LANES):
        sl = slice(blk * LANES, (blk + 1) * LANES)
        ub = u[:, sl]
        g = jnp.dot(ub.astype(BF16), wg_ref[blk], preferred_element_type=F32) + bg_ref[blk]
        r = _sigmoid(g[:, :LANES])
        i = _sigmoid(g[:, LANES:])
        log_a = (-LRU_C) * r * sp[:, sl]
        a_blk = jnp.exp(log_a)
        a_parts.append(a_blk)
        b_parts.append(jnp.sqrt(1.0 - a_blk * a_blk) * (i * ub))
    a = a_parts[0] if len(a_parts) == 1 else jnp.concatenate(a_parts, axis=1)
    b = b_parts[0] if len(b_parts) == 1 else jnp.concatenate(b_parts, axis=1)
    groups = tt // SUBLANES
    a3, b3 = _group_scan(a.reshape(groups, SUBLANES, cdim),
                         b.reshape(groups, SUBLANES, cdim), reverse)
    hs = [None] * groups
    order = range(groups - 1, -1, -1) if reverse else range(groups)
    for g_i in order:
        h = b3[g_i] + a3[g_i] * carry
        carry = h[0:1, :] if reverse else h[SUBLANES - 1:SUBLANES, :]
        hs[g_i] = h
    return jnp.concatenate(hs, axis=0), carry


def _rglru_kernel(xa_ref, ga_ref, xc_ref, cw_ref, cb_ref, wgf_ref, bgf_ref, wgb_ref, bgb_ref,
                  lam_ref, o_ref, xf_ref, xcf_ref, hf_ref, *, tt):
    t_len = xa_ref.shape[1]
    tc_len = xc_ref.shape[1]
    cdim = xa_ref.shape[2]
    zeros = jnp.zeros((SUBLANES, cdim), F32)
    xf_ref[0:SUBLANES, :] = zeros
    xf_ref[SUBLANES:SUBLANES + t_len, :] = xa_ref[0].astype(F32)
    xf_ref[SUBLANES + t_len:, :] = zeros
    xcf_ref[0:SUBLANES, :] = zeros
    xcf_ref[SUBLANES:SUBLANES + tc_len, :] = xc_ref[0].astype(F32)
    xcf_ref[SUBLANES + tc_len:, :] = zeros

    sp_f = _softplus(-lam_ref[0:1, :])
    sp_b = _softplus(-lam_ref[1:2, :])
    win = tt + 2 * SUBLANES

    def tile(src_ref, idx, carry, reverse):
        t0 = pl.multiple_of(idx * tt, tt)
        xw = src_ref[pl.ds(t0, win), :]
        if reverse:
            return _lru_tile(xw, cw_ref, cb_ref, wgb_ref, bgb_ref, sp_b, carry, True, tt)
        return _lru_tile(xw, cw_ref, cb_ref, wgf_ref, bgf_ref, sp_f, carry, False, tt)

    n_lat = t_len // tt
    n_ctx = tc_len // tt
    carry0 = jnp.zeros((1, cdim), F32)

    def ctx_f(i, carry):
        return tile(xcf_ref, i, carry, False)[1]

    carry = lax.fori_loop(0, n_ctx, ctx_f, carry0)

    def lat_f(i, carry):
        h, carry = tile(xf_ref, i, carry, False)
        hf_ref[pl.ds(pl.multiple_of(i * tt, tt), tt), :] = h
        return carry

    lax.fori_loop(0, n_lat, lat_f, carry)

    def ctx_b(i, carry):
        return tile(xcf_ref, n_ctx - 1 - i, carry, True)[1]

    carry = lax.fori_loop(0, n_ctx, ctx_b, carry0)

    def lat_b(i, carry):
        idx = n_lat - 1 - i
        h, carry = tile(xf_ref, idx, carry, True)
        rows = pl.ds(pl.multiple_of(idx * tt, tt), tt)
        gate = ga_ref[0, rows, :].astype(F32)
        o_ref[0, rows, :] = ((hf_ref[rows, :] + h) * _silu(gate)).astype(o_ref.dtype)
        return carry

    lax.fori_loop(0, n_lat, lat_b, carry)


def _rglru(p_lat, p_ctx, conv_w, conv_b, wg_f, bg_f, wg_b, bg_b, lam, d_a, cb, tt):
    bsz, t_len, _ = p_lat.shape
    tc_len = p_ctx.shape[1]
    nblk = cb // LANES
    ncb = d_a // cb
    kern = functools.partial(_rglru_kernel, tt=tt)
    return pl.pallas_call(
        kern,
        grid=(bsz, ncb),
        in_specs=[pl.BlockSpec((1, t_len, cb), lambda b, c: (b, 0, c)),
                  pl.BlockSpec((1, t_len, cb), lambda b, c: (b, 0, ncb + c)),
                  pl.BlockSpec((1, tc_len, cb), lambda b, c: (b, 0, c)),
                  pl.BlockSpec((conv_w.shape[0], cb), lambda b, c: (0, c)),
                  pl.BlockSpec((1, cb), lambda b, c: (0, c)),
                  pl.BlockSpec((nblk, LANES, 2 * LANES), lambda b, c: (c, 0, 0)),
                  pl.BlockSpec((nblk, 1, 2 * LANES), lambda b, c: (c, 0, 0)),
                  pl.BlockSpec((nblk, LANES, 2 * LANES), lambda b, c: (c, 0, 0)),
                  pl.BlockSpec((nblk, 1, 2 * LANES), lambda b, c: (c, 0, 0)),
                  pl.BlockSpec((2, cb), lambda b, c: (0, c))],
        out_specs=pl.BlockSpec((1, t_len, cb), lambda b, c: (b, 0, c)),
        out_shape=jax.ShapeDtypeStruct((bsz, t_len, d_a), BF16),
        scratch_shapes=[pltpu.VMEM((t_len + 2 * SUBLANES, cb), F32),
                        pltpu.VMEM((tc_len + 2 * SUBLANES, cb), F32),
                        pltpu.VMEM((t_len, cb), F32)],
        compiler_params=pltpu.CompilerParams(
            dimension_semantics=("parallel", "parallel"),
            vmem_limit_bytes=VMEM_LIMIT),
        name="rglru",
    )(p_lat, p_lat, p_ctx, conv_w, conv_b.reshape(1, d_a), wg_f, bg_f, wg_b, bg_b, lam)


def _hgrn_head(q, fpre, v, lb, st, tri, reverse):
    c_len = q.shape[0]
    f = lb + (1.0 - lb) * _sigmoid(fpre)
    logf = jnp.log(f)
    kk = 1.0 - f
    hi = logf.astype(BF16)
    lo = (logf - hi.astype(F32)).astype(BF16)
    tri_b = tri.astype(BF16)
    cum = (jnp.dot(tri_b, hi, preferred_element_type=F32)
           + jnp.dot(tri_b, lo, preferred_element_type=F32))
    half = c_len // 2
    if reverse:
        last = cum[0:1, :]
        ref = cum[half:half + 1, :]
    else:
        last = cum[c_len - 1:c_len, :]
        ref = cum[half - 1:half, :]
    qs = _silu(q)
    qd = qs * jnp.exp(jnp.minimum(cum - ref, EXP_CLAMP))
    kd = kk * jnp.exp(jnp.minimum(ref - cum, EXP_CLAMP))
    scores = lax.dot_general(qd.astype(BF16), kd.astype(BF16), (((1,), (1,)), ((), ())),
                             preferred_element_type=F32)
    scores = jnp.where(tri, scores, 0.0)
    vb = v.astype(BF16)
    q_in = (qs * jnp.exp(cum)).astype(BF16)
    k_out = (kk * jnp.exp(last - cum)).astype(BF16)
    o = (jnp.dot(scores.astype(BF16), vb, preferred_element_type=F32)
         + lax.dot_general(q_in, st.astype(BF16), (((1,), (1,)), ((), ())),
                           preferred_element_type=F32))
    st_new = st * jnp.exp(last) + lax.dot_general(vb, k_out, (((0,), (0,)), ((), ())),
                                                  preferred_element_type=F32)
    return o, st_new


def _lower_bounds(logits_ref, layer):
    out = []
    for d in range(2):
        rows = [logits_ref[d, l:l + 1, :] for l in range(logits_ref.shape[1])]
        m = functools.reduce(jnp.maximum, rows)
        e = [jnp.exp(r - m) for r in rows]
        out.append(sum(e[:layer + 1]) / sum(e))
    return out


def _hgrn_kernel(*refs, n_heads, hd, has_init, emit_o, emit_state):
    qf_ref, ff_ref, vf_ref, qb_ref, fb_ref, vb_ref, lg_ref = refs[:7]
    pos = 7
    if has_init:
        s0f_ref, s0b_ref = refs[pos:pos + 2]
        pos += 2
    if emit_o:
        of_ref, ob_ref = refs[pos:pos + 2]
        pos += 2
    if emit_state:
        sof_ref, sob_ref = refs[pos:pos + 2]
        pos += 2
    sf_ref, sb_ref = refs[pos:pos + 2]

    j = pl.program_id(1)
    c_len = qf_ref.shape[2]

    @pl.when(j == 0)
    def _():
        if has_init:
            sf_ref[...] = s0f_ref[0]
            sb_ref[...] = s0b_ref[0]
        else:
            sf_ref[...] = jnp.zeros_like(sf_ref)
            sb_ref[...] = jnp.zeros_like(sb_ref)

    lb_f, lb_b = _lower_bounds(lg_ref, 0)
    r_i = lax.broadcasted_iota(jnp.int32, (c_len, c_len), 0)
    c_i = lax.broadcasted_iota(jnp.int32, (c_len, c_len), 1)
    tri_f = r_i >= c_i
    tri_b = r_i <= c_i

    for h in range(n_heads):
        sl = slice(h * hd, (h + 1) * hd)
        o, st = _hgrn_head(qf_ref[0, 0, :, sl].astype(F32), ff_ref[0, 0, :, sl].astype(F32),
                           vf_ref[0, 0, :, sl].astype(F32), lb_f[:, sl], sf_ref[h], tri_f, False)
        sf_ref[h] = st
        if emit_o:
            of_ref[0, 0, :, sl] = o.astype(of_ref.dtype)
        o, st = _hgrn_head(qb_ref[0, 0, :, sl].astype(F32), fb_ref[0, 0, :, sl].astype(F32),
                           vb_ref[0, 0, :, sl].astype(F32), lb_b[:, sl], sb_ref[h], tri_b, True)
        sb_ref[h] = st
        if emit_o:
            ob_ref[0, 0, :, sl] = o.astype(ob_ref.dtype)

    if emit_state:
        @pl.when(j == pl.num_programs(1) - 1)
        def _():
            sof_ref[0] = sf_ref[...]
            sob_ref[0] = sb_ref[...]


def _hgrn(p_view, logits, d_b, n_heads, first_group, init_states, emit_o):
    bsz, n_chunks = p_view.shape[:2]
    hd = d_b // n_heads
    has_init = init_states is not None
    emit_state = not emit_o

    def spec(group, reverse):
        def imap(b, j):
            return (b, n_chunks - 1 - j if reverse else j, 0, first_group + group)
        return pl.BlockSpec((1, 1, CHUNK, d_b), imap)

    in_specs = [spec(0, False), spec(1, False), spec(3, False),
                spec(0, True), spec(2, True), spec(3, True),
                pl.BlockSpec(logits.shape, lambda b, j: (0, 0, 0))]
    args = [p_view] * 6 + [logits]
    state_spec = pl.BlockSpec((1, n_heads, hd, hd), lambda b, j: (b, 0, 0, 0))
    state_shape = jax.ShapeDtypeStruct((bsz, n_heads, hd, hd), F32)
    if has_init:
        in_specs += [state_spec, state_spec]
        args += list(init_states)
    out_specs, out_shape = [], []
    if emit_o:
        o_shape = jax.ShapeDtypeStruct((bsz, n_chunks, CHUNK, d_b), BF16)
        out_specs += [pl.BlockSpec((1, 1, CHUNK, d_b), lambda b, j: (b, j, 0, 0)),
                      pl.BlockSpec((1, 1, CHUNK, d_b), lambda b, j: (b, n_chunks - 1 - j, 0, 0))]
        out_shape += [o_shape, o_shape]
    if emit_state:
        out_specs += [state_spec, state_spec]
        out_shape += [state_shape, state_shape]
    kern = functools.partial(_hgrn_kernel, n_heads=n_heads, hd=hd, has_init=has_init,
                             emit_o=emit_o, emit_state=emit_state)
    return pl.pallas_call(
        kern,
        grid=(bsz, n_chunks),
        in_specs=in_specs,
        out_specs=out_specs,
        out_shape=out_shape,
        scratch_shapes=[pltpu.VMEM((n_heads, hd, hd), F32), pltpu.VMEM((n_heads, hd, hd), F32)],
        compiler_params=pltpu.CompilerParams(
            dimension_semantics=("parallel", "arbitrary"),
            vmem_limit_bytes=VMEM_LIMIT),
        name="hgrn2_latent" if emit_o else "hgrn2_context",
    )(*args)


def _outproj_kernel(ya_ref, of_ref, ob_ref, gb_ref, x_ref, gate_ref, hnw_ref, fnw_ref, w_ref,
                    o_ref, y_ref, *, n_heads, hd):
    _, rb, wb, d_a = ya_ref.shape
    rows = rb * wb
    d = x_ref.shape[3]
    y_ref[:, 0:d_a] = ya_ref[0].reshape(rows, d_a)
    hnw = hnw_ref[...]
    for h in range(n_heads):
        sl = slice(h * hd, (h + 1) * hd)
        o = of_ref[0, :, :, sl].astype(F32) + ob_ref[0, :, :, sl].astype(F32)
        ms = jnp.mean(o * o, axis=-1, keepdims=True)
        on = jnp.transpose(o * lax.rsqrt(ms + EPS) * hnw, (1, 0, 2))
        yb = on * _silu(gb_ref[0, :, :, sl].astype(F32))
        y_ref[:, d_a + h * hd:d_a + (h + 1) * hd] = yb.reshape(rows, hd).astype(BF16)
    acc = jnp.dot(y_ref[...], w_ref[...], preferred_element_type=F32)
    z = x_ref[0].reshape(rows, d) + gate_ref[0] * acc
    ms = jnp.mean(z * z, axis=-1, keepdims=True)
    o_ref[0] = (z * lax.rsqrt(ms + EPS) * fnw_ref[...]).reshape(rb, wb, d)


def _out_projection(ya4, of4, ob4, pa4, gb_block, x4, mod3, hnw, fnw, w_bf16, n_heads, rb, wb):
    bsz, n_r, n_w, d = x4.shape
    d_a = ya4.shape[3]
    d_b = of4.shape[3]
    hd = d_b // n_heads
    kern = functools.partial(_outproj_kernel, n_heads=n_heads, hd=hd)

    def raster(c, col=0):
        return pl.BlockSpec((1, rb, wb, c), lambda b, r, w: (b, r, w, col))

    def colmajor(c):
        return pl.BlockSpec((1, wb, rb, c), lambda b, r, w: (b, w, r, 0))

    return pl.pallas_call(
        kern,
        grid=(bsz, n_r // rb, n_w // wb),
        in_specs=[raster(d_a), colmajor(d_b), colmajor(d_b), raster(d_b, gb_block), raster(d),
                  pl.BlockSpec((1, 1, d), lambda b, r, w: (b, 0, 2)),
                  pl.BlockSpec((1, hd), lambda b, r, w: (0, 0)),
                  pl.BlockSpec((1, d), lambda b, r, w: (0, 0)),
                  pl.BlockSpec((d_a + d_b, d), lambda b, r, w: (0, 0),
                               pipeline_mode=pl.Buffered(1))],
        out_specs=raster(d),
        out_shape=jax.ShapeDtypeStruct((bsz, n_r, n_w, d), F32),
        scratch_shapes=[pltpu.VMEM((rb * wb, d_a + d_b), BF16)],
        compiler_params=pltpu.CompilerParams(
            dimension_semantics=("parallel", "parallel", "parallel"),
            vmem_limit_bytes=VMEM_LIMIT),
        name="out_projection",
    )(ya4, of4, ob4, pa4, x4, mod3, hnw.reshape(1, hd), fnw.reshape(1, d), w_bf16)


def kernel(x, c, ctx, c_ctx, ada_w, ada_b, norm_w, w_in, conv_w, conv_b, lru_wr, lru_br, lru_wi,
           lru_bi, lru_lambda, hgrn_lb_logits, hgrn_norm_w, w_out, final_norm_w):
    bsz, t_len, d = x.shape
    tc_len = ctx.shape[1]
    assert ada_w.shape[0] == 1, "single-layer stack only"
    d_a = conv_w.shape[2]
    d_b = hgrn_lb_logits.shape[2]
    hd = hgrn_norm_w.shape[1]
    n_heads = d_b // hd
    n_blocks_a = lru_wr.shape[2]
    n_cols = w_in.shape[2]
    assert t_len == GRID_W * CHUNK and tc_len % CHUNK == 0
    assert d_a // n_blocks_a == LANES and hd == LANES
    assert (2 * d_a) % d_b == 0 and n_cols == 2 * d_a + 5 * d_b
    first_b_group = (2 * d_a) // d_b
    n_rows_grid = t_len // GRID_W

    n_rows = -(-(bsz + 1) // SUBLANES) * SUBLANES
    cc = jnp.zeros((n_rows, d), F32).at[:bsz].set(c).at[bsz].set(c_ctx)
    mod3 = _modulation(cc, ada_w[0], ada_b[0]).reshape(n_rows, 1, 3 * d)

    w_in_b = w_in[0].astype(BF16)
    w_out_b = w_out[0].astype(BF16)

    tm = 1024 if t_len % 1024 == 0 else t_len
    tn = 1024 if d_a % 1024 == 0 and d_b % 1024 == 0 else min(d_a, d_b)
    tpb = t_len // tm
    a_tiles = (2 * d_a) // tn
    b_tiles = (4 * d_b) // tn
    p_a = _in_projection(x.reshape(bsz * t_len, d), mod3, norm_w[0], w_in_b,
                         lambda i: i // tpb, lambda j: jnp.where(j < a_tiles, j, j + b_tiles),
                         2 * d_a + d_b, tm, tn)
    wb = 16
    p_b = _in_projection_colmajor(x.reshape(bsz, n_rows_grid, GRID_W, d), mod3, norm_w[0], w_in_b,
                                  lambda j: j + a_tiles, 4 * d_b, wb, tn)
    p_ctx = _in_projection(ctx.reshape(bsz * tc_len, d), mod3, norm_w[0], w_in_b,
                           lambda i: bsz, lambda j: j, n_cols, bsz * tc_len, tn)
    p_lat = p_a.reshape(bsz, t_len, 2 * d_a + d_b)
    p_ctx = p_ctx.reshape(bsz, tc_len, n_cols)

    def gate_w(dirn):
        return jnp.concatenate([lru_wr[0, dirn], lru_wi[0, dirn]], axis=-1).astype(BF16)

    def gate_b(dirn):
        return jnp.concatenate([lru_br[0, dirn].reshape(n_blocks_a, 1, LANES),
                                lru_bi[0, dirn].reshape(n_blocks_a, 1, LANES)], axis=-1)

    cb = 256 if d_a % 256 == 0 else LANES
    ya = _rglru(p_lat, p_ctx, conv_w[0], conv_b[0], gate_w(0), gate_b(0), gate_w(1), gate_b(1),
                lru_lambda[0], d_a, cb, tt=128)

    states = _hgrn(p_ctx.reshape(bsz, tc_len // CHUNK, CHUNK, n_cols), hgrn_lb_logits, d_b,
                   n_heads, first_b_group, None, False)
    o_f, o_b = _hgrn(p_b, hgrn_lb_logits, d_b, n_heads, 0, states, True)

    grid4 = lambda z: z.reshape(bsz, n_rows_grid, GRID_W, z.shape[-1])
    out = _out_projection(grid4(ya), o_f, o_b, grid4(p_lat), first_b_group, grid4(x), mod3,
                          hgrn_norm_w[0], final_norm_w, w_out_b, n_heads, 16, 16)
    return out.reshape(bsz, t_len, d)
```

```python
import functools

import jax
import jax.numpy as jnp
from jax import lax
from jax.experimental import pallas as pl
from jax.experimental.pallas import tpu as pltpu

GRID_W = 64
CHUNK = 64
LRU_C = 8.0
EPS = 1e-6
CONV_PAD_L = 2
LANES = 128
SUBLANES = 8
EXP_CLAMP = 80.0
VMEM_LIMIT = 56 * 1024 * 1024

F32 = jnp.float32
BF16 = jnp.bfloat16


def _sigmoid(z):
    return 1.0 / (1.0 + jnp.exp(-z))


def _silu(z):
    return z * _sigmoid(z)


def _softplus(z):
    return jnp.maximum(z, 0.0) + jnp.log1p(jnp.exp(-jnp.abs(z)))


def _mod_kernel(c_ref, w_ref, b_ref, o_ref):
    s = _silu(c_ref[...])
    o_ref[...] = jnp.dot(s.astype(BF16), w_ref[...].astype(BF16),
                         preferred_element_type=F32) + b_ref[...]


def _modulation(cc, w, b):
    rows, d = cc.shape
    n = w.shape[1]
    tn = 512 if n % 512 == 0 else n
    return pl.pallas_call(
        _mod_kernel,
        grid=(n // tn,),
        in_specs=[pl.BlockSpec((rows, d), lambda j: (0, 0)),
                  pl.BlockSpec((d, tn), lambda j: (0, j)),
                  pl.BlockSpec((1, tn), lambda j: (0, j))],
        out_specs=pl.BlockSpec((rows, tn), lambda j: (0, j)),
        out_shape=jax.ShapeDtypeStruct((rows, n), F32),
        compiler_params=pltpu.CompilerParams(
            dimension_semantics=("arbitrary",), vmem_limit_bytes=VMEM_LIMIT),
        name="adaln_modulation",
    )(cc, w, b.reshape(1, n))


def _inproj_kernel(x_ref, shift_ref, scale_ref, nw_ref, w_ref, o_ref, h_ref):
    @pl.when(pl.program_id(1) == 0)
    def _():
        x = x_ref[...]
        ms = jnp.mean(x * x, axis=-1, keepdims=True)
        y = x * lax.rsqrt(ms + EPS) * nw_ref[...]
        h_ref[...] = (y * (1.0 + scale_ref[0]) + shift_ref[0]).astype(BF16)

    o_ref[...] = jnp.dot(h_ref[...], w_ref[...],
                         preferred_element_type=F32).astype(o_ref.dtype)


def _in_projection(x2d, mod3, norm_w, w_bf16, row_of_tile, wcol_of_tile, n_out, tm, tn):
    m, d = x2d.shape
    return pl.pallas_call(
        _inproj_kernel,
        grid=(m // tm, n_out // tn),
        in_specs=[pl.BlockSpec((tm, d), lambda i, j: (i, 0)),
                  pl.BlockSpec((1, 1, d), lambda i, j: (row_of_tile(i), 0, 0)),
                  pl.BlockSpec((1, 1, d), lambda i, j: (row_of_tile(i), 0, 1)),
                  pl.BlockSpec((1, d), lambda i, j: (0, 0)),
                  pl.BlockSpec((d, tn), lambda i, j: (0, wcol_of_tile(j)))],
        out_specs=pl.BlockSpec((tm, tn), lambda i, j: (i, j)),
        out_shape=jax.ShapeDtypeStruct((m, n_out), BF16),
        scratch_shapes=[pltpu.VMEM((tm, d), BF16)],
        compiler_params=pltpu.CompilerParams(
            dimension_semantics=("parallel", "arbitrary"),
            vmem_limit_bytes=VMEM_LIMIT),
        name="in_projection",
    )(x2d, mod3, mod3, norm_w.reshape(1, d), w_bf16)


def _inproj_colmajor_kernel(x_ref, shift_ref, scale_ref, nw_ref, w_ref, o_ref, h_ref, *, slab):
    _, n_r, n_w, d = x_ref.shape

    @pl.when(pl.program_id(2) == 0)
    def _():
        x = x_ref[0]
        rs = lax.rsqrt(jnp.mean(x * x, axis=-1, keepdims=True) + EPS)
        for s in range(d // slab):
            sl = slice(s * slab, (s + 1) * slab)
            y = x_ref[0, :, :, sl] * rs * nw_ref[:, sl]
            y = y * (1.0 + scale_ref[0][:, sl]) + shift_ref[0][:, sl]
            yt = jnp.transpose(y, (1, 0, 2))
            h_ref[:, sl] = yt.reshape(n_w * n_r, slab).astype(BF16)

    res = jnp.dot(h_ref[...], w_ref[...], preferred_element_type=F32)
    o_ref[0] = res.reshape(n_w, n_r, res.shape[1]).astype(o_ref.dtype)


def _in_projection_colmajor(x4d, mod3, norm_w, w_bf16, wcol_of_tile, n_out, wb, tn):
    bsz, n_r, n_w, d = x4d.shape
    kern = functools.partial(_inproj_colmajor_kernel, slab=2 * LANES)
    return pl.pallas_call(
        kern,
        grid=(bsz, n_w // wb, n_out // tn),
        in_specs=[pl.BlockSpec((1, n_r, wb, d), lambda b, w, j: (b, 0, w, 0)),
                  pl.BlockSpec((1, 1, d), lambda b, w, j: (b, 0, 0)),
                  pl.BlockSpec((1, 1, d), lambda b, w, j: (b, 0, 1)),
                  pl.BlockSpec((1, d), lambda b, w, j: (0, 0)),
                  pl.BlockSpec((d, tn), lambda b, w, j: (0, wcol_of_tile(j)))],
        out_specs=pl.BlockSpec((1, wb, n_r, tn), lambda b, w, j: (b, w, 0, j)),
        out_shape=jax.ShapeDtypeStruct((bsz, n_w, n_r, n_out), BF16),
        scratch_shapes=[pltpu.VMEM((wb * n_r, d), BF16)],
        compiler_params=pltpu.CompilerParams(
            dimension_semantics=("parallel", "parallel", "arbitrary"),
            vmem_limit_bytes=VMEM_LIMIT),
        name="in_projection_colmajor",
    )(x4d, mod3, mod3, norm_w.reshape(1, d), w_bf16)


def _group_scan(a, b, reverse):
    row = lax.broadcasted_iota(jnp.int32, a.shape, 1)
    for k in (1, 2, 4):
        if reverse:
            a_sh = pltpu.roll(a, SUBLANES - k, axis=1)
            b_sh = pltpu.roll(b, SUBLANES - k, axis=1)
            m = row < SUBLANES - k
        else:
            a_sh = pltpu.roll(a, k, axis=1)
            b_sh = pltpu.roll(b, k, axis=1)
            m = row >= k
        b = jnp.where(m, a * b_sh + b, b)
        a = jnp.where(m, a * a_sh, a)
    return a, b


def _lru_tile(xw, cw_ref, cb_ref, wg_ref, bg_ref, sp, carry, reverse, tt):
    cdim = xw.shape[1]
    u = cb_ref[...]
    for k in range(cw_ref.shape[0]):
        off = SUBLANES - CONV_PAD_L + k
        u = u + xw[off:off + tt, :] * cw_ref[k:k + 1, :]
    a_parts, b_parts = [], []
    for blk in range(cdim // LANES):
        sl = slice(blk * LANES, (blk + 1) * LANES)
        ub = u[:, sl]
        g = jnp.dot(ub.astype(BF16), wg_ref[blk], preferred_element_type=F32) + bg_ref[blk]
        r = _sigmoid(g[:, :LANES])
        i = _sigmoid(g[:, LANES:])
        log_a = (-LRU_C) * r * sp[:, sl]
        a_blk = jnp.exp(log_a)
        a_parts.append(a_blk)
        b_parts.append(jnp.sqrt(1.0 - a_blk * a_blk) * (i * ub))
    a = a_parts[0] if len(a_parts) == 1 else jnp.concatenate(a_parts, axis=1)
    b = b_parts[0] if len(b_parts) == 1 else jnp.concatenate(b_parts, axis=1)
    groups = tt // SUBLANES
    a3, b3 = _group_scan(a.reshape(groups, SUBLANES, cdim),
                         b.reshape(groups, SUBLANES, cdim), reverse)
    hs = [None] * groups
    order = range(groups - 1, -1, -1) if reverse else range(groups)
    for g_i in order:
        h = b3[g_i] + a3[g_i] * carry
        carry = h[0:1, :] if reverse else h[SUBLANES - 1:SUBLANES, :]
        hs[g_i] = h
    return jnp.concatenate(hs, axis=0), carry


def _rglru_kernel(xa_ref, ga_ref, xc_ref, cw_ref, cb_ref, wgf_ref, bgf_ref, wgb_ref, bgb_ref,
                  lam_ref, o_ref, xf_ref, xcf_ref, hf_ref, *, tt):
    t_len = xa_ref.shape[1]
    tc_len = xc_ref.shape[1]
    cdim = xa_ref.shape[2]
    zeros = jnp.zeros((SUBLANES, cdim), F32)
    xf_ref[0:SUBLANES, :] = zeros
    xf_ref[SUBLANES:SUBLANES + t_len, :] = xa_ref[0].astype(F32)
    xf_ref[SUBLANES + t_len:, :] = zeros
    xcf_ref[0:SUBLANES, :] = zeros
    xcf_ref[SUBLANES:SUBLANES + tc_len, :] = xc_ref[0].astype(F32)
    xcf_ref[SUBLANES + tc_len:, :] = zeros

    sp_f = _softplus(-lam_ref[0:1, :])
    sp_b = _softplus(-lam_ref[1:2, :])
    win = tt + 2 * SUBLANES

    def tile(src_ref, idx, carry, reverse):
        t0 = pl.multiple_of(idx * tt, tt)
        xw = src_ref[pl.ds(t0, win), :]
        if reverse:
            return _lru_tile(xw, cw_ref, cb_ref, wgb_ref, bgb_ref, sp_b, carry, True, tt)
        return _lru_tile(xw, cw_ref, cb_ref, wgf_ref, bgf_ref, sp_f, carry, False, tt)

    n_lat = t_len // tt
    n_ctx = tc_len // tt
    carry0 = jnp.zeros((1, cdim), F32)

    def ctx_f(i, carry):
        return tile(xcf_ref, i, carry, False)[1]

    carry = lax.fori_loop(0, n_ctx, ctx_f, carry0)

    def lat_f(i, carry):
        h, carry = tile(xf_ref, i, carry, False)
        hf_ref[pl.ds(pl.multiple_of(i * tt, tt), tt), :] = h
        return carry

    lax.fori_loop(0, n_lat, lat_f, carry)

    def ctx_b(i, carry):
        return tile(xcf_ref, n_ctx - 1 - i, carry, True)[1]

    carry = lax.fori_loop(0, n_ctx, ctx_b, carry0)

    def lat_b(i, carry):
        idx = n_lat - 1 - i
        h, carry = tile(xf_ref, idx, carry, True)
        rows = pl.ds(pl.multiple_of(idx * tt, tt), tt)
        gate = ga_ref[0, rows, :].astype(F32)
        o_ref[0, rows, :] = ((hf_ref[rows, :] + h) * _silu(gate)).astype(o_ref.dtype)
        return carry

    lax.fori_loop(0, n_lat, lat_b, carry)


def _rglru(p_lat, p_ctx, conv_w, conv_b, wg_f, bg_f, wg_b, bg_b, lam, d_a, cb, tt):
    bsz, t_len, _ = p_lat.shape
    tc_len = p_ctx.shape[1]
    nblk = cb // LANES
    ncb = d_a // cb
    kern = functools.partial(_rglru_kernel, tt=tt)
    return pl.pallas_call(
        kern,
        grid=(bsz, ncb),
        in_specs=[pl.BlockSpec((1, t_len, cb), lambda b, c: (b, 0, c)),
                  pl.BlockSpec((1, t_len, cb), lambda b, c: (b, 0, ncb + c)),
                  pl.BlockSpec((1, tc_len, cb), lambda b, c: (b, 0, c)),
                  pl.BlockSpec((conv_w.shape[0], cb), lambda b, c: (0, c)),
                  pl.BlockSpec((1, cb), lambda b, c: (0, c)),
                  pl.BlockSpec((nblk, LANES, 2 * LANES), lambda b, c: (c, 0, 0)),
                  pl.BlockSpec((nblk, 1, 2 * LANES), lambda b, c: (c, 0, 0)),
                  pl.BlockSpec((nblk, LANES, 2 * LANES), lambda b, c: (c, 0, 0)),
                  pl.BlockSpec((nblk, 1, 2 * LANES), lambda b, c: (c, 0, 0)),
                  pl.BlockSpec((2, cb), lambda b, c: (0, c))],
        out_specs=pl.BlockSpec((1, t_len, cb), lambda b, c: (b, 0, c)),
        out_shape=jax.ShapeDtypeStruct((bsz, t_len, d_a), BF16),
        scratch_shapes=[pltpu.VMEM((t_len + 2 * SUBLANES, cb), F32),
                        pltpu.VMEM((tc_len + 2 * SUBLANES, cb), F32),
                        pltpu.VMEM((t_len, cb), F32)],
        compiler_params=pltpu.CompilerParams(
            dimension_semantics=("parallel", "parallel"),
            vmem_limit_bytes=VMEM_LIMIT),
        name="rglru",
    )(p_lat, p_lat, p_ctx, conv_w, conv_b.reshape(1, d_a), wg_f, bg_f, wg_b, bg_b, lam)


_NT = (((1,), (1,)), ((), ()))
_TN = (((0,), (0,)), ((), ()))


class _HgrnDir:
    def __init__(self, d, q_ref, f_ref, v_ref, lb, st_ref, o_ref, scratch, reverse, n_heads, hd):
        self.d, self.q_ref, self.f_ref, self.v_ref, self.lb = d, q_ref, f_ref, v_ref, lb
        self.st_ref, self.o_ref, self.reverse, self.n_heads, self.hd = st_ref, o_ref, reverse, n_heads, hd
        (self.hl_ref, self.kk_ref, self.cum_ref, self.qd_ref, self.kd_ref, self.qin_ref,
         self.kout_ref, self.dec_ref, self.sc_ref) = scratch
        self.c_len = q_ref.shape[2]
        r_i = lax.broadcasted_iota(jnp.int32, (self.c_len, self.c_len), 0)
        c_i = lax.broadcasted_iota(jnp.int32, (self.c_len, self.c_len), 1)
        self.tri = (r_i <= c_i) if reverse else (r_i >= c_i)

    def _slabs(self, width):
        total = self.n_heads * self.hd
        return [slice(s, s + width) for s in range(0, total, width)]

    def gates(self):
        d, c_len = self.d, self.c_len
        for sl in self._slabs(2 * LANES):
            lb = self.lb[:, sl]
            f = lb + (1.0 - lb) * _sigmoid(self.f_ref[0, 0, :, sl].astype(F32))
            logf = jnp.log(f)
            self.kk_ref[d, :, sl] = 1.0 - f
            hi = logf.astype(BF16)
            self.hl_ref[d, 0:c_len, sl] = hi
            self.hl_ref[d, c_len:2 * c_len, sl] = (logf - hi.astype(F32)).astype(BF16)

    def cumulate(self):
        tri_b = self.tri.astype(BF16)
        tri2 = jnp.concatenate([tri_b, tri_b], axis=1)
        self.cum_ref[self.d] = jnp.dot(tri2, self.hl_ref[self.d], preferred_element_type=F32)

    def decays(self):
        d, c_len = self.d, self.c_len
        half = c_len // 2
        row_last = 0 if self.reverse else c_len - 1
        row_ref = half if self.reverse else half - 1
        for sl in self._slabs(2 * LANES):
            cum = self.cum_ref[d, :, sl]
            last = cum[row_last:row_last + 1, :]
            ref = cum[row_ref:row_ref + 1, :]
            kk = self.kk_ref[d, :, sl]
            qs = _silu(self.q_ref[0, 0, :, sl].astype(F32))
            self.qd_ref[d, :, sl] = (qs * jnp.exp(jnp.minimum(cum - ref, EXP_CLAMP))).astype(BF16)
            self.kd_ref[d, :, sl] = (kk * jnp.exp(jnp.minimum(ref - cum, EXP_CLAMP))).astype(BF16)
            self.qin_ref[d, :, sl] = (qs * jnp.exp(cum)).astype(BF16)
            self.kout_ref[d, :, sl] = (kk * jnp.exp(last - cum)).astype(BF16)
            self.dec_ref[d, :, sl] = jnp.exp(last)

    def _head(self, h):
        return slice(h * self.hd, (h + 1) * self.hd)

    def scores(self):
        d = self.d
        for h in range(self.n_heads):
            sl = self._head(h)
            s = lax.dot_general(self.qd_ref[d, :, sl], self.kd_ref[d, :, sl], _NT,
                                preferred_element_type=F32)
            self.sc_ref[d, h] = jnp.where(self.tri, s, 0.0).astype(BF16)

    def outputs(self):
        d = self.d
        for h in range(self.n_heads):
            sl = self._head(h)
            o = (jnp.dot(self.sc_ref[d, h], self.v_ref[0, 0, :, sl], preferred_element_type=F32)
                 + lax.dot_general(self.qin_ref[d, :, sl], self.st_ref[h].astype(BF16), _NT,
                                   preferred_element_type=F32))
            self.o_ref[0, 0, :, sl] = o.astype(self.o_ref.dtype)

    def update_state(self):
        d = self.d
        for h in range(self.n_heads):
            sl = self._head(h)
            self.st_ref[h] = (self.st_ref[h] * self.dec_ref[d, :, sl]
                              + lax.dot_general(self.v_ref[0, 0, :, sl], self.kout_ref[d, :, sl], _TN,
                                                preferred_element_type=F32))


def _lower_bounds(logits_ref, layer):
    out = []
    for d in range(2):
        rows = [logits_ref[d, l:l + 1, :] for l in range(logits_ref.shape[1])]
        m = functools.reduce(jnp.maximum, rows)
        e = [jnp.exp(r - m) for r in rows]
        out.append(sum(e[:layer + 1]) / sum(e))
    return out


def _hgrn_kernel(*refs, n_heads, hd, has_init, emit_o, emit_state):
    qf_ref, ff_ref, vf_ref, qb_ref, fb_ref, vb_ref, lg_ref = refs[:7]
    pos = 7
    if has_init:
        s0f_ref, s0b_ref = refs[pos:pos + 2]
        pos += 2
    if emit_o:
        of_ref, ob_ref = refs[pos:pos + 2]
        pos += 2
    if emit_state:
        sof_ref, sob_ref = refs[pos:pos + 2]
        pos += 2
    sf_ref, sb_ref = refs[pos:pos + 2]
    scratch = refs[pos + 2:]

    j = pl.program_id(1)

    @pl.when(j == 0)
    def _():
        if has_init:
            sf_ref[...] = s0f_ref[0]
            sb_ref[...] = s0b_ref[0]
        else:
            sf_ref[...] = jnp.zeros_like(sf_ref)
            sb_ref[...] = jnp.zeros_like(sb_ref)

    lb_f, lb_b = _lower_bounds(lg_ref, 0)
    dirs = [_HgrnDir(0, qf_ref, ff_ref, vf_ref, lb_f, sf_ref, of_ref if emit_o else None,
                     scratch, False, n_heads, hd),
            _HgrnDir(1, qb_ref, fb_ref, vb_ref, lb_b, sb_ref, ob_ref if emit_o else None,
                     scratch, True, n_heads, hd)]
    stages = ["gates", "cumulate", "decays"]
    stages += ["scores", "outputs"] if emit_o else []
    stages += ["update_state"]
    for stage in stages:
        for dirn in dirs:
            getattr(dirn, stage)()

    if emit_state:
        @pl.when(j == pl.num_programs(1) - 1)
        def _():
            sof_ref[0] = sf_ref[...]
            sob_ref[0] = sb_ref[...]


def _hgrn(p_view, logits, d_b, n_heads, first_group, init_states, emit_o):
    bsz, n_chunks = p_view.shape[:2]
    hd = d_b // n_heads
    has_init = init_states is not None
    emit_state = not emit_o

    def spec(group, reverse):
        def imap(b, j):
            return (b, n_chunks - 1 - j if reverse else j, 0, first_group + group)
        return pl.BlockSpec((1, 1, CHUNK, d_b), imap)

    in_specs = [spec(0, False), spec(1, False), spec(3, False),
                spec(0, True), spec(2, True), spec(3, True),
                pl.BlockSpec(logits.shape, lambda b, j: (0, 0, 0))]
    args = [p_view] * 6 + [logits]
    state_spec = pl.BlockSpec((1, n_heads, hd, hd), lambda b, j: (b, 0, 0, 0))
    state_shape = jax.ShapeDtypeStruct((bsz, n_heads, hd, hd), F32)
    if has_init:
        in_specs += [state_spec, state_spec]
        args += list(init_states)
    out_specs, out_shape = [], []
    if emit_o:
        o_shape = jax.ShapeDtypeStruct((bsz, n_chunks, CHUNK, d_b), BF16)
        out_specs += [pl.BlockSpec((1, 1, CHUNK, d_b), lambda b, j: (b, j, 0, 0)),
                      pl.BlockSpec((1, 1, CHUNK, d_b), lambda b, j: (b, n_chunks - 1 - j, 0, 0))]
        out_shape += [o_shape, o_shape]
    if emit_state:
        out_specs += [state_spec, state_spec]
        out_shape += [state_shape, state_shape]
    kern = functools.partial(_hgrn_kernel, n_heads=n_heads, hd=hd, has_init=has_init,
                             emit_o=emit_o, emit_state=emit_state)
    return pl.pallas_call(
        kern,
        grid=(bsz, n_chunks),
        in_specs=in_specs,
        out_specs=out_specs,
        out_shape=out_shape,
        scratch_shapes=[pltpu.VMEM((n_heads, hd, hd), F32),
                        pltpu.VMEM((n_heads, hd, hd), F32),
                        pltpu.VMEM((2, 2 * CHUNK, d_b), BF16),
                        pltpu.VMEM((2, CHUNK, d_b), F32),
                        pltpu.VMEM((2, CHUNK, d_b), F32),
                        pltpu.VMEM((2, CHUNK, d_b), BF16),
                        pltpu.VMEM((2, CHUNK, d_b), BF16),
                        pltpu.VMEM((2, CHUNK, d_b), BF16),
                        pltpu.VMEM((2, CHUNK, d_b), BF16),
                        pltpu.VMEM((2, 1, d_b), F32),
                        pltpu.VMEM((2, n_heads, CHUNK, CHUNK), BF16)],
        compiler_params=pltpu.CompilerParams(
            dimension_semantics=("parallel", "arbitrary"),
            vmem_limit_bytes=VMEM_LIMIT),
        name="hgrn2_latent" if emit_o else "hgrn2_context",
    )(*args)


def _outproj_kernel(ya_ref, of_ref, ob_ref, gb_ref, x_ref, gate_ref, hnw_ref, fnw_ref, w_ref,
                    o_ref, y_ref, *, n_heads, hd):
    _, rb, wb, d_a = ya_ref.shape
    rows = rb * wb
    d = x_ref.shape[3]
    y_ref[:, 0:d_a] = ya_ref[0].reshape(rows, d_a)
    hnw = hnw_ref[...]
    for h in range(n_heads):
        sl = slice(h * hd, (h + 1) * hd)
        o = of_ref[0, :, :, sl].astype(F32) + ob_ref[0, :, :, sl].astype(F32)
        ms = jnp.mean(o * o, axis=-1, keepdims=True)
        on = jnp.transpose(o * lax.rsqrt(ms + EPS) * hnw, (1, 0, 2))
        yb = on * _silu(gb_ref[0, :, :, sl].astype(F32))
        y_ref[:, d_a + h * hd:d_a + (h + 1) * hd] = yb.reshape(rows, hd).astype(BF16)
    acc = jnp.dot(y_ref[...], w_ref[...], preferred_element_type=F32)
    z = x_ref[0].reshape(rows, d) + gate_ref[0] * acc
    ms = jnp.mean(z * z, axis=-1, keepdims=True)
    o_ref[0] = (z * lax.rsqrt(ms + EPS) * fnw_ref[...]).reshape(rb, wb, d)


def _out_projection(ya4, of4, ob4, pa4, gb_block, x4, mod3, hnw, fnw, w_bf16, n_heads, rb, wb):
    bsz, n_r, n_w, d = x4.shape
    d_a = ya4.shape[3]
    d_b = of4.shape[3]
    hd = d_b // n_heads
    kern = functools.partial(_outproj_kernel, n_heads=n_heads, hd=hd)

    def raster(c, col=0):
        return pl.BlockSpec((1, rb, wb, c), lambda b, r, w: (b, r, w, col))

    def colmajor(c):
        return pl.BlockSpec((1, wb, rb, c), lambda b, r, w: (b, w, r, 0))

    return pl.pallas_call(
        kern,
        grid=(bsz, n_r // rb, n_w // wb),
        in_specs=[raster(d_a), colmajor(d_b), colmajor(d_b), raster(d_b, gb_block), raster(d),
                  pl.BlockSpec((1, 1, d), lambda b, r, w: (b, 0, 2)),
                  pl.BlockSpec((1, hd), lambda b, r, w: (0, 0)),
                  pl.BlockSpec((1, d), lambda b, r, w: (0, 0)),
                  pl.BlockSpec((d_a + d_b, d), lambda b, r, w: (0, 0),
                               pipeline_mode=pl.Buffered(1))],
        out_specs=raster(d),
        out_shape=jax.ShapeDtypeStruct((bsz, n_r, n_w, d), F32),
        scratch_shapes=[pltpu.VMEM((rb * wb, d_a + d_b), BF16)],
        compiler_params=pltpu.CompilerParams(
            dimension_semantics=("parallel", "parallel", "parallel"),
            vmem_limit_bytes=VMEM_LIMIT),
        name="out_projection",
    )(ya4, of4, ob4, pa4, x4, mod3, hnw.reshape(1, hd), fnw.reshape(1, d), w_bf16)


def kernel(x, c, ctx, c_ctx, ada_w, ada_b, norm_w, w_in, conv_w, conv_b, lru_wr, lru_br, lru_wi,
           lru_bi, lru_lambda, hgrn_lb_logits, hgrn_norm_w, w_out, final_norm_w):
    bsz, t_len, d = x.shape
    tc_len = ctx.shape[1]
    assert ada_w.shape[0] == 1, "single-layer stack only"
    d_a = conv_w.shape[2]
    d_b = hgrn_lb_logits.shape[2]
    hd = hgrn_norm_w.shape[1]
    n_heads = d_b // hd
    n_blocks_a = lru_wr.shape[2]
    n_cols = w_in.shape[2]
    assert t_len == GRID_W * CHUNK and tc_len % CHUNK == 0
    assert d_a // n_blocks_a == LANES and hd == LANES
    assert (2 * d_a) % d_b == 0 and n_cols == 2 * d_a + 5 * d_b
    first_b_group = (2 * d_a) // d_b
    n_rows_grid = t_len // GRID_W

    n_rows = -(-(bsz + 1) // SUBLANES) * SUBLANES
    cc = jnp.zeros((n_rows, d), F32).at[:bsz].set(c).at[bsz].set(c_ctx)
    mod3 = _modulation(cc, ada_w[0], ada_b[0]).reshape(n_rows, 1, 3 * d)

    w_in_b = w_in[0].astype(BF16)
    w_out_b = w_out[0].astype(BF16)

    tm = 1024 if t_len % 1024 == 0 else t_len
    tn = 1024 if d_a % 1024 == 0 and d_b % 1024 == 0 else min(d_a, d_b)
    tpb = t_len // tm
    a_tiles = (2 * d_a) // tn
    b_tiles = (4 * d_b) // tn
    p_a = _in_projection(x.reshape(bsz * t_len, d), mod3, norm_w[0], w_in_b,
                         lambda i: i // tpb, lambda j: jnp.where(j < a_tiles, j, j + b_tiles),
                         2 * d_a + d_b, tm, tn)
    wb = 16
    p_b = _in_projection_colmajor(x.reshape(bsz, n_rows_grid, GRID_W, d), mod3, norm_w[0], w_in_b,
                                  lambda j: j + a_tiles, 4 * d_b, wb, tn)
    p_ctx = _in_projection(ctx.reshape(bsz * tc_len, d), mod3, norm_w[0], w_in_b,
                           lambda i: bsz, lambda j: j, n_cols, bsz * tc_len, tn)
    p_lat = p_a.reshape(bsz, t_len, 2 * d_a + d_b)
    p_ctx = p_ctx.reshape(bsz, tc_len, n_cols)

    def gate_w(dirn):
        return jnp.concatenate([lru_wr[0, dirn], lru_wi[0, dirn]], axis=-1).astype(BF16)

    def gate_b(dirn):
        return jnp.concatenate([lru_br[0, dirn].reshape(n_blocks_a, 1, LANES),
                                lru_bi[0, dirn].reshape(n_blocks_a, 1, LANES)], axis=-1)

    cb = 256 if d_a % 256 == 0 else LANES
    ya = _rglru(p_lat, p_ctx, conv_w[0], conv_b[0], gate_w(0), gate_b(0), gate_w(1), gate_b(1),
                lru_lambda[0], d_a, cb, tt=128)

    states = _hgrn(p_ctx.reshape(bsz, tc_len // CHUNK, CHUNK, n_cols), hgrn_lb_logits, d_b,
                   n_heads, first_b_group, None, False)
    o_f, o_b = _hgrn(p_b, hgrn_lb_logits, d_b, n_heads, 0, states, True)

    grid4 = lambda z: z.reshape(bsz, n_rows_grid, GRID_W, z.shape[-1])
    out = _out_projection(grid4(ya), o_f, o_b, grid4(p_lat), first_b_group, grid4(x), mod3,
                          hgrn_norm_w[0], final_norm_w, w_out_b, n_heads, 16, 16)
    return out.reshape(bsz, t_len, d)
```

```python
import functools

import jax
import jax.numpy as jnp
from jax import lax
from jax.experimental import pallas as pl
from jax.experimental.pallas import tpu as pltpu

GRID_W = 64
CHUNK = 64
LRU_C = 8.0
EPS = 1e-6
CONV_PAD_L = 2
LANES = 128
SUBLANES = 8
EXP_CLAMP = 80.0
VMEM_LIMIT = 56 * 1024 * 1024

F32 = jnp.float32
BF16 = jnp.bfloat16


def _sigmoid(z):
    return 1.0 / (1.0 + jnp.exp(-z))


def _silu(z):
    return z * _sigmoid(z)


def _softplus(z):
    return jnp.maximum(z, 0.0) + jnp.log1p(jnp.exp(-jnp.abs(z)))


def _mod_kernel(c_ref, w_ref, b_ref, o_ref):
    s = _silu(c_ref[...])
    o_ref[...] = jnp.dot(s.astype(BF16), w_ref[...].astype(BF16),
                         preferred_element_type=F32) + b_ref[...]


def _modulation(cc, w, b):
    rows, d = cc.shape
    n = w.shape[1]
    tn = 512 if n % 512 == 0 else n
    return pl.pallas_call(
        _mod_kernel,
        grid=(n // tn,),
        in_specs=[pl.BlockSpec((rows, d), lambda j: (0, 0)),
                  pl.BlockSpec((d, tn), lambda j: (0, j)),
                  pl.BlockSpec((1, tn), lambda j: (0, j))],
        out_specs=pl.BlockSpec((rows, tn), lambda j: (0, j)),
        out_shape=jax.ShapeDtypeStruct((rows, n), F32),
        compiler_params=pltpu.CompilerParams(
            dimension_semantics=("arbitrary",), vmem_limit_bytes=VMEM_LIMIT),
        name="adaln_modulation",
    )(cc, w, b.reshape(1, n))


def _inproj_kernel(x_ref, shift_ref, scale_ref, nw_ref, w_ref, o_ref, h_ref):
    @pl.when(pl.program_id(1) == 0)
    def _():
        x = x_ref[...]
        ms = jnp.mean(x * x, axis=-1, keepdims=True)
        y = x * lax.rsqrt(ms + EPS) * nw_ref[...]
        h_ref[...] = (y * (1.0 + scale_ref[0]) + shift_ref[0]).astype(BF16)

    o_ref[...] = jnp.dot(h_ref[...], w_ref[...],
                         preferred_element_type=F32).astype(o_ref.dtype)


def _in_projection(x2d, mod3, norm_w, w_bf16, row_of_tile, wcol_of_tile, n_out, tm, tn):
    m, d = x2d.shape
    return pl.pallas_call(
        _inproj_kernel,
        grid=(m // tm, n_out // tn),
        in_specs=[pl.BlockSpec((tm, d), lambda i, j: (i, 0)),
                  pl.BlockSpec((1, 1, d), lambda i, j: (row_of_tile(i), 0, 0)),
                  pl.BlockSpec((1, 1, d), lambda i, j: (row_of_tile(i), 0, 1)),
                  pl.BlockSpec((1, d), lambda i, j: (0, 0)),
                  pl.BlockSpec((d, tn), lambda i, j: (0, wcol_of_tile(j)))],
        out_specs=pl.BlockSpec((tm, tn), lambda i, j: (i, j)),
        out_shape=jax.ShapeDtypeStruct((m, n_out), BF16),
        scratch_shapes=[pltpu.VMEM((tm, d), BF16)],
        compiler_params=pltpu.CompilerParams(
            dimension_semantics=("parallel", "arbitrary"),
            vmem_limit_bytes=VMEM_LIMIT),
        name="in_projection",
    )(x2d, mod3, mod3, norm_w.reshape(1, d), w_bf16)


def _inproj_colmajor_kernel(x_ref, shift_ref, scale_ref, nw_ref, w_ref, o_ref, h_ref, *, slab):
    _, n_r, n_w, d = x_ref.shape

    @pl.when(pl.program_id(2) == 0)
    def _():
        x = x_ref[0]
        rs = lax.rsqrt(jnp.mean(x * x, axis=-1, keepdims=True) + EPS)
        for s in range(d // slab):
            sl = slice(s * slab, (s + 1) * slab)
            y = x_ref[0, :, :, sl] * rs * nw_ref[:, sl]
            y = y * (1.0 + scale_ref[0][:, sl]) + shift_ref[0][:, sl]
            yt = jnp.transpose(y, (1, 0, 2))
            h_ref[:, sl] = yt.reshape(n_w * n_r, slab).astype(BF16)

    res = jnp.dot(h_ref[...], w_ref[...], preferred_element_type=F32)
    o_ref[0] = res.reshape(n_w, n_r, res.shape[1]).astype(o_ref.dtype)


def _in_projection_colmajor(x4d, mod3, norm_w, w_bf16, wcol_of_tile, n_out, wb, tn):
    bsz, n_r, n_w, d = x4d.shape
    kern = functools.partial(_inproj_colmajor_kernel, slab=2 * LANES)
    return pl.pallas_call(
        kern,
        grid=(bsz, n_w // wb, n_out // tn),
        in_specs=[pl.BlockSpec((1, n_r, wb, d), lambda b, w, j: (b, 0, w, 0)),
                  pl.BlockSpec((1, 1, d), lambda b, w, j: (b, 0, 0)),
                  pl.BlockSpec((1, 1, d), lambda b, w, j: (b, 0, 1)),
                  pl.BlockSpec((1, d), lambda b, w, j: (0, 0)),
                  pl.BlockSpec((d, tn), lambda b, w, j: (0, wcol_of_tile(j)))],
        out_specs=pl.BlockSpec((1, wb, n_r, tn), lambda b, w, j: (b, w, 0, j)),
        out_shape=jax.ShapeDtypeStruct((bsz, n_w, n_r, n_out), BF16),
        scratch_shapes=[pltpu.VMEM((wb * n_r, d), BF16)],
        compiler_params=pltpu.CompilerParams(
            dimension_semantics=("parallel", "parallel", "arbitrary"),
            vmem_limit_bytes=VMEM_LIMIT),
        name="in_projection_colmajor",
    )(x4d, mod3, mod3, norm_w.reshape(1, d), w_bf16)


def _group_scan(a, b, reverse):
    row = lax.broadcasted_iota(jnp.int32, a.shape, 1)
    for k in (1, 2, 4):
        if reverse:
            a_sh = pltpu.roll(a, SUBLANES - k, axis=1)
            b_sh = pltpu.roll(b, SUBLANES - k, axis=1)
            m = row < SUBLANES - k
        else:
            a_sh = pltpu.roll(a, k, axis=1)
            b_sh = pltpu.roll(b, k, axis=1)
            m = row >= k
        b = jnp.where(m, a * b_sh + b, b)
        a = jnp.where(m, a * a_sh, a)
    return a, b


def _sqrt_unit(x):
    return jnp.where(x > 0.0, x * lax.rsqrt(x), 0.0)


def _lru_block(u, wg, bg, sp, carry, reverse):
    tt = u.shape[0]
    g = jnp.dot(u.astype(BF16), wg, preferred_element_type=F32) + bg
    r = _sigmoid(g[:, :LANES])
    i = _sigmoid(g[:, LANES:])
    a = jnp.exp((-LRU_C) * r * sp)
    b = _sqrt_unit(1.0 - a * a) * (i * u)
    groups = tt // SUBLANES
    a3, b3 = _group_scan(a.reshape(groups, SUBLANES, LANES), b.reshape(groups, SUBLANES, LANES),
                         reverse)
    hs = [None] * groups
    order = range(groups - 1, -1, -1) if reverse else range(groups)
    for g_i in order:
        h = b3[g_i] + a3[g_i] * carry
        carry = h[0:1, :] if reverse else h[SUBLANES - 1:SUBLANES, :]
        hs[g_i] = h
    return jnp.concatenate(hs, axis=0), carry


def _rglru_kernel(xa_ref, ga_ref, xc_ref, cw_ref, cb_ref, wgf_ref, bgf_ref, wgb_ref, bgb_ref,
                  lam_ref, o_ref, xf_ref, xcf_ref, u_ref, hf_ref, *, tt):
    t_len = xa_ref.shape[1]
    tc_len = xc_ref.shape[1]
    nblk = xa_ref.shape[2] // LANES
    lanes = [slice(k * LANES, (k + 1) * LANES) for k in range(nblk)]
    zeros = jnp.zeros((SUBLANES, LANES), F32)
    for k, sl in enumerate(lanes):
        xf_ref[k, 0:SUBLANES, :] = zeros
        xf_ref[k, SUBLANES:SUBLANES + t_len, :] = xa_ref[0, :, sl].astype(F32)
        xf_ref[k, SUBLANES + t_len:, :] = zeros
        xcf_ref[k, 0:SUBLANES, :] = zeros
        xcf_ref[k, SUBLANES:SUBLANES + tc_len, :] = xc_ref[0, :, sl].astype(F32)
        xcf_ref[k, SUBLANES + tc_len:, :] = zeros

    sp_f = _softplus(-lam_ref[0:1, :])
    sp_b = _softplus(-lam_ref[1:2, :])

    def conv(src_ref, k, t0):
        u = cb_ref[:, lanes[k]]
        for tap in range(cw_ref.shape[0]):
            off = SUBLANES - CONV_PAD_L + tap
            if off % SUBLANES == 0:
                rows = pl.ds(pl.multiple_of(t0 + off, SUBLANES), tt)
            else:
                rows = pl.ds(t0 + off, tt, stride=1)
            u = u + src_ref[k, rows, :] * cw_ref[tap:tap + 1, lanes[k]]
        return u

    def block(u, k, carry, reverse):
        if reverse:
            return _lru_block(u, wgb_ref[k], bgb_ref[k], sp_b[:, lanes[k]], carry, True)
        return _lru_block(u, wgf_ref[k], bgf_ref[k], sp_f[:, lanes[k]], carry, False)

    n_lat = t_len // tt
    n_ctx = tc_len // tt
    carry0 = tuple(jnp.zeros((1, LANES), F32) for _ in lanes)

    def ctx_pass(reverse):
        def body(i, carry):
            t0 = pl.multiple_of((n_ctx - 1 - i if reverse else i) * tt, tt)
            return tuple(block(conv(xcf_ref, k, t0), k, carry[k], reverse)[1] for k in range(nblk))
        return lax.fori_loop(0, n_ctx, body, carry0)

    def lat_f(i, carry):
        t0 = pl.multiple_of(i * tt, tt)
        rows = pl.ds(t0, tt)
        out = []
        for k, sl in enumerate(lanes):
            u = conv(xf_ref, k, t0)
            u_ref[rows, sl] = u
            h, c_out = block(u, k, carry[k], False)
            hf_ref[rows, sl] = h
            out.append(c_out)
        return tuple(out)

    lax.fori_loop(0, n_lat, lat_f, ctx_pass(False), unroll=2)

    def lat_b(i, carry):
        rows = pl.ds(pl.multiple_of((n_lat - 1 - i) * tt, tt), tt)
        out = []
        for k, sl in enumerate(lanes):
            h, c_out = block(u_ref[rows, sl], k, carry[k], True)
            gate = ga_ref[0, rows, sl].astype(F32)
            o_ref[0, rows, sl] = ((hf_ref[rows, sl] + h) * _silu(gate)).astype(o_ref.dtype)
            out.append(c_out)
        return tuple(out)

    lax.fori_loop(0, n_lat, lat_b, ctx_pass(True), unroll=2)


def _rglru(p_lat, p_ctx, conv_w, conv_b, wg_f, bg_f, wg_b, bg_b, lam, d_a, cb, tt):
    bsz, t_len, _ = p_lat.shape
    tc_len = p_ctx.shape[1]
    nblk = cb // LANES
    ncb = d_a // cb
    kern = functools.partial(_rglru_kernel, tt=tt)
    return pl.pallas_call(
        kern,
        grid=(bsz, ncb),
        in_specs=[pl.BlockSpec((1, t_len, cb), lambda b, c: (b, 0, c)),
                  pl.BlockSpec((1, t_len, cb), lambda b, c: (b, 0, ncb + c)),
                  pl.BlockSpec((1, tc_len, cb), lambda b, c: (b, 0, c)),
                  pl.BlockSpec((conv_w.shape[0], cb), lambda b, c: (0, c)),
                  pl.BlockSpec((1, cb), lambda b, c: (0, c)),
                  pl.BlockSpec((nblk, LANES, 2 * LANES), lambda b, c: (c, 0, 0)),
                  pl.BlockSpec((nblk, 1, 2 * LANES), lambda b, c: (c, 0, 0)),
                  pl.BlockSpec((nblk, LANES, 2 * LANES), lambda b, c: (c, 0, 0)),
                  pl.BlockSpec((nblk, 1, 2 * LANES), lambda b, c: (c, 0, 0)),
                  pl.BlockSpec((2, cb), lambda b, c: (0, c))],
        out_specs=pl.BlockSpec((1, t_len, cb), lambda b, c: (b, 0, c)),
        out_shape=jax.ShapeDtypeStruct((bsz, t_len, d_a), BF16),
        scratch_shapes=[pltpu.VMEM((nblk, t_len + 2 * SUBLANES, LANES), F32),
                        pltpu.VMEM((nblk, tc_len + 2 * SUBLANES, LANES), F32),
                        pltpu.VMEM((t_len, cb), F32),
                        pltpu.VMEM((t_len, cb), F32)],
        compiler_params=pltpu.CompilerParams(
            dimension_semantics=("parallel", "parallel"),
            vmem_limit_bytes=VMEM_LIMIT),
        name="rglru",
    )(p_lat, p_lat, p_ctx, conv_w, conv_b.reshape(1, d_a), wg_f, bg_f, wg_b, bg_b, lam)


_NT = (((1,), (1,)), ((), ()))
_TN = (((0,), (0,)), ((), ()))


class _HgrnDir:
    def __init__(self, d, q_ref, f_ref, v_ref, lb, st_ref, o_ref, scratch, reverse, n_heads, hd):
        self.d, self.q_ref, self.f_ref, self.v_ref, self.lb = d, q_ref, f_ref, v_ref, lb
        self.st_ref, self.o_ref, self.reverse, self.n_heads, self.hd = st_ref, o_ref, reverse, n_heads, hd
        (self.hl_ref, self.kk_ref, self.cum_ref, self.qd_ref, self.kd_ref, self.qin_ref,
         self.kout_ref, self.dec_ref, self.sc_ref) = scratch
        self.c_len = q_ref.shape[2]
        r_i = lax.broadcasted_iota(jnp.int32, (self.c_len, self.c_len), 0)
        c_i = lax.broadcasted_iota(jnp.int32, (self.c_len, self.c_len), 1)
        self.tri = (r_i <= c_i) if reverse else (r_i >= c_i)

    def _slabs(self, width):
        total = self.n_heads * self.hd
        return [slice(s, s + width) for s in range(0, total, width)]

    def gates(self):
        d, c_len = self.d, self.c_len
        for sl in self._slabs(2 * LANES):
            lb = self.lb[:, sl]
            f = lb + (1.0 - lb) * _sigmoid(self.f_ref[0, 0, :, sl].astype(F32))
            logf = jnp.log(f)
            self.kk_ref[d, :, sl] = 1.0 - f
            hi = logf.astype(BF16)
            self.hl_ref[d, 0:c_len, sl] = hi
            self.hl_ref[d, c_len:2 * c_len, sl] = (logf - hi.astype(F32)).astype(BF16)

    def cumulate(self):
        tri_b = self.tri.astype(BF16)
        tri2 = jnp.concatenate([tri_b, tri_b], axis=1)
        self.cum_ref[self.d] = jnp.dot(tri2, self.hl_ref[self.d], preferred_element_type=F32)

    def decays(self):
        d, c_len = self.d, self.c_len
        half = c_len // 2
        row_last = 0 if self.reverse else c_len - 1
        row_ref = half if self.reverse else half - 1
        for sl in self._slabs(2 * LANES):
            cum = self.cum_ref[d, :, sl]
            last = cum[row_last:row_last + 1, :]
            ref = cum[row_ref:row_ref + 1, :]
            kk = self.kk_ref[d, :, sl]
            qs = _silu(self.q_ref[0, 0, :, sl].astype(F32))
            self.qd_ref[d, :, sl] = (qs * jnp.exp(jnp.minimum(cum - ref, EXP_CLAMP))).astype(BF16)
            self.kd_ref[d, :, sl] = (kk * jnp.exp(jnp.minimum(ref - cum, EXP_CLAMP))).astype(BF16)
            self.qin_ref[d, :, sl] = (qs * jnp.exp(cum)).astype(BF16)
            self.kout_ref[d, :, sl] = (kk * jnp.exp(last - cum)).astype(BF16)
            self.dec_ref[d, :, sl] = jnp.exp(last)

    def _head(self, h):
        return slice(h * self.hd, (h + 1) * self.hd)

    def scores(self):
        d = self.d
        for h in range(self.n_heads):
            sl = self._head(h)
            s = lax.dot_general(self.qd_ref[d, :, sl], self.kd_ref[d, :, sl], _NT,
                                preferred_element_type=F32)
            self.sc_ref[d, h] = jnp.where(self.tri, s, 0.0).astype(BF16)

    def outputs(self):
        d = self.d
        for h in range(self.n_heads):
            sl = self._head(h)
            o = (jnp.dot(self.sc_ref[d, h], self.v_ref[0, 0, :, sl], preferred_element_type=F32)
                 + lax.dot_general(self.qin_ref[d, :, sl], self.st_ref[h].astype(BF16), _NT,
                                   preferred_element_type=F32))
            self.o_ref[0, 0, :, sl] = o.astype(self.o_ref.dtype)

    def update_state(self):
        d = self.d
        for h in range(self.n_heads):
            sl = self._head(h)
            self.st_ref[h] = (self.st_ref[h] * self.dec_ref[d, :, sl]
                              + lax.dot_general(self.v_ref[0, 0, :, sl], self.kout_ref[d, :, sl], _TN,
                                                preferred_element_type=F32))


def _lower_bounds(logits_ref, layer):
    out = []
    for d in range(2):
        rows = [logits_ref[d, l:l + 1, :] for l in range(logits_ref.shape[1])]
        m = functools.reduce(jnp.maximum, rows)
        e = [jnp.exp(r - m) for r in rows]
        out.append(sum(e[:layer + 1]) / sum(e))
    return out


def _hgrn_kernel(*refs, n_heads, hd, has_init, emit_o, emit_state):
    qf_ref, ff_ref, vf_ref, qb_ref, fb_ref, vb_ref, lg_ref = refs[:7]
    pos = 7
    if has_init:
        s0f_ref, s0b_ref = refs[pos:pos + 2]
        pos += 2
    if emit_o:
        of_ref, ob_ref = refs[pos:pos + 2]
        pos += 2
    if emit_state:
        sof_ref, sob_ref = refs[pos:pos + 2]
        pos += 2
    sf_ref, sb_ref = refs[pos:pos + 2]
    scratch = refs[pos + 2:]

    j = pl.program_id(1)

    @pl.when(j == 0)
    def _():
        if has_init:
            sf_ref[...] = s0f_ref[0]
            sb_ref[...] = s0b_ref[0]
        else:
            sf_ref[...] = jnp.zeros_like(sf_ref)
            sb_ref[...] = jnp.zeros_like(sb_ref)

    lb_f, lb_b = _lower_bounds(lg_ref, 0)
    dirs = [_HgrnDir(0, qf_ref, ff_ref, vf_ref, lb_f, sf_ref, of_ref if emit_o else None,
                     scratch, False, n_heads, hd),
            _HgrnDir(1, qb_ref, fb_ref, vb_ref, lb_b, sb_ref, ob_ref if emit_o else None,
                     scratch, True, n_heads, hd)]
    stages = ["gates", "cumulate", "decays"]
    stages += ["scores", "outputs"] if emit_o else []
    stages += ["update_state"]
    for stage in stages:
        for dirn in dirs:
            getattr(dirn, stage)()

    if emit_state:
        @pl.when(j == pl.num_programs(1) - 1)
        def _():
            sof_ref[0] = sf_ref[...]
            sob_ref[0] = sb_ref[...]


def _hgrn(p_view, logits, d_b, n_heads, first_group, init_states, emit_o):
    bsz, n_chunks = p_view.shape[:2]
    hd = d_b // n_heads
    has_init = init_states is not None
    emit_state = not emit_o

    def spec(group, reverse):
        def imap(b, j):
            return (b, n_chunks - 1 - j if reverse else j, 0, first_group + group)
        return pl.BlockSpec((1, 1, CHUNK, d_b), imap)

    in_specs = [spec(0, False), spec(1, False), spec(3, False),
                spec(0, True), spec(2, True), spec(3, True),
                pl.BlockSpec(logits.shape, lambda b, j: (0, 0, 0))]
    args = [p_view] * 6 + [logits]
    state_spec = pl.BlockSpec((1, n_heads, hd, hd), lambda b, j: (b, 0, 0, 0))
    state_shape = jax.ShapeDtypeStruct((bsz, n_heads, hd, hd), F32)
    if has_init:
        in_specs += [state_spec, state_spec]
        args += list(init_states)
    out_specs, out_shape = [], []
    if emit_o:
        o_shape = jax.ShapeDtypeStruct((bsz, n_chunks, CHUNK, d_b), BF16)
        out_specs += [pl.BlockSpec((1, 1, CHUNK, d_b), lambda b, j: (b, j, 0, 0)),
                      pl.BlockSpec((1, 1, CHUNK, d_b), lambda b, j: (b, n_chunks - 1 - j, 0, 0))]
        out_shape += [o_shape, o_shape]
    if emit_state:
        out_specs += [state_spec, state_spec]
        out_shape += [state_shape, state_shape]
    kern = functools.partial(_hgrn_kernel, n_heads=n_heads, hd=hd, has_init=has_init,
                             emit_o=emit_o, emit_state=emit_state)
    return pl.pallas_call(
        kern,
        grid=(bsz, n_chunks),
        in_specs=in_specs,
        out_specs=out_specs,
        out_shape=out_shape,
        scratch_shapes=[pltpu.VMEM((n_heads, hd, hd), F32),
                        pltpu.VMEM((n_heads, hd, hd), F32),
                        pltpu.VMEM((2, 2 * CHUNK, d_b), BF16),
                        pltpu.VMEM((2, CHUNK, d_b), F32),
                        pltpu.VMEM((2, CHUNK, d_b), F32),
                        pltpu.VMEM((2, CHUNK, d_b), BF16),
                        pltpu.VMEM((2, CHUNK, d_b), BF16),
                        pltpu.VMEM((2, CHUNK, d_b), BF16),
                        pltpu.VMEM((2, CHUNK, d_b), BF16),
                        pltpu.VMEM((2, 1, d_b), F32),
                        pltpu.VMEM((2, n_heads, CHUNK, CHUNK), BF16)],
        compiler_params=pltpu.CompilerParams(
            dimension_semantics=("parallel", "arbitrary"),
            vmem_limit_bytes=VMEM_LIMIT),
        name="hgrn2_latent" if emit_o else "hgrn2_context",
    )(*args)


def _outproj_kernel(ya_ref, of_ref, ob_ref, gb_ref, x_ref, gate_ref, hnw_ref, fnw_ref, w_ref,
                    o_ref, y_ref, *, n_heads, hd):
    _, rb, wb, d_a = ya_ref.shape
    rows = rb * wb
    d = x_ref.shape[3]
    y_ref[:, 0:d_a] = ya_ref[0].reshape(rows, d_a)
    hnw = hnw_ref[...]
    for h in range(n_heads):
        sl = slice(h * hd, (h + 1) * hd)
        o = of_ref[0, :, :, sl].astype(F32) + ob_ref[0, :, :, sl].astype(F32)
        ms = jnp.mean(o * o, axis=-1, keepdims=True)
        on = jnp.transpose(o * lax.rsqrt(ms + EPS) * hnw, (1, 0, 2))
        yb = on * _silu(gb_ref[0, :, :, sl].astype(F32))
        y_ref[:, d_a + h * hd:d_a + (h + 1) * hd] = yb.reshape(rows, hd).astype(BF16)
    acc = jnp.dot(y_ref[...], w_ref[...], preferred_element_type=F32)
    z = x_ref[0].reshape(rows, d) + gate_ref[0] * acc
    ms = jnp.mean(z * z, axis=-1, keepdims=True)
    o_ref[0] = (z * lax.rsqrt(ms + EPS) * fnw_ref[...]).reshape(rb, wb, d)


def _out_projection(ya4, of4, ob4, pa4, gb_block, x4, mod3, hnw, fnw, w_bf16, n_heads, rb, wb):
    bsz, n_r, n_w, d = x4.shape
    d_a = ya4.shape[3]
    d_b = of4.shape[3]
    hd = d_b // n_heads
    kern = functools.partial(_outproj_kernel, n_heads=n_heads, hd=hd)

    def raster(c, col=0):
        return pl.BlockSpec((1, rb, wb, c), lambda b, r, w: (b, r, w, col))

    def colmajor(c):
        return pl.BlockSpec((1, wb, rb, c), lambda b, r, w: (b, w, r, 0))

    return pl.pallas_call(
        kern,
        grid=(bsz, n_r // rb, n_w // wb),
        in_specs=[raster(d_a), colmajor(d_b), colmajor(d_b), raster(d_b, gb_block), raster(d),
                  pl.BlockSpec((1, 1, d), lambda b, r, w: (b, 0, 2)),
                  pl.BlockSpec((1, hd), lambda b, r, w: (0, 0)),
                  pl.BlockSpec((1, d), lambda b, r, w: (0, 0)),
                  pl.BlockSpec((d_a + d_b, d), lambda b, r, w: (0, 0),
                               pipeline_mode=pl.Buffered(1))],
        out_specs=raster(d),
        out_shape=jax.ShapeDtypeStruct((bsz, n_r, n_w, d), F32),
        scratch_shapes=[pltpu.VMEM((rb * wb, d_a + d_b), BF16)],
        compiler_params=pltpu.CompilerParams(
            dimension_semantics=("parallel", "parallel", "parallel"),
            vmem_limit_bytes=VMEM_LIMIT),
        name="out_projection",
    )(ya4, of4, ob4, pa4, x4, mod3, hnw.reshape(1, hd), fnw.reshape(1, d), w_bf16)


def kernel(x, c, ctx, c_ctx, ada_w, ada_b, norm_w, w_in, conv_w, conv_b, lru_wr, lru_br, lru_wi,
           lru_bi, lru_lambda, hgrn_lb_logits, hgrn_norm_w, w_out, final_norm_w):
    bsz, t_len, d = x.shape
    tc_len = ctx.shape[1]
    assert ada_w.shape[0] == 1, "single-layer stack only"
    d_a = conv_w.shape[2]
    d_b = hgrn_lb_logits.shape[2]
    hd = hgrn_norm_w.shape[1]
    n_heads = d_b // hd
    n_blocks_a = lru_wr.shape[2]
    n_cols = w_in.shape[2]
    assert t_len == GRID_W * CHUNK and tc_len % CHUNK == 0
    assert d_a // n_blocks_a == LANES and hd == LANES
    assert (2 * d_a) % d_b == 0 and n_cols == 2 * d_a + 5 * d_b
    first_b_group = (2 * d_a) // d_b
    n_rows_grid = t_len // GRID_W

    n_rows = -(-(bsz + 1) // SUBLANES) * SUBLANES
    cc = jnp.zeros((n_rows, d), F32).at[:bsz].set(c).at[bsz].set(c_ctx)
    mod3 = _modulation(cc, ada_w[0], ada_b[0]).reshape(n_rows, 1, 3 * d)

    w_in_b = w_in[0].astype(BF16)
    w_out_b = w_out[0].astype(BF16)

    tm = 1024 if t_len % 1024 == 0 else t_len
    tn = 1024 if d_a % 1024 == 0 and d_b % 1024 == 0 else min(d_a, d_b)
    tpb = t_len // tm
    a_tiles = (2 * d_a) // tn
    b_tiles = (4 * d_b) // tn
    p_a = _in_projection(x.reshape(bsz * t_len, d), mod3, norm_w[0], w_in_b,
                         lambda i: i // tpb, lambda j: jnp.where(j < a_tiles, j, j + b_tiles),
                         2 * d_a + d_b, tm, tn)
    wb = 16
    p_b = _in_projection_colmajor(x.reshape(bsz, n_rows_grid, GRID_W, d), mod3, norm_w[0], w_in_b,
                                  lambda j: j + a_tiles, 4 * d_b, wb, tn)
    p_ctx = _in_projection(ctx.reshape(bsz * tc_len, d), mod3, norm_w[0], w_in_b,
                           lambda i: bsz, lambda j: j, n_cols, bsz * tc_len, tn)
    p_lat = p_a.reshape(bsz, t_len, 2 * d_a + d_b)
    p_ctx = p_ctx.reshape(bsz, tc_len, n_cols)

    def gate_w(dirn):
        return jnp.concatenate([lru_wr[0, dirn], lru_wi[0, dirn]], axis=-1).astype(BF16)

    def gate_b(dirn):
        return jnp.concatenate([lru_br[0, dirn].reshape(n_blocks_a, 1, LANES),
                                lru_bi[0, dirn].reshape(n_blocks_a, 1, LANES)], axis=-1)

    cb = 256 if d_a % 256 == 0 else LANES
    ya = _rglru(p_lat, p_ctx, conv_w[0], conv_b[0], gate_w(0), gate_b(0), gate_w(1), gate_b(1),
                lru_lambda[0], d_a, cb, tt=128)

    states = _hgrn(p_ctx.reshape(bsz, tc_len // CHUNK, CHUNK, n_cols), hgrn_lb_logits, d_b,
                   n_heads, first_b_group, None, False)
    o_f, o_b = _hgrn(p_b, hgrn_lb_logits, d_b, n_heads, 0, states, True)

    grid4 = lambda z: z.reshape(bsz, n_rows_grid, GRID_W, z.shape[-1])
    out = _out_projection(grid4(ya), o_f, o_b, grid4(p_lat), first_b_group, grid4(x), mod3,
                          hgrn_norm_w[0], final_norm_w, w_out_b, n_heads, 16, 16)
    return out.reshape(bsz, t_len, d)
```

```python
import functools

import jax
import jax.numpy as jnp
from jax import lax
from jax.experimental import pallas as pl
from jax.experimental.pallas import tpu as pltpu

GRID_W = 64
CHUNK = 64
LRU_C = 8.0
EPS = 1e-6
CONV_PAD_L = 2
LANES = 128
SUBLANES = 8
EXP2_CLAMP = 115.0
NEG_LOG2E = -1.4426950408889634
VMEM_LIMIT = 56 * 1024 * 1024

F32 = jnp.float32
BF16 = jnp.bfloat16


def _sigmoid(z):
    return 1.0 / (1.0 + jnp.exp2(z * NEG_LOG2E))


def _silu(z):
    return z * _sigmoid(z)


def _softplus(z):
    return jnp.maximum(z, 0.0) + jnp.log1p(jnp.exp(-jnp.abs(z)))


def _mod_kernel(c_ref, w_ref, b_ref, o_ref):
    s = _silu(c_ref[...])
    o_ref[...] = jnp.dot(s.astype(BF16), w_ref[...].astype(BF16),
                         preferred_element_type=F32) + b_ref[...]


def _modulation(cc, w, b):
    rows, d = cc.shape
    n = w.shape[1]
    tn = 512 if n % 512 == 0 else n
    return pl.pallas_call(
        _mod_kernel,
        grid=(n // tn,),
        in_specs=[pl.BlockSpec((rows, d), lambda j: (0, 0)),
                  pl.BlockSpec((d, tn), lambda j: (0, j)),
                  pl.BlockSpec((1, tn), lambda j: (0, j))],
        out_specs=pl.BlockSpec((rows, tn), lambda j: (0, j)),
        out_shape=jax.ShapeDtypeStruct((rows, n), F32),
        compiler_params=pltpu.CompilerParams(
            dimension_semantics=("arbitrary",), vmem_limit_bytes=VMEM_LIMIT),
        name="adaln_modulation",
    )(cc, w, b.reshape(1, n))


def _store_projection(o_ref, res, j, silu_tiles):
    lo, hi = silu_tiles
    if hi <= lo:
        o_ref[...] = res.astype(o_ref.dtype)
        return
    in_range = jnp.logical_and(j >= lo, j < hi)

    @pl.when(in_range)
    def _():
        o_ref[...] = _silu(res).astype(o_ref.dtype)

    @pl.when(jnp.logical_not(in_range))
    def _():
        o_ref[...] = res.astype(o_ref.dtype)


def _inproj_kernel(x_ref, shift_ref, scale_ref, nw_ref, w_ref, o_ref, h_ref, *, silu_tiles):
    @pl.when(pl.program_id(1) == 0)
    def _():
        x = x_ref[...]
        ms = jnp.mean(x * x, axis=-1, keepdims=True)
        y = x * lax.rsqrt(ms + EPS) * nw_ref[...]
        h_ref[...] = (y * (1.0 + scale_ref[0]) + shift_ref[0]).astype(BF16)

    res = jnp.dot(h_ref[...], w_ref[...], preferred_element_type=F32)
    _store_projection(o_ref, res, pl.program_id(1), silu_tiles)


def _in_projection(x2d, mod3, norm_w, w_bf16, row_of_tile, wcol_of_tile, n_out, tm, tn, silu_tiles):
    m, d = x2d.shape
    return pl.pallas_call(
        functools.partial(_inproj_kernel, silu_tiles=silu_tiles),
        grid=(m // tm, n_out // tn),
        in_specs=[pl.BlockSpec((tm, d), lambda i, j: (i, 0)),
                  pl.BlockSpec((1, 1, d), lambda i, j: (row_of_tile(i), 0, 0)),
                  pl.BlockSpec((1, 1, d), lambda i, j: (row_of_tile(i), 0, 1)),
                  pl.BlockSpec((1, d), lambda i, j: (0, 0)),
                  pl.BlockSpec((d, tn), lambda i, j: (0, wcol_of_tile(j)))],
        out_specs=pl.BlockSpec((tm, tn), lambda i, j: (i, j)),
        out_shape=jax.ShapeDtypeStruct((m, n_out), BF16),
        scratch_shapes=[pltpu.VMEM((tm, d), BF16)],
        compiler_params=pltpu.CompilerParams(
            dimension_semantics=("parallel", "arbitrary"),
            vmem_limit_bytes=VMEM_LIMIT),
        name="in_projection",
    )(x2d, mod3, mod3, norm_w.reshape(1, d), w_bf16)


def _inproj_colmajor_kernel(x_ref, shift_ref, scale_ref, nw_ref, w_ref, o_ref, h_ref, *, slab,
                            silu_tiles):
    _, n_r, n_w, d = x_ref.shape

    @pl.when(pl.program_id(2) == 0)
    def _():
        x = x_ref[0]
        rs = lax.rsqrt(jnp.mean(x * x, axis=-1, keepdims=True) + EPS)
        for s in range(d // slab):
            sl = slice(s * slab, (s + 1) * slab)
            y = x_ref[0, :, :, sl] * rs * nw_ref[:, sl]
            y = y * (1.0 + scale_ref[0][:, sl]) + shift_ref[0][:, sl]
            yt = jnp.transpose(y, (1, 0, 2))
            h_ref[:, sl] = yt.reshape(n_w * n_r, slab).astype(BF16)

    res = jnp.dot(h_ref[...], w_ref[...], preferred_element_type=F32)
    _store_projection(o_ref, res.reshape(1, n_w, n_r, res.shape[1]), pl.program_id(2), silu_tiles)


def _in_projection_colmajor(x4d, mod3, norm_w, w_bf16, wcol_of_tile, n_out, wb, tn, silu_tiles):
    bsz, n_r, n_w, d = x4d.shape
    kern = functools.partial(_inproj_colmajor_kernel, slab=2 * LANES, silu_tiles=silu_tiles)
    return pl.pallas_call(
        kern,
        grid=(bsz, n_w // wb, n_out // tn),
        in_specs=[pl.BlockSpec((1, n_r, wb, d), lambda b, w, j: (b, 0, w, 0)),
                  pl.BlockSpec((1, 1, d), lambda b, w, j: (b, 0, 0)),
                  pl.BlockSpec((1, 1, d), lambda b, w, j: (b, 0, 1)),
                  pl.BlockSpec((1, d), lambda b, w, j: (0, 0)),
                  pl.BlockSpec((d, tn), lambda b, w, j: (0, wcol_of_tile(j)))],
        out_specs=pl.BlockSpec((1, wb, n_r, tn), lambda b, w, j: (b, w, 0, j)),
        out_shape=jax.ShapeDtypeStruct((bsz, n_w, n_r, n_out), BF16),
        scratch_shapes=[pltpu.VMEM((wb * n_r, d), BF16)],
        compiler_params=pltpu.CompilerParams(
            dimension_semantics=("parallel", "parallel", "arbitrary"),
            vmem_limit_bytes=VMEM_LIMIT),
        name="in_projection_colmajor",
    )(x4d, mod3, mod3, norm_w.reshape(1, d), w_bf16)


def _group_scan(a, b, reverse):
    row = lax.broadcasted_iota(jnp.int32, a.shape, 1)
    for k in (1, 2, 4):
        if reverse:
            a_sh = pltpu.roll(a, SUBLANES - k, axis=1)
            b_sh = pltpu.roll(b, SUBLANES - k, axis=1)
            m = row < SUBLANES - k
        else:
            a_sh = pltpu.roll(a, k, axis=1)
            b_sh = pltpu.roll(b, k, axis=1)
            m = row >= k
        b = jnp.where(m, a * b_sh + b, b)
        a = jnp.where(m, a * a_sh, a)
    return a, b


def _sqrt_unit(x):
    return jnp.where(x > 0.0, x * lax.rsqrt(x), 0.0)


def _lru_block(u, wg, bg, sp, carry, reverse):
    tt = u.shape[0]
    g = jnp.dot(u.astype(BF16), wg, preferred_element_type=F32) + bg
    r = _sigmoid(g[:, :LANES])
    i = _sigmoid(g[:, LANES:])
    a = jnp.exp2(r * sp)
    b = _sqrt_unit(1.0 - a * a) * (i * u)
    groups = tt // SUBLANES
    a3, b3 = _group_scan(a.reshape(groups, SUBLANES, LANES), b.reshape(groups, SUBLANES, LANES),
                         reverse)
    hs = [None] * groups
    order = range(groups - 1, -1, -1) if reverse else range(groups)
    for g_i in order:
        h = b3[g_i] + a3[g_i] * carry
        carry = h[0:1, :] if reverse else h[SUBLANES - 1:SUBLANES, :]
        hs[g_i] = h
    return jnp.concatenate(hs, axis=0), carry


def _rglru_kernel(xa_ref, ga_ref, xc_ref, cw_ref, cb_ref, wgf_ref, bgf_ref, wgb_ref, bgb_ref,
                  lam_ref, o_ref, xf_ref, xcf_ref, u_ref, hf_ref, *, tt):
    t_len = xa_ref.shape[1]
    tc_len = xc_ref.shape[1]
    nblk = xa_ref.shape[2] // LANES
    lanes = [slice(k * LANES, (k + 1) * LANES) for k in range(nblk)]
    zeros = jnp.zeros((SUBLANES, LANES), F32)
    for k, sl in enumerate(lanes):
        xf_ref[k, 0:SUBLANES, :] = zeros
        xf_ref[k, SUBLANES:SUBLANES + t_len, :] = xa_ref[0, :, sl].astype(F32)
        xf_ref[k, SUBLANES + t_len:, :] = zeros
        xcf_ref[k, 0:SUBLANES, :] = zeros
        xcf_ref[k, SUBLANES:SUBLANES + tc_len, :] = xc_ref[0, :, sl].astype(F32)
        xcf_ref[k, SUBLANES + tc_len:, :] = zeros

    sp_f = _softplus(-lam_ref[0:1, :]) * (LRU_C * NEG_LOG2E)
    sp_b = _softplus(-lam_ref[1:2, :]) * (LRU_C * NEG_LOG2E)

    def conv(src_ref, k, t0):
        u = cb_ref[:, lanes[k]]
        for tap in range(cw_ref.shape[0]):
            off = SUBLANES - CONV_PAD_L + tap
            if off % SUBLANES == 0:
                rows = pl.ds(pl.multiple_of(t0 + off, SUBLANES), tt)
            else:
                rows = pl.ds(t0 + off, tt, stride=1)
            u = u + src_ref[k, rows, :] * cw_ref[tap:tap + 1, lanes[k]]
        return u

    def block(u, k, carry, reverse):
        if reverse:
            return _lru_block(u, wgb_ref[k], bgb_ref[k], sp_b[:, lanes[k]], carry, True)
        return _lru_block(u, wgf_ref[k], bgf_ref[k], sp_f[:, lanes[k]], carry, False)

    n_lat = t_len // tt
    n_ctx = tc_len // tt
    carry0 = tuple(jnp.zeros((1, LANES), F32) for _ in lanes)

    def ctx_pass(reverse):
        def body(i, carry):
            t0 = pl.multiple_of((n_ctx - 1 - i if reverse else i) * tt, tt)
            return tuple(block(conv(xcf_ref, k, t0), k, carry[k], reverse)[1] for k in range(nblk))
        return lax.fori_loop(0, n_ctx, body, carry0)

    def lat_f(i, carry):
        t0 = pl.multiple_of(i * tt, tt)
        rows = pl.ds(t0, tt)
        out = []
        for k, sl in enumerate(lanes):
            u = conv(xf_ref, k, t0)
            u_ref[rows, sl] = u
            h, c_out = block(u, k, carry[k], False)
            hf_ref[rows, sl] = h
            out.append(c_out)
        return tuple(out)

    lax.fori_loop(0, n_lat, lat_f, ctx_pass(False), unroll=2)

    def lat_b(i, carry):
        rows = pl.ds(pl.multiple_of((n_lat - 1 - i) * tt, tt), tt)
        out = []
        for k, sl in enumerate(lanes):
            h, c_out = block(u_ref[rows, sl], k, carry[k], True)
            gate = ga_ref[0, rows, sl].astype(F32)
            o_ref[0, rows, sl] = ((hf_ref[rows, sl] + h) * gate).astype(o_ref.dtype)
            out.append(c_out)
        return tuple(out)

    lax.fori_loop(0, n_lat, lat_b, ctx_pass(True), unroll=2)


def _rglru(p_lat, p_ctx, conv_w, conv_b, wg_f, bg_f, wg_b, bg_b, lam, d_a, cb, tt):
    bsz, t_len, _ = p_lat.shape
    tc_len = p_ctx.shape[1]
    nblk = cb // LANES
    ncb = d_a // cb
    kern = functools.partial(_rglru_kernel, tt=tt)
    return pl.pallas_call(
        kern,
        grid=(bsz, ncb),
        in_specs=[pl.BlockSpec((1, t_len, cb), lambda b, c: (b, 0, c)),
                  pl.BlockSpec((1, t_len, cb), lambda b, c: (b, 0, ncb + c)),
                  pl.BlockSpec((1, tc_len, cb), lambda b, c: (b, 0, c)),
                  pl.BlockSpec((conv_w.shape[0], cb), lambda b, c: (0, c)),
                  pl.BlockSpec((1, cb), lambda b, c: (0, c)),
                  pl.BlockSpec((nblk, LANES, 2 * LANES), lambda b, c: (c, 0, 0)),
                  pl.BlockSpec((nblk, 1, 2 * LANES), lambda b, c: (c, 0, 0)),
                  pl.BlockSpec((nblk, LANES, 2 * LANES), lambda b, c: (c, 0, 0)),
                  pl.BlockSpec((nblk, 1, 2 * LANES), lambda b, c: (c, 0, 0)),
                  pl.BlockSpec((2, cb), lambda b, c: (0, c))],
        out_specs=pl.BlockSpec((1, t_len, cb), lambda b, c: (b, 0, c)),
        out_shape=jax.ShapeDtypeStruct((bsz, t_len, d_a), BF16),
        scratch_shapes=[pltpu.VMEM((nblk, t_len + 2 * SUBLANES, LANES), F32),
                        pltpu.VMEM((nblk, tc_len + 2 * SUBLANES, LANES), F32),
                        pltpu.VMEM((t_len, cb), F32),
                        pltpu.VMEM((t_len, cb), F32)],
        compiler_params=pltpu.CompilerParams(
            dimension_semantics=("parallel", "parallel"),
            vmem_limit_bytes=VMEM_LIMIT),
        name="rglru",
    )(p_lat, p_lat, p_ctx, conv_w, conv_b.reshape(1, d_a), wg_f, bg_f, wg_b, bg_b, lam)


_NT = (((1,), (1,)), ((), ()))
_TN = (((0,), (0,)), ((), ()))


class _HgrnDir:
    def __init__(self, d, q_ref, f_ref, v_ref, lb, st_ref, o_ref, scratch, reverse, n_heads, hd):
        self.d, self.q_ref, self.f_ref, self.v_ref, self.lb = d, q_ref, f_ref, v_ref, lb
        self.st_ref, self.o_ref, self.reverse, self.n_heads, self.hd = st_ref, o_ref, reverse, n_heads, hd
        (self.hl_ref, self.kk_ref, self.cum_ref, self.qd_ref, self.kd_ref, self.qin_ref,
         self.kout_ref, self.dec_ref, self.sc_ref) = scratch
        self.c_len = q_ref.shape[2]
        r_i = lax.broadcasted_iota(jnp.int32, (self.c_len, self.c_len), 0)
        c_i = lax.broadcasted_iota(jnp.int32, (self.c_len, self.c_len), 1)
        self.tri = (r_i <= c_i) if reverse else (r_i >= c_i)

    def _slabs(self, width):
        total = self.n_heads * self.hd
        return [slice(s, s + width) for s in range(0, total, width)]

    def gates(self):
        d, c_len = self.d, self.c_len
        for sl in self._slabs(2 * LANES):
            lb = self.lb[:, sl]
            f = lb + (1.0 - lb) * _sigmoid(self.f_ref[0, 0, :, sl].astype(F32))
            logf = jnp.log2(f)
            self.kk_ref[d, :, sl] = 1.0 - f
            hi = logf.astype(BF16)
            self.hl_ref[d, 0:c_len, sl] = hi
            self.hl_ref[d, c_len:2 * c_len, sl] = (logf - hi.astype(F32)).astype(BF16)

    def cumulate(self):
        tri_b = self.tri.astype(BF16)
        tri2 = jnp.concatenate([tri_b, tri_b], axis=1)
        self.cum_ref[self.d] = jnp.dot(tri2, self.hl_ref[self.d], preferred_element_type=F32)

    def decays(self):
        d, c_len = self.d, self.c_len
        half = c_len // 2
        row_last = 0 if self.reverse else c_len - 1
        row_ref = half if self.reverse else half - 1
        for sl in self._slabs(2 * LANES):
            cum = self.cum_ref[d, :, sl]
            last = cum[row_last:row_last + 1, :]
            ref = cum[row_ref:row_ref + 1, :]
            kk = self.kk_ref[d, :, sl]
            qs = self.q_ref[0, 0, :, sl].astype(F32)
            self.qd_ref[d, :, sl] = (qs * jnp.exp2(jnp.minimum(cum - ref, EXP2_CLAMP))).astype(BF16)
            self.kd_ref[d, :, sl] = (kk * jnp.exp2(jnp.minimum(ref - cum, EXP2_CLAMP))).astype(BF16)
            self.qin_ref[d, :, sl] = (qs * jnp.exp2(cum)).astype(BF16)
            self.kout_ref[d, :, sl] = (kk * jnp.exp2(last - cum)).astype(BF16)
            self.dec_ref[d, :, sl] = jnp.exp2(last)

    def _head(self, h):
        return slice(h * self.hd, (h + 1) * self.hd)

    def scores(self):
        d = self.d
        for h in range(self.n_heads):
            sl = self._head(h)
            s = lax.dot_general(self.qd_ref[d, :, sl], self.kd_ref[d, :, sl], _NT,
                                preferred_element_type=F32)
            self.sc_ref[d, h] = jnp.where(self.tri, s, 0.0).astype(BF16)

    def outputs(self):
        d = self.d
        for h in range(self.n_heads):
            sl = self._head(h)
            o = (jnp.dot(self.sc_ref[d, h], self.v_ref[0, 0, :, sl], preferred_element_type=F32)
                 + lax.dot_general(self.qin_ref[d, :, sl], self.st_ref[h].astype(BF16), _NT,
                                   preferred_element_type=F32))
            self.o_ref[0, 0, :, sl] = o.astype(self.o_ref.dtype)

    def update_state(self):
        d = self.d
        for h in range(self.n_heads):
            sl = self._head(h)
            self.st_ref[h] = (self.st_ref[h] * self.dec_ref[d, :, sl]
                              + lax.dot_general(self.v_ref[0, 0, :, sl], self.kout_ref[d, :, sl], _TN,
                                                preferred_element_type=F32))


def _lower_bounds(logits_ref, layer):
    out = []
    for d in range(2):
        rows = [logits_ref[d, l:l + 1, :] for l in range(logits_ref.shape[1])]
        m = functools.reduce(jnp.maximum, rows)
        e = [jnp.exp(r - m) for r in rows]
        out.append(sum(e[:layer + 1]) / sum(e))
    return out


def _hgrn_kernel(*refs, n_heads, hd, has_init, emit_o, emit_state):
    qf_ref, ff_ref, vf_ref, qb_ref, fb_ref, vb_ref, lg_ref = refs[:7]
    pos = 7
    if has_init:
        s0f_ref, s0b_ref = refs[pos:pos + 2]
        pos += 2
    if emit_o:
        of_ref, ob_ref = refs[pos:pos + 2]
        pos += 2
    if emit_state:
        sof_ref, sob_ref = refs[pos:pos + 2]
        pos += 2
    sf_ref, sb_ref = refs[pos:pos + 2]
    scratch = refs[pos + 2:]

    j = pl.program_id(1)

    @pl.when(j == 0)
    def _():
        if has_init:
            sf_ref[...] = s0f_ref[0]
            sb_ref[...] = s0b_ref[0]
        else:
            sf_ref[...] = jnp.zeros_like(sf_ref)
            sb_ref[...] = jnp.zeros_like(sb_ref)

    lb_f, lb_b = _lower_bounds(lg_ref, 0)
    dirs = [_HgrnDir(0, qf_ref, ff_ref, vf_ref, lb_f, sf_ref, of_ref if emit_o else None,
                     scratch, False, n_heads, hd),
            _HgrnDir(1, qb_ref, fb_ref, vb_ref, lb_b, sb_ref, ob_ref if emit_o else None,
                     scratch, True, n_heads, hd)]
    stages = ["gates", "cumulate", "decays"]
    stages += ["scores", "outputs"] if emit_o else []
    stages += ["update_state"]
    for stage in stages:
        for dirn in dirs:
            getattr(dirn, stage)()

    if emit_state:
        @pl.when(j == pl.num_programs(1) - 1)
        def _():
            sof_ref[0] = sf_ref[...]
            sob_ref[0] = sb_ref[...]


def _hgrn(p_view, logits, d_b, n_heads, first_group, init_states, emit_o):
    bsz, n_chunks = p_view.shape[:2]
    hd = d_b // n_heads
    has_init = init_states is not None
    emit_state = not emit_o

    def spec(group, reverse):
        def imap(b, j):
            return (b, n_chunks - 1 - j if reverse else j, 0, first_group + group)
        return pl.BlockSpec((1, 1, CHUNK, d_b), imap)

    in_specs = [spec(0, False), spec(1, False), spec(3, False),
                spec(0, True), spec(2, True), spec(3, True),
                pl.BlockSpec(logits.shape, lambda b, j: (0, 0, 0))]
    args = [p_view] * 6 + [logits]
    state_spec = pl.BlockSpec((1, n_heads, hd, hd), lambda b, j: (b, 0, 0, 0))
    state_shape = jax.ShapeDtypeStruct((bsz, n_heads, hd, hd), F32)
    if has_init:
        in_specs += [state_spec, state_spec]
        args += list(init_states)
    out_specs, out_shape = [], []
    if emit_o:
        o_shape = jax.ShapeDtypeStruct((bsz, n_chunks, CHUNK, d_b), BF16)
        out_specs += [pl.BlockSpec((1, 1, CHUNK, d_b), lambda b, j: (b, j, 0, 0)),
                      pl.BlockSpec((1, 1, CHUNK, d_b), lambda b, j: (b, n_chunks - 1 - j, 0, 0))]
        out_shape += [o_shape, o_shape]
    if emit_state:
        out_specs += [state_spec, state_spec]
        out_shape += [state_shape, state_shape]
    kern = functools.partial(_hgrn_kernel, n_heads=n_heads, hd=hd, has_init=has_init,
                             emit_o=emit_o, emit_state=emit_state)
    return pl.pallas_call(
        kern,
        grid=(bsz, n_chunks),
        in_specs=in_specs,
        out_specs=out_specs,
        out_shape=out_shape,
        scratch_shapes=[pltpu.VMEM((n_heads, hd, hd), F32),
                        pltpu.VMEM((n_heads, hd, hd), F32),
                        pltpu.VMEM((2, 2 * CHUNK, d_b), BF16),
                        pltpu.VMEM((2, CHUNK, d_b), F32),
                        pltpu.VMEM((2, CHUNK, d_b), F32),
                        pltpu.VMEM((2, CHUNK, d_b), BF16),
                        pltpu.VMEM((2, CHUNK, d_b), BF16),
                        pltpu.VMEM((2, CHUNK, d_b), BF16),
                        pltpu.VMEM((2, CHUNK, d_b), BF16),
                        pltpu.VMEM((2, 1, d_b), F32),
                        pltpu.VMEM((2, n_heads, CHUNK, CHUNK), BF16)],
        compiler_params=pltpu.CompilerParams(
            dimension_semantics=("parallel", "arbitrary"),
            vmem_limit_bytes=VMEM_LIMIT),
        name="hgrn2_latent" if emit_o else "hgrn2_context",
    )(*args)


def _outproj_kernel(ya_ref, of_ref, ob_ref, gb_ref, x_ref, gate_ref, hnw_ref, fnw_ref, w_ref,
                    o_ref, y_ref, *, n_heads, hd):
    _, rb, wb, d_a = ya_ref.shape
    rows = rb * wb
    d = x_ref.shape[3]
    y_ref[:, 0:d_a] = ya_ref[0].reshape(rows, d_a)
    hnw = hnw_ref[...]
    for h in range(n_heads):
        sl = slice(h * hd, (h + 1) * hd)
        o = of_ref[0, :, :, sl].astype(F32) + ob_ref[0, :, :, sl].astype(F32)
        ms = jnp.mean(o * o, axis=-1, keepdims=True)
        on = jnp.transpose(o * lax.rsqrt(ms + EPS) * hnw, (1, 0, 2))
        yb = on * gb_ref[0, :, :, sl].astype(F32)
        y_ref[:, d_a + h * hd:d_a + (h + 1) * hd] = yb.reshape(rows, hd).astype(BF16)
    acc = jnp.dot(y_ref[...], w_ref[...], preferred_element_type=F32)
    z = x_ref[0].reshape(rows, d) + gate_ref[0] * acc
    ms = jnp.mean(z * z, axis=-1, keepdims=True)
    o_ref[0] = (z * lax.rsqrt(ms + EPS) * fnw_ref[...]).reshape(rb, wb, d)


def _out_projection(ya4, of4, ob4, pa4, gb_block, x4, mod3, hnw, fnw, w_bf16, n_heads, rb, wb):
    bsz, n_r, n_w, d = x4.shape
    d_a = ya4.shape[3]
    d_b = of4.shape[3]
    hd = d_b // n_heads
    kern = functools.partial(_outproj_kernel, n_heads=n_heads, hd=hd)

    def raster(c, col=0):
        return pl.BlockSpec((1, rb, wb, c), lambda b, r, w: (b, r, w, col))

    def colmajor(c):
        return pl.BlockSpec((1, wb, rb, c), lambda b, r, w: (b, w, r, 0))

    return pl.pallas_call(
        kern,
        grid=(bsz, n_r // rb, n_w // wb),
        in_specs=[raster(d_a), colmajor(d_b), colmajor(d_b), raster(d_b, gb_block), raster(d),
                  pl.BlockSpec((1, 1, d), lambda b, r, w: (b, 0, 2)),
                  pl.BlockSpec((1, hd), lambda b, r, w: (0, 0)),
                  pl.BlockSpec((1, d), lambda b, r, w: (0, 0)),
                  pl.BlockSpec((d_a + d_b, d), lambda b, r, w: (0, 0),
                               pipeline_mode=pl.Buffered(1))],
        out_specs=raster(d),
        out_shape=jax.ShapeDtypeStruct((bsz, n_r, n_w, d), F32),
        scratch_shapes=[pltpu.VMEM((rb * wb, d_a + d_b), BF16)],
        compiler_params=pltpu.CompilerParams(
            dimension_semantics=("parallel", "parallel", "parallel"),
            vmem_limit_bytes=VMEM_LIMIT),
        name="out_projection",
    )(ya4, of4, ob4, pa4, x4, mod3, hnw.reshape(1, hd), fnw.reshape(1, d), w_bf16)


def kernel(x, c, ctx, c_ctx, ada_w, ada_b, norm_w, w_in, conv_w, conv_b, lru_wr, lru_br, lru_wi,
           lru_bi, lru_lambda, hgrn_lb_logits, hgrn_norm_w, w_out, final_norm_w):
    bsz, t_len, d = x.shape
    tc_len = ctx.shape[1]
    assert ada_w.shape[0] == 1, "single-layer stack only"
    d_a = conv_w.shape[2]
    d_b = hgrn_lb_logits.shape[2]
    hd = hgrn_norm_w.shape[1]
    n_heads = d_b // hd
    n_blocks_a = lru_wr.shape[2]
    n_cols = w_in.shape[2]
    assert t_len == GRID_W * CHUNK and tc_len % CHUNK == 0
    assert d_a // n_blocks_a == LANES and hd == LANES
    assert (2 * d_a) % d_b == 0 and n_cols == 2 * d_a + 5 * d_b
    first_b_group = (2 * d_a) // d_b
    n_rows_grid = t_len // GRID_W

    n_rows = -(-(bsz + 1) // SUBLANES) * SUBLANES
    cc = jnp.zeros((n_rows, d), F32).at[:bsz].set(c).at[bsz].set(c_ctx)
    mod3 = _modulation(cc, ada_w[0], ada_b[0]).reshape(n_rows, 1, 3 * d)

    w_in_b = w_in[0].astype(BF16)
    w_out_b = w_out[0].astype(BF16)

    tm = 1024 if t_len % 1024 == 0 else t_len
    tn = 1024 if d_a % 1024 == 0 and d_b % 1024 == 0 else min(d_a, d_b)
    tpb = t_len // tm
    a_tiles = (2 * d_a) // tn
    b_tiles = (4 * d_b) // tn
    q_tiles = d_b // tn
    p_a = _in_projection(x.reshape(bsz * t_len, d), mod3, norm_w[0], w_in_b,
                         lambda i: i // tpb, lambda j: jnp.where(j < a_tiles, j, j + b_tiles),
                         2 * d_a + d_b, tm, tn, (d_a // tn, (2 * d_a + d_b) // tn))
    wb = 16
    p_b = _in_projection_colmajor(x.reshape(bsz, n_rows_grid, GRID_W, d), mod3, norm_w[0], w_in_b,
                                  lambda j: j + a_tiles, 4 * d_b, wb, tn, (0, q_tiles))
    p_ctx = _in_projection(ctx.reshape(bsz * tc_len, d), mod3, norm_w[0], w_in_b,
                           lambda i: bsz, lambda j: j, n_cols, bsz * tc_len, tn,
                           (a_tiles, a_tiles + q_tiles))
    p_lat = p_a.reshape(bsz, t_len, 2 * d_a + d_b)
    p_ctx = p_ctx.reshape(bsz, tc_len, n_cols)

    def gate_w(dirn):
        return jnp.concatenate([lru_wr[0, dirn], lru_wi[0, dirn]], axis=-1).astype(BF16)

    def gate_b(dirn):
        return jnp.concatenate([lru_br[0, dirn].reshape(n_blocks_a, 1, LANES),
                                lru_bi[0, dirn].reshape(n_blocks_a, 1, LANES)], axis=-1)

    cb = 256 if d_a % 256 == 0 else LANES
    ya = _rglru(p_lat, p_ctx, conv_w[0], conv_b[0], gate_w(0), gate_b(0), gate_w(1), gate_b(1),
                lru_lambda[0], d_a, cb, tt=128)

    states = _hgrn(p_ctx.reshape(bsz, tc_len // CHUNK, CHUNK, n_cols), hgrn_lb_logits, d_b,
                   n_heads, first_b_group, None, False)
    o_f, o_b = _hgrn(p_b, hgrn_lb_logits, d_b, n_heads, 0, states, True)

    grid4 = lambda z: z.reshape(bsz, n_rows_grid, GRID_W, z.shape[-1])
    out = _out_projection(grid4(ya), o_f, o_b, grid4(p_lat), first_b_group, grid4(x), mod3,
                          hgrn_norm_w[0], final_norm_w, w_out_b, n_heads, 16, 16)
    return out.reshape(bsz, t_len, d)
```

```python
import functools

import jax
import jax.numpy as jnp
from jax import lax
from jax.experimental import pallas as pl
from jax.experimental.pallas import tpu as pltpu

GRID_W = 64
CHUNK = 64
LRU_C = 8.0
EPS = 1e-6
CONV_PAD_L = 2
LANES = 128
SUBLANES = 8
EXP2_CLAMP = 115.0
NEG_LOG2E = -1.4426950408889634
VMEM_LIMIT = 56 * 1024 * 1024
PROLOGUE_ROWS = 16

F32 = jnp.float32
BF16 = jnp.bfloat16


def _sigmoid(z):
    return 1.0 / (1.0 + jnp.exp2(z * NEG_LOG2E))


def _silu(z):
    return z * _sigmoid(z)


def _softplus(z):
    return jnp.maximum(z, 0.0) + jnp.log1p(jnp.exp(-jnp.abs(z)))


def _mod_kernel(c_ref, w_ref, b_ref, o_ref):
    s = _silu(c_ref[...])
    o_ref[...] = jnp.dot(s.astype(BF16), w_ref[...].astype(BF16),
                         preferred_element_type=F32) + b_ref[...]


def _modulation(cc, w, b):
    rows, d = cc.shape
    n = w.shape[1]
    tn = 512 if n % 512 == 0 else n
    return pl.pallas_call(
        _mod_kernel,
        grid=(n // tn,),
        in_specs=[pl.BlockSpec((rows, d), lambda j: (0, 0)),
                  pl.BlockSpec((d, tn), lambda j: (0, j)),
                  pl.BlockSpec((1, tn), lambda j: (0, j))],
        out_specs=pl.BlockSpec((rows, tn), lambda j: (0, j)),
        out_shape=jax.ShapeDtypeStruct((rows, n), F32),
        compiler_params=pltpu.CompilerParams(
            dimension_semantics=("arbitrary",), vmem_limit_bytes=VMEM_LIMIT),
        name="adaln_modulation",
    )(cc, w, b.reshape(1, n))


def _store_projection(o_ref, res, j, silu_tiles):
    lo, hi = silu_tiles
    if hi <= lo:
        o_ref[...] = res.astype(o_ref.dtype)
        return
    in_range = jnp.logical_and(j >= lo, j < hi)
    o_ref[...] = jnp.where(in_range, _silu(res), res).astype(o_ref.dtype)


def _inproj_kernel(x_ref, shift_ref, scale_ref, nw_ref, w_ref, o_ref, h_ref, *, silu_tiles):
    @pl.when(pl.program_id(1) == 0)
    def _():
        rows = min(PROLOGUE_ROWS, x_ref.shape[0])
        gain = nw_ref[...] * (1.0 + scale_ref[0])

        def slab(s, carry):
            sl = pl.ds(pl.multiple_of(s * rows, rows), rows)
            x = x_ref[sl, :]
            rs = lax.rsqrt(jnp.mean(x * x, axis=-1, keepdims=True) + EPS)
            h_ref[sl, :] = (x_ref[sl, :] * rs * gain + shift_ref[0]).astype(BF16)
            return carry

        lax.fori_loop(0, x_ref.shape[0] // rows, slab, 0, unroll=4)

    res = jnp.dot(h_ref[...], w_ref[...], preferred_element_type=F32)
    _store_projection(o_ref, res, pl.program_id(1), silu_tiles)


def _in_projection(x2d, mod3, norm_w, w_bf16, row_of_tile, wcol_of_tile, n_out, tm, tn, silu_tiles):
    m, d = x2d.shape
    return pl.pallas_call(
        functools.partial(_inproj_kernel, silu_tiles=silu_tiles),
        grid=(m // tm, n_out // tn),
        in_specs=[pl.BlockSpec((tm, d), lambda i, j: (i, 0)),
                  pl.BlockSpec((1, 1, d), lambda i, j: (row_of_tile(i), 0, 0)),
                  pl.BlockSpec((1, 1, d), lambda i, j: (row_of_tile(i), 0, 1)),
                  pl.BlockSpec((1, d), lambda i, j: (0, 0)),
                  pl.BlockSpec((d, tn), lambda i, j: (0, wcol_of_tile(j)))],
        out_specs=pl.BlockSpec((tm, tn), lambda i, j: (i, j)),
        out_shape=jax.ShapeDtypeStruct((m, n_out), BF16),
        scratch_shapes=[pltpu.VMEM((tm, d), BF16)],
        compiler_params=pltpu.CompilerParams(
            dimension_semantics=("parallel", "arbitrary"),
            vmem_limit_bytes=VMEM_LIMIT),
        name="in_projection",
    )(x2d, mod3, mod3, norm_w.reshape(1, d), w_bf16)


def _inproj_colmajor_kernel(x_ref, shift_ref, scale_ref, nw_ref, w_ref, o_ref, h_ref, *, slab,
                            silu_tiles):
    _, n_r, n_w, d = x_ref.shape

    rr = 2 * SUBLANES
    lanes = [slice(s * slab, (s + 1) * slab) for s in range(d // slab)]

    @pl.when(pl.program_id(2) == 0)
    def _():
        def row_group(g, carry):
            rows = pl.ds(pl.multiple_of(g * rr, rr), rr)
            sq = jnp.zeros((rr, n_w, slab), F32)
            for sl in lanes:
                xs = x_ref[0, rows, :, sl]
                sq = sq + xs * xs
            rs = lax.rsqrt(jnp.sum(sq, axis=-1, keepdims=True) * (1.0 / d) + EPS)
            for sl in lanes:
                y = x_ref[0, rows, :, sl] * rs * nw_ref[:, sl]
                y = y * (1.0 + scale_ref[0][:, sl]) + shift_ref[0][:, sl]
                h_ref[:, rows, sl] = jnp.transpose(y, (1, 0, 2)).astype(BF16)
            return carry

        lax.fori_loop(0, n_r // rr, row_group, 0)

    res = jnp.dot(h_ref[...].reshape(n_w * n_r, d), w_ref[...], preferred_element_type=F32)
    _store_projection(o_ref, res.reshape(1, n_w, n_r, res.shape[1]), pl.program_id(2), silu_tiles)


def _in_projection_colmajor(x4d, mod3, norm_w, w_bf16, wcol_of_tile, n_out, wb, tn, silu_tiles):
    bsz, n_r, n_w, d = x4d.shape
    kern = functools.partial(_inproj_colmajor_kernel, slab=2 * LANES, silu_tiles=silu_tiles)
    return pl.pallas_call(
        kern,
        grid=(bsz, n_w // wb, n_out // tn),
        in_specs=[pl.BlockSpec((1, n_r, wb, d), lambda b, w, j: (b, 0, w, 0)),
                  pl.BlockSpec((1, 1, d), lambda b, w, j: (b, 0, 0)),
                  pl.BlockSpec((1, 1, d), lambda b, w, j: (b, 0, 1)),
                  pl.BlockSpec((1, d), lambda b, w, j: (0, 0)),
                  pl.BlockSpec((d, tn), lambda b, w, j: (0, wcol_of_tile(j)))],
        out_specs=pl.BlockSpec((1, wb, n_r, tn), lambda b, w, j: (b, w, 0, j)),
        out_shape=jax.ShapeDtypeStruct((bsz, n_w, n_r, n_out), BF16),
        scratch_shapes=[pltpu.VMEM((wb, n_r, d), BF16)],
        compiler_params=pltpu.CompilerParams(
            dimension_semantics=("parallel", "parallel", "arbitrary"),
            vmem_limit_bytes=VMEM_LIMIT),
        name="in_projection_colmajor",
    )(x4d, mod3, mod3, norm_w.reshape(1, d), w_bf16)


def _group_scan(a, b, reverse):
    row = lax.broadcasted_iota(jnp.int32, a.shape, 1)
    for k in (1, 2, 4):
        if reverse:
            a_sh = pltpu.roll(a, SUBLANES - k, axis=1)
            b_sh = pltpu.roll(b, SUBLANES - k, axis=1)
            m = row < SUBLANES - k
        else:
            a_sh = pltpu.roll(a, k, axis=1)
            b_sh = pltpu.roll(b, k, axis=1)
            m = row >= k
        b = jnp.where(m, a * b_sh + b, b)
        a = jnp.where(m, a * a_sh, a)
    return a, b


def _sqrt_unit(x):
    return jnp.where(x > 0.0, x * lax.rsqrt(x), 0.0)


def _lru_block(u, wg, bg, sp, carry, reverse):
    tt = u.shape[0]
    g = jnp.dot(u.astype(BF16), wg, preferred_element_type=F32) + bg
    r = _sigmoid(g[:, :LANES])
    i = _sigmoid(g[:, LANES:])
    a = jnp.exp2(r * sp)
    b = _sqrt_unit(1.0 - a * a) * (i * u)
    groups = tt // SUBLANES
    a3, b3 = _group_scan(a.reshape(groups, SUBLANES, LANES), b.reshape(groups, SUBLANES, LANES),
                         reverse)
    hs = [None] * groups
    order = range(groups - 1, -1, -1) if reverse else range(groups)
    for g_i in order:
        h = b3[g_i] + a3[g_i] * carry
        carry = h[0:1, :] if reverse else h[SUBLANES - 1:SUBLANES, :]
        hs[g_i] = h
    return jnp.concatenate(hs, axis=0), carry


def _rglru_kernel(xa_ref, ga_ref, xc_ref, cw_ref, cb_ref, wgf_ref, bgf_ref, wgb_ref, bgb_ref,
                  lam_ref, o_ref, xf_ref, xcf_ref, u_ref, hf_ref, *, tt):
    t_len = xa_ref.shape[1]
    tc_len = xc_ref.shape[1]
    nblk = xa_ref.shape[2] // LANES
    lanes = [slice(k * LANES, (k + 1) * LANES) for k in range(nblk)]
    zeros = jnp.zeros((SUBLANES, LANES), F32)
    for k, sl in enumerate(lanes):
        xf_ref[k, 0:SUBLANES, :] = zeros
        xf_ref[k, SUBLANES:SUBLANES + t_len, :] = xa_ref[0, :, sl].astype(F32)
        xf_ref[k, SUBLANES + t_len:, :] = zeros
        xcf_ref[k, 0:SUBLANES, :] = zeros
        xcf_ref[k, SUBLANES:SUBLANES + tc_len, :] = xc_ref[0, :, sl].astype(F32)
        xcf_ref[k, SUBLANES + tc_len:, :] = zeros

    sp_f = _softplus(-lam_ref[0:1, :]) * (LRU_C * NEG_LOG2E)
    sp_b = _softplus(-lam_ref[1:2, :]) * (LRU_C * NEG_LOG2E)

    def conv(src_ref, k, t0):
        u = cb_ref[:, lanes[k]]
        for tap in range(cw_ref.shape[0]):
            off = SUBLANES - CONV_PAD_L + tap
            if off % SUBLANES == 0:
                rows = pl.ds(pl.multiple_of(t0 + off, SUBLANES), tt)
            else:
                rows = pl.ds(t0 + off, tt, stride=1)
            u = u + src_ref[k, rows, :] * cw_ref[tap:tap + 1, lanes[k]]
        return u

    def block(u, k, carry, reverse):
        if reverse:
            return _lru_block(u, wgb_ref[k], bgb_ref[k], sp_b[:, lanes[k]], carry, True)
        return _lru_block(u, wgf_ref[k], bgf_ref[k], sp_f[:, lanes[k]], carry, False)

    n_lat = t_len // tt
    n_ctx = tc_len // tt
    carry0 = tuple(jnp.zeros((1, LANES), F32) for _ in lanes)

    def ctx_pass(reverse):
        def body(i, carry):
            t0 = pl.multiple_of((n_ctx - 1 - i if reverse else i) * tt, tt)
            return tuple(block(conv(xcf_ref, k, t0), k, carry[k], reverse)[1] for k in range(nblk))
        return lax.fori_loop(0, n_ctx, body, carry0)

    def lat_f(i, carry):
        t0 = pl.multiple_of(i * tt, tt)
        rows = pl.ds(t0, tt)
        out = []
        for k, sl in enumerate(lanes):
            u = conv(xf_ref, k, t0)
            u_ref[rows, sl] = u
            h, c_out = block(u, k, carry[k], False)
            hf_ref[rows, sl] = h
            out.append(c_out)
        return tuple(out)

    lax.fori_loop(0, n_lat, lat_f, ctx_pass(False), unroll=2)

    def lat_b(i, carry):
        rows = pl.ds(pl.multiple_of((n_lat - 1 - i) * tt, tt), tt)
        out = []
        for k, sl in enumerate(lanes):
            h, c_out = block(u_ref[rows, sl], k, carry[k], True)
            gate = ga_ref[0, rows, sl].astype(F32)
            o_ref[0, rows, sl] = ((hf_ref[rows, sl] + h) * gate).astype(o_ref.dtype)
            out.append(c_out)
        return tuple(out)

    lax.fori_loop(0, n_lat, lat_b, ctx_pass(True), unroll=2)


def _rglru(p_lat, p_ctx, conv_w, conv_b, wg_f, bg_f, wg_b, bg_b, lam, d_a, cb, tt):
    bsz, t_len, _ = p_lat.shape
    tc_len = p_ctx.shape[1]
    nblk = cb // LANES
    ncb = d_a // cb
    kern = functools.partial(_rglru_kernel, tt=tt)
    return pl.pallas_call(
        kern,
        grid=(bsz, ncb),
        in_specs=[pl.BlockSpec((1, t_len, cb), lambda b, c: (b, 0, c)),
                  pl.BlockSpec((1, t_len, cb), lambda b, c: (b, 0, ncb + c)),
                  pl.BlockSpec((1, tc_len, cb), lambda b, c: (b, 0, c)),
                  pl.BlockSpec((conv_w.shape[0], cb), lambda b, c: (0, c)),
                  pl.BlockSpec((1, cb), lambda b, c: (0, c)),
                  pl.BlockSpec((nblk, LANES, 2 * LANES), lambda b, c: (c, 0, 0)),
                  pl.BlockSpec((nblk, 1, 2 * LANES), lambda b, c: (c, 0, 0)),
                  pl.BlockSpec((nblk, LANES, 2 * LANES), lambda b, c: (c, 0, 0)),
                  pl.BlockSpec((nblk, 1, 2 * LANES), lambda b, c: (c, 0, 0)),
                  pl.BlockSpec((2, cb), lambda b, c: (0, c))],
        out_specs=pl.BlockSpec((1, t_len, cb), lambda b, c: (b, 0, c)),
        out_shape=jax.ShapeDtypeStruct((bsz, t_len, d_a), BF16),
        scratch_shapes=[pltpu.VMEM((nblk, t_len + 2 * SUBLANES, LANES), F32),
                        pltpu.VMEM((nblk, tc_len + 2 * SUBLANES, LANES), F32),
                        pltpu.VMEM((t_len, cb), F32),
                        pltpu.VMEM((t_len, cb), F32)],
        compiler_params=pltpu.CompilerParams(
            dimension_semantics=("parallel", "parallel"),
            vmem_limit_bytes=VMEM_LIMIT),
        name="rglru",
    )(p_lat, p_lat, p_ctx, conv_w, conv_b.reshape(1, d_a), wg_f, bg_f, wg_b, bg_b, lam)


_NT = (((1,), (1,)), ((), ()))
_TN = (((0,), (0,)), ((), ()))


class _HgrnDir:
    def __init__(self, d, q_ref, f_ref, v_ref, lb, st_ref, o_ref, scratch, reverse, n_heads, hd):
        self.d, self.q_ref, self.f_ref, self.v_ref, self.lb = d, q_ref, f_ref, v_ref, lb
        self.st_ref, self.o_ref, self.reverse, self.n_heads, self.hd = st_ref, o_ref, reverse, n_heads, hd
        (self.hl_ref, self.kk_ref, self.cum_ref, self.qd_ref, self.kd_ref, self.qin_ref,
         self.kout_ref, self.dec_ref, self.sc_ref) = scratch
        self.c_len = q_ref.shape[2]
        r_i = lax.broadcasted_iota(jnp.int32, (self.c_len, self.c_len), 0)
        c_i = lax.broadcasted_iota(jnp.int32, (self.c_len, self.c_len), 1)
        self.tri = (r_i <= c_i) if reverse else (r_i >= c_i)

    def _slabs(self, width):
        total = self.n_heads * self.hd
        return [slice(s, s + width) for s in range(0, total, width)]

    def gates(self):
        d, c_len = self.d, self.c_len
        for sl in self._slabs(2 * LANES):
            lb = self.lb[:, sl]
            f = lb + (1.0 - lb) * _sigmoid(self.f_ref[0, 0, :, sl].astype(F32))
            logf = jnp.log2(f)
            self.kk_ref[d, :, sl] = 1.0 - f
            hi = logf.astype(BF16)
            self.hl_ref[d, 0:c_len, sl] = hi
            self.hl_ref[d, c_len:2 * c_len, sl] = (logf - hi.astype(F32)).astype(BF16)

    def cumulate(self):
        tri_b = self.tri.astype(BF16)
        tri2 = jnp.concatenate([tri_b, tri_b], axis=1)
        self.cum_ref[self.d] = jnp.dot(tri2, self.hl_ref[self.d], preferred_element_type=F32)

    def decays(self):
        d, c_len = self.d, self.c_len
        half = c_len // 2
        row_last = 0 if self.reverse else c_len - 1
        row_ref = half if self.reverse else half - 1
        for sl in self._slabs(2 * LANES):
            cum = self.cum_ref[d, :, sl]
            last = cum[row_last:row_last + 1, :]
            ref = cum[row_ref:row_ref + 1, :]
            kk = self.kk_ref[d, :, sl]
            qs = self.q_ref[0, 0, :, sl].astype(F32)
            self.qd_ref[d, :, sl] = (qs * jnp.exp2(jnp.minimum(cum - ref, EXP2_CLAMP))).astype(BF16)
            self.kd_ref[d, :, sl] = (kk * jnp.exp2(jnp.minimum(ref - cum, EXP2_CLAMP))).astype(BF16)
            self.qin_ref[d, :, sl] = (qs * jnp.exp2(cum)).astype(BF16)
            self.kout_ref[d, :, sl] = (kk * jnp.exp2(last - cum)).astype(BF16)
            self.dec_ref[d, :, sl] = jnp.exp2(last)

    def _head(self, h):
        return slice(h * self.hd, (h + 1) * self.hd)

    def scores(self):
        d = self.d
        for h in range(self.n_heads):
            sl = self._head(h)
            s = lax.dot_general(self.qd_ref[d, :, sl], self.kd_ref[d, :, sl], _NT,
                                preferred_element_type=F32)
            self.sc_ref[d, h] = jnp.where(self.tri, s, 0.0).astype(BF16)

    def outputs(self):
        d = self.d
        for h in range(self.n_heads):
            sl = self._head(h)
            o = (jnp.dot(self.sc_ref[d, h], self.v_ref[0, 0, :, sl], preferred_element_type=F32)
                 + lax.dot_general(self.qin_ref[d, :, sl], self.st_ref[h].astype(BF16), _NT,
                                   preferred_element_type=F32))
            self.o_ref[0, 0, :, sl] = o.astype(self.o_ref.dtype)

    def update_state(self):
        d = self.d
        for h in range(self.n_heads):
            sl = self._head(h)
            self.st_ref[h] = (self.st_ref[h] * self.dec_ref[d, :, sl]
                              + lax.dot_general(self.v_ref[0, 0, :, sl], self.kout_ref[d, :, sl], _TN,
                                                preferred_element_type=F32))


def _lower_bounds(logits_ref, layer):
    out = []
    for d in range(2):
        rows = [logits_ref[d, l:l + 1, :] for l in range(logits_ref.shape[1])]
        m = functools.reduce(jnp.maximum, rows)
        e = [jnp.exp(r - m) for r in rows]
        out.append(sum(e[:layer + 1]) / sum(e))
    return out


def _hgrn_kernel(*refs, n_heads, hd, has_init, emit_o, emit_state):
    qf_ref, ff_ref, vf_ref, qb_ref, fb_ref, vb_ref, lg_ref = refs[:7]
    pos = 7
    if has_init:
        s0f_ref, s0b_ref = refs[pos:pos + 2]
        pos += 2
    if emit_o:
        of_ref, ob_ref = refs[pos:pos + 2]
        pos += 2
    if emit_state:
        sof_ref, sob_ref = refs[pos:pos + 2]
        pos += 2
    sf_ref, sb_ref = refs[pos:pos + 2]
    scratch = refs[pos + 2:]

    j = pl.program_id(1)

    @pl.when(j == 0)
    def _():
        if has_init:
            sf_ref[...] = s0f_ref[0]
            sb_ref[...] = s0b_ref[0]
        else:
            sf_ref[...] = jnp.zeros_like(sf_ref)
            sb_ref[...] = jnp.zeros_like(sb_ref)

    lb_f, lb_b = _lower_bounds(lg_ref, 0)
    dirs = [_HgrnDir(0, qf_ref, ff_ref, vf_ref, lb_f, sf_ref, of_ref if emit_o else None,
                     scratch, False, n_heads, hd),
            _HgrnDir(1, qb_ref, fb_ref, vb_ref, lb_b, sb_ref, ob_ref if emit_o else None,
                     scratch, True, n_heads, hd)]
    stages = ["gates", "cumulate", "decays"]
    stages += ["scores", "outputs"] if emit_o else []
    stages += ["update_state"]
    for stage in stages:
        for dirn in dirs:
            getattr(dirn, stage)()

    if emit_state:
        @pl.when(j == pl.num_programs(1) - 1)
        def _():
            sof_ref[0] = sf_ref[...]
            sob_ref[0] = sb_ref[...]


def _hgrn(p_view, logits, d_b, n_heads, first_group, init_states, emit_o):
    bsz, n_chunks = p_view.shape[:2]
    hd = d_b // n_heads
    has_init = init_states is not None
    emit_state = not emit_o

    def spec(group, reverse):
        def imap(b, j):
            return (b, n_chunks - 1 - j if reverse else j, 0, first_group + group)
        return pl.BlockSpec((1, 1, CHUNK, d_b), imap)

    in_specs = [spec(0, False), spec(1, False), spec(3, False),
                spec(0, True), spec(2, True), spec(3, True),
                pl.BlockSpec(logits.shape, lambda b, j: (0, 0, 0))]
    args = [p_view] * 6 + [logits]
    state_spec = pl.BlockSpec((1, n_heads, hd, hd), lambda b, j: (b, 0, 0, 0))
    state_shape = jax.ShapeDtypeStruct((bsz, n_heads, hd, hd), F32)
    if has_init:
        in_specs += [state_spec, state_spec]
        args += list(init_states)
    out_specs, out_shape = [], []
    if emit_o:
        o_shape = jax.ShapeDtypeStruct((bsz, n_chunks, CHUNK, d_b), BF16)
        out_specs += [pl.BlockSpec((1, 1, CHUNK, d_b), lambda b, j: (b, j, 0, 0)),
                      pl.BlockSpec((1, 1, CHUNK, d_b), lambda b, j: (b, n_chunks - 1 - j, 0, 0))]
        out_shape += [o_shape, o_shape]
    if emit_state:
        out_specs += [state_spec, state_spec]
        out_shape += [state_shape, state_shape]
    kern = functools.partial(_hgrn_kernel, n_heads=n_heads, hd=hd, has_init=has_init,
                             emit_o=emit_o, emit_state=emit_state)
    return pl.pallas_call(
        kern,
        grid=(bsz, n_chunks),
        in_specs=in_specs,
        out_specs=out_specs,
        out_shape=out_shape,
        scratch_shapes=[pltpu.VMEM((n_heads, hd, hd), F32),
                        pltpu.VMEM((n_heads, hd, hd), F32),
                        pltpu.VMEM((2, 2 * CHUNK, d_b), BF16),
                        pltpu.VMEM((2, CHUNK, d_b), F32),
                        pltpu.VMEM((2, CHUNK, d_b), F32),
                        pltpu.VMEM((2, CHUNK, d_b), BF16),
                        pltpu.VMEM((2, CHUNK, d_b), BF16),
                        pltpu.VMEM((2, CHUNK, d_b), BF16),
                        pltpu.VMEM((2, CHUNK, d_b), BF16),
                        pltpu.VMEM((2, 1, d_b), F32),
                        pltpu.VMEM((2, n_heads, CHUNK, CHUNK), BF16)],
        compiler_params=pltpu.CompilerParams(
            dimension_semantics=("parallel", "arbitrary"),
            vmem_limit_bytes=VMEM_LIMIT),
        name="hgrn2_latent" if emit_o else "hgrn2_context",
    )(*args)


def _outproj_kernel(ya_ref, of_ref, ob_ref, gb_ref, x_ref, gate_ref, hnw_ref, fnw_ref, w_ref,
                    o_ref, y_ref, *, n_heads, hd):
    _, rb, wb, d_a = ya_ref.shape
    rows = rb * wb
    d = x_ref.shape[3]
    y_ref[:, 0:d_a] = ya_ref[0].reshape(rows, d_a)
    hnw = hnw_ref[...]
    for h in range(n_heads):
        sl = slice(h * hd, (h + 1) * hd)
        o = of_ref[0, :, :, sl].astype(F32) + ob_ref[0, :, :, sl].astype(F32)
        ms = jnp.mean(o * o, axis=-1, keepdims=True)
        on = jnp.transpose(o * lax.rsqrt(ms + EPS) * hnw, (1, 0, 2))
        yb = on * gb_ref[0, :, :, sl].astype(F32)
        y_ref[:, d_a + h * hd:d_a + (h + 1) * hd] = yb.reshape(rows, hd).astype(BF16)
    acc = jnp.dot(y_ref[...], w_ref[...], preferred_element_type=F32)
    z = x_ref[0].reshape(rows, d) + gate_ref[0] * acc
    ms = jnp.mean(z * z, axis=-1, keepdims=True)
    o_ref[0] = (z * lax.rsqrt(ms + EPS) * fnw_ref[...]).reshape(rb, wb, d)


def _out_projection(ya4, of4, ob4, pa4, gb_block, x4, mod3, hnw, fnw, w_bf16, n_heads, rb, wb):
    bsz, n_r, n_w, d = x4.shape
    d_a = ya4.shape[3]
    d_b = of4.shape[3]
    hd = d_b // n_heads
    kern = functools.partial(_outproj_kernel, n_heads=n_heads, hd=hd)

    def raster(c, col=0):
        return pl.BlockSpec((1, rb, wb, c), lambda b, r, w: (b, r, w, col))

    def colmajor(c):
        return pl.BlockSpec((1, wb, rb, c), lambda b, r, w: (b, w, r, 0))

    return pl.pallas_call(
        kern,
        grid=(bsz, n_r // rb, n_w // wb),
        in_specs=[raster(d_a), colmajor(d_b), colmajor(d_b), raster(d_b, gb_block), raster(d),
                  pl.BlockSpec((1, 1, d), lambda b, r, w: (b, 0, 2)),
                  pl.BlockSpec((1, hd), lambda b, r, w: (0, 0)),
                  pl.BlockSpec((1, d), lambda b, r, w: (0, 0)),
                  pl.BlockSpec((d_a + d_b, d), lambda b, r, w: (0, 0),
                               pipeline_mode=pl.Buffered(1))],
        out_specs=raster(d),
        out_shape=jax.ShapeDtypeStruct((bsz, n_r, n_w, d), F32),
        scratch_shapes=[pltpu.VMEM((rb * wb, d_a + d_b), BF16)],
        compiler_params=pltpu.CompilerParams(
            dimension_semantics=("parallel", "parallel", "parallel"),
            vmem_limit_bytes=VMEM_LIMIT),
        name="out_projection",
    )(ya4, of4, ob4, pa4, x4, mod3, hnw.reshape(1, hd), fnw.reshape(1, d), w_bf16)


def kernel(x, c, ctx, c_ctx, ada_w, ada_b, norm_w, w_in, conv_w, conv_b, lru_wr, lru_br, lru_wi,
           lru_bi, lru_lambda, hgrn_lb_logits, hgrn_norm_w, w_out, final_norm_w):
    bsz, t_len, d = x.shape
    tc_len = ctx.shape[1]
    assert ada_w.shape[0] == 1, "single-layer stack only"
    d_a = conv_w.shape[2]
    d_b = hgrn_lb_logits.shape[2]
    hd = hgrn_norm_w.shape[1]
    n_heads = d_b // hd
    n_blocks_a = lru_wr.shape[2]
    n_cols = w_in.shape[2]
    assert t_len == GRID_W * CHUNK and tc_len % CHUNK == 0
    assert d_a // n_blocks_a == LANES and hd == LANES
    assert (2 * d_a) % d_b == 0 and n_cols == 2 * d_a + 5 * d_b
    first_b_group = (2 * d_a) // d_b
    n_rows_grid = t_len // GRID_W

    n_rows = -(-(bsz + 1) // SUBLANES) * SUBLANES
    cc = jnp.zeros((n_rows, d), F32).at[:bsz].set(c).at[bsz].set(c_ctx)
    mod3 = _modulation(cc, ada_w[0], ada_b[0]).reshape(n_rows, 1, 3 * d)

    w_in_b = w_in[0].astype(BF16)
    w_out_b = w_out[0].astype(BF16)

    tm = 1024 if t_len % 1024 == 0 else t_len
    tn = 1024 if d_a % 1024 == 0 and d_b % 1024 == 0 else min(d_a, d_b)
    tpb = t_len // tm
    a_tiles = (2 * d_a) // tn
    b_tiles = (4 * d_b) // tn
    q_tiles = d_b // tn
    p_a = _in_projection(x.reshape(bsz * t_len, d), mod3, norm_w[0], w_in_b,
                         lambda i: i // tpb, lambda j: jnp.where(j < a_tiles, j, j + b_tiles),
                         2 * d_a + d_b, tm, tn, (d_a // tn, (2 * d_a + d_b) // tn))
    wb = 16
    p_b = _in_projection_colmajor(x.reshape(bsz, n_rows_grid, GRID_W, d), mod3, norm_w[0], w_in_b,
                                  lambda j: j + a_tiles, 4 * d_b, wb, tn, (0, q_tiles))
    p_ctx = _in_projection(ctx.reshape(bsz * tc_len, d), mod3, norm_w[0], w_in_b,
                           lambda i: bsz, lambda j: j, n_cols, bsz * tc_len, tn,
                           (a_tiles, a_tiles + q_tiles))
    p_lat = p_a.reshape(bsz, t_len, 2 * d_a + d_b)
    p_ctx = p_ctx.reshape(bsz, tc_len, n_cols)

    def gate_w(dirn):
        return jnp.concatenate([lru_wr[0, dirn], lru_wi[0, dirn]], axis=-1).astype(BF16)

    def gate_b(dirn):
        return jnp.concatenate([lru_br[0, dirn].reshape(n_blocks_a, 1, LANES),
                                lru_bi[0, dirn].reshape(n_blocks_a, 1, LANES)], axis=-1)

    cb = 256 if d_a % 256 == 0 else LANES
    ya = _rglru(p_lat, p_ctx, conv_w[0], conv_b[0], gate_w(0), gate_b(0), gate_w(1), gate_b(1),
                lru_lambda[0], d_a, cb, tt=128)

    states = _hgrn(p_ctx.reshape(bsz, tc_len // CHUNK, CHUNK, n_cols), hgrn_lb_logits, d_b,
                   n_heads, first_b_group, None, False)
    o_f, o_b = _hgrn(p_b, hgrn_lb_logits, d_b, n_heads, 0, states, True)

    grid4 = lambda z: z.reshape(bsz, n_rows_grid, GRID_W, z.shape[-1])
    out = _out_projection(grid4(ya), o_f, o_b, grid4(p_lat), first_b_group, grid4(x), mod3,
                          hgrn_norm_w[0], final_norm_w, w_out_b, n_heads, 16, 16)
    return out.reshape(bsz, t_len, d)
```

```python
import functools

import jax
import jax.numpy as jnp
from jax import lax
from jax.experimental import pallas as pl
from jax.experimental.pallas import tpu as pltpu

GRID_W = 64
CHUNK = 64
LRU_C = 8.0
EPS = 1e-6
CONV_PAD_L = 2
LANES = 128
SUBLANES = 8
EXP2_CLAMP = 115.0
NEG_LOG2E = -1.4426950408889634
VMEM_LIMIT = 56 * 1024 * 1024
PROLOGUE_ROWS = 16

F32 = jnp.float32
BF16 = jnp.bfloat16


def _sigmoid(z):
    return 1.0 / (1.0 + jnp.exp2(z * NEG_LOG2E))


def _silu(z):
    return z * _sigmoid(z)


def _softplus(z):
    return jnp.maximum(z, 0.0) + jnp.log1p(jnp.exp(-jnp.abs(z)))


def _mod_kernel(c_ref, w_ref, b_ref, o_ref):
    s = _silu(c_ref[...])
    o_ref[...] = jnp.dot(s.astype(BF16), w_ref[...].astype(BF16),
                         preferred_element_type=F32) + b_ref[...]


def _modulation(cc, w, b):
    rows, d = cc.shape
    n = w.shape[1]
    tn = 512 if n % 512 == 0 else n
    return pl.pallas_call(
        _mod_kernel,
        grid=(n // tn,),
        in_specs=[pl.BlockSpec((rows, d), lambda j: (0, 0)),
                  pl.BlockSpec((d, tn), lambda j: (0, j)),
                  pl.BlockSpec((1, tn), lambda j: (0, j))],
        out_specs=pl.BlockSpec((rows, tn), lambda j: (0, j)),
        out_shape=jax.ShapeDtypeStruct((rows, n), F32),
        compiler_params=pltpu.CompilerParams(
            dimension_semantics=("arbitrary",), vmem_limit_bytes=VMEM_LIMIT),
        name="adaln_modulation",
    )(cc, w, b.reshape(1, n))


def _store_projection(o_ref, res, j, silu_tiles):
    lo, hi = silu_tiles
    if hi <= lo:
        o_ref[...] = res.astype(o_ref.dtype)
        return
    in_range = jnp.logical_and(j >= lo, j < hi)
    o_ref[...] = jnp.where(in_range, _silu(res), res).astype(o_ref.dtype)


def _inproj_kernel(x_ref, shift_ref, scale_ref, nw_ref, w_ref, o_ref, h_ref, *, silu_tiles):
    @pl.when(pl.program_id(1) == 0)
    def _():
        rows = min(PROLOGUE_ROWS, x_ref.shape[0])
        gain = nw_ref[...] * (1.0 + scale_ref[0])

        def slab(s, carry):
            sl = pl.ds(pl.multiple_of(s * rows, rows), rows)
            x = x_ref[sl, :]
            rs = lax.rsqrt(jnp.mean(x * x, axis=-1, keepdims=True) + EPS)
            h_ref[sl, :] = (x_ref[sl, :] * rs * gain + shift_ref[0]).astype(BF16)
            return carry

        lax.fori_loop(0, x_ref.shape[0] // rows, slab, 0, unroll=4)

    res = jnp.dot(h_ref[...], w_ref[...].astype(BF16), preferred_element_type=F32)
    _store_projection(o_ref, res, pl.program_id(1), silu_tiles)


def _in_projection(x2d, mod3, norm_w, w_bf16, row_of_tile, wcol_of_tile, n_out, tm, tn, silu_tiles):
    m, d = x2d.shape
    return pl.pallas_call(
        functools.partial(_inproj_kernel, silu_tiles=silu_tiles),
        grid=(m // tm, n_out // tn),
        in_specs=[pl.BlockSpec((tm, d), lambda i, j: (i, 0)),
                  pl.BlockSpec((1, 1, d), lambda i, j: (row_of_tile(i), 0, 0)),
                  pl.BlockSpec((1, 1, d), lambda i, j: (row_of_tile(i), 0, 1)),
                  pl.BlockSpec((1, d), lambda i, j: (0, 0)),
                  pl.BlockSpec((d, tn), lambda i, j: (0, wcol_of_tile(j)))],
        out_specs=pl.BlockSpec((tm, tn), lambda i, j: (i, j)),
        out_shape=jax.ShapeDtypeStruct((m, n_out), BF16),
        scratch_shapes=[pltpu.VMEM((tm, d), BF16)],
        compiler_params=pltpu.CompilerParams(
            dimension_semantics=("parallel", "arbitrary"),
            vmem_limit_bytes=VMEM_LIMIT),
        name="in_projection",
    )(x2d, mod3, mod3, norm_w.reshape(1, d), w_bf16)


def _inproj_colmajor_kernel(x_ref, shift_ref, scale_ref, nw_ref, w_ref, o_ref, h_ref, *, slab,
                            silu_tiles):
    _, n_r, n_w, d = x_ref.shape

    rr = 2 * SUBLANES
    lanes = [slice(s * slab, (s + 1) * slab) for s in range(d // slab)]

    @pl.when(pl.program_id(2) == 0)
    def _():
        def row_group(g, carry):
            rows = pl.ds(pl.multiple_of(g * rr, rr), rr)
            sq = jnp.zeros((rr, n_w, slab), F32)
            for sl in lanes:
                xs = x_ref[0, rows, :, sl]
                sq = sq + xs * xs
            rs = lax.rsqrt(jnp.sum(sq, axis=-1, keepdims=True) * (1.0 / d) + EPS)
            for sl in lanes:
                y = x_ref[0, rows, :, sl] * rs * nw_ref[:, sl]
                y = y * (1.0 + scale_ref[0][:, sl]) + shift_ref[0][:, sl]
                h_ref[:, rows, sl] = jnp.transpose(y, (1, 0, 2)).astype(BF16)
            return carry

        lax.fori_loop(0, n_r // rr, row_group, 0)

    res = jnp.dot(h_ref[...].reshape(n_w * n_r, d), w_ref[...].astype(BF16),
                  preferred_element_type=F32)
    _store_projection(o_ref, res.reshape(1, n_w, n_r, res.shape[1]), pl.program_id(2), silu_tiles)


def _in_projection_colmajor(x4d, mod3, norm_w, w_bf16, wcol_of_tile, n_out, wb, tn, silu_tiles):
    bsz, n_r, n_w, d = x4d.shape
    kern = functools.partial(_inproj_colmajor_kernel, slab=2 * LANES, silu_tiles=silu_tiles)
    return pl.pallas_call(
        kern,
        grid=(bsz, n_w // wb, n_out // tn),
        in_specs=[pl.BlockSpec((1, n_r, wb, d), lambda b, w, j: (b, 0, w, 0)),
                  pl.BlockSpec((1, 1, d), lambda b, w, j: (b, 0, 0)),
                  pl.BlockSpec((1, 1, d), lambda b, w, j: (b, 0, 1)),
                  pl.BlockSpec((1, d), lambda b, w, j: (0, 0)),
                  pl.BlockSpec((d, tn), lambda b, w, j: (0, wcol_of_tile(j)))],
        out_specs=pl.BlockSpec((1, wb, n_r, tn), lambda b, w, j: (b, w, 0, j)),
        out_shape=jax.ShapeDtypeStruct((bsz, n_w, n_r, n_out), BF16),
        scratch_shapes=[pltpu.VMEM((wb, n_r, d), BF16)],
        compiler_params=pltpu.CompilerParams(
            dimension_semantics=("parallel", "parallel", "arbitrary"),
            vmem_limit_bytes=VMEM_LIMIT),
        name="in_projection_colmajor",
    )(x4d, mod3, mod3, norm_w.reshape(1, d), w_bf16)


def _group_scan(a, b, reverse):
    row = lax.broadcasted_iota(jnp.int32, a.shape, 1)
    for k in (1, 2, 4):
        if reverse:
            a_sh = pltpu.roll(a, SUBLANES - k, axis=1)
            b_sh = pltpu.roll(b, SUBLANES - k, axis=1)
            m = row < SUBLANES - k
        else:
            a_sh = pltpu.roll(a, k, axis=1)
            b_sh = pltpu.roll(b, k, axis=1)
            m = row >= k
        b = jnp.where(m, a * b_sh + b, b)
        a = jnp.where(m, a * a_sh, a)
    return a, b


def _sqrt_unit(x):
    return jnp.where(x > 0.0, x * lax.rsqrt(x), 0.0)


def _lru_block(u, wg, bg, sp, carry, reverse):
    tt = u.shape[0]
    g = jnp.dot(u.astype(BF16), wg, preferred_element_type=F32) + bg
    r = _sigmoid(g[:, :LANES])
    i = _sigmoid(g[:, LANES:])
    a = jnp.exp2(r * sp)
    b = _sqrt_unit(1.0 - a * a) * (i * u)
    groups = tt // SUBLANES
    a3, b3 = _group_scan(a.reshape(groups, SUBLANES, LANES), b.reshape(groups, SUBLANES, LANES),
                         reverse)
    hs = [None] * groups
    order = range(groups - 1, -1, -1) if reverse else range(groups)
    for g_i in order:
        h = b3[g_i] + a3[g_i] * carry
        carry = h[0:1, :] if reverse else h[SUBLANES - 1:SUBLANES, :]
        hs[g_i] = h
    return jnp.concatenate(hs, axis=0), carry


def _rglru_kernel(xa_ref, ga_ref, xc_ref, cw_ref, cb_ref, wgf_ref, bgf_ref, wgb_ref, bgb_ref,
                  lam_ref, o_ref, xf_ref, xcf_ref, u_ref, hf_ref, *, tt):
    t_len = xa_ref.shape[1]
    tc_len = xc_ref.shape[1]
    nblk = xa_ref.shape[2] // LANES
    lanes = [slice(k * LANES, (k + 1) * LANES) for k in range(nblk)]
    zeros = jnp.zeros((SUBLANES, LANES), F32)
    for k, sl in enumerate(lanes):
        xf_ref[k, 0:SUBLANES, :] = zeros
        xf_ref[k, SUBLANES:SUBLANES + t_len, :] = xa_ref[0, :, sl].astype(F32)
        xf_ref[k, SUBLANES + t_len:, :] = zeros
        xcf_ref[k, 0:SUBLANES, :] = zeros
        xcf_ref[k, SUBLANES:SUBLANES + tc_len, :] = xc_ref[0, :, sl].astype(F32)
        xcf_ref[k, SUBLANES + tc_len:, :] = zeros

    sp_f = _softplus(-lam_ref[0:1, :]) * (LRU_C * NEG_LOG2E)
    sp_b = _softplus(-lam_ref[1:2, :]) * (LRU_C * NEG_LOG2E)

    def conv(src_ref, k, t0):
        u = cb_ref[:, lanes[k]]
        for tap in range(cw_ref.shape[0]):
            off = SUBLANES - CONV_PAD_L + tap
            if off % SUBLANES == 0:
                rows = pl.ds(pl.multiple_of(t0 + off, SUBLANES), tt)
            else:
                rows = pl.ds(t0 + off, tt, stride=1)
            u = u + src_ref[k, rows, :] * cw_ref[tap:tap + 1, lanes[k]]
        return u

    def block(u, k, carry, reverse):
        if reverse:
            return _lru_block(u, wgb_ref[k], bgb_ref[k], sp_b[:, lanes[k]], carry, True)
        return _lru_block(u, wgf_ref[k], bgf_ref[k], sp_f[:, lanes[k]], carry, False)

    n_lat = t_len // tt
    n_ctx = tc_len // tt
    carry0 = tuple(jnp.zeros((1, LANES), F32) for _ in lanes)

    def ctx_pass(reverse):
        def body(i, carry):
            t0 = pl.multiple_of((n_ctx - 1 - i if reverse else i) * tt, tt)
            return tuple(block(conv(xcf_ref, k, t0), k, carry[k], reverse)[1] for k in range(nblk))
        return lax.fori_loop(0, n_ctx, body, carry0)

    def lat_f(i, carry):
        t0 = pl.multiple_of(i * tt, tt)
        rows = pl.ds(t0, tt)
        out = []
        for k, sl in enumerate(lanes):
            u = conv(xf_ref, k, t0)
            u_ref[rows, sl] = u
            h, c_out = block(u, k, carry[k], False)
            hf_ref[rows, sl] = h
            out.append(c_out)
        return tuple(out)

    lax.fori_loop(0, n_lat, lat_f, ctx_pass(False), unroll=2)

    def lat_b(i, carry):
        rows = pl.ds(pl.multiple_of((n_lat - 1 - i) * tt, tt), tt)
        out = []
        for k, sl in enumerate(lanes):
            h, c_out = block(u_ref[rows, sl], k, carry[k], True)
            gate = ga_ref[0, rows, sl].astype(F32)
            o_ref[0, rows, sl] = ((hf_ref[rows, sl] + h) * gate).astype(o_ref.dtype)
            out.append(c_out)
        return tuple(out)

    lax.fori_loop(0, n_lat, lat_b, ctx_pass(True), unroll=2)


def _rglru(p_lat, p_ctx, conv_w, conv_b, wg_f, bg_f, wg_b, bg_b, lam, d_a, cb, tt):
    bsz, t_len, _ = p_lat.shape
    tc_len = p_ctx.shape[1]
    nblk = cb // LANES
    ncb = d_a // cb
    kern = functools.partial(_rglru_kernel, tt=tt)
    return pl.pallas_call(
        kern,
        grid=(bsz, ncb),
        in_specs=[pl.BlockSpec((1, t_len, cb), lambda b, c: (b, 0, c)),
                  pl.BlockSpec((1, t_len, cb), lambda b, c: (b, 0, ncb + c)),
                  pl.BlockSpec((1, tc_len, cb), lambda b, c: (b, 0, c)),
                  pl.BlockSpec((conv_w.shape[0], cb), lambda b, c: (0, c)),
                  pl.BlockSpec((1, cb), lambda b, c: (0, c)),
                  pl.BlockSpec((nblk, LANES, 2 * LANES), lambda b, c: (c, 0, 0)),
                  pl.BlockSpec((nblk, 1, 2 * LANES), lambda b, c: (c, 0, 0)),
                  pl.BlockSpec((nblk, LANES, 2 * LANES), lambda b, c: (c, 0, 0)),
                  pl.BlockSpec((nblk, 1, 2 * LANES), lambda b, c: (c, 0, 0)),
                  pl.BlockSpec((2, cb), lambda b, c: (0, c))],
        out_specs=pl.BlockSpec((1, t_len, cb), lambda b, c: (b, 0, c)),
        out_shape=jax.ShapeDtypeStruct((bsz, t_len, d_a), BF16),
        scratch_shapes=[pltpu.VMEM((nblk, t_len + 2 * SUBLANES, LANES), F32),
                        pltpu.VMEM((nblk, tc_len + 2 * SUBLANES, LANES), F32),
                        pltpu.VMEM((t_len, cb), F32),
                        pltpu.VMEM((t_len, cb), F32)],
        compiler_params=pltpu.CompilerParams(
            dimension_semantics=("parallel", "parallel"),
            vmem_limit_bytes=VMEM_LIMIT),
        name="rglru",
    )(p_lat, p_lat, p_ctx, conv_w, conv_b.reshape(1, d_a), wg_f, bg_f, wg_b, bg_b, lam)


_NT = (((1,), (1,)), ((), ()))
_TN = (((0,), (0,)), ((), ()))


class _HgrnDir:
    def __init__(self, d, q_ref, f_ref, v_ref, lb, st_ref, o_ref, scratch, reverse, n_heads, hd):
        self.d, self.q_ref, self.f_ref, self.v_ref, self.lb = d, q_ref, f_ref, v_ref, lb
        self.st_ref, self.o_ref, self.reverse, self.n_heads, self.hd = st_ref, o_ref, reverse, n_heads, hd
        (self.hl_ref, self.kk_ref, self.cum_ref, self.qd_ref, self.kd_ref, self.qin_ref,
         self.kout_ref, self.dec_ref, self.sc_ref) = scratch
        self.c_len = q_ref.shape[2]
        r_i = lax.broadcasted_iota(jnp.int32, (self.c_len, self.c_len), 0)
        c_i = lax.broadcasted_iota(jnp.int32, (self.c_len, self.c_len), 1)
        self.tri = (r_i <= c_i) if reverse else (r_i >= c_i)

    def _slabs(self, width):
        total = self.n_heads * self.hd
        return [slice(s, s + width) for s in range(0, total, width)]

    def gates(self):
        d, c_len = self.d, self.c_len
        for sl in self._slabs(2 * LANES):
            lb = self.lb[:, sl]
            f = lb + (1.0 - lb) * _sigmoid(self.f_ref[0, 0, :, sl].astype(F32))
            logf = jnp.log2(f)
            self.kk_ref[d, :, sl] = 1.0 - f
            hi = logf.astype(BF16)
            self.hl_ref[d, 0:c_len, sl] = hi
            self.hl_ref[d, c_len:2 * c_len, sl] = (logf - hi.astype(F32)).astype(BF16)

    def cumulate(self):
        tri_b = self.tri.astype(BF16)
        tri2 = jnp.concatenate([tri_b, tri_b], axis=1)
        self.cum_ref[self.d] = jnp.dot(tri2, self.hl_ref[self.d], preferred_element_type=F32)

    def decays(self):
        d, c_len = self.d, self.c_len
        half = c_len // 2
        row_last = 0 if self.reverse else c_len - 1
        row_ref = half if self.reverse else half - 1
        for sl in self._slabs(2 * LANES):
            cum = self.cum_ref[d, :, sl]
            last = cum[row_last:row_last + 1, :]
            ref = cum[row_ref:row_ref + 1, :]
            kk = self.kk_ref[d, :, sl]
            qs = self.q_ref[0, 0, :, sl].astype(F32)
            self.qd_ref[d, :, sl] = (qs * jnp.exp2(jnp.minimum(cum - ref, EXP2_CLAMP))).astype(BF16)
            self.kd_ref[d, :, sl] = (kk * jnp.exp2(jnp.minimum(ref - cum, EXP2_CLAMP))).astype(BF16)
            self.qin_ref[d, :, sl] = (qs * jnp.exp2(cum)).astype(BF16)
            self.kout_ref[d, :, sl] = (kk * jnp.exp2(last - cum)).astype(BF16)
            self.dec_ref[d, :, sl] = jnp.exp2(last)

    def _head(self, h):
        return slice(h * self.hd, (h + 1) * self.hd)

    def scores(self):
        d = self.d
        for h in range(self.n_heads):
            sl = self._head(h)
            s = lax.dot_general(self.qd_ref[d, :, sl], self.kd_ref[d, :, sl], _NT,
                                preferred_element_type=F32)
            self.sc_ref[d, h] = jnp.where(self.tri, s, 0.0).astype(BF16)

    def outputs(self):
        d = self.d
        for h in range(self.n_heads):
            sl = self._head(h)
            o = (jnp.dot(self.sc_ref[d, h], self.v_ref[0, 0, :, sl], preferred_element_type=F32)
                 + lax.dot_general(self.qin_ref[d, :, sl], self.st_ref[h].astype(BF16), _NT,
                                   preferred_element_type=F32))
            self.o_ref[0, 0, :, sl] = o.astype(self.o_ref.dtype)

    def update_state(self):
        d = self.d
        for h in range(self.n_heads):
            sl = self._head(h)
            self.st_ref[h] = (self.st_ref[h] * self.dec_ref[d, :, sl]
                              + lax.dot_general(self.v_ref[0, 0, :, sl], self.kout_ref[d, :, sl], _TN,
                                                preferred_element_type=F32))


def _lower_bounds(logits_ref, layer):
    out = []
    for d in range(2):
        rows = [logits_ref[d, l:l + 1, :] for l in range(logits_ref.shape[1])]
        m = functools.reduce(jnp.maximum, rows)
        e = [jnp.exp(r - m) for r in rows]
        out.append(sum(e[:layer + 1]) / sum(e))
    return out


def _hgrn_kernel(*refs, n_heads, hd, has_init, emit_o, emit_state):
    qf_ref, ff_ref, vf_ref, qb_ref, fb_ref, vb_ref, lg_ref = refs[:7]
    pos = 7
    if has_init:
        s0f_ref, s0b_ref = refs[pos:pos + 2]
        pos += 2
    if emit_o:
        of_ref, ob_ref = refs[pos:pos + 2]
        pos += 2
    if emit_state:
        sof_ref, sob_ref = refs[pos:pos + 2]
        pos += 2
    sf_ref, sb_ref = refs[pos:pos + 2]
    scratch = refs[pos + 2:]

    j = pl.program_id(1)

    @pl.when(j == 0)
    def _():
        if has_init:
            sf_ref[...] = s0f_ref[0]
            sb_ref[...] = s0b_ref[0]
        else:
            sf_ref[...] = jnp.zeros_like(sf_ref)
            sb_ref[...] = jnp.zeros_like(sb_ref)

    lb_f, lb_b = _lower_bounds(lg_ref, 0)
    dirs = [_HgrnDir(0, qf_ref, ff_ref, vf_ref, lb_f, sf_ref, of_ref if emit_o else None,
                     scratch, False, n_heads, hd),
            _HgrnDir(1, qb_ref, fb_ref, vb_ref, lb_b, sb_ref, ob_ref if emit_o else None,
                     scratch, True, n_heads, hd)]
    stages = ["gates", "cumulate", "decays"]
    stages += ["scores", "outputs"] if emit_o else []
    stages += ["update_state"]
    for stage in stages:
        for dirn in dirs:
            getattr(dirn, stage)()

    if emit_state:
        @pl.when(j == pl.num_programs(1) - 1)
        def _():
            sof_ref[0] = sf_ref[...]
            sob_ref[0] = sb_ref[...]


def _hgrn(p_view, logits, d_b, n_heads, first_group, init_states, emit_o):
    bsz, n_chunks = p_view.shape[:2]
    hd = d_b // n_heads
    has_init = init_states is not None
    emit_state = not emit_o

    def spec(group, reverse):
        def imap(b, j):
            return (b, n_chunks - 1 - j if reverse else j, 0, first_group + group)
        return pl.BlockSpec((1, 1, CHUNK, d_b), imap)

    in_specs = [spec(0, False), spec(1, False), spec(3, False),
                spec(0, True), spec(2, True), spec(3, True),
                pl.BlockSpec(logits.shape, lambda b, j: (0, 0, 0))]
    args = [p_view] * 6 + [logits]
    state_spec = pl.BlockSpec((1, n_heads, hd, hd), lambda b, j: (b, 0, 0, 0))
    state_shape = jax.ShapeDtypeStruct((bsz, n_heads, hd, hd), F32)
    if has_init:
        in_specs += [state_spec, state_spec]
        args += list(init_states)
    out_specs, out_shape = [], []
    if emit_o:
        o_shape = jax.ShapeDtypeStruct((bsz, n_chunks, CHUNK, d_b), BF16)
        out_specs += [pl.BlockSpec((1, 1, CHUNK, d_b), lambda b, j: (b, j, 0, 0)),
                      pl.BlockSpec((1, 1, CHUNK, d_b), lambda b, j: (b, n_chunks - 1 - j, 0, 0))]
        out_shape += [o_shape, o_shape]
    if emit_state:
        out_specs += [state_spec, state_spec]
        out_shape += [state_shape, state_shape]
    kern = functools.partial(_hgrn_kernel, n_heads=n_heads, hd=hd, has_init=has_init,
                             emit_o=emit_o, emit_state=emit_state)
    return pl.pallas_call(
        kern,
        grid=(bsz, n_chunks),
        in_specs=in_specs,
        out_specs=out_specs,
        out_shape=out_shape,
        scratch_shapes=[pltpu.VMEM((n_heads, hd, hd), F32),
                        pltpu.VMEM((n_heads, hd, hd), F32),
                        pltpu.VMEM((2, 2 * CHUNK, d_b), BF16),
                        pltpu.VMEM((2, CHUNK, d_b), F32),
                        pltpu.VMEM((2, CHUNK, d_b), F32),
                        pltpu.VMEM((2, CHUNK, d_b), BF16),
                        pltpu.VMEM((2, CHUNK, d_b), BF16),
                        pltpu.VMEM((2, CHUNK, d_b), BF16),
                        pltpu.VMEM((2, CHUNK, d_b), BF16),
                        pltpu.VMEM((2, 1, d_b), F32),
                        pltpu.VMEM((2, n_heads, CHUNK, CHUNK), BF16)],
        compiler_params=pltpu.CompilerParams(
            dimension_semantics=("parallel", "arbitrary"),
            vmem_limit_bytes=VMEM_LIMIT),
        name="hgrn2_latent" if emit_o else "hgrn2_context",
    )(*args)


def _outproj_kernel(ya_ref, of_ref, ob_ref, gb_ref, x_ref, gate_ref, hnw_ref, fnw_ref, w_ref,
                    o_ref, y_ref, *, n_heads, hd):
    _, rb, wb, d_a = ya_ref.shape
    rows = rb * wb
    d = x_ref.shape[3]
    y_ref[:, 0:d_a] = ya_ref[0].reshape(rows, d_a)
    hnw = hnw_ref[...]
    for h in range(n_heads):
        sl = slice(h * hd, (h + 1) * hd)
        o = of_ref[0, :, :, sl].astype(F32) + ob_ref[0, :, :, sl].astype(F32)
        ms = jnp.mean(o * o, axis=-1, keepdims=True)
        on = jnp.transpose(o * lax.rsqrt(ms + EPS) * hnw, (1, 0, 2))
        yb = on * gb_ref[0, :, :, sl].astype(F32)
        y_ref[:, d_a + h * hd:d_a + (h + 1) * hd] = yb.reshape(rows, hd).astype(BF16)
    acc = jnp.dot(y_ref[...], w_ref[...], preferred_element_type=F32)
    z = x_ref[0].reshape(rows, d) + gate_ref[0] * acc
    ms = jnp.mean(z * z, axis=-1, keepdims=True)
    o_ref[0] = (z * lax.rsqrt(ms + EPS) * fnw_ref[...]).reshape(rb, wb, d)


def _out_projection(ya4, of4, ob4, pa4, gb_block, x4, mod3, hnw, fnw, w_bf16, n_heads, rb, wb):
    bsz, n_r, n_w, d = x4.shape
    d_a = ya4.shape[3]
    d_b = of4.shape[3]
    hd = d_b // n_heads
    kern = functools.partial(_outproj_kernel, n_heads=n_heads, hd=hd)

    def raster(c, col=0):
        return pl.BlockSpec((1, rb, wb, c), lambda b, r, w: (b, r, w, col))

    def colmajor(c):
        return pl.BlockSpec((1, wb, rb, c), lambda b, r, w: (b, w, r, 0))

    return pl.pallas_call(
        kern,
        grid=(bsz, n_r // rb, n_w // wb),
        in_specs=[raster(d_a), colmajor(d_b), colmajor(d_b), raster(d_b, gb_block), raster(d),
                  pl.BlockSpec((1, 1, d), lambda b, r, w: (b, 0, 2)),
                  pl.BlockSpec((1, hd), lambda b, r, w: (0, 0)),
                  pl.BlockSpec((1, d), lambda b, r, w: (0, 0)),
                  pl.BlockSpec((d_a + d_b, d), lambda b, r, w: (0, 0),
                               pipeline_mode=pl.Buffered(1))],
        out_specs=raster(d),
        out_shape=jax.ShapeDtypeStruct((bsz, n_r, n_w, d), F32),
        scratch_shapes=[pltpu.VMEM((rb * wb, d_a + d_b), BF16)],
        compiler_params=pltpu.CompilerParams(
            dimension_semantics=("parallel", "parallel", "parallel"),
            vmem_limit_bytes=VMEM_LIMIT),
        name="out_projection",
    )(ya4, of4, ob4, pa4, x4, mod3, hnw.reshape(1, hd), fnw.reshape(1, d), w_bf16)


def kernel(x, c, ctx, c_ctx, ada_w, ada_b, norm_w, w_in, conv_w, conv_b, lru_wr, lru_br, lru_wi,
           lru_bi, lru_lambda, hgrn_lb_logits, hgrn_norm_w, w_out, final_norm_w):
    bsz, t_len, d = x.shape
    tc_len = ctx.shape[1]
    assert ada_w.shape[0] == 1, "single-layer stack only"
    d_a = conv_w.shape[2]
    d_b = hgrn_lb_logits.shape[2]
    hd = hgrn_norm_w.shape[1]
    n_heads = d_b // hd
    n_blocks_a = lru_wr.shape[2]
    n_cols = w_in.shape[2]
    assert t_len == GRID_W * CHUNK and tc_len % CHUNK == 0
    assert d_a // n_blocks_a == LANES and hd == LANES
    assert (2 * d_a) % d_b == 0 and n_cols == 2 * d_a + 5 * d_b
    first_b_group = (2 * d_a) // d_b
    n_rows_grid = t_len // GRID_W

    n_rows = -(-(bsz + 1) // SUBLANES) * SUBLANES
    cc = jnp.zeros((n_rows, d), F32).at[:bsz].set(c).at[bsz].set(c_ctx)
    mod3 = _modulation(cc, ada_w[0], ada_b[0]).reshape(n_rows, 1, 3 * d)

    w_in_b = w_in[0]
    w_out_b = w_out[0].astype(BF16)

    tm = 1024 if t_len % 1024 == 0 else t_len
    tn = 1024 if d_a % 1024 == 0 and d_b % 1024 == 0 else min(d_a, d_b)
    tpb = t_len // tm
    a_tiles = (2 * d_a) // tn
    b_tiles = (4 * d_b) // tn
    q_tiles = d_b // tn
    p_a = _in_projection(x.reshape(bsz * t_len, d), mod3, norm_w[0], w_in_b,
                         lambda i: i // tpb, lambda j: jnp.where(j < a_tiles, j, j + b_tiles),
                         2 * d_a + d_b, tm, tn, (d_a // tn, (2 * d_a + d_b) // tn))
    wb = 16
    p_b = _in_projection_colmajor(x.reshape(bsz, n_rows_grid, GRID_W, d), mod3, norm_w[0], w_in_b,
                                  lambda j: j + a_tiles, 4 * d_b, wb, tn, (0, q_tiles))
    p_ctx = _in_projection(ctx.reshape(bsz * tc_len, d), mod3, norm_w[0], w_in_b,
                           lambda i: bsz, lambda j: j, n_cols, bsz * tc_len, tn,
                           (a_tiles, a_tiles + q_tiles))
    p_lat = p_a.reshape(bsz, t_len, 2 * d_a + d_b)
    p_ctx = p_ctx.reshape(bsz, tc_len, n_cols)

    def gate_w(dirn):
        return jnp.concatenate([lru_wr[0, dirn], lru_wi[0, dirn]], axis=-1).astype(BF16)

    def gate_b(dirn):
        return jnp.concatenate([lru_br[0, dirn].reshape(n_blocks_a, 1, LANES),
                                lru_bi[0, dirn].reshape(n_blocks_a, 1, LANES)], axis=-1)

    cb = 256 if d_a % 256 == 0 else LANES
    ya = _rglru(p_lat, p_ctx, conv_w[0], conv_b[0], gate_w(0), gate_b(0), gate_w(1), gate_b(1),
                lru_lambda[0], d_a, cb, tt=128)

    states = _hgrn(p_ctx.reshape(bsz, tc_len // CHUNK, CHUNK, n_cols), hgrn_lb_logits, d_b,
                   n_heads, first_b_group, None, False)
    o_f, o_b = _hgrn(p_b, hgrn_lb_logits, d_b, n_heads, 0, states, True)

    grid4 = lambda z: z.reshape(bsz, n_rows_grid, GRID_W, z.shape[-1])
    out = _out_projection(grid4(ya), o_f, o_b, grid4(p_lat), first_b_group, grid4(x), mod3,
                          hgrn_norm_w[0], final_norm_w, w_out_b, n_heads, 16, 16)
    return out.reshape(bsz, t_len, d)
```

```python
import functools

import jax
import jax.numpy as jnp
from jax import lax
from jax.experimental import pallas as pl
from jax.experimental.pallas import tpu as pltpu

GRID_W = 64
CHUNK = 64
LRU_C = 8.0
EPS = 1e-6
CONV_PAD_L = 2
LANES = 128
SUBLANES = 8
EXP2_CLAMP = 115.0
NEG_LOG2E = -1.4426950408889634
VMEM_LIMIT = 56 * 1024 * 1024
PROLOGUE_ROWS = 16

F32 = jnp.float32
BF16 = jnp.bfloat16


def _sigmoid(z):
    return 1.0 / (1.0 + jnp.exp2(z * NEG_LOG2E))


def _silu(z):
    return z * _sigmoid(z)


def _softplus(z):
    return jnp.maximum(z, 0.0) + jnp.log1p(jnp.exp(-jnp.abs(z)))


def _mod_kernel(c_ref, w_ref, b_ref, o_ref):
    s = _silu(c_ref[...])
    o_ref[...] = jnp.dot(s.astype(BF16), w_ref[...].astype(BF16),
                         preferred_element_type=F32) + b_ref[...]


def _modulation(cc, w, b):
    rows, d = cc.shape
    n = w.shape[1]
    tn = 512 if n % 512 == 0 else n
    return pl.pallas_call(
        _mod_kernel,
        grid=(n // tn,),
        in_specs=[pl.BlockSpec((rows, d), lambda j: (0, 0)),
                  pl.BlockSpec((d, tn), lambda j: (0, j)),
                  pl.BlockSpec((1, tn), lambda j: (0, j))],
        out_specs=pl.BlockSpec((rows, tn), lambda j: (0, j)),
        out_shape=jax.ShapeDtypeStruct((rows, n), F32),
        compiler_params=pltpu.CompilerParams(
            dimension_semantics=("arbitrary",), vmem_limit_bytes=VMEM_LIMIT),
        name="adaln_modulation",
    )(cc, w, b.reshape(1, n))


def _store_projection(o_ref, res, j, silu_tiles):
    lo, hi = silu_tiles
    if hi <= lo:
        o_ref[...] = res.astype(o_ref.dtype)
        return
    in_range = jnp.logical_and(j >= lo, j < hi)
    o_ref[...] = jnp.where(in_range, _silu(res), res).astype(o_ref.dtype)


def _inproj_kernel(x_ref, shift_ref, scale_ref, nw_ref, w_ref, o_ref, h_ref, *, silu_tiles):
    @pl.when(pl.program_id(1) == 0)
    def _():
        rows = min(PROLOGUE_ROWS, x_ref.shape[0])
        gain = nw_ref[...] * (1.0 + scale_ref[0])

        def slab(s, carry):
            sl = pl.ds(pl.multiple_of(s * rows, rows), rows)
            x = x_ref[sl, :]
            rs = lax.rsqrt(jnp.mean(x * x, axis=-1, keepdims=True) + EPS)
            h_ref[sl, :] = (x_ref[sl, :] * rs * gain + shift_ref[0]).astype(BF16)
            return carry

        lax.fori_loop(0, x_ref.shape[0] // rows, slab, 0, unroll=4)

    res = jnp.dot(h_ref[...], w_ref[...].astype(BF16), preferred_element_type=F32)
    _store_projection(o_ref, res, pl.program_id(1), silu_tiles)


def _in_projection(x2d, mod3, norm_w, w_bf16, row_of_tile, wcol_of_tile, n_out, tm, tn, silu_tiles):
    m, d = x2d.shape
    return pl.pallas_call(
        functools.partial(_inproj_kernel, silu_tiles=silu_tiles),
        grid=(m // tm, n_out // tn),
        in_specs=[pl.BlockSpec((tm, d), lambda i, j: (i, 0)),
                  pl.BlockSpec((1, 1, d), lambda i, j: (row_of_tile(i), 0, 0)),
                  pl.BlockSpec((1, 1, d), lambda i, j: (row_of_tile(i), 0, 1)),
                  pl.BlockSpec((1, d), lambda i, j: (0, 0)),
                  pl.BlockSpec((d, tn), lambda i, j: (0, wcol_of_tile(j)))],
        out_specs=pl.BlockSpec((tm, tn), lambda i, j: (i, j)),
        out_shape=jax.ShapeDtypeStruct((m, n_out), BF16),
        scratch_shapes=[pltpu.VMEM((tm, d), BF16)],
        compiler_params=pltpu.CompilerParams(
            dimension_semantics=("parallel", "arbitrary"),
            vmem_limit_bytes=VMEM_LIMIT),
        name="in_projection",
    )(x2d, mod3, mod3, norm_w.reshape(1, d), w_bf16)


def _inproj_colmajor_kernel(x_ref, shift_ref, scale_ref, nw_ref, w_ref, o_ref, h_ref, *, slab,
                            silu_tiles):
    _, n_r, n_w, d = x_ref.shape

    rr = 2 * SUBLANES
    lanes = [slice(s * slab, (s + 1) * slab) for s in range(d // slab)]

    @pl.when(pl.program_id(2) == 0)
    def _():
        def row_group(g, carry):
            rows = pl.ds(pl.multiple_of(g * rr, rr), rr)
            sq = jnp.zeros((rr, n_w, slab), F32)
            for sl in lanes:
                xs = x_ref[0, rows, :, sl]
                sq = sq + xs * xs
            rs = lax.rsqrt(jnp.sum(sq, axis=-1, keepdims=True) * (1.0 / d) + EPS)
            for sl in lanes:
                y = x_ref[0, rows, :, sl] * rs * nw_ref[:, sl]
                y = y * (1.0 + scale_ref[0][:, sl]) + shift_ref[0][:, sl]
                h_ref[:, rows, sl] = jnp.transpose(y, (1, 0, 2)).astype(BF16)
            return carry

        lax.fori_loop(0, n_r // rr, row_group, 0)

    res = jnp.dot(h_ref[...].reshape(n_w * n_r, d), w_ref[...].astype(BF16),
                  preferred_element_type=F32)
    _store_projection(o_ref, res.reshape(1, n_w, n_r, res.shape[1]), pl.program_id(2), silu_tiles)


def _in_projection_colmajor(x4d, mod3, norm_w, w_bf16, wcol_of_tile, n_out, wb, tn, silu_tiles):
    bsz, n_r, n_w, d = x4d.shape
    kern = functools.partial(_inproj_colmajor_kernel, slab=2 * LANES, silu_tiles=silu_tiles)
    return pl.pallas_call(
        kern,
        grid=(bsz, n_w // wb, n_out // tn),
        in_specs=[pl.BlockSpec((1, n_r, wb, d), lambda b, w, j: (b, 0, w, 0)),
                  pl.BlockSpec((1, 1, d), lambda b, w, j: (b, 0, 0)),
                  pl.BlockSpec((1, 1, d), lambda b, w, j: (b, 0, 1)),
                  pl.BlockSpec((1, d), lambda b, w, j: (0, 0)),
                  pl.BlockSpec((d, tn), lambda b, w, j: (0, wcol_of_tile(j)))],
        out_specs=pl.BlockSpec((1, wb, n_r, tn), lambda b, w, j: (b, w, 0, j)),
        out_shape=jax.ShapeDtypeStruct((bsz, n_w, n_r, n_out), BF16),
        scratch_shapes=[pltpu.VMEM((wb, n_r, d), BF16)],
        compiler_params=pltpu.CompilerParams(
            dimension_semantics=("parallel", "parallel", "arbitrary"),
            vmem_limit_bytes=VMEM_LIMIT),
        name="in_projection_colmajor",
    )(x4d, mod3, mod3, norm_w.reshape(1, d), w_bf16)


def _group_scan(a, b, reverse):
    row = lax.broadcasted_iota(jnp.int32, a.shape, 1)
    for k in (1, 2, 4):
        if reverse:
            a_sh = pltpu.roll(a, SUBLANES - k, axis=1)
            b_sh = pltpu.roll(b, SUBLANES - k, axis=1)
            m = row < SUBLANES - k
        else:
            a_sh = pltpu.roll(a, k, axis=1)
            b_sh = pltpu.roll(b, k, axis=1)
            m = row >= k
        b = jnp.where(m, a * b_sh + b, b)
        a = jnp.where(m, a * a_sh, a)
    return a, b


def _sqrt_unit(x):
    return jnp.where(x > 0.0, x * lax.rsqrt(x), 0.0)


def _lru_block(u, wg, bg, sp, carry, reverse):
    tt = u.shape[0]
    g = jnp.dot(u.astype(BF16), wg, preferred_element_type=F32) + bg
    r = _sigmoid(g[:, :LANES])
    i = _sigmoid(g[:, LANES:])
    a = jnp.exp2(r * sp)
    b = _sqrt_unit(1.0 - a * a) * (i * u)
    groups = tt // SUBLANES
    a3, b3 = _group_scan(a.reshape(groups, SUBLANES, LANES), b.reshape(groups, SUBLANES, LANES),
                         reverse)
    hs = [None] * groups
    order = range(groups - 1, -1, -1) if reverse else range(groups)
    for g_i in order:
        h = b3[g_i] + a3[g_i] * carry
        carry = h[0:1, :] if reverse else h[SUBLANES - 1:SUBLANES, :]
        hs[g_i] = h
    return jnp.concatenate(hs, axis=0), carry


def _rglru_kernel(xa_ref, ga_ref, xc_ref, cw_ref, cb_ref, wgf_ref, bgf_ref, wgb_ref, bgb_ref,
                  lam_ref, o_ref, xf_ref, xcf_ref, u_ref, hf_ref, *, tt):
    t_len = xa_ref.shape[1]
    tc_len = xc_ref.shape[1]
    nblk = xa_ref.shape[2] // LANES
    lanes = [slice(k * LANES, (k + 1) * LANES) for k in range(nblk)]
    zeros = jnp.zeros((SUBLANES, LANES), F32)
    for k, sl in enumerate(lanes):
        xf_ref[k, 0:SUBLANES, :] = zeros
        xf_ref[k, SUBLANES:SUBLANES + t_len, :] = xa_ref[0, :, sl].astype(F32)
        xf_ref[k, SUBLANES + t_len:, :] = zeros
        xcf_ref[k, 0:SUBLANES, :] = zeros
        xcf_ref[k, SUBLANES:SUBLANES + tc_len, :] = xc_ref[0, :, sl].astype(F32)
        xcf_ref[k, SUBLANES + tc_len:, :] = zeros

    sp_f = _softplus(-lam_ref[0:1, :]) * (LRU_C * NEG_LOG2E)
    sp_b = _softplus(-lam_ref[1:2, :]) * (LRU_C * NEG_LOG2E)

    def conv(src_ref, k, t0):
        u = cb_ref[:, lanes[k]]
        for tap in range(cw_ref.shape[0]):
            off = SUBLANES - CONV_PAD_L + tap
            if off % SUBLANES == 0:
                rows = pl.ds(pl.multiple_of(t0 + off, SUBLANES), tt)
            else:
                rows = pl.ds(t0 + off, tt, stride=1)
            u = u + src_ref[k, rows, :] * cw_ref[tap:tap + 1, lanes[k]]
        return u

    def block(u, k, carry, reverse):
        if reverse:
            return _lru_block(u, wgb_ref[k], bgb_ref[k], sp_b[:, lanes[k]], carry, True)
        return _lru_block(u, wgf_ref[k], bgf_ref[k], sp_f[:, lanes[k]], carry, False)

    n_lat = t_len // tt
    n_ctx = tc_len // tt
    carry0 = tuple(jnp.zeros((1, LANES), F32) for _ in lanes)

    def ctx_pass(reverse):
        def body(i, carry):
            t0 = pl.multiple_of((n_ctx - 1 - i if reverse else i) * tt, tt)
            return tuple(block(conv(xcf_ref, k, t0), k, carry[k], reverse)[1] for k in range(nblk))
        return lax.fori_loop(0, n_ctx, body, carry0)

    def lat_f(i, carry):
        t0 = pl.multiple_of(i * tt, tt)
        rows = pl.ds(t0, tt)
        out = []
        for k, sl in enumerate(lanes):
            u = conv(xf_ref, k, t0)
            u_ref[rows, sl] = u
            h, c_out = block(u, k, carry[k], False)
            hf_ref[rows, sl] = h
            out.append(c_out)
        return tuple(out)

    lax.fori_loop(0, n_lat, lat_f, ctx_pass(False), unroll=2)

    def lat_b(i, carry):
        rows = pl.ds(pl.multiple_of((n_lat - 1 - i) * tt, tt), tt)
        out = []
        for k, sl in enumerate(lanes):
            h, c_out = block(u_ref[rows, sl], k, carry[k], True)
            gate = ga_ref[0, rows, sl].astype(F32)
            o_ref[0, rows, sl] = ((hf_ref[rows, sl] + h) * gate).astype(o_ref.dtype)
            out.append(c_out)
        return tuple(out)

    lax.fori_loop(0, n_lat, lat_b, ctx_pass(True), unroll=2)


def _rglru(p_lat, p_ctx, conv_w, conv_b, wg_f, bg_f, wg_b, bg_b, lam, d_a, cb, tt):
    bsz, t_len, _ = p_lat.shape
    tc_len = p_ctx.shape[1]
    nblk = cb // LANES
    ncb = d_a // cb
    kern = functools.partial(_rglru_kernel, tt=tt)
    return pl.pallas_call(
        kern,
        grid=(bsz, ncb),
        in_specs=[pl.BlockSpec((1, t_len, cb), lambda b, c: (b, 0, c)),
                  pl.BlockSpec((1, t_len, cb), lambda b, c: (b, 0, ncb + c)),
                  pl.BlockSpec((1, tc_len, cb), lambda b, c: (b, 0, c)),
                  pl.BlockSpec((conv_w.shape[0], cb), lambda b, c: (0, c)),
                  pl.BlockSpec((1, cb), lambda b, c: (0, c)),
                  pl.BlockSpec((nblk, LANES, 2 * LANES), lambda b, c: (c, 0, 0)),
                  pl.BlockSpec((nblk, 1, 2 * LANES), lambda b, c: (c, 0, 0)),
                  pl.BlockSpec((nblk, LANES, 2 * LANES), lambda b, c: (c, 0, 0)),
                  pl.BlockSpec((nblk, 1, 2 * LANES), lambda b, c: (c, 0, 0)),
                  pl.BlockSpec((2, cb), lambda b, c: (0, c))],
        out_specs=pl.BlockSpec((1, t_len, cb), lambda b, c: (b, 0, c)),
        out_shape=jax.ShapeDtypeStruct((bsz, t_len, d_a), BF16),
        scratch_shapes=[pltpu.VMEM((nblk, t_len + 2 * SUBLANES, LANES), F32),
                        pltpu.VMEM((nblk, tc_len + 2 * SUBLANES, LANES), F32),
                        pltpu.VMEM((t_len, cb), F32),
                        pltpu.VMEM((t_len, cb), F32)],
        compiler_params=pltpu.CompilerParams(
            dimension_semantics=("parallel", "parallel"),
            vmem_limit_bytes=VMEM_LIMIT),
        name="rglru",
    )(p_lat, p_lat, p_ctx, conv_w, conv_b.reshape(1, d_a), wg_f, bg_f, wg_b, bg_b, lam)


_NT = (((1,), (1,)), ((), ()))
_TN = (((0,), (0,)), ((), ()))


class _HgrnDir:
    def __init__(self, d, q_ref, f_ref, v_ref, lb, st_ref, o_ref, scratch, reverse, n_heads, hd):
        self.d, self.q_ref, self.f_ref, self.v_ref, self.lb = d, q_ref, f_ref, v_ref, lb
        self.st_ref, self.o_ref, self.reverse, self.n_heads, self.hd = st_ref, o_ref, reverse, n_heads, hd
        (self.hl_ref, self.kk_ref, self.cum_ref, self.qd_ref, self.kd_ref, self.qin_ref,
         self.kout_ref, self.dec_ref, self.sc_ref) = scratch
        self.c_len = q_ref.shape[2]
        r_i = lax.broadcasted_iota(jnp.int32, (self.c_len, self.c_len), 0)
        c_i = lax.broadcasted_iota(jnp.int32, (self.c_len, self.c_len), 1)
        self.tri = (r_i <= c_i) if reverse else (r_i >= c_i)

    def _slabs(self, width):
        total = self.n_heads * self.hd
        return [slice(s, s + width) for s in range(0, total, width)]


    def _gate_slab(self, sl):
        d, c_len = self.d, self.c_len
        lb = self.lb[:, sl]
        f = lb + (1.0 - lb) * _sigmoid(self.f_ref[0, 0, :, sl].astype(F32))
        logf = jnp.log2(f)
        self.kk_ref[d, :, sl] = 1.0 - f
        hi = logf.astype(BF16)
        self.hl_ref[d, 0:c_len, sl] = hi
        self.hl_ref[d, c_len:2 * c_len, sl] = (logf - hi.astype(F32)).astype(BF16)

    def _cumulate(self):
        tri_b = self.tri.astype(BF16)
        tri2 = jnp.concatenate([tri_b, tri_b], axis=1)
        self.cum_ref[self.d] = jnp.dot(tri2, self.hl_ref[self.d], preferred_element_type=F32)

    def _decay_slab(self, sl):
        d, c_len = self.d, self.c_len
        half = c_len // 2
        row_last = 0 if self.reverse else c_len - 1
        row_ref = half if self.reverse else half - 1
        cum = self.cum_ref[d, :, sl]
        last = cum[row_last:row_last + 1, :]
        ref = cum[row_ref:row_ref + 1, :]
        kk = self.kk_ref[d, :, sl]
        qs = self.q_ref[0, 0, :, sl].astype(F32)
        self.qd_ref[d, :, sl] = (qs * jnp.exp2(jnp.minimum(cum - ref, EXP2_CLAMP))).astype(BF16)
        self.kd_ref[d, :, sl] = (kk * jnp.exp2(jnp.minimum(ref - cum, EXP2_CLAMP))).astype(BF16)
        self.qin_ref[d, :, sl] = (qs * jnp.exp2(cum)).astype(BF16)
        self.kout_ref[d, :, sl] = (kk * jnp.exp2(last - cum)).astype(BF16)
        self.dec_ref[d, :, sl] = jnp.exp2(last)

    def _head(self, h):
        return slice(h * self.hd, (h + 1) * self.hd)

    def _scores(self, h):
        d, sl = self.d, self._head(h)
        s = lax.dot_general(self.qd_ref[d, :, sl], self.kd_ref[d, :, sl], _NT,
                            preferred_element_type=F32)
        self.sc_ref[d, h] = jnp.where(self.tri, s, 0.0).astype(BF16)

    def _output(self, h):
        d, sl = self.d, self._head(h)
        o = (jnp.dot(self.sc_ref[d, h], self.v_ref[0, 0, :, sl], preferred_element_type=F32)
             + lax.dot_general(self.qin_ref[d, :, sl], self.st_ref[h].astype(BF16), _NT,
                               preferred_element_type=F32))
        self.o_ref[0, 0, :, sl] = o.astype(self.o_ref.dtype)

    def _update(self, h):
        d, sl = self.d, self._head(h)
        self.st_ref[h] = (self.st_ref[h] * self.dec_ref[d, :, sl]
                          + lax.dot_general(self.v_ref[0, 0, :, sl], self.kout_ref[d, :, sl], _TN,
                                            preferred_element_type=F32))

    def emit(self):
        heads = range(self.n_heads)
        slabs = self._slabs(self.hd)
        order = []
        for h in heads:
            if self.o_ref is not None:
                order.append(functools.partial(self._scores, h))
            order.append(functools.partial(self._gate_slab, slabs[h]))
        order.append(self._cumulate)
        for h in heads:
            if self.o_ref is not None:
                order.append(functools.partial(self._output, h))
            order.append(functools.partial(self._update, h))
            order.append(functools.partial(self._decay_slab, slabs[h]))
        return order


def _lower_bounds(logits_ref, layer):
    out = []
    for d in range(2):
        rows = [logits_ref[d, l:l + 1, :] for l in range(logits_ref.shape[1])]
        m = functools.reduce(jnp.maximum, rows)
        e = [jnp.exp(r - m) for r in rows]
        out.append(sum(e[:layer + 1]) / sum(e))
    return out


def _hgrn_kernel(*refs, n_heads, hd, has_init, emit_o, emit_state):
    qf_ref, ff_ref, vf_ref, qb_ref, fb_ref, vb_ref, lg_ref = refs[:7]
    pos = 7
    if has_init:
        s0f_ref, s0b_ref = refs[pos:pos + 2]
        pos += 2
    if emit_o:
        of_ref, ob_ref = refs[pos:pos + 2]
        pos += 2
    if emit_state:
        sof_ref, sob_ref = refs[pos:pos + 2]
        pos += 2
    sf_ref, sb_ref = refs[pos:pos + 2]
    scratch = refs[pos + 2:]

    j = pl.program_id(1)
    qd_ref, kd_ref, qin_ref, kout_ref, dec_ref = scratch[3:8]

    @pl.when(j == 0)
    def _():
        if has_init:
            sf_ref[...] = s0f_ref[0]
            sb_ref[...] = s0b_ref[0]
        else:
            sf_ref[...] = jnp.zeros_like(sf_ref)
            sb_ref[...] = jnp.zeros_like(sb_ref)
        for ref in (qd_ref, kd_ref, qin_ref, kout_ref):
            ref[...] = jnp.zeros(ref.shape, ref.dtype)
        dec_ref[...] = jnp.ones(dec_ref.shape, dec_ref.dtype)

    lb_f, lb_b = _lower_bounds(lg_ref, 0)
    dirs = [_HgrnDir(0, qf_ref, ff_ref, vf_ref, lb_f, sf_ref, of_ref if emit_o else None,
                     scratch, False, n_heads, hd),
            _HgrnDir(1, qb_ref, fb_ref, vb_ref, lb_b, sb_ref, ob_ref if emit_o else None,
                     scratch, True, n_heads, hd)]
    for unit_f, unit_b in zip(dirs[0].emit(), dirs[1].emit()):
        unit_f()
        unit_b()

    if emit_state:
        @pl.when(j == pl.num_programs(1) - 1)
        def _():
            sof_ref[0] = sf_ref[...]
            sob_ref[0] = sb_ref[...]


def _hgrn(p_view, logits, d_b, n_heads, first_group, init_states, emit_o):
    bsz, n_chunks = p_view.shape[:2]
    hd = d_b // n_heads
    has_init = init_states is not None
    emit_state = not emit_o

    def chunk(j, reverse, lag):
        ch = jnp.maximum(j - 1, 0) if lag else jnp.minimum(j, n_chunks - 1)
        return n_chunks - 1 - ch if reverse else ch

    def spec(group, reverse, lag):
        def imap(b, j):
            return (b, chunk(j, reverse, lag), 0, first_group + group)
        return pl.BlockSpec((1, 1, CHUNK, d_b), imap)

    in_specs = [spec(0, False, False), spec(1, False, False), spec(3, False, True),
                spec(0, True, False), spec(2, True, False), spec(3, True, True),
                pl.BlockSpec(logits.shape, lambda b, j: (0, 0, 0))]
    args = [p_view] * 6 + [logits]
    state_spec = pl.BlockSpec((1, n_heads, hd, hd), lambda b, j: (b, 0, 0, 0))
    state_shape = jax.ShapeDtypeStruct((bsz, n_heads, hd, hd), F32)
    if has_init:
        in_specs += [state_spec, state_spec]
        args += list(init_states)
    out_specs, out_shape = [], []
    if emit_o:
        o_shape = jax.ShapeDtypeStruct((bsz, n_chunks, CHUNK, d_b), BF16)
        out_specs += [pl.BlockSpec((1, 1, CHUNK, d_b), lambda b, j: (b, chunk(j, False, True), 0, 0)),
                      pl.BlockSpec((1, 1, CHUNK, d_b), lambda b, j: (b, chunk(j, True, True), 0, 0))]
        out_shape += [o_shape, o_shape]
    if emit_state:
        out_specs += [state_spec, state_spec]
        out_shape += [state_shape, state_shape]
    kern = functools.partial(_hgrn_kernel, n_heads=n_heads, hd=hd, has_init=has_init,
                             emit_o=emit_o, emit_state=emit_state)
    return pl.pallas_call(
        kern,
        grid=(bsz, n_chunks + 1),
        in_specs=in_specs,
        out_specs=out_specs,
        out_shape=out_shape,
        scratch_shapes=[pltpu.VMEM((n_heads, hd, hd), F32),
                        pltpu.VMEM((n_heads, hd, hd), F32),
                        pltpu.VMEM((2, 2 * CHUNK, d_b), BF16),
                        pltpu.VMEM((2, CHUNK, d_b), F32),
                        pltpu.VMEM((2, CHUNK, d_b), F32),
                        pltpu.VMEM((2, CHUNK, d_b), BF16),
                        pltpu.VMEM((2, CHUNK, d_b), BF16),
                        pltpu.VMEM((2, CHUNK, d_b), BF16),
                        pltpu.VMEM((2, CHUNK, d_b), BF16),
                        pltpu.VMEM((2, 1, d_b), F32),
                        pltpu.VMEM((2, n_heads, CHUNK, CHUNK), BF16)],
        compiler_params=pltpu.CompilerParams(
            dimension_semantics=("parallel", "arbitrary"),
            vmem_limit_bytes=VMEM_LIMIT),
        name="hgrn2_latent" if emit_o else "hgrn2_context",
    )(*args)


def _outproj_kernel(ya_ref, of_ref, ob_ref, gb_ref, x_ref, gate_ref, hnw_ref, fnw_ref, w_ref,
                    o_ref, y_ref, *, n_heads, hd):
    _, rb, wb, d_a = ya_ref.shape
    rows = rb * wb
    d = x_ref.shape[3]
    y_ref[:, 0:d_a] = ya_ref[0].reshape(rows, d_a)
    hnw = hnw_ref[...]
    for h in range(n_heads):
        sl = slice(h * hd, (h + 1) * hd)
        o = of_ref[0, :, :, sl].astype(F32) + ob_ref[0, :, :, sl].astype(F32)
        ms = jnp.mean(o * o, axis=-1, keepdims=True)
        on = jnp.transpose(o * lax.rsqrt(ms + EPS) * hnw, (1, 0, 2))
        yb = on * gb_ref[0, :, :, sl].astype(F32)
        y_ref[:, d_a + h * hd:d_a + (h + 1) * hd] = yb.reshape(rows, hd).astype(BF16)
    acc = jnp.dot(y_ref[...], w_ref[...], preferred_element_type=F32)
    z = x_ref[0].reshape(rows, d) + gate_ref[0] * acc
    ms = jnp.mean(z * z, axis=-1, keepdims=True)
    o_ref[0] = (z * lax.rsqrt(ms + EPS) * fnw_ref[...]).reshape(rb, wb, d)


def _out_projection(ya4, of4, ob4, pa4, gb_block, x4, mod3, hnw, fnw, w_bf16, n_heads, rb, wb):
    bsz, n_r, n_w, d = x4.shape
    d_a = ya4.shape[3]
    d_b = of4.shape[3]
    hd = d_b // n_heads
    kern = functools.partial(_outproj_kernel, n_heads=n_heads, hd=hd)

    def raster(c, col=0):
        return pl.BlockSpec((1, rb, wb, c), lambda b, r, w: (b, r, w, col))

    def colmajor(c):
        return pl.BlockSpec((1, wb, rb, c), lambda b, r, w: (b, w, r, 0))

    return pl.pallas_call(
        kern,
        grid=(bsz, n_r // rb, n_w // wb),
        in_specs=[raster(d_a), colmajor(d_b), colmajor(d_b), raster(d_b, gb_block), raster(d),
                  pl.BlockSpec((1, 1, d), lambda b, r, w: (b, 0, 2)),
                  pl.BlockSpec((1, hd), lambda b, r, w: (0, 0)),
                  pl.BlockSpec((1, d), lambda b, r, w: (0, 0)),
                  pl.BlockSpec((d_a + d_b, d), lambda b, r, w: (0, 0),
                               pipeline_mode=pl.Buffered(1))],
        out_specs=raster(d),
        out_shape=jax.ShapeDtypeStruct((bsz, n_r, n_w, d), F32),
        scratch_shapes=[pltpu.VMEM((rb * wb, d_a + d_b), BF16)],
        compiler_params=pltpu.CompilerParams(
            dimension_semantics=("parallel", "parallel", "parallel"),
            vmem_limit_bytes=VMEM_LIMIT),
        name="out_projection",
    )(ya4, of4, ob4, pa4, x4, mod3, hnw.reshape(1, hd), fnw.reshape(1, d), w_bf16)


def kernel(x, c, ctx, c_ctx, ada_w, ada_b, norm_w, w_in, conv_w, conv_b, lru_wr, lru_br, lru_wi,
           lru_bi, lru_lambda, hgrn_lb_logits, hgrn_norm_w, w_out, final_norm_w):
    bsz, t_len, d = x.shape
    tc_len = ctx.shape[1]
    assert ada_w.shape[0] == 1, "single-layer stack only"
    d_a = conv_w.shape[2]
    d_b = hgrn_lb_logits.shape[2]
    hd = hgrn_norm_w.shape[1]
    n_heads = d_b // hd
    n_blocks_a = lru_wr.shape[2]
    n_cols = w_in.shape[2]
    assert t_len == GRID_W * CHUNK and tc_len % CHUNK == 0
    assert d_a // n_blocks_a == LANES and hd == LANES
    assert (2 * d_a) % d_b == 0 and n_cols == 2 * d_a + 5 * d_b
    first_b_group = (2 * d_a) // d_b
    n_rows_grid = t_len // GRID_W

    n_rows = -(-(bsz + 1) // SUBLANES) * SUBLANES
    cc = jnp.zeros((n_rows, d), F32).at[:bsz].set(c).at[bsz].set(c_ctx)
    mod3 = _modulation(cc, ada_w[0], ada_b[0]).reshape(n_rows, 1, 3 * d)

    w_in_b = w_in[0]
    w_out_b = w_out[0].astype(BF16)

    tm = 1024 if t_len % 1024 == 0 else t_len
    tn = 1024 if d_a % 1024 == 0 and d_b % 1024 == 0 else min(d_a, d_b)
    tpb = t_len // tm
    a_tiles = (2 * d_a) // tn
    b_tiles = (4 * d_b) // tn
    q_tiles = d_b // tn
    p_a = _in_projection(x.reshape(bsz * t_len, d), mod3, norm_w[0], w_in_b,
                         lambda i: i // tpb, lambda j: jnp.where(j < a_tiles, j, j + b_tiles),
                         2 * d_a + d_b, tm, tn, (d_a // tn, (2 * d_a + d_b) // tn))
    wb = 16
    p_b = _in_projection_colmajor(x.reshape(bsz, n_rows_grid, GRID_W, d), mod3, norm_w[0], w_in_b,
                                  lambda j: j + a_tiles, 4 * d_b, wb, tn, (0, q_tiles))
    p_ctx = _in_projection(ctx.reshape(bsz * tc_len, d), mod3, norm_w[0], w_in_b,
                           lambda i: bsz, lambda j: j, n_cols, bsz * tc_len, tn,
                           (a_tiles, a_tiles + q_tiles))
    p_lat = p_a.reshape(bsz, t_len, 2 * d_a + d_b)
    p_ctx = p_ctx.reshape(bsz, tc_len, n_cols)

    def gate_w(dirn):
        return jnp.concatenate([lru_wr[0, dirn], lru_wi[0, dirn]], axis=-1).astype(BF16)

    def gate_b(dirn):
        return jnp.concatenate([lru_br[0, dirn].reshape(n_blocks_a, 1, LANES),
                                lru_bi[0, dirn].reshape(n_blocks_a, 1, LANES)], axis=-1)

    cb = 256 if d_a % 256 == 0 else LANES
    ya = _rglru(p_lat, p_ctx, conv_w[0], conv_b[0], gate_w(0), gate_b(0), gate_w(1), gate_b(1),
                lru_lambda[0], d_a, cb, tt=128)

    states = _hgrn(p_ctx.reshape(bsz, tc_len // CHUNK, CHUNK, n_cols), hgrn_lb_logits, d_b,
                   n_heads, first_b_group, None, False)
    o_f, o_b = _hgrn(p_b, hgrn_lb_logits, d_b, n_heads, 0, states, True)

    grid4 = lambda z: z.reshape(bsz, n_rows_grid, GRID_W, z.shape[-1])
    out = _out_projection(grid4(ya), o_f, o_b, grid4(p_lat), first_b_group, grid4(x), mod3,
                          hgrn_norm_w[0], final_norm_w, w_out_b, n_heads, 16, 16)
    return out.reshape(bsz, t_len, d)
```

```python
import functools

import jax
import jax.numpy as jnp
from jax import lax
from jax.experimental import pallas as pl
from jax.experimental.pallas import tpu as pltpu

GRID_W = 64
CHUNK = 64
LRU_C = 8.0
EPS = 1e-6
CONV_PAD_L = 2
LANES = 128
SUBLANES = 8
EXP2_CLAMP = 115.0
NEG_LOG2E = -1.4426950408889634
VMEM_LIMIT = 56 * 1024 * 1024
PROLOGUE_ROWS = 16
SEG_PAD = 4
TILES_PER_TRIP = 4

F32 = jnp.float32
BF16 = jnp.bfloat16


def _sigmoid(z):
    return 1.0 / (1.0 + jnp.exp2(z * NEG_LOG2E))


def _silu(z):
    return z * _sigmoid(z)


def _softplus(z):
    return jnp.maximum(z, 0.0) + jnp.log1p(jnp.exp(-jnp.abs(z)))


def _mod_kernel(c_ref, w_ref, b_ref, o_ref):
    s = _silu(c_ref[...])
    o_ref[...] = jnp.dot(s.astype(BF16), w_ref[...].astype(BF16),
                         preferred_element_type=F32) + b_ref[...]


def _modulation(cc, w, b):
    rows, d = cc.shape
    n = w.shape[1]
    tn = 512 if n % 512 == 0 else n
    return pl.pallas_call(
        _mod_kernel,
        grid=(n // tn,),
        in_specs=[pl.BlockSpec((rows, d), lambda j: (0, 0)),
                  pl.BlockSpec((d, tn), lambda j: (0, j)),
                  pl.BlockSpec((1, tn), lambda j: (0, j))],
        out_specs=pl.BlockSpec((rows, tn), lambda j: (0, j)),
        out_shape=jax.ShapeDtypeStruct((rows, n), F32),
        compiler_params=pltpu.CompilerParams(
            dimension_semantics=("arbitrary",), vmem_limit_bytes=VMEM_LIMIT),
        name="adaln_modulation",
    )(cc, w, b.reshape(1, n))


def _store_projection(o_ref, res, j, silu_tiles):
    lo, hi = silu_tiles
    if hi <= lo:
        o_ref[...] = res.astype(o_ref.dtype)
        return
    in_range = jnp.logical_and(j >= lo, j < hi)
    o_ref[...] = jnp.where(in_range, _silu(res), res).astype(o_ref.dtype)


def _inproj_kernel(x_ref, shift_ref, scale_ref, nw_ref, w_ref, o_ref, h_ref, *, silu_tiles):
    @pl.when(pl.program_id(1) == 0)
    def _():
        rows = min(PROLOGUE_ROWS, x_ref.shape[0])
        gain = nw_ref[...] * (1.0 + scale_ref[0])

        def slab(s, carry):
            sl = pl.ds(pl.multiple_of(s * rows, rows), rows)
            x = x_ref[sl, :]
            rs = lax.rsqrt(jnp.mean(x * x, axis=-1, keepdims=True) + EPS)
            h_ref[sl, :] = (x_ref[sl, :] * rs * gain + shift_ref[0]).astype(BF16)
            return carry

        lax.fori_loop(0, x_ref.shape[0] // rows, slab, 0, unroll=4)

    res = jnp.dot(h_ref[...], w_ref[...].astype(BF16), preferred_element_type=F32)
    _store_projection(o_ref, res, pl.program_id(1), silu_tiles)


def _in_projection(x2d, mod3, norm_w, w_bf16, row_of_tile, wcol_of_tile, n_out, tm, tn, silu_tiles):
    m, d = x2d.shape
    return pl.pallas_call(
        functools.partial(_inproj_kernel, silu_tiles=silu_tiles),
        grid=(m // tm, n_out // tn),
        in_specs=[pl.BlockSpec((tm, d), lambda i, j: (i, 0)),
                  pl.BlockSpec((1, 1, d), lambda i, j: (row_of_tile(i), 0, 0)),
                  pl.BlockSpec((1, 1, d), lambda i, j: (row_of_tile(i), 0, 1)),
                  pl.BlockSpec((1, d), lambda i, j: (0, 0)),
                  pl.BlockSpec((d, tn), lambda i, j: (0, wcol_of_tile(j)))],
        out_specs=pl.BlockSpec((tm, tn), lambda i, j: (i, j)),
        out_shape=jax.ShapeDtypeStruct((m, n_out), BF16),
        scratch_shapes=[pltpu.VMEM((tm, d), BF16)],
        compiler_params=pltpu.CompilerParams(
            dimension_semantics=("parallel", "arbitrary"),
            vmem_limit_bytes=VMEM_LIMIT),
        name="in_projection",
    )(x2d, mod3, mod3, norm_w.reshape(1, d), w_bf16)


def _inproj_colmajor_kernel(x_ref, shift_ref, scale_ref, nw_ref, w_ref, o_ref, h_ref, *, slab,
                            silu_tiles):
    _, n_r, n_w, d = x_ref.shape

    rr = 2 * SUBLANES
    lanes = [slice(s * slab, (s + 1) * slab) for s in range(d // slab)]

    @pl.when(pl.program_id(2) == 0)
    def _():
        def row_group(g, carry):
            rows = pl.ds(pl.multiple_of(g * rr, rr), rr)
            sq = jnp.zeros((rr, n_w, slab), F32)
            for sl in lanes:
                xs = x_ref[0, rows, :, sl]
                sq = sq + xs * xs
            rs = lax.rsqrt(jnp.sum(sq, axis=-1, keepdims=True) * (1.0 / d) + EPS)
            for sl in lanes:
                y = x_ref[0, rows, :, sl] * rs * nw_ref[:, sl]
                y = y * (1.0 + scale_ref[0][:, sl]) + shift_ref[0][:, sl]
                h_ref[:, rows, sl] = jnp.transpose(y, (1, 0, 2)).astype(BF16)
            return carry

        lax.fori_loop(0, n_r // rr, row_group, 0)

    res = jnp.dot(h_ref[...].reshape(n_w * n_r, d), w_ref[...].astype(BF16),
                  preferred_element_type=F32)
    _store_projection(o_ref, res.reshape(1, n_w, n_r, res.shape[1]), pl.program_id(2), silu_tiles)


def _in_projection_colmajor(x4d, mod3, norm_w, w_bf16, wcol_of_tile, n_out, wb, tn, silu_tiles):
    bsz, n_r, n_w, d = x4d.shape
    kern = functools.partial(_inproj_colmajor_kernel, slab=2 * LANES, silu_tiles=silu_tiles)
    return pl.pallas_call(
        kern,
        grid=(bsz, n_w // wb, n_out // tn),
        in_specs=[pl.BlockSpec((1, n_r, wb, d), lambda b, w, j: (b, 0, w, 0)),
                  pl.BlockSpec((1, 1, d), lambda b, w, j: (b, 0, 0)),
                  pl.BlockSpec((1, 1, d), lambda b, w, j: (b, 0, 1)),
                  pl.BlockSpec((1, d), lambda b, w, j: (0, 0)),
                  pl.BlockSpec((d, tn), lambda b, w, j: (0, wcol_of_tile(j)))],
        out_specs=pl.BlockSpec((1, wb, n_r, tn), lambda b, w, j: (b, w, 0, j)),
        out_shape=jax.ShapeDtypeStruct((bsz, n_w, n_r, n_out), BF16),
        scratch_shapes=[pltpu.VMEM((wb, n_r, d), BF16)],
        compiler_params=pltpu.CompilerParams(
            dimension_semantics=("parallel", "parallel", "arbitrary"),
            vmem_limit_bytes=VMEM_LIMIT),
        name="in_projection_colmajor",
    )(x4d, mod3, mod3, norm_w.reshape(1, d), w_bf16)


def _group_scan(a, b, reverse):
    row = lax.broadcasted_iota(jnp.int32, a.shape, 1)
    for k in (1, 2, 4):
        if reverse:
            a_sh = pltpu.roll(a, SUBLANES - k, axis=1)
            b_sh = pltpu.roll(b, SUBLANES - k, axis=1)
            m = row < SUBLANES - k
        else:
            a_sh = pltpu.roll(a, k, axis=1)
            b_sh = pltpu.roll(b, k, axis=1)
            m = row >= k
        b = jnp.where(m, a * b_sh + b, b)
        a = jnp.where(m, a * a_sh, a)
    return a, b


def _sqrt_unit(x):
    return jnp.where(x > 0.0, x * lax.rsqrt(x), 0.0)


def _segment_scan(a, b, carry, reverse, scr_ref):
    tt = a.shape[0]
    seg = tt // SUBLANES
    pitch = seg + SEG_PAD
    for s in range(SUBLANES):
        rows = pl.ds(pitch * s, seg, stride=1)
        scr_ref[0, rows, :] = a[seg * s:seg * (s + 1), :]
        scr_ref[1, rows, :] = b[seg * s:seg * (s + 1), :]
    hs, ps = [None] * seg, [None] * seg
    h = p = None
    for j in (range(seg - 1, -1, -1) if reverse else range(seg)):
        step = pl.ds(j, SUBLANES, stride=pitch)
        a_j = scr_ref[0, step, :]
        b_j = scr_ref[1, step, :]
        h = b_j if h is None else a_j * h + b_j
        p = a_j if p is None else a_j * p
        hs[j], ps[j] = h, p
    g3, e3 = _group_scan(p[None], h[None], reverse)
    end = e3[0] + g3[0] * carry
    row = lax.broadcasted_iota(jnp.int32, end.shape, 0)
    if reverse:
        enter = jnp.where(row == SUBLANES - 1, carry, pltpu.roll(end, SUBLANES - 1, axis=0))
        carry_out = end[0:1, :]
    else:
        enter = jnp.where(row == 0, carry, pltpu.roll(end, 1, axis=0))
        carry_out = end[SUBLANES - 1:SUBLANES, :]
    for j in range(seg):
        scr_ref[2, pl.ds(j, SUBLANES, stride=pitch), :] = hs[j] + ps[j] * enter
    h_time = [scr_ref[2, pl.ds(pitch * s, seg, stride=1), :] for s in range(SUBLANES)]
    return jnp.concatenate(h_time, axis=0), carry_out


def _lru_block(u, wg, bg, sp, carry, reverse, scr_ref):
    g = jnp.dot(u.astype(BF16), wg, preferred_element_type=F32) + bg
    r = _sigmoid(g[:, :LANES])
    i = _sigmoid(g[:, LANES:])
    a = jnp.exp2(r * sp)
    b = _sqrt_unit(1.0 - a * a) * (i * u)
    return _segment_scan(a, b, carry, reverse, scr_ref)


def _rglru_kernel(xa_ref, ga_ref, xc_ref, cw_ref, cb_ref, wgf_ref, bgf_ref, wgb_ref, bgb_ref,
                  lam_ref, o_ref, xf_ref, xcf_ref, u_ref, hf_ref, scr_ref, *, tt):
    t_len = xa_ref.shape[1]
    tc_len = xc_ref.shape[1]
    nblk = xa_ref.shape[2] // LANES
    lanes = [slice(k * LANES, (k + 1) * LANES) for k in range(nblk)]
    zeros = jnp.zeros((SUBLANES, LANES), F32)
    for k, sl in enumerate(lanes):
        xf_ref[k, 0:SUBLANES, :] = zeros
        xf_ref[k, SUBLANES:SUBLANES + t_len, :] = xa_ref[0, :, sl].astype(F32)
        xf_ref[k, SUBLANES + t_len:, :] = zeros
        xcf_ref[k, 0:SUBLANES, :] = zeros
        xcf_ref[k, SUBLANES:SUBLANES + tc_len, :] = xc_ref[0, :, sl].astype(F32)
        xcf_ref[k, SUBLANES + tc_len:, :] = zeros

    sp_f = _softplus(-lam_ref[0:1, :]) * (LRU_C * NEG_LOG2E)
    sp_b = _softplus(-lam_ref[1:2, :]) * (LRU_C * NEG_LOG2E)

    def conv(src_ref, k, t0):
        u = cb_ref[:, lanes[k]]
        for tap in range(cw_ref.shape[0]):
            off = SUBLANES - CONV_PAD_L + tap
            if off % SUBLANES == 0:
                start = t0 + off
                rows = pl.ds(start if isinstance(start, int) else pl.multiple_of(start, SUBLANES), tt)
            else:
                rows = pl.ds(t0 + off, tt, stride=1)
            u = u + src_ref[k, rows, :] * cw_ref[tap:tap + 1, lanes[k]]
        return u

    def block(u, k, carry, reverse, slot):
        scr = scr_ref.at[k * TILES_PER_TRIP + slot]
        if reverse:
            return _lru_block(u, wgb_ref[k], bgb_ref[k], sp_b[:, lanes[k]], carry, True, scr)
        return _lru_block(u, wgf_ref[k], bgf_ref[k], sp_f[:, lanes[k]], carry, False, scr)

    n_lat = t_len // tt
    n_ctx = tc_len // tt
    assert n_lat % TILES_PER_TRIP == 0
    carry0 = tuple(jnp.zeros((1, LANES), F32) for _ in lanes)

    def tile_start(trip, slot, n_tiles, reverse):
        idx = trip * TILES_PER_TRIP + slot
        return pl.multiple_of((n_tiles - 1 - idx if reverse else idx) * tt, tt)

    def ctx_pass(reverse):
        carry = carry0
        for idx in range(n_ctx):
            t0 = (n_ctx - 1 - idx if reverse else idx) * tt
            carry = tuple(block(conv(xcf_ref, k, t0), k, carry[k], reverse, idx % TILES_PER_TRIP)[1]
                          for k in range(nblk))
        return carry

    def lat_f(trip, carry):
        for slot in range(TILES_PER_TRIP):
            t0 = tile_start(trip, slot, n_lat, False)
            rows = pl.ds(t0, tt)
            out = []
            for k, sl in enumerate(lanes):
                u = conv(xf_ref, k, t0)
                u_ref[rows, sl] = u
                h, c_out = block(u, k, carry[k], False, slot)
                hf_ref[rows, sl] = h
                out.append(c_out)
            carry = tuple(out)
        return carry

    lax.fori_loop(0, n_lat // TILES_PER_TRIP, lat_f, ctx_pass(False))

    def lat_b(trip, carry):
        for slot in range(TILES_PER_TRIP):
            rows = pl.ds(tile_start(trip, slot, n_lat, True), tt)
            out = []
            for k, sl in enumerate(lanes):
                h, c_out = block(u_ref[rows, sl], k, carry[k], True, slot)
                gate = ga_ref[0, rows, sl].astype(F32)
                o_ref[0, rows, sl] = ((hf_ref[rows, sl] + h) * gate).astype(o_ref.dtype)
                out.append(c_out)
            carry = tuple(out)
        return carry

    lax.fori_loop(0, n_lat // TILES_PER_TRIP, lat_b, ctx_pass(True))


def _rglru(p_lat, p_ctx, conv_w, conv_b, wg_f, bg_f, wg_b, bg_b, lam, d_a, cb, tt):
    bsz, t_len, _ = p_lat.shape
    tc_len = p_ctx.shape[1]
    nblk = cb // LANES
    ncb = d_a // cb
    kern = functools.partial(_rglru_kernel, tt=tt)
    return pl.pallas_call(
        kern,
        grid=(bsz, ncb),
        in_specs=[pl.BlockSpec((1, t_len, cb), lambda b, c: (b, 0, c)),
                  pl.BlockSpec((1, t_len, cb), lambda b, c: (b, 0, ncb + c)),
                  pl.BlockSpec((1, tc_len, cb), lambda b, c: (b, 0, c)),
                  pl.BlockSpec((conv_w.shape[0], cb), lambda b, c: (0, c)),
                  pl.BlockSpec((1, cb), lambda b, c: (0, c)),
                  pl.BlockSpec((nblk, LANES, 2 * LANES), lambda b, c: (c, 0, 0)),
                  pl.BlockSpec((nblk, 1, 2 * LANES), lambda b, c: (c, 0, 0)),
                  pl.BlockSpec((nblk, LANES, 2 * LANES), lambda b, c: (c, 0, 0)),
                  pl.BlockSpec((nblk, 1, 2 * LANES), lambda b, c: (c, 0, 0)),
                  pl.BlockSpec((2, cb), lambda b, c: (0, c))],
        out_specs=pl.BlockSpec((1, t_len, cb), lambda b, c: (b, 0, c)),
        out_shape=jax.ShapeDtypeStruct((bsz, t_len, d_a), BF16),
        scratch_shapes=[pltpu.VMEM((nblk, t_len + 2 * SUBLANES, LANES), F32),
                        pltpu.VMEM((nblk, tc_len + 2 * SUBLANES, LANES), F32),
                        pltpu.VMEM((t_len, cb), F32),
                        pltpu.VMEM((t_len, cb), F32),
                        pltpu.VMEM((nblk * TILES_PER_TRIP, 3, tt + SUBLANES * SEG_PAD, LANES),
                                   F32)],
        compiler_params=pltpu.CompilerParams(
            dimension_semantics=("parallel", "parallel"),
            vmem_limit_bytes=VMEM_LIMIT),
        name="rglru",
    )(p_lat, p_lat, p_ctx, conv_w, conv_b.reshape(1, d_a), wg_f, bg_f, wg_b, bg_b, lam)


_NT = (((1,), (1,)), ((), ()))
_TN = (((0,), (0,)), ((), ()))


class _HgrnDir:
    def __init__(self, d, q_ref, f_ref, v_ref, lb, st_ref, o_ref, scratch, reverse, n_heads, hd):
        self.d, self.q_ref, self.f_ref, self.v_ref, self.lb = d, q_ref, f_ref, v_ref, lb
        self.st_ref, self.o_ref, self.reverse, self.n_heads, self.hd = st_ref, o_ref, reverse, n_heads, hd
        (self.hl_ref, self.kk_ref, self.cum_ref, self.qd_ref, self.kd_ref, self.qin_ref,
         self.kout_ref, self.dec_ref, self.sc_ref) = scratch
        self.c_len = q_ref.shape[2]
        r_i = lax.broadcasted_iota(jnp.int32, (self.c_len, self.c_len), 0)
        c_i = lax.broadcasted_iota(jnp.int32, (self.c_len, self.c_len), 1)
        self.tri = (r_i <= c_i) if reverse else (r_i >= c_i)

    def _slabs(self, width):
        total = self.n_heads * self.hd
        return [slice(s, s + width) for s in range(0, total, width)]


    def _gate_slab(self, sl):
        d, c_len = self.d, self.c_len
        lb = self.lb[:, sl]
        f = lb + (1.0 - lb) * _sigmoid(self.f_ref[0, 0, :, sl].astype(F32))
        logf = jnp.log2(f)
        self.kk_ref[d, :, sl] = 1.0 - f
        hi = logf.astype(BF16)
        self.hl_ref[d, 0:c_len, sl] = hi
        self.hl_ref[d, c_len:2 * c_len, sl] = (logf - hi.astype(F32)).astype(BF16)

    def _cumulate(self):
        tri_b = self.tri.astype(BF16)
        tri2 = jnp.concatenate([tri_b, tri_b], axis=1)
        self.cum_ref[self.d] = jnp.dot(tri2, self.hl_ref[self.d], preferred_element_type=F32)

    def _decay_slab(self, sl):
        d, c_len = self.d, self.c_len
        half = c_len // 2
        row_last = 0 if self.reverse else c_len - 1
        row_ref = half if self.reverse else half - 1
        cum = self.cum_ref[d, :, sl]
        last = cum[row_last:row_last + 1, :]
        ref = cum[row_ref:row_ref + 1, :]
        kk = self.kk_ref[d, :, sl]
        qs = self.q_ref[0, 0, :, sl].astype(F32)
        self.qd_ref[d, :, sl] = (qs * jnp.exp2(jnp.minimum(cum - ref, EXP2_CLAMP))).astype(BF16)
        self.kd_ref[d, :, sl] = (kk * jnp.exp2(jnp.minimum(ref - cum, EXP2_CLAMP))).astype(BF16)
        self.qin_ref[d, :, sl] = (qs * jnp.exp2(cum)).astype(BF16)
        self.kout_ref[d, :, sl] = (kk * jnp.exp2(last - cum)).astype(BF16)
        self.dec_ref[d, :, sl] = jnp.exp2(last)

    def _head(self, h):
        return slice(h * self.hd, (h + 1) * self.hd)

    def _scores(self, h):
        d, sl = self.d, self._head(h)
        s = lax.dot_general(self.qd_ref[d, :, sl], self.kd_ref[d, :, sl], _NT,
                            preferred_element_type=F32)
        self.sc_ref[d, h] = jnp.where(self.tri, s, 0.0).astype(BF16)

    def _output(self, h):
        d, sl = self.d, self._head(h)
        o = (jnp.dot(self.sc_ref[d, h], self.v_ref[0, 0, :, sl], preferred_element_type=F32)
             + lax.dot_general(self.qin_ref[d, :, sl], self.st_ref[h].astype(BF16), _NT,
                               preferred_element_type=F32))
        self.o_ref[0, 0, :, sl] = o.astype(self.o_ref.dtype)

    def _update(self, h):
        d, sl = self.d, self._head(h)
        self.st_ref[h] = (self.st_ref[h] * self.dec_ref[d, :, sl]
                          + lax.dot_general(self.v_ref[0, 0, :, sl], self.kout_ref[d, :, sl], _TN,
                                            preferred_element_type=F32))

    def stages(self):
        heads = range(self.n_heads)
        slabs = self._slabs(2 * LANES)
        out = [[functools.partial(self._gate_slab, sl) for sl in slabs], [self._cumulate],
               [functools.partial(self._decay_slab, sl) for sl in slabs]]
        if self.o_ref is not None:
            out += [[functools.partial(self._scores, h) for h in heads],
                    [functools.partial(self._output, h) for h in heads]]
        return out + [[functools.partial(self._update, h) for h in heads]]


def _lower_bounds(logits_ref, layer):
    out = []
    for d in range(2):
        rows = [logits_ref[d, l:l + 1, :] for l in range(logits_ref.shape[1])]
        m = functools.reduce(jnp.maximum, rows)
        e = [jnp.exp(r - m) for r in rows]
        out.append(sum(e[:layer + 1]) / sum(e))
    return out


def _hgrn_kernel(*refs, n_heads, hd, has_init, emit_o, emit_state):
    qf_ref, ff_ref, vf_ref, qb_ref, fb_ref, vb_ref, lg_ref = refs[:7]
    pos = 7
    if has_init:
        s0f_ref, s0b_ref = refs[pos:pos + 2]
        pos += 2
    if emit_o:
        of_ref, ob_ref = refs[pos:pos + 2]
        pos += 2
    if emit_state:
        sof_ref, sob_ref = refs[pos:pos + 2]
        pos += 2
    sf_ref, sb_ref = refs[pos:pos + 2]
    scratch = refs[pos + 2:]

    j = pl.program_id(1)

    @pl.when(j == 0)
    def _():
        if has_init:
            sf_ref[...] = s0f_ref[0]
            sb_ref[...] = s0b_ref[0]
        else:
            sf_ref[...] = jnp.zeros_like(sf_ref)
            sb_ref[...] = jnp.zeros_like(sb_ref)

    lb_f, lb_b = _lower_bounds(lg_ref, 0)
    dirs = [_HgrnDir(0, qf_ref, ff_ref, vf_ref, lb_f, sf_ref, of_ref if emit_o else None,
                     scratch, False, n_heads, hd),
            _HgrnDir(1, qb_ref, fb_ref, vb_ref, lb_b, sb_ref, ob_ref if emit_o else None,
                     scratch, True, n_heads, hd)]
    for stage_f, stage_b in zip(dirs[0].stages(), dirs[1].stages()):
        for unit in stage_f + stage_b:
            unit()

    if emit_state:
        @pl.when(j == pl.num_programs(1) - 1)
        def _():
            sof_ref[0] = sf_ref[...]
            sob_ref[0] = sb_ref[...]


def _hgrn(p_view, logits, d_b, n_heads, first_group, init_states, emit_o):
    bsz, n_chunks = p_view.shape[:2]
    hd = d_b // n_heads
    has_init = init_states is not None
    emit_state = not emit_o

    def chunk(j, reverse):
        return n_chunks - 1 - j if reverse else j

    def spec(group, reverse):
        def imap(b, j):
            return (b, chunk(j, reverse), 0, first_group + group)
        return pl.BlockSpec((1, 1, CHUNK, d_b), imap)

    in_specs = [spec(0, False), spec(1, False), spec(3, False),
                spec(0, True), spec(2, True), spec(3, True),
                pl.BlockSpec(logits.shape, lambda b, j: (0, 0, 0))]
    args = [p_view] * 6 + [logits]
    state_spec = pl.BlockSpec((1, n_heads, hd, hd), lambda b, j: (b, 0, 0, 0))
    state_shape = jax.ShapeDtypeStruct((bsz, n_heads, hd, hd), F32)
    if has_init:
        in_specs += [state_spec, state_spec]
        args += list(init_states)
    out_specs, out_shape = [], []
    if emit_o:
        o_shape = jax.ShapeDtypeStruct((bsz, n_chunks, CHUNK, d_b), BF16)
        out_specs += [pl.BlockSpec((1, 1, CHUNK, d_b), lambda b, j: (b, chunk(j, False), 0, 0)),
                      pl.BlockSpec((1, 1, CHUNK, d_b), lambda b, j: (b, chunk(j, True), 0, 0))]
        out_shape += [o_shape, o_shape]
    if emit_state:
        out_specs += [state_spec, state_spec]
        out_shape += [state_shape, state_shape]
    kern = functools.partial(_hgrn_kernel, n_heads=n_heads, hd=hd, has_init=has_init,
                             emit_o=emit_o, emit_state=emit_state)
    return pl.pallas_call(
        kern,
        grid=(bsz, n_chunks),
        in_specs=in_specs,
        out_specs=out_specs,
        out_shape=out_shape,
        scratch_shapes=[pltpu.VMEM((n_heads, hd, hd), F32),
                        pltpu.VMEM((n_heads, hd, hd), F32),
                        pltpu.VMEM((2, 2 * CHUNK, d_b), BF16),
                        pltpu.VMEM((2, CHUNK, d_b), F32),
                        pltpu.VMEM((2, CHUNK, d_b), F32),
                        pltpu.VMEM((2, CHUNK, d_b), BF16),
                        pltpu.VMEM((2, CHUNK, d_b), BF16),
                        pltpu.VMEM((2, CHUNK, d_b), BF16),
                        pltpu.VMEM((2, CHUNK, d_b), BF16),
                        pltpu.VMEM((2, 1, d_b), F32),
                        pltpu.VMEM((2, n_heads, CHUNK, CHUNK), BF16)],
        compiler_params=pltpu.CompilerParams(
            dimension_semantics=("parallel", "arbitrary"),
            vmem_limit_bytes=VMEM_LIMIT),
        name="hgrn2_latent" if emit_o else "hgrn2_context",
    )(*args)


def _outproj_kernel(ya_ref, of_ref, ob_ref, gb_ref, x_ref, gate_ref, hnw_ref, fnw_ref, w_ref,
                    o_ref, y_ref, *, n_heads, hd):
    _, rb, wb, d_a = ya_ref.shape
    rows = rb * wb
    d = x_ref.shape[3]
    y_ref[:, 0:d_a] = ya_ref[0].reshape(rows, d_a)
    hnw = hnw_ref[...]
    for h in range(n_heads):
        sl = slice(h * hd, (h + 1) * hd)
        o = of_ref[0, :, :, sl].astype(F32) + ob_ref[0, :, :, sl].astype(F32)
        ms = jnp.mean(o * o, axis=-1, keepdims=True)
        on = jnp.transpose(o * lax.rsqrt(ms + EPS) * hnw, (1, 0, 2))
        yb = on * gb_ref[0, :, :, sl].astype(F32)
        y_ref[:, d_a + h * hd:d_a + (h + 1) * hd] = yb.reshape(rows, hd).astype(BF16)
    acc = jnp.dot(y_ref[...], w_ref[...], preferred_element_type=F32)
    z = x_ref[0].reshape(rows, d) + gate_ref[0] * acc
    ms = jnp.mean(z * z, axis=-1, keepdims=True)
    o_ref[0] = (z * lax.rsqrt(ms + EPS) * fnw_ref[...]).reshape(rb, wb, d)


def _out_projection(ya4, of4, ob4, pa4, gb_block, x4, mod3, hnw, fnw, w_bf16, n_heads, rb, wb):
    bsz, n_r, n_w, d = x4.shape
    d_a = ya4.shape[3]
    d_b = of4.shape[3]
    hd = d_b // n_heads
    kern = functools.partial(_outproj_kernel, n_heads=n_heads, hd=hd)

    def raster(c, col=0):
        return pl.BlockSpec((1, rb, wb, c), lambda b, r, w: (b, r, w, col))

    def colmajor(c):
        return pl.BlockSpec((1, wb, rb, c), lambda b, r, w: (b, w, r, 0))

    return pl.pallas_call(
        kern,
        grid=(bsz, n_r // rb, n_w // wb),
        in_specs=[raster(d_a), colmajor(d_b), colmajor(d_b), raster(d_b, gb_block), raster(d),
                  pl.BlockSpec((1, 1, d), lambda b, r, w: (b, 0, 2)),
                  pl.BlockSpec((1, hd), lambda b, r, w: (0, 0)),
                  pl.BlockSpec((1, d), lambda b, r, w: (0, 0)),
                  pl.BlockSpec((d_a + d_b, d), lambda b, r, w: (0, 0),
                               pipeline_mode=pl.Buffered(1))],
        out_specs=raster(d),
        out_shape=jax.ShapeDtypeStruct((bsz, n_r, n_w, d), F32),
        scratch_shapes=[pltpu.VMEM((rb * wb, d_a + d_b), BF16)],
        compiler_params=pltpu.CompilerParams(
            dimension_semantics=("parallel", "parallel", "parallel"),
            vmem_limit_bytes=VMEM_LIMIT),
        name="out_projection",
    )(ya4, of4, ob4, pa4, x4, mod3, hnw.reshape(1, hd), fnw.reshape(1, d), w_bf16)


def kernel(x, c, ctx, c_ctx, ada_w, ada_b, norm_w, w_in, conv_w, conv_b, lru_wr, lru_br, lru_wi,
           lru_bi, lru_lambda, hgrn_lb_logits, hgrn_norm_w, w_out, final_norm_w):
    bsz, t_len, d = x.shape
    tc_len = ctx.shape[1]
    assert ada_w.shape[0] == 1, "single-layer stack only"
    d_a = conv_w.shape[2]
    d_b = hgrn_lb_logits.shape[2]
    hd = hgrn_norm_w.shape[1]
    n_heads = d_b // hd
    n_blocks_a = lru_wr.shape[2]
    n_cols = w_in.shape[2]
    assert t_len == GRID_W * CHUNK and tc_len % CHUNK == 0
    assert d_a // n_blocks_a == LANES and hd == LANES
    assert (2 * d_a) % d_b == 0 and n_cols == 2 * d_a + 5 * d_b
    first_b_group = (2 * d_a) // d_b
    n_rows_grid = t_len // GRID_W

    n_rows = -(-(bsz + 1) // SUBLANES) * SUBLANES
    cc = jnp.zeros((n_rows, d), F32).at[:bsz].set(c).at[bsz].set(c_ctx)
    mod3 = _modulation(cc, ada_w[0], ada_b[0]).reshape(n_rows, 1, 3 * d)

    w_in_b = w_in[0]
    w_out_b = w_out[0].astype(BF16)

    tm = 1024 if t_len % 1024 == 0 else t_len
    tn = 1024 if d_a % 1024 == 0 and d_b % 1024 == 0 else min(d_a, d_b)
    tpb = t_len // tm
    a_tiles = (2 * d_a) // tn
    b_tiles = (4 * d_b) // tn
    q_tiles = d_b // tn
    p_a = _in_projection(x.reshape(bsz * t_len, d), mod3, norm_w[0], w_in_b,
                         lambda i: i // tpb, lambda j: jnp.where(j < a_tiles, j, j + b_tiles),
                         2 * d_a + d_b, tm, tn, (d_a // tn, (2 * d_a + d_b) // tn))
    wb = 16
    p_b = _in_projection_colmajor(x.reshape(bsz, n_rows_grid, GRID_W, d), mod3, norm_w[0], w_in_b,
                                  lambda j: j + a_tiles, 4 * d_b, wb, tn, (0, q_tiles))
    p_ctx = _in_projection(ctx.reshape(bsz * tc_len, d), mod3, norm_w[0], w_in_b,
                           lambda i: bsz, lambda j: j, n_cols, bsz * tc_len, tn,
                           (a_tiles, a_tiles + q_tiles))
    p_lat = p_a.reshape(bsz, t_len, 2 * d_a + d_b)
    p_ctx = p_ctx.reshape(bsz, tc_len, n_cols)

    def gate_w(dirn):
        return jnp.concatenate([lru_wr[0, dirn], lru_wi[0, dirn]], axis=-1).astype(BF16)

    def gate_b(dirn):
        return jnp.concatenate([lru_br[0, dirn].reshape(n_blocks_a, 1, LANES),
                                lru_bi[0, dirn].reshape(n_blocks_a, 1, LANES)], axis=-1)

    cb = 256 if d_a % 256 == 0 else LANES
    ya = _rglru(p_lat, p_ctx, conv_w[0], conv_b[0], gate_w(0), gate_b(0), gate_w(1), gate_b(1),
                lru_lambda[0], d_a, cb, tt=128)

    states = _hgrn(p_ctx.reshape(bsz, tc_len // CHUNK, CHUNK, n_cols), hgrn_lb_logits, d_b,
                   n_heads, first_b_group, None, False)
    o_f, o_b = _hgrn(p_b, hgrn_lb_logits, d_b, n_heads, 0, states, True)

    grid4 = lambda z: z.reshape(bsz, n_rows_grid, GRID_W, z.shape[-1])
    out = _out_projection(grid4(ya), o_f, o_b, grid4(p_lat), first_b_group, grid4(x), mod3,
                          hgrn_norm_w[0], final_norm_w, w_out_b, n_heads, 16, 16)
    return out.reshape(bsz, t_len, d)
```

```python
import functools

import jax
import jax.numpy as jnp
from jax import lax
from jax.experimental import pallas as pl
from jax.experimental.pallas import tpu as pltpu

GRID_W = 64
CHUNK = 64
LRU_C = 8.0
EPS = 1e-6
CONV_PAD_L = 2
LANES = 128
SUBLANES = 8
EXP2_CLAMP = 115.0
NEG_LOG2E = -1.4426950408889634
VMEM_LIMIT = 56 * 1024 * 1024
PROLOGUE_ROWS = 16
SEG_PAD = 4
TILES_PER_TRIP = 4

F32 = jnp.float32
BF16 = jnp.bfloat16


def _sigmoid(z):
    return 1.0 / (1.0 + jnp.exp2(z * NEG_LOG2E))


def _silu(z):
    return z * _sigmoid(z)


def _softplus(z):
    return jnp.maximum(z, 0.0) + jnp.log1p(jnp.exp(-jnp.abs(z)))


def _mod_kernel(c_ref, w_ref, b_ref, o_ref):
    s = _silu(c_ref[...])
    o_ref[...] = jnp.dot(s.astype(BF16), w_ref[...].astype(BF16),
                         preferred_element_type=F32) + b_ref[...]


def _modulation(cc, w, b):
    rows, d = cc.shape
    n = w.shape[1]
    tn = 512 if n % 512 == 0 else n
    return pl.pallas_call(
        _mod_kernel,
        grid=(n // tn,),
        in_specs=[pl.BlockSpec((rows, d), lambda j: (0, 0)),
                  pl.BlockSpec((d, tn), lambda j: (0, j)),
                  pl.BlockSpec((1, tn), lambda j: (0, j))],
        out_specs=pl.BlockSpec((rows, tn), lambda j: (0, j)),
        out_shape=jax.ShapeDtypeStruct((rows, n), F32),
        compiler_params=pltpu.CompilerParams(
            dimension_semantics=("arbitrary",), vmem_limit_bytes=VMEM_LIMIT),
        name="adaln_modulation",
    )(cc, w, b.reshape(1, n))


def _store_projection(o_ref, res, j, silu_tiles):
    lo, hi = silu_tiles
    z = res.astype(o_ref.dtype)
    if hi <= lo:
        o_ref[...] = z
        return
    in_range = jnp.logical_and(j >= lo, j < hi)
    o_ref[...] = jnp.where(in_range, z * (0.5 * jnp.tanh(0.5 * z) + 0.5), z)


def _inproj_kernel(x_ref, shift_ref, scale_ref, nw_ref, w_ref, o_ref, h_ref, *, silu_tiles):
    @pl.when(pl.program_id(1) == 0)
    def _():
        rows = min(PROLOGUE_ROWS, x_ref.shape[0])
        gain = nw_ref[...] * (1.0 + scale_ref[0])

        def slab(s, carry):
            sl = pl.ds(pl.multiple_of(s * rows, rows), rows)
            x = x_ref[sl, :]
            rs = lax.rsqrt(jnp.mean(x * x, axis=-1, keepdims=True) + EPS)
            h_ref[sl, :] = (x_ref[sl, :] * rs * gain + shift_ref[0]).astype(BF16)
            return carry

        lax.fori_loop(0, x_ref.shape[0] // rows, slab, 0, unroll=4)

    res = jnp.dot(h_ref[...], w_ref[...].astype(BF16), preferred_element_type=F32)
    _store_projection(o_ref, res, pl.program_id(1), silu_tiles)


def _in_projection(x2d, mod3, norm_w, w_bf16, row_of_tile, wcol_of_tile, n_out, tm, tn, silu_tiles):
    m, d = x2d.shape
    return pl.pallas_call(
        functools.partial(_inproj_kernel, silu_tiles=silu_tiles),
        grid=(m // tm, n_out // tn),
        in_specs=[pl.BlockSpec((tm, d), lambda i, j: (i, 0)),
                  pl.BlockSpec((1, 1, d), lambda i, j: (row_of_tile(i), 0, 0)),
                  pl.BlockSpec((1, 1, d), lambda i, j: (row_of_tile(i), 0, 1)),
                  pl.BlockSpec((1, d), lambda i, j: (0, 0)),
                  pl.BlockSpec((d, tn), lambda i, j: (0, wcol_of_tile(j)))],
        out_specs=pl.BlockSpec((tm, tn), lambda i, j: (i, j)),
        out_shape=jax.ShapeDtypeStruct((m, n_out), BF16),
        scratch_shapes=[pltpu.VMEM((tm, d), BF16)],
        compiler_params=pltpu.CompilerParams(
            dimension_semantics=("parallel", "arbitrary"),
            vmem_limit_bytes=VMEM_LIMIT),
        name="in_projection",
    )(x2d, mod3, mod3, norm_w.reshape(1, d), w_bf16)


def _inproj_colmajor_kernel(x_ref, shift_ref, scale_ref, nw_ref, w_ref, o_ref, h_ref, *, slab,
                            silu_tiles):
    _, n_r, n_w, d = x_ref.shape

    rr = 2 * SUBLANES
    lanes = [slice(s * slab, (s + 1) * slab) for s in range(d // slab)]

    @pl.when(pl.program_id(2) == 0)
    def _():
        def row_group(g, carry):
            rows = pl.ds(pl.multiple_of(g * rr, rr), rr)
            sq = jnp.zeros((rr, n_w, slab), F32)
            for sl in lanes:
                xs = x_ref[0, rows, :, sl]
                sq = sq + xs * xs
            rs = lax.rsqrt(jnp.sum(sq, axis=-1, keepdims=True) * (1.0 / d) + EPS)
            for sl in lanes:
                y = x_ref[0, rows, :, sl] * rs * nw_ref[:, sl]
                y = y * (1.0 + scale_ref[0][:, sl]) + shift_ref[0][:, sl]
                h_ref[:, rows, sl] = jnp.transpose(y, (1, 0, 2)).astype(BF16)
            return carry

        lax.fori_loop(0, n_r // rr, row_group, 0)

    res = jnp.dot(h_ref[...].reshape(n_w * n_r, d), w_ref[...].astype(BF16),
                  preferred_element_type=F32)
    _store_projection(o_ref, res.reshape(1, n_w, n_r, res.shape[1]), pl.program_id(2), silu_tiles)


def _in_projection_colmajor(x4d, mod3, norm_w, w_bf16, wcol_of_tile, n_out, wb, tn, silu_tiles):
    bsz, n_r, n_w, d = x4d.shape
    kern = functools.partial(_inproj_colmajor_kernel, slab=2 * LANES, silu_tiles=silu_tiles)
    return pl.pallas_call(
        kern,
        grid=(bsz, n_w // wb, n_out // tn),
        in_specs=[pl.BlockSpec((1, n_r, wb, d), lambda b, w, j: (b, 0, w, 0)),
                  pl.BlockSpec((1, 1, d), lambda b, w, j: (b, 0, 0)),
                  pl.BlockSpec((1, 1, d), lambda b, w, j: (b, 0, 1)),
                  pl.BlockSpec((1, d), lambda b, w, j: (0, 0)),
                  pl.BlockSpec((d, tn), lambda b, w, j: (0, wcol_of_tile(j)))],
        out_specs=pl.BlockSpec((1, wb, n_r, tn), lambda b, w, j: (b, w, 0, j)),
        out_shape=jax.ShapeDtypeStruct((bsz, n_w, n_r, n_out), BF16),
        scratch_shapes=[pltpu.VMEM((wb, n_r, d), BF16)],
        compiler_params=pltpu.CompilerParams(
            dimension_semantics=("parallel", "parallel", "arbitrary"),
            vmem_limit_bytes=VMEM_LIMIT),
        name="in_projection_colmajor",
    )(x4d, mod3, mod3, norm_w.reshape(1, d), w_bf16)


def _group_scan(a, b, reverse):
    row = lax.broadcasted_iota(jnp.int32, a.shape, 1)
    for k in (1, 2, 4):
        if reverse:
            a_sh = pltpu.roll(a, SUBLANES - k, axis=1)
            b_sh = pltpu.roll(b, SUBLANES - k, axis=1)
            m = row < SUBLANES - k
        else:
            a_sh = pltpu.roll(a, k, axis=1)
            b_sh = pltpu.roll(b, k, axis=1)
            m = row >= k
        b = jnp.where(m, a * b_sh + b, b)
        a = jnp.where(m, a * a_sh, a)
    return a, b


def _sqrt_unit(x):
    return jnp.where(x > 0.0, x * lax.rsqrt(x), 0.0)


def _segment_scan(a, b, carry, reverse, scr_ref):
    tt = a.shape[0]
    seg = tt // SUBLANES
    pitch = seg + SEG_PAD
    for s in range(SUBLANES):
        rows = pl.ds(pitch * s, seg, stride=1)
        scr_ref[0, rows, :] = a[seg * s:seg * (s + 1), :]
        scr_ref[1, rows, :] = b[seg * s:seg * (s + 1), :]
    hs, ps = [None] * seg, [None] * seg
    h = p = None
    for j in (range(seg - 1, -1, -1) if reverse else range(seg)):
        step = pl.ds(j, SUBLANES, stride=pitch)
        a_j = scr_ref[0, step, :]
        b_j = scr_ref[1, step, :]
        h = b_j if h is None else a_j * h + b_j
        p = a_j if p is None else a_j * p
        hs[j], ps[j] = h, p
    g3, e3 = _group_scan(p[None], h[None], reverse)
    end = e3[0] + g3[0] * carry
    row = lax.broadcasted_iota(jnp.int32, end.shape, 0)
    if reverse:
        enter = jnp.where(row == SUBLANES - 1, carry, pltpu.roll(end, SUBLANES - 1, axis=0))
        carry_out = end[0:1, :]
    else:
        enter = jnp.where(row == 0, carry, pltpu.roll(end, 1, axis=0))
        carry_out = end[SUBLANES - 1:SUBLANES, :]
    for j in range(seg):
        scr_ref[2, pl.ds(j, SUBLANES, stride=pitch), :] = hs[j] + ps[j] * enter
    h_time = [scr_ref[2, pl.ds(pitch * s, seg, stride=1), :] for s in range(SUBLANES)]
    return jnp.concatenate(h_time, axis=0), carry_out


def _lru_block(u, wg, bg, sp, carry, reverse, scr_ref):
    g = jnp.dot(u.astype(BF16), wg, preferred_element_type=F32) + bg
    r = _sigmoid(g[:, :LANES])
    i = _sigmoid(g[:, LANES:])
    a = jnp.exp2(r * sp)
    b = _sqrt_unit(1.0 - a * a) * (i * u)
    return _segment_scan(a, b, carry, reverse, scr_ref)


def _rglru_kernel(xa_ref, ga_ref, xc_ref, cw_ref, cb_ref, wgf_ref, bgf_ref, wgb_ref, bgb_ref,
                  lam_ref, o_ref, xf_ref, xcf_ref, u_ref, hf_ref, scr_ref, *, tt):
    t_len = xa_ref.shape[1]
    tc_len = xc_ref.shape[1]
    nblk = xa_ref.shape[2] // LANES
    lanes = [slice(k * LANES, (k + 1) * LANES) for k in range(nblk)]
    zeros = jnp.zeros((SUBLANES, LANES), F32)
    for k, sl in enumerate(lanes):
        xf_ref[k, 0:SUBLANES, :] = zeros
        xf_ref[k, SUBLANES:SUBLANES + t_len, :] = xa_ref[0, :, sl].astype(F32)
        xf_ref[k, SUBLANES + t_len:, :] = zeros
        xcf_ref[k, 0:SUBLANES, :] = zeros
        xcf_ref[k, SUBLANES:SUBLANES + tc_len, :] = xc_ref[0, :, sl].astype(F32)
        xcf_ref[k, SUBLANES + tc_len:, :] = zeros

    sp_f = _softplus(-lam_ref[0:1, :]) * (LRU_C * NEG_LOG2E)
    sp_b = _softplus(-lam_ref[1:2, :]) * (LRU_C * NEG_LOG2E)

    def conv(src_ref, k, t0):
        u = cb_ref[:, lanes[k]]
        for tap in range(cw_ref.shape[0]):
            off = SUBLANES - CONV_PAD_L + tap
            if off % SUBLANES == 0:
                start = t0 + off
                rows = pl.ds(start if isinstance(start, int) else pl.multiple_of(start, SUBLANES), tt)
            else:
                rows = pl.ds(t0 + off, tt, stride=1)
            u = u + src_ref[k, rows, :] * cw_ref[tap:tap + 1, lanes[k]]
        return u

    def block(u, k, carry, reverse, slot):
        scr = scr_ref.at[k * TILES_PER_TRIP + slot]
        if reverse:
            return _lru_block(u, wgb_ref[k], bgb_ref[k], sp_b[:, lanes[k]], carry, True, scr)
        return _lru_block(u, wgf_ref[k], bgf_ref[k], sp_f[:, lanes[k]], carry, False, scr)

    n_lat = t_len // tt
    n_ctx = tc_len // tt
    assert n_lat % TILES_PER_TRIP == 0
    carry0 = tuple(jnp.zeros((1, LANES), F32) for _ in lanes)

    def tile_start(trip, slot, n_tiles, reverse):
        idx = trip * TILES_PER_TRIP + slot
        return pl.multiple_of((n_tiles - 1 - idx if reverse else idx) * tt, tt)

    def ctx_pass(reverse):
        carry = carry0
        for idx in range(n_ctx):
            t0 = (n_ctx - 1 - idx if reverse else idx) * tt
            carry = tuple(block(conv(xcf_ref, k, t0), k, carry[k], reverse, idx % TILES_PER_TRIP)[1]
                          for k in range(nblk))
        return carry

    def lat_f(trip, carry):
        for slot in range(TILES_PER_TRIP):
            t0 = tile_start(trip, slot, n_lat, False)
            rows = pl.ds(t0, tt)
            out = []
            for k, sl in enumerate(lanes):
                u = conv(xf_ref, k, t0)
                u_ref[rows, sl] = u
                h, c_out = block(u, k, carry[k], False, slot)
                hf_ref[rows, sl] = h
                out.append(c_out)
            carry = tuple(out)
        return carry

    lax.fori_loop(0, n_lat // TILES_PER_TRIP, lat_f, ctx_pass(False))

    def lat_b(trip, carry):
        for slot in range(TILES_PER_TRIP):
            rows = pl.ds(tile_start(trip, slot, n_lat, True), tt)
            out = []
            for k, sl in enumerate(lanes):
                h, c_out = block(u_ref[rows, sl], k, carry[k], True, slot)
                gate = ga_ref[0, rows, sl].astype(F32)
                o_ref[0, rows, sl] = ((hf_ref[rows, sl] + h) * gate).astype(o_ref.dtype)
                out.append(c_out)
            carry = tuple(out)
        return carry

    lax.fori_loop(0, n_lat // TILES_PER_TRIP, lat_b, ctx_pass(True))


def _rglru(p_lat, p_ctx, conv_w, conv_b, wg_f, bg_f, wg_b, bg_b, lam, d_a, cb, tt):
    bsz, t_len, _ = p_lat.shape
    tc_len = p_ctx.shape[1]
    nblk = cb // LANES
    ncb = d_a // cb
    kern = functools.partial(_rglru_kernel, tt=tt)
    return pl.pallas_call(
        kern,
        grid=(bsz, ncb),
        in_specs=[pl.BlockSpec((1, t_len, cb), lambda b, c: (b, 0, c)),
                  pl.BlockSpec((1, t_len, cb), lambda b, c: (b, 0, ncb + c)),
                  pl.BlockSpec((1, tc_len, cb), lambda b, c: (b, 0, c)),
                  pl.BlockSpec((conv_w.shape[0], cb), lambda b, c: (0, c)),
                  pl.BlockSpec((1, cb), lambda b, c: (0, c)),
                  pl.BlockSpec((nblk, LANES, 2 * LANES), lambda b, c: (c, 0, 0)),
                  pl.BlockSpec((nblk, 1, 2 * LANES), lambda b, c: (c, 0, 0)),
                  pl.BlockSpec((nblk, LANES, 2 * LANES), lambda b, c: (c, 0, 0)),
                  pl.BlockSpec((nblk, 1, 2 * LANES), lambda b, c: (c, 0, 0)),
                  pl.BlockSpec((2, cb), lambda b, c: (0, c))],
        out_specs=pl.BlockSpec((1, t_len, cb), lambda b, c: (b, 0, c)),
        out_shape=jax.ShapeDtypeStruct((bsz, t_len, d_a), BF16),
        scratch_shapes=[pltpu.VMEM((nblk, t_len + 2 * SUBLANES, LANES), F32),
                        pltpu.VMEM((nblk, tc_len + 2 * SUBLANES, LANES), F32),
                        pltpu.VMEM((t_len, cb), F32),
                        pltpu.VMEM((t_len, cb), F32),
                        pltpu.VMEM((nblk * TILES_PER_TRIP, 3, tt + SUBLANES * SEG_PAD, LANES),
                                   F32)],
        compiler_params=pltpu.CompilerParams(
            dimension_semantics=("parallel", "parallel"),
            vmem_limit_bytes=VMEM_LIMIT),
        name="rglru",
    )(p_lat, p_lat, p_ctx, conv_w, conv_b.reshape(1, d_a), wg_f, bg_f, wg_b, bg_b, lam)


_NT = (((1,), (1,)), ((), ()))
_TN = (((0,), (0,)), ((), ()))


class _HgrnDir:
    def __init__(self, d, q_ref, f_ref, v_ref, lb, st_ref, o_ref, scratch, reverse, n_heads, hd):
        self.d, self.q_ref, self.f_ref, self.v_ref, self.lb = d, q_ref, f_ref, v_ref, lb
        self.st_ref, self.o_ref, self.reverse, self.n_heads, self.hd = st_ref, o_ref, reverse, n_heads, hd
        (self.hl_ref, self.kk_ref, self.cum_ref, self.qd_ref, self.kd_ref, self.qin_ref,
         self.kout_ref, self.dec_ref, self.sc_ref) = scratch
        self.c_len = q_ref.shape[2]
        r_i = lax.broadcasted_iota(jnp.int32, (self.c_len, self.c_len), 0)
        c_i = lax.broadcasted_iota(jnp.int32, (self.c_len, self.c_len), 1)
        self.tri = (r_i <= c_i) if reverse else (r_i >= c_i)

    def _slabs(self, width):
        total = self.n_heads * self.hd
        return [slice(s, s + width) for s in range(0, total, width)]


    def _gate_slab(self, sl):
        d, c_len = self.d, self.c_len
        lb = self.lb[:, sl]
        f = lb + (1.0 - lb) * _sigmoid(self.f_ref[0, 0, :, sl].astype(F32))
        logf = jnp.log2(f)
        self.kk_ref[d, :, sl] = 1.0 - f
        hi = logf.astype(BF16)
        self.hl_ref[d, 0:c_len, sl] = hi
        self.hl_ref[d, c_len:2 * c_len, sl] = (logf - hi.astype(F32)).astype(BF16)

    def _cumulate(self):
        tri_b = self.tri.astype(BF16)
        tri2 = jnp.concatenate([tri_b, tri_b], axis=1)
        self.cum_ref[self.d] = jnp.dot(tri2, self.hl_ref[self.d], preferred_element_type=F32)

    def _decay_slab(self, sl):
        d, c_len = self.d, self.c_len
        half = c_len // 2
        row_last = 0 if self.reverse else c_len - 1
        row_ref = half if self.reverse else half - 1
        cum = self.cum_ref[d, :, sl]
        last = cum[row_last:row_last + 1, :]
        ref = cum[row_ref:row_ref + 1, :]
        kk = self.kk_ref[d, :, sl]
        qs = self.q_ref[0, 0, :, sl].astype(F32)
        self.qd_ref[d, :, sl] = (qs * jnp.exp2(jnp.minimum(cum - ref, EXP2_CLAMP))).astype(BF16)
        self.kd_ref[d, :, sl] = (kk * jnp.exp2(jnp.minimum(ref - cum, EXP2_CLAMP))).astype(BF16)
        self.qin_ref[d, :, sl] = (qs * jnp.exp2(cum)).astype(BF16)
        self.kout_ref[d, :, sl] = (kk * jnp.exp2(last - cum)).astype(BF16)
        self.dec_ref[d, :, sl] = jnp.exp2(last)

    def _head(self, h):
        return slice(h * self.hd, (h + 1) * self.hd)

    def _scores(self, h):
        d, sl = self.d, self._head(h)
        s = lax.dot_general(self.qd_ref[d, :, sl], self.kd_ref[d, :, sl], _NT,
                            preferred_element_type=F32)
        self.sc_ref[d, h] = jnp.where(self.tri, s, 0.0).astype(BF16)

    def _output(self, h):
        d, sl = self.d, self._head(h)
        o = (jnp.dot(self.sc_ref[d, h], self.v_ref[0, 0, :, sl], preferred_element_type=F32)
             + lax.dot_general(self.qin_ref[d, :, sl], self.st_ref[h].astype(BF16), _NT,
                               preferred_element_type=F32))
        self.o_ref[0, 0, :, sl] = o.astype(self.o_ref.dtype)

    def _update(self, h):
        d, sl = self.d, self._head(h)
        self.st_ref[h] = (self.st_ref[h] * self.dec_ref[d, :, sl]
                          + lax.dot_general(self.v_ref[0, 0, :, sl], self.kout_ref[d, :, sl], _TN,
                                            preferred_element_type=F32))

    def stages(self):
        heads = range(self.n_heads)
        slabs = self._slabs(2 * LANES)
        out = [[functools.partial(self._gate_slab, sl) for sl in slabs], [self._cumulate],
               [functools.partial(self._decay_slab, sl) for sl in slabs]]
        if self.o_ref is not None:
            out += [[functools.partial(self._scores, h) for h in heads],
                    [functools.partial(self._output, h) for h in heads]]
        return out + [[functools.partial(self._update, h) for h in heads]]


def _lower_bounds(logits_ref, layer):
    out = []
    for d in range(2):
        rows = [logits_ref[d, l:l + 1, :] for l in range(logits_ref.shape[1])]
        m = functools.reduce(jnp.maximum, rows)
        e = [jnp.exp(r - m) for r in rows]
        out.append(sum(e[:layer + 1]) / sum(e))
    return out


def _hgrn_kernel(*refs, n_heads, hd, has_init, emit_o, emit_state):
    qf_ref, ff_ref, vf_ref, qb_ref, fb_ref, vb_ref, lg_ref = refs[:7]
    pos = 7
    if has_init:
        s0f_ref, s0b_ref = refs[pos:pos + 2]
        pos += 2
    if emit_o:
        of_ref, ob_ref = refs[pos:pos + 2]
        pos += 2
    if emit_state:
        sof_ref, sob_ref = refs[pos:pos + 2]
        pos += 2
    sf_ref, sb_ref = refs[pos:pos + 2]
    scratch = refs[pos + 2:]

    j = pl.program_id(1)

    @pl.when(j == 0)
    def _():
        if has_init:
            sf_ref[...] = s0f_ref[0]
            sb_ref[...] = s0b_ref[0]
        else:
            sf_ref[...] = jnp.zeros_like(sf_ref)
            sb_ref[...] = jnp.zeros_like(sb_ref)

    lb_f, lb_b = _lower_bounds(lg_ref, 0)
    dirs = [_HgrnDir(0, qf_ref, ff_ref, vf_ref, lb_f, sf_ref, of_ref if emit_o else None,
                     scratch, False, n_heads, hd),
            _HgrnDir(1, qb_ref, fb_ref, vb_ref, lb_b, sb_ref, ob_ref if emit_o else None,
                     scratch, True, n_heads, hd)]
    for stage_f, stage_b in zip(dirs[0].stages(), dirs[1].stages()):
        for unit in stage_f + stage_b:
            unit()

    if emit_state:
        @pl.when(j == pl.num_programs(1) - 1)
        def _():
            sof_ref[0] = sf_ref[...]
            sob_ref[0] = sb_ref[...]


def _hgrn(p_view, logits, d_b, n_heads, first_group, init_states, emit_o):
    bsz, n_chunks = p_view.shape[:2]
    hd = d_b // n_heads
    has_init = init_states is not None
    emit_state = not emit_o

    def chunk(j, reverse):
        return n_chunks - 1 - j if reverse else j

    def spec(group, reverse):
        def imap(b, j):
            return (b, chunk(j, reverse), 0, first_group + group)
        return pl.BlockSpec((1, 1, CHUNK, d_b), imap)

    in_specs = [spec(0, False), spec(1, False), spec(3, False),
                spec(0, True), spec(2, True), spec(3, True),
                pl.BlockSpec(logits.shape, lambda b, j: (0, 0, 0))]
    args = [p_view] * 6 + [logits]
    state_spec = pl.BlockSpec((1, n_heads, hd, hd), lambda b, j: (b, 0, 0, 0))
    state_shape = jax.ShapeDtypeStruct((bsz, n_heads, hd, hd), F32)
    if has_init:
        in_specs += [state_spec, state_spec]
        args += list(init_states)
    out_specs, out_shape = [], []
    if emit_o:
        o_shape = jax.ShapeDtypeStruct((bsz, n_chunks, CHUNK, d_b), BF16)
        out_specs += [pl.BlockSpec((1, 1, CHUNK, d_b), lambda b, j: (b, chunk(j, False), 0, 0)),
                      pl.BlockSpec((1, 1, CHUNK, d_b), lambda b, j: (b, chunk(j, True), 0, 0))]
        out_shape += [o_shape, o_shape]
    if emit_state:
        out_specs += [state_spec, state_spec]
        out_shape += [state_shape, state_shape]
    kern = functools.partial(_hgrn_kernel, n_heads=n_heads, hd=hd, has_init=has_init,
                             emit_o=emit_o, emit_state=emit_state)
    return pl.pallas_call(
        kern,
        grid=(bsz, n_chunks),
        in_specs=in_specs,
        out_specs=out_specs,
        out_shape=out_shape,
        scratch_shapes=[pltpu.VMEM((n_heads, hd, hd), F32),
                        pltpu.VMEM((n_heads, hd, hd), F32),
                        pltpu.VMEM((2, 2 * CHUNK, d_b), BF16),
                        pltpu.VMEM((2, CHUNK, d_b), F32),
                        pltpu.VMEM((2, CHUNK, d_b), F32),
                        pltpu.VMEM((2, CHUNK, d_b), BF16),
                        pltpu.VMEM((2, CHUNK, d_b), BF16),
                        pltpu.VMEM((2, CHUNK, d_b), BF16),
                        pltpu.VMEM((2, CHUNK, d_b), BF16),
                        pltpu.VMEM((2, 1, d_b), F32),
                        pltpu.VMEM((2, n_heads, CHUNK, CHUNK), BF16)],
        compiler_params=pltpu.CompilerParams(
            dimension_semantics=("parallel", "arbitrary"),
            vmem_limit_bytes=VMEM_LIMIT),
        name="hgrn2_latent" if emit_o else "hgrn2_context",
    )(*args)


def _outproj_kernel(ya_ref, of_ref, ob_ref, gb_ref, x_ref, gate_ref, hnw_ref, fnw_ref, w_ref,
                    o_ref, y_ref, *, n_heads, hd):
    _, rb, wb, d_a = ya_ref.shape
    rows = rb * wb
    d = x_ref.shape[3]
    y_ref[:, 0:d_a] = ya_ref[0].reshape(rows, d_a)
    hnw = hnw_ref[...]
    for h in range(n_heads):
        sl = slice(h * hd, (h + 1) * hd)
        o = of_ref[0, :, :, sl].astype(F32) + ob_ref[0, :, :, sl].astype(F32)
        ms = jnp.mean(o * o, axis=-1, keepdims=True)
        on = jnp.transpose(o * lax.rsqrt(ms + EPS) * hnw, (1, 0, 2))
        yb = on * gb_ref[0, :, :, sl].astype(F32)
        y_ref[:, d_a + h * hd:d_a + (h + 1) * hd] = yb.reshape(rows, hd).astype(BF16)
    acc = jnp.dot(y_ref[...], w_ref[...], preferred_element_type=F32)
    z = x_ref[0].reshape(rows, d) + gate_ref[0] * acc
    ms = jnp.mean(z * z, axis=-1, keepdims=True)
    o_ref[0] = (z * lax.rsqrt(ms + EPS) * fnw_ref[...]).reshape(rb, wb, d)


def _out_projection(ya4, of4, ob4, pa4, gb_block, x4, mod3, hnw, fnw, w_bf16, n_heads, rb, wb):
    bsz, n_r, n_w, d = x4.shape
    d_a = ya4.shape[3]
    d_b = of4.shape[3]
    hd = d_b // n_heads
    kern = functools.partial(_outproj_kernel, n_heads=n_heads, hd=hd)

    def raster(c, col=0):
        return pl.BlockSpec((1, rb, wb, c), lambda b, r, w: (b, r, w, col))

    def colmajor(c):
        return pl.BlockSpec((1, wb, rb, c), lambda b, r, w: (b, w, r, 0))

    return pl.pallas_call(
        kern,
        grid=(bsz, n_r // rb, n_w // wb),
        in_specs=[raster(d_a), colmajor(d_b), colmajor(d_b), raster(d_b, gb_block), raster(d),
                  pl.BlockSpec((1, 1, d), lambda b, r, w: (b, 0, 2)),
                  pl.BlockSpec((1, hd), lambda b, r, w: (0, 0)),
                  pl.BlockSpec((1, d), lambda b, r, w: (0, 0)),
                  pl.BlockSpec((d_a + d_b, d), lambda b, r, w: (0, 0),
                               pipeline_mode=pl.Buffered(1))],
        out_specs=raster(d),
        out_shape=jax.ShapeDtypeStruct((bsz, n_r, n_w, d), F32),
        scratch_shapes=[pltpu.VMEM((rb * wb, d_a + d_b), BF16)],
        compiler_params=pltpu.CompilerParams(
            dimension_semantics=("parallel", "parallel", "parallel"),
            vmem_limit_bytes=VMEM_LIMIT),
        name="out_projection",
    )(ya4, of4, ob4, pa4, x4, mod3, hnw.reshape(1, hd), fnw.reshape(1, d), w_bf16)


def kernel(x, c, ctx, c_ctx, ada_w, ada_b, norm_w, w_in, conv_w, conv_b, lru_wr, lru_br, lru_wi,
           lru_bi, lru_lambda, hgrn_lb_logits, hgrn_norm_w, w_out, final_norm_w):
    bsz, t_len, d = x.shape
    tc_len = ctx.shape[1]
    assert ada_w.shape[0] == 1, "single-layer stack only"
    d_a = conv_w.shape[2]
    d_b = hgrn_lb_logits.shape[2]
    hd = hgrn_norm_w.shape[1]
    n_heads = d_b // hd
    n_blocks_a = lru_wr.shape[2]
    n_cols = w_in.shape[2]
    assert t_len == GRID_W * CHUNK and tc_len % CHUNK == 0
    assert d_a // n_blocks_a == LANES and hd == LANES
    assert (2 * d_a) % d_b == 0 and n_cols == 2 * d_a + 5 * d_b
    first_b_group = (2 * d_a) // d_b
    n_rows_grid = t_len // GRID_W

    n_rows = -(-(bsz + 1) // SUBLANES) * SUBLANES
    cc = jnp.zeros((n_rows, d), F32).at[:bsz].set(c).at[bsz].set(c_ctx)
    mod3 = _modulation(cc, ada_w[0], ada_b[0]).reshape(n_rows, 1, 3 * d)

    w_in_b = w_in[0]
    w_out_b = w_out[0].astype(BF16)

    tm = 1024 if t_len % 1024 == 0 else t_len
    tn = 1024 if d_a % 1024 == 0 and d_b % 1024 == 0 else min(d_a, d_b)
    tpb = t_len // tm
    a_tiles = (2 * d_a) // tn
    b_tiles = (4 * d_b) // tn
    q_tiles = d_b // tn
    p_a = _in_projection(x.reshape(bsz * t_len, d), mod3, norm_w[0], w_in_b,
                         lambda i: i // tpb, lambda j: jnp.where(j < a_tiles, j, j + b_tiles),
                         2 * d_a + d_b, tm, tn, (d_a // tn, (2 * d_a + d_b) // tn))
    wb = 16
    p_b = _in_projection_colmajor(x.reshape(bsz, n_rows_grid, GRID_W, d), mod3, norm_w[0], w_in_b,
                                  lambda j: j + a_tiles, 4 * d_b, wb, tn, (0, q_tiles))
    p_ctx = _in_projection(ctx.reshape(bsz * tc_len, d), mod3, norm_w[0], w_in_b,
                           lambda i: bsz, lambda j: j, n_cols, bsz * tc_len, tn,
                           (a_tiles, a_tiles + q_tiles))
    p_lat = p_a.reshape(bsz, t_len, 2 * d_a + d_b)
    p_ctx = p_ctx.reshape(bsz, tc_len, n_cols)

    def gate_w(dirn):
        return jnp.concatenate([lru_wr[0, dirn], lru_wi[0, dirn]], axis=-1).astype(BF16)

    def gate_b(dirn):
        return jnp.concatenate([lru_br[0, dirn].reshape(n_blocks_a, 1, LANES),
                                lru_bi[0, dirn].reshape(n_blocks_a, 1, LANES)], axis=-1)

    cb = 256 if d_a % 256 == 0 else LANES
    ya = _rglru(p_lat, p_ctx, conv_w[0], conv_b[0], gate_w(0), gate_b(0), gate_w(1), gate_b(1),
                lru_lambda[0], d_a, cb, tt=128)

    states = _hgrn(p_ctx.reshape(bsz, tc_len // CHUNK, CHUNK, n_cols), hgrn_lb_logits, d_b,
                   n_heads, first_b_group, None, False)
    o_f, o_b = _hgrn(p_b, hgrn_lb_logits, d_b, n_heads, 0, states, True)

    grid4 = lambda z: z.reshape(bsz, n_rows_grid, GRID_W, z.shape[-1])
    out = _out_projection(grid4(ya), o_f, o_b, grid4(p_lat), first_b_group, grid4(x), mod3,
                          hgrn_norm_w[0], final_norm_w, w_out_b, n_heads, 16, 16)
    return out.reshape(bsz, t_len, d)
```

```python
import functools

import jax
import jax.numpy as jnp
from jax import lax
from jax.experimental import pallas as pl
from jax.experimental.pallas import tpu as pltpu

GRID_W = 64
CHUNK = 64
LRU_C = 8.0
EPS = 1e-6
CONV_PAD_L = 2
LANES = 128
SUBLANES = 8
EXP2_CLAMP = 115.0
NEG_LOG2E = -1.4426950408889634
VMEM_LIMIT = 56 * 1024 * 1024
PROLOGUE_ROWS = 16
SEG_PAD = 4
TILES_PER_TRIP = 4

F32 = jnp.float32
BF16 = jnp.bfloat16


def _sigmoid(z):
    return 1.0 / (1.0 + jnp.exp2(z * NEG_LOG2E))


def _silu(z):
    return z * _sigmoid(z)


def _softplus(z):
    return jnp.maximum(z, 0.0) + jnp.log1p(jnp.exp(-jnp.abs(z)))


def _mod_kernel(c_ref, w_ref, b_ref, o_ref):
    s = _silu(c_ref[...])
    o_ref[...] = jnp.dot(s.astype(BF16), w_ref[...].astype(BF16),
                         preferred_element_type=F32) + b_ref[...]


def _modulation(cc, w, b):
    rows, d = cc.shape
    n = w.shape[1]
    tn = 512 if n % 512 == 0 else n
    return pl.pallas_call(
        _mod_kernel,
        grid=(n // tn,),
        in_specs=[pl.BlockSpec((rows, d), lambda j: (0, 0)),
                  pl.BlockSpec((d, tn), lambda j: (0, j)),
                  pl.BlockSpec((1, tn), lambda j: (0, j))],
        out_specs=pl.BlockSpec((rows, tn), lambda j: (0, j)),
        out_shape=jax.ShapeDtypeStruct((rows, n), F32),
        compiler_params=pltpu.CompilerParams(
            dimension_semantics=("arbitrary",), vmem_limit_bytes=VMEM_LIMIT),
        name="adaln_modulation",
    )(cc, w, b.reshape(1, n))


def _store_projection(o_ref, res, j, silu_tiles):
    lo, hi = silu_tiles
    z = res.astype(o_ref.dtype)
    if hi <= lo:
        o_ref[...] = z
        return
    in_range = jnp.logical_and(j >= lo, j < hi)
    o_ref[...] = jnp.where(in_range, z * (0.5 * jnp.tanh(0.5 * z) + 0.5), z)


def _inproj_kernel(x_ref, shift_ref, scale_ref, nw_ref, w_ref, o_ref, h_ref, *, silu_tiles):
    @pl.when(pl.program_id(1) == 0)
    def _():
        rows = min(PROLOGUE_ROWS, x_ref.shape[0])
        gain = nw_ref[...] * (1.0 + scale_ref[0])

        def slab(s, carry):
            sl = pl.ds(pl.multiple_of(s * rows, rows), rows)
            x = x_ref[sl, :]
            rs = lax.rsqrt(jnp.mean(x * x, axis=-1, keepdims=True) + EPS)
            h_ref[sl, :] = (x_ref[sl, :] * rs * gain + shift_ref[0]).astype(BF16)
            return carry

        lax.fori_loop(0, x_ref.shape[0] // rows, slab, 0, unroll=4)

    res = jnp.dot(h_ref[...], w_ref[...].astype(BF16), preferred_element_type=F32)
    _store_projection(o_ref, res, pl.program_id(1), silu_tiles)


def _in_projection(x2d, mod3, norm_w, w_bf16, row_of_tile, wcol_of_tile, n_out, tm, tn, silu_tiles):
    m, d = x2d.shape
    return pl.pallas_call(
        functools.partial(_inproj_kernel, silu_tiles=silu_tiles),
        grid=(m // tm, n_out // tn),
        in_specs=[pl.BlockSpec((tm, d), lambda i, j: (i, 0)),
                  pl.BlockSpec((1, 1, d), lambda i, j: (row_of_tile(i), 0, 0)),
                  pl.BlockSpec((1, 1, d), lambda i, j: (row_of_tile(i), 0, 1)),
                  pl.BlockSpec((1, d), lambda i, j: (0, 0)),
                  pl.BlockSpec((d, tn), lambda i, j: (0, wcol_of_tile(j)))],
        out_specs=pl.BlockSpec((tm, tn), lambda i, j: (i, j)),
        out_shape=jax.ShapeDtypeStruct((m, n_out), BF16),
        scratch_shapes=[pltpu.VMEM((tm, d), BF16)],
        compiler_params=pltpu.CompilerParams(
            dimension_semantics=("parallel", "arbitrary"),
            vmem_limit_bytes=VMEM_LIMIT),
        name="in_projection",
    )(x2d, mod3, mod3, norm_w.reshape(1, d), w_bf16)


def _inproj_colmajor_kernel(x_ref, shift_ref, scale_ref, nw_ref, w_ref, o_ref, h_ref, *, slab,
                            silu_tiles):
    _, n_r, n_w, d = x_ref.shape

    rr = 2 * SUBLANES
    lanes = [slice(s * slab, (s + 1) * slab) for s in range(d // slab)]

    @pl.when(pl.program_id(2) == 0)
    def _():
        def row_group(g, carry):
            rows = pl.ds(pl.multiple_of(g * rr, rr), rr)
            sq = jnp.zeros((rr, n_w, slab), F32)
            for sl in lanes:
                xs = x_ref[0, rows, :, sl]
                sq = sq + xs * xs
            rs = lax.rsqrt(jnp.sum(sq, axis=-1, keepdims=True) * (1.0 / d) + EPS)
            for sl in lanes:
                y = x_ref[0, rows, :, sl] * rs * nw_ref[:, sl]
                y = y * (1.0 + scale_ref[0][:, sl]) + shift_ref[0][:, sl]
                h_ref[:, rows, sl] = jnp.transpose(y, (1, 0, 2)).astype(BF16)
            return carry

        lax.fori_loop(0, n_r // rr, row_group, 0)

    res = jnp.dot(h_ref[...].reshape(n_w * n_r, d), w_ref[...].astype(BF16),
                  preferred_element_type=F32)
    _store_projection(o_ref, res.reshape(1, n_w, n_r, res.shape[1]), pl.program_id(2), silu_tiles)


def _in_projection_colmajor(x4d, mod3, norm_w, w_bf16, wcol_of_tile, n_out, wb, tn, silu_tiles):
    bsz, n_r, n_w, d = x4d.shape
    kern = functools.partial(_inproj_colmajor_kernel, slab=2 * LANES, silu_tiles=silu_tiles)
    return pl.pallas_call(
        kern,
        grid=(bsz, n_w // wb, n_out // tn),
        in_specs=[pl.BlockSpec((1, n_r, wb, d), lambda b, w, j: (b, 0, w, 0)),
                  pl.BlockSpec((1, 1, d), lambda b, w, j: (b, 0, 0)),
                  pl.BlockSpec((1, 1, d), lambda b, w, j: (b, 0, 1)),
                  pl.BlockSpec((1, d), lambda b, w, j: (0, 0)),
                  pl.BlockSpec((d, tn), lambda b, w, j: (0, wcol_of_tile(j)))],
        out_specs=pl.BlockSpec((1, wb, n_r, tn), lambda b, w, j: (b, w, 0, j)),
        out_shape=jax.ShapeDtypeStruct((bsz, n_w, n_r, n_out), BF16),
        scratch_shapes=[pltpu.VMEM((wb, n_r, d), BF16)],
        compiler_params=pltpu.CompilerParams(
            dimension_semantics=("parallel", "parallel", "arbitrary"),
            vmem_limit_bytes=VMEM_LIMIT),
        name="in_projection_colmajor",
    )(x4d, mod3, mod3, norm_w.reshape(1, d), w_bf16)


def _group_scan(a, b, reverse):
    row = lax.broadcasted_iota(jnp.int32, a.shape, 1)
    for k in (1, 2, 4):
        if reverse:
            a_sh = pltpu.roll(a, SUBLANES - k, axis=1)
            b_sh = pltpu.roll(b, SUBLANES - k, axis=1)
            m = row < SUBLANES - k
        else:
            a_sh = pltpu.roll(a, k, axis=1)
            b_sh = pltpu.roll(b, k, axis=1)
            m = row >= k
        b = jnp.where(m, a * b_sh + b, b)
        a = jnp.where(m, a * a_sh, a)
    return a, b


def _sqrt_unit(x):
    return jnp.where(x > 0.0, x * lax.rsqrt(x), 0.0)


def _segment_scan(a, b, carry, reverse, scr_ref):
    tt = a.shape[0]
    seg = tt // SUBLANES
    pitch = seg + SEG_PAD
    for s in range(SUBLANES):
        rows = pl.ds(pitch * s, seg, stride=1)
        scr_ref[0, rows, :] = a[seg * s:seg * (s + 1), :]
        scr_ref[1, rows, :] = b[seg * s:seg * (s + 1), :]
    hs, ps = [None] * seg, [None] * seg
    h = p = None
    for j in (range(seg - 1, -1, -1) if reverse else range(seg)):
        step = pl.ds(j, SUBLANES, stride=pitch)
        a_j = scr_ref[0, step, :]
        b_j = scr_ref[1, step, :]
        h = b_j if h is None else a_j * h + b_j
        p = a_j if p is None else a_j * p
        hs[j], ps[j] = h, p
    g3, e3 = _group_scan(p[None], h[None], reverse)
    end = e3[0] + g3[0] * carry
    row = lax.broadcasted_iota(jnp.int32, end.shape, 0)
    if reverse:
        enter = jnp.where(row == SUBLANES - 1, carry, pltpu.roll(end, SUBLANES - 1, axis=0))
        carry_out = end[0:1, :]
    else:
        enter = jnp.where(row == 0, carry, pltpu.roll(end, 1, axis=0))
        carry_out = end[SUBLANES - 1:SUBLANES, :]
    for j in range(seg):
        scr_ref[2, pl.ds(j, SUBLANES, stride=pitch), :] = hs[j] + ps[j] * enter
    h_time = [scr_ref[2, pl.ds(pitch * s, seg, stride=1), :] for s in range(SUBLANES)]
    return jnp.concatenate(h_time, axis=0), carry_out


def _lru_block(u, wg, bg, sp, carry, reverse, scr_ref):
    g = jnp.dot(u.astype(BF16), wg, preferred_element_type=F32) + bg
    r = _sigmoid(g[:, :LANES])
    i = _sigmoid(g[:, LANES:])
    a = jnp.exp2(r * sp)
    b = _sqrt_unit(1.0 - a * a) * (i * u)
    return _segment_scan(a, b, carry, reverse, scr_ref)


def _rglru_kernel(xa_ref, ga_ref, xc_ref, cw_ref, cb_ref, wgf_ref, bgf_ref, wgb_ref, bgb_ref,
                  lam_ref, o_ref, xf_ref, xcf_ref, u_ref, hf_ref, scr_ref, *, tt):
    t_len = xa_ref.shape[1]
    tc_len = xc_ref.shape[1]
    nblk = xa_ref.shape[2] // LANES
    lanes = [slice(k * LANES, (k + 1) * LANES) for k in range(nblk)]
    zeros = jnp.zeros((SUBLANES, LANES), F32)
    for k, sl in enumerate(lanes):
        xf_ref[k, 0:SUBLANES, :] = zeros
        xf_ref[k, SUBLANES:SUBLANES + t_len, :] = xa_ref[0, :, sl].astype(F32)
        xf_ref[k, SUBLANES + t_len:, :] = zeros
        xcf_ref[k, 0:SUBLANES, :] = zeros
        xcf_ref[k, SUBLANES:SUBLANES + tc_len, :] = xc_ref[0, :, sl].astype(F32)
        xcf_ref[k, SUBLANES + tc_len:, :] = zeros

    sp_f = _softplus(-lam_ref[0:1, :]) * (LRU_C * NEG_LOG2E)
    sp_b = _softplus(-lam_ref[1:2, :]) * (LRU_C * NEG_LOG2E)

    def conv(src_ref, k, t0):
        u = cb_ref[:, lanes[k]]
        for tap in range(cw_ref.shape[0]):
            off = SUBLANES - CONV_PAD_L + tap
            if off % SUBLANES == 0:
                start = t0 + off
                rows = pl.ds(start if isinstance(start, int) else pl.multiple_of(start, SUBLANES), tt)
            else:
                rows = pl.ds(t0 + off, tt, stride=1)
            u = u + src_ref[k, rows, :] * cw_ref[tap:tap + 1, lanes[k]]
        return u

    def block(u, k, carry, reverse, slot):
        scr = scr_ref.at[k * TILES_PER_TRIP + slot]
        if reverse:
            return _lru_block(u, wgb_ref[k], bgb_ref[k], sp_b[:, lanes[k]], carry, True, scr)
        return _lru_block(u, wgf_ref[k], bgf_ref[k], sp_f[:, lanes[k]], carry, False, scr)

    n_lat = t_len // tt
    n_ctx = tc_len // tt
    assert n_lat % TILES_PER_TRIP == 0
    carry0 = tuple(jnp.zeros((1, LANES), F32) for _ in lanes)

    def tile_start(trip, slot, n_tiles, reverse):
        idx = trip * TILES_PER_TRIP + slot
        return pl.multiple_of((n_tiles - 1 - idx if reverse else idx) * tt, tt)

    def ctx_pass(reverse):
        carry = carry0
        for idx in range(n_ctx):
            t0 = (n_ctx - 1 - idx if reverse else idx) * tt
            carry = tuple(block(conv(xcf_ref, k, t0), k, carry[k], reverse, idx % TILES_PER_TRIP)[1]
                          for k in range(nblk))
        return carry

    def lat_f(trip, carry):
        for slot in range(TILES_PER_TRIP):
            t0 = tile_start(trip, slot, n_lat, False)
            rows = pl.ds(t0, tt)
            out = []
            for k, sl in enumerate(lanes):
                u = conv(xf_ref, k, t0)
                u_ref[rows, sl] = u
                h, c_out = block(u, k, carry[k], False, slot)
                hf_ref[rows, sl] = h
                out.append(c_out)
            carry = tuple(out)
        return carry

    lax.fori_loop(0, n_lat // TILES_PER_TRIP, lat_f, ctx_pass(False))

    def lat_b(trip, carry):
        for slot in range(TILES_PER_TRIP):
            rows = pl.ds(tile_start(trip, slot, n_lat, True), tt)
            out = []
            for k, sl in enumerate(lanes):
                h, c_out = block(u_ref[rows, sl], k, carry[k], True, slot)
                gate = ga_ref[0, rows, sl].astype(F32)
                o_ref[0, rows, sl] = ((hf_ref[rows, sl] + h) * gate).astype(o_ref.dtype)
                out.append(c_out)
            carry = tuple(out)
        return carry

    lax.fori_loop(0, n_lat // TILES_PER_TRIP, lat_b, ctx_pass(True))


def _rglru(p_lat, p_ctx, conv_w, conv_b, wg_f, bg_f, wg_b, bg_b, lam, d_a, cb, tt):
    bsz, t_len, _ = p_lat.shape
    tc_len = p_ctx.shape[1]
    nblk = cb // LANES
    ncb = d_a // cb
    kern = functools.partial(_rglru_kernel, tt=tt)
    return pl.pallas_call(
        kern,
        grid=(bsz, ncb),
        in_specs=[pl.BlockSpec((1, t_len, cb), lambda b, c: (b, 0, c)),
                  pl.BlockSpec((1, t_len, cb), lambda b, c: (b, 0, ncb + c)),
                  pl.BlockSpec((1, tc_len, cb), lambda b, c: (b, 0, c)),
                  pl.BlockSpec((conv_w.shape[0], cb), lambda b, c: (0, c)),
                  pl.BlockSpec((1, cb), lambda b, c: (0, c)),
                  pl.BlockSpec((nblk, LANES, 2 * LANES), lambda b, c: (c, 0, 0)),
                  pl.BlockSpec((nblk, 1, 2 * LANES), lambda b, c: (c, 0, 0)),
                  pl.BlockSpec((nblk, LANES, 2 * LANES), lambda b, c: (c, 0, 0)),
                  pl.BlockSpec((nblk, 1, 2 * LANES), lambda b, c: (c, 0, 0)),
                  pl.BlockSpec((2, cb), lambda b, c: (0, c))],
        out_specs=pl.BlockSpec((1, t_len, cb), lambda b, c: (b, 0, c)),
        out_shape=jax.ShapeDtypeStruct((bsz, t_len, d_a), BF16),
        scratch_shapes=[pltpu.VMEM((nblk, t_len + 2 * SUBLANES, LANES), F32),
                        pltpu.VMEM((nblk, tc_len + 2 * SUBLANES, LANES), F32),
                        pltpu.VMEM((t_len, cb), F32),
                        pltpu.VMEM((t_len, cb), F32),
                        pltpu.VMEM((nblk * TILES_PER_TRIP, 3, tt + SUBLANES * SEG_PAD, LANES),
                                   F32)],
        compiler_params=pltpu.CompilerParams(
            dimension_semantics=("parallel", "parallel"),
            vmem_limit_bytes=VMEM_LIMIT),
        name="rglru",
    )(p_lat, p_lat, p_ctx, conv_w, conv_b.reshape(1, d_a), wg_f, bg_f, wg_b, bg_b, lam)


_NT = (((1,), (1,)), ((), ()))
_TN = (((0,), (0,)), ((), ()))


class _HgrnDir:
    def __init__(self, d, q_ref, f_ref, v_ref, lb, st_ref, o_ref, scratch, reverse, n_heads, hd):
        self.d, self.q_ref, self.f_ref, self.v_ref, self.lb = d, q_ref, f_ref, v_ref, lb
        self.st_ref, self.o_ref, self.reverse, self.n_heads, self.hd = st_ref, o_ref, reverse, n_heads, hd
        (self.hl_ref, self.kk_ref, self.cum_ref, self.qd_ref, self.kd_ref, self.qin_ref,
         self.kout_ref, self.dec_ref, self.sc_ref) = scratch
        self.c_len = q_ref.shape[2]
        r_i = lax.broadcasted_iota(jnp.int32, (self.c_len, self.c_len), 0)
        c_i = lax.broadcasted_iota(jnp.int32, (self.c_len, self.c_len), 1)
        self.tri = (r_i <= c_i) if reverse else (r_i >= c_i)

    def _slabs(self, width):
        total = self.n_heads * self.hd
        return [slice(s, s + width) for s in range(0, total, width)]


    def _gate_slab(self, sl):
        d, c_len = self.d, self.c_len
        lb = self.lb[:, sl]
        f = lb + (1.0 - lb) * _sigmoid(self.f_ref[0, 0, :, sl].astype(F32))
        logf = jnp.log2(f)
        self.kk_ref[d, :, sl] = (1.0 - f).astype(BF16)
        hi = logf.astype(BF16)
        self.hl_ref[d, 0:c_len, sl] = hi
        self.hl_ref[d, c_len:2 * c_len, sl] = (logf - hi.astype(F32)).astype(BF16)

    def _cumulate(self):
        tri_b = self.tri.astype(BF16)
        tri2 = jnp.concatenate([tri_b, tri_b], axis=1)
        self.cum_ref[self.d] = jnp.dot(tri2, self.hl_ref[self.d], preferred_element_type=F32)

    def _decay_slab(self, sl):
        d, c_len = self.d, self.c_len
        half = c_len // 2
        row_last = 0 if self.reverse else c_len - 1
        row_ref = half if self.reverse else half - 1
        cum = self.cum_ref[d, :, sl]
        last = cum[row_last:row_last + 1, :]
        ref = cum[row_ref:row_ref + 1, :]
        kk = self.kk_ref[d, :, sl]
        qs = self.q_ref[0, 0, :, sl]
        self.qd_ref[d, :, sl] = qs * jnp.exp2(jnp.minimum(cum - ref, EXP2_CLAMP)).astype(BF16)
        self.kd_ref[d, :, sl] = kk * jnp.exp2(jnp.minimum(ref - cum, EXP2_CLAMP)).astype(BF16)
        self.qin_ref[d, :, sl] = qs * jnp.exp2(cum).astype(BF16)
        self.kout_ref[d, :, sl] = kk * jnp.exp2(last - cum).astype(BF16)
        self.dec_ref[d, :, sl] = jnp.exp2(last)

    def _head(self, h):
        return slice(h * self.hd, (h + 1) * self.hd)

    def _scores(self, h):
        d, sl = self.d, self._head(h)
        s = lax.dot_general(self.qd_ref[d, :, sl], self.kd_ref[d, :, sl], _NT,
                            preferred_element_type=F32)
        self.sc_ref[d, h] = jnp.where(self.tri, s, 0.0).astype(BF16)

    def _output(self, h):
        d, sl = self.d, self._head(h)
        o = (jnp.dot(self.sc_ref[d, h], self.v_ref[0, 0, :, sl], preferred_element_type=F32)
             + lax.dot_general(self.qin_ref[d, :, sl], self.st_ref[h].astype(BF16), _NT,
                               preferred_element_type=F32))
        self.o_ref[0, 0, :, sl] = o.astype(self.o_ref.dtype)

    def _update(self, h):
        d, sl = self.d, self._head(h)
        self.st_ref[h] = (self.st_ref[h] * self.dec_ref[d, :, sl]
                          + lax.dot_general(self.v_ref[0, 0, :, sl], self.kout_ref[d, :, sl], _TN,
                                            preferred_element_type=F32))

    def stages(self):
        heads = range(self.n_heads)
        slabs = self._slabs(2 * LANES)
        out = [[functools.partial(self._gate_slab, sl) for sl in slabs], [self._cumulate],
               [functools.partial(self._decay_slab, sl) for sl in slabs]]
        if self.o_ref is not None:
            out += [[functools.partial(self._scores, h) for h in heads],
                    [functools.partial(self._output, h) for h in heads]]
        return out + [[functools.partial(self._update, h) for h in heads]]


def _lower_bounds(logits_ref, layer):
    out = []
    for d in range(2):
        rows = [logits_ref[d, l:l + 1, :] for l in range(logits_ref.shape[1])]
        m = functools.reduce(jnp.maximum, rows)
        e = [jnp.exp(r - m) for r in rows]
        out.append(sum(e[:layer + 1]) / sum(e))
    return out


def _hgrn_kernel(*refs, n_heads, hd, has_init, emit_o, emit_state):
    qf_ref, ff_ref, vf_ref, qb_ref, fb_ref, vb_ref, lg_ref = refs[:7]
    pos = 7
    if has_init:
        s0f_ref, s0b_ref = refs[pos:pos + 2]
        pos += 2
    if emit_o:
        of_ref, ob_ref = refs[pos:pos + 2]
        pos += 2
    if emit_state:
        sof_ref, sob_ref = refs[pos:pos + 2]
        pos += 2
    sf_ref, sb_ref = refs[pos:pos + 2]
    scratch = refs[pos + 2:]

    j = pl.program_id(1)

    @pl.when(j == 0)
    def _():
        if has_init:
            sf_ref[...] = s0f_ref[0]
            sb_ref[...] = s0b_ref[0]
        else:
            sf_ref[...] = jnp.zeros_like(sf_ref)
            sb_ref[...] = jnp.zeros_like(sb_ref)

    lb_f, lb_b = _lower_bounds(lg_ref, 0)
    dirs = [_HgrnDir(0, qf_ref, ff_ref, vf_ref, lb_f, sf_ref, of_ref if emit_o else None,
                     scratch, False, n_heads, hd),
            _HgrnDir(1, qb_ref, fb_ref, vb_ref, lb_b, sb_ref, ob_ref if emit_o else None,
                     scratch, True, n_heads, hd)]
    for stage_f, stage_b in zip(dirs[0].stages(), dirs[1].stages()):
        for unit in stage_f + stage_b:
            unit()

    if emit_state:
        @pl.when(j == pl.num_programs(1) - 1)
        def _():
            sof_ref[0] = sf_ref[...]
            sob_ref[0] = sb_ref[...]


def _hgrn(p_view, logits, d_b, n_heads, first_group, init_states, emit_o):
    bsz, n_chunks = p_view.shape[:2]
    hd = d_b // n_heads
    has_init = init_states is not None
    emit_state = not emit_o

    def chunk(j, reverse):
        return n_chunks - 1 - j if reverse else j

    def spec(group, reverse):
        def imap(b, j):
            return (b, chunk(j, reverse), 0, first_group + group)
        return pl.BlockSpec((1, 1, CHUNK, d_b), imap)

    in_specs = [spec(0, False), spec(1, False), spec(3, False),
                spec(0, True), spec(2, True), spec(3, True),
                pl.BlockSpec(logits.shape, lambda b, j: (0, 0, 0))]
    args = [p_view] * 6 + [logits]
    state_spec = pl.BlockSpec((1, n_heads, hd, hd), lambda b, j: (b, 0, 0, 0))
    state_shape = jax.ShapeDtypeStruct((bsz, n_heads, hd, hd), F32)
    if has_init:
        in_specs += [state_spec, state_spec]
        args += list(init_states)
    out_specs, out_shape = [], []
    if emit_o:
        o_shape = jax.ShapeDtypeStruct((bsz, n_chunks, CHUNK, d_b), BF16)
        out_specs += [pl.BlockSpec((1, 1, CHUNK, d_b), lambda b, j: (b, chunk(j, False), 0, 0)),
                      pl.BlockSpec((1, 1, CHUNK, d_b), lambda b, j: (b, chunk(j, True), 0, 0))]
        out_shape += [o_shape, o_shape]
    if emit_state:
        out_specs += [state_spec, state_spec]
        out_shape += [state_shape, state_shape]
    kern = functools.partial(_hgrn_kernel, n_heads=n_heads, hd=hd, has_init=has_init,
                             emit_o=emit_o, emit_state=emit_state)
    return pl.pallas_call(
        kern,
        grid=(bsz, n_chunks),
        in_specs=in_specs,
        out_specs=out_specs,
        out_shape=out_shape,
        scratch_shapes=[pltpu.VMEM((n_heads, hd, hd), F32),
                        pltpu.VMEM((n_heads, hd, hd), F32),
                        pltpu.VMEM((2, 2 * CHUNK, d_b), BF16),
                        pltpu.VMEM((2, CHUNK, d_b), BF16),
                        pltpu.VMEM((2, CHUNK, d_b), F32),
                        pltpu.VMEM((2, CHUNK, d_b), BF16),
                        pltpu.VMEM((2, CHUNK, d_b), BF16),
                        pltpu.VMEM((2, CHUNK, d_b), BF16),
                        pltpu.VMEM((2, CHUNK, d_b), BF16),
                        pltpu.VMEM((2, 1, d_b), F32),
                        pltpu.VMEM((2, n_heads, CHUNK, CHUNK), BF16)],
        compiler_params=pltpu.CompilerParams(
            dimension_semantics=("parallel", "arbitrary"),
            vmem_limit_bytes=VMEM_LIMIT),
        name="hgrn2_latent" if emit_o else "hgrn2_context",
    )(*args)


def _outproj_kernel(ya_ref, of_ref, ob_ref, gb_ref, x_ref, gate_ref, hnw_ref, fnw_ref, w_ref,
                    o_ref, y_ref, *, n_heads, hd):
    _, rb, wb, d_a = ya_ref.shape
    rows = rb * wb
    d = x_ref.shape[3]
    y_ref[:, 0:d_a] = ya_ref[0].reshape(rows, d_a)
    hnw = hnw_ref[...]
    for h in range(n_heads):
        sl = slice(h * hd, (h + 1) * hd)
        o = of_ref[0, :, :, sl].astype(F32) + ob_ref[0, :, :, sl].astype(F32)
        ms = jnp.mean(o * o, axis=-1, keepdims=True)
        on = jnp.transpose(o * lax.rsqrt(ms + EPS) * hnw, (1, 0, 2))
        yb = on * gb_ref[0, :, :, sl].astype(F32)
        y_ref[:, d_a + h * hd:d_a + (h + 1) * hd] = yb.reshape(rows, hd).astype(BF16)
    acc = jnp.dot(y_ref[...], w_ref[...], preferred_element_type=F32)
    z = x_ref[0].reshape(rows, d) + gate_ref[0] * acc
    ms = jnp.mean(z * z, axis=-1, keepdims=True)
    o_ref[0] = (z * lax.rsqrt(ms + EPS) * fnw_ref[...]).reshape(rb, wb, d)


def _out_projection(ya4, of4, ob4, pa4, gb_block, x4, mod3, hnw, fnw, w_bf16, n_heads, rb, wb):
    bsz, n_r, n_w, d = x4.shape
    d_a = ya4.shape[3]
    d_b = of4.shape[3]
    hd = d_b // n_heads
    kern = functools.partial(_outproj_kernel, n_heads=n_heads, hd=hd)

    def raster(c, col=0):
        return pl.BlockSpec((1, rb, wb, c), lambda b, r, w: (b, r, w, col))

    def colmajor(c):
        return pl.BlockSpec((1, wb, rb, c), lambda b, r, w: (b, w, r, 0))

    return pl.pallas_call(
        kern,
        grid=(bsz, n_r // rb, n_w // wb),
        in_specs=[raster(d_a), colmajor(d_b), colmajor(d_b), raster(d_b, gb_block), raster(d),
                  pl.BlockSpec((1, 1, d), lambda b, r, w: (b, 0, 2)),
                  pl.BlockSpec((1, hd), lambda b, r, w: (0, 0)),
                  pl.BlockSpec((1, d), lambda b, r, w: (0, 0)),
                  pl.BlockSpec((d_a + d_b, d), lambda b, r, w: (0, 0),
                               pipeline_mode=pl.Buffered(1))],
        out_specs=raster(d),
        out_shape=jax.ShapeDtypeStruct((bsz, n_r, n_w, d), F32),
        scratch_shapes=[pltpu.VMEM((rb * wb, d_a + d_b), BF16)],
        compiler_params=pltpu.CompilerParams(
            dimension_semantics=("parallel", "parallel", "parallel"),
            vmem_limit_bytes=VMEM_LIMIT),
        name="out_projection",
    )(ya4, of4, ob4, pa4, x4, mod3, hnw.reshape(1, hd), fnw.reshape(1, d), w_bf16)


def kernel(x, c, ctx, c_ctx, ada_w, ada_b, norm_w, w_in, conv_w, conv_b, lru_wr, lru_br, lru_wi,
           lru_bi, lru_lambda, hgrn_lb_logits, hgrn_norm_w, w_out, final_norm_w):
    bsz, t_len, d = x.shape
    tc_len = ctx.shape[1]
    assert ada_w.shape[0] == 1, "single-layer stack only"
    d_a = conv_w.shape[2]
    d_b = hgrn_lb_logits.shape[2]
    hd = hgrn_norm_w.shape[1]
    n_heads = d_b // hd
    n_blocks_a = lru_wr.shape[2]
    n_cols = w_in.shape[2]
    assert t_len == GRID_W * CHUNK and tc_len % CHUNK == 0
    assert d_a // n_blocks_a == LANES and hd == LANES
    assert (2 * d_a) % d_b == 0 and n_cols == 2 * d_a + 5 * d_b
    first_b_group = (2 * d_a) // d_b
    n_rows_grid = t_len // GRID_W

    n_rows = -(-(bsz + 1) // SUBLANES) * SUBLANES
    cc = jnp.zeros((n_rows, d), F32).at[:bsz].set(c).at[bsz].set(c_ctx)
    mod3 = _modulation(cc, ada_w[0], ada_b[0]).reshape(n_rows, 1, 3 * d)

    w_in_b = w_in[0]
    w_out_b = w_out[0].astype(BF16)

    tm = 1024 if t_len % 1024 == 0 else t_len
    tn = 1024 if d_a % 1024 == 0 and d_b % 1024 == 0 else min(d_a, d_b)
    tpb = t_len // tm
    a_tiles = (2 * d_a) // tn
    b_tiles = (4 * d_b) // tn
    q_tiles = d_b // tn
    p_a = _in_projection(x.reshape(bsz * t_len, d), mod3, norm_w[0], w_in_b,
                         lambda i: i // tpb, lambda j: jnp.where(j < a_tiles, j, j + b_tiles),
                         2 * d_a + d_b, tm, tn, (d_a // tn, (2 * d_a + d_b) // tn))
    wb = 16
    p_b = _in_projection_colmajor(x.reshape(bsz, n_rows_grid, GRID_W, d), mod3, norm_w[0], w_in_b,
                                  lambda j: j + a_tiles, 4 * d_b, wb, tn, (0, q_tiles))
    xa_tiles = d_a // tn
    ctx_cols = d_a + 4 * d_b
    p_ctx = _in_projection(ctx.reshape(bsz * tc_len, d), mod3, norm_w[0], w_in_b,
                           lambda i: bsz, lambda j: jnp.where(j < xa_tiles, j, j + xa_tiles),
                           ctx_cols, bsz * tc_len, tn, (xa_tiles, xa_tiles + q_tiles))
    p_lat = p_a.reshape(bsz, t_len, 2 * d_a + d_b)
    p_ctx = p_ctx.reshape(bsz, tc_len, ctx_cols)

    def gate_w(dirn):
        return jnp.concatenate([lru_wr[0, dirn], lru_wi[0, dirn]], axis=-1).astype(BF16)

    def gate_b(dirn):
        return jnp.concatenate([lru_br[0, dirn].reshape(n_blocks_a, 1, LANES),
                                lru_bi[0, dirn].reshape(n_blocks_a, 1, LANES)], axis=-1)

    cb = 256 if d_a % 256 == 0 else LANES
    ya = _rglru(p_lat, p_ctx, conv_w[0], conv_b[0], gate_w(0), gate_b(0), gate_w(1), gate_b(1),
                lru_lambda[0], d_a, cb, tt=128)

    assert d_a % d_b == 0
    states = _hgrn(p_ctx.reshape(bsz, tc_len // CHUNK, CHUNK, ctx_cols), hgrn_lb_logits, d_b,
                   n_heads, d_a // d_b, None, False)
    o_f, o_b = _hgrn(p_b, hgrn_lb_logits, d_b, n_heads, 0, states, True)

    grid4 = lambda z: z.reshape(bsz, n_rows_grid, GRID_W, z.shape[-1])
    out = _out_projection(grid4(ya), o_f, o_b, grid4(p_lat), first_b_group, grid4(x), mod3,
                          hgrn_norm_w[0], final_norm_w, w_out_b, n_heads, 16, 16)
    return out.reshape(bsz, t_len, d)
```

```python
import functools

import jax
import jax.numpy as jnp
from jax import lax
from jax.experimental import pallas as pl
from jax.experimental.pallas import tpu as pltpu

GRID_W = 64
CHUNK = 64
LRU_C = 8.0
EPS = 1e-6
CONV_PAD_L = 2
LANES = 128
SUBLANES = 8
EXP2_CLAMP = 115.0
NEG_LOG2E = -1.4426950408889634
VMEM_LIMIT = 56 * 1024 * 1024
PROLOGUE_ROWS = 16
SEG_PAD = 4
CHUNKS_PER_STEP = 2
TILES_PER_TRIP = 4

F32 = jnp.float32
BF16 = jnp.bfloat16


def _sigmoid(z):
    return 1.0 / (1.0 + jnp.exp2(z * NEG_LOG2E))


def _silu(z):
    return z * _sigmoid(z)


def _softplus(z):
    return jnp.maximum(z, 0.0) + jnp.log1p(jnp.exp(-jnp.abs(z)))


def _mod_kernel(c_ref, w_ref, b_ref, o_ref):
    s = _silu(c_ref[...])
    o_ref[...] = jnp.dot(s.astype(BF16), w_ref[...].astype(BF16),
                         preferred_element_type=F32) + b_ref[...]


def _modulation(cc, w, b):
    rows, d = cc.shape
    n = w.shape[1]
    tn = 512 if n % 512 == 0 else n
    return pl.pallas_call(
        _mod_kernel,
        grid=(n // tn,),
        in_specs=[pl.BlockSpec((rows, d), lambda j: (0, 0)),
                  pl.BlockSpec((d, tn), lambda j: (0, j)),
                  pl.BlockSpec((1, tn), lambda j: (0, j))],
        out_specs=pl.BlockSpec((rows, tn), lambda j: (0, j)),
        out_shape=jax.ShapeDtypeStruct((rows, n), F32),
        compiler_params=pltpu.CompilerParams(
            dimension_semantics=("arbitrary",), vmem_limit_bytes=VMEM_LIMIT),
        name="adaln_modulation",
    )(cc, w, b.reshape(1, n))


def _store_projection(o_ref, res, j, silu_tiles):
    lo, hi = silu_tiles
    z = res.astype(o_ref.dtype)
    if hi <= lo:
        o_ref[...] = z
        return
    in_range = jnp.logical_and(j >= lo, j < hi)
    o_ref[...] = jnp.where(in_range, z * (0.5 * jnp.tanh(0.5 * z) + 0.5), z)


def _inproj_kernel(x_ref, shift_ref, scale_ref, nw_ref, w_ref, o_ref, h_ref, *, silu_tiles):
    @pl.when(pl.program_id(1) == 0)
    def _():
        rows = min(PROLOGUE_ROWS, x_ref.shape[0])
        gain = nw_ref[...] * (1.0 + scale_ref[0])

        def slab(s, carry):
            sl = pl.ds(pl.multiple_of(s * rows, rows), rows)
            x = x_ref[sl, :]
            rs = lax.rsqrt(jnp.mean(x * x, axis=-1, keepdims=True) + EPS)
            h_ref[sl, :] = (x_ref[sl, :] * rs * gain + shift_ref[0]).astype(BF16)
            return carry

        lax.fori_loop(0, x_ref.shape[0] // rows, slab, 0, unroll=4)

    res = jnp.dot(h_ref[...], w_ref[...].astype(BF16), preferred_element_type=F32)
    _store_projection(o_ref, res, pl.program_id(1), silu_tiles)


def _in_projection(x2d, mod3, norm_w, w_bf16, row_of_tile, wcol_of_tile, n_out, tm, tn, silu_tiles):
    m, d = x2d.shape
    return pl.pallas_call(
        functools.partial(_inproj_kernel, silu_tiles=silu_tiles),
        grid=(m // tm, n_out // tn),
        in_specs=[pl.BlockSpec((tm, d), lambda i, j: (i, 0)),
                  pl.BlockSpec((1, 1, d), lambda i, j: (row_of_tile(i), 0, 0)),
                  pl.BlockSpec((1, 1, d), lambda i, j: (row_of_tile(i), 0, 1)),
                  pl.BlockSpec((1, d), lambda i, j: (0, 0)),
                  pl.BlockSpec((d, tn), lambda i, j: (0, wcol_of_tile(j)))],
        out_specs=pl.BlockSpec((tm, tn), lambda i, j: (i, j)),
        out_shape=jax.ShapeDtypeStruct((m, n_out), BF16),
        scratch_shapes=[pltpu.VMEM((tm, d), BF16)],
        compiler_params=pltpu.CompilerParams(
            dimension_semantics=("parallel", "arbitrary"),
            vmem_limit_bytes=VMEM_LIMIT),
        name="in_projection",
    )(x2d, mod3, mod3, norm_w.reshape(1, d), w_bf16)


def _inproj_colmajor_kernel(x_ref, shift_ref, scale_ref, nw_ref, w_ref, o_ref, h_ref, *, slab,
                            silu_tiles):
    _, n_r, n_w, d = x_ref.shape

    rr = 2 * SUBLANES
    lanes = [slice(s * slab, (s + 1) * slab) for s in range(d // slab)]

    @pl.when(pl.program_id(2) == 0)
    def _():
        def row_group(g, carry):
            rows = pl.ds(pl.multiple_of(g * rr, rr), rr)
            sq = jnp.zeros((rr, n_w, slab), F32)
            for sl in lanes:
                xs = x_ref[0, rows, :, sl]
                sq = sq + xs * xs
            rs = lax.rsqrt(jnp.sum(sq, axis=-1, keepdims=True) * (1.0 / d) + EPS)
            for sl in lanes:
                y = x_ref[0, rows, :, sl] * rs * nw_ref[:, sl]
                y = y * (1.0 + scale_ref[0][:, sl]) + shift_ref[0][:, sl]
                h_ref[:, rows, sl] = jnp.transpose(y, (1, 0, 2)).astype(BF16)
            return carry

        lax.fori_loop(0, n_r // rr, row_group, 0)

    res = jnp.dot(h_ref[...].reshape(n_w * n_r, d), w_ref[...].astype(BF16),
                  preferred_element_type=F32)
    _store_projection(o_ref, res.reshape(1, n_w, n_r, res.shape[1]), pl.program_id(2), silu_tiles)


def _in_projection_colmajor(x4d, mod3, norm_w, w_bf16, wcol_of_tile, n_out, wb, tn, silu_tiles):
    bsz, n_r, n_w, d = x4d.shape
    kern = functools.partial(_inproj_colmajor_kernel, slab=2 * LANES, silu_tiles=silu_tiles)
    return pl.pallas_call(
        kern,
        grid=(bsz, n_w // wb, n_out // tn),
        in_specs=[pl.BlockSpec((1, n_r, wb, d), lambda b, w, j: (b, 0, w, 0)),
                  pl.BlockSpec((1, 1, d), lambda b, w, j: (b, 0, 0)),
                  pl.BlockSpec((1, 1, d), lambda b, w, j: (b, 0, 1)),
                  pl.BlockSpec((1, d), lambda b, w, j: (0, 0)),
                  pl.BlockSpec((d, tn), lambda b, w, j: (0, wcol_of_tile(j)))],
        out_specs=pl.BlockSpec((1, wb, n_r, tn), lambda b, w, j: (b, w, 0, j)),
        out_shape=jax.ShapeDtypeStruct((bsz, n_w, n_r, n_out), BF16),
        scratch_shapes=[pltpu.VMEM((wb, n_r, d), BF16)],
        compiler_params=pltpu.CompilerParams(
            dimension_semantics=("parallel", "parallel", "arbitrary"),
            vmem_limit_bytes=VMEM_LIMIT),
        name="in_projection_colmajor",
    )(x4d, mod3, mod3, norm_w.reshape(1, d), w_bf16)


def _group_scan(a, b, reverse):
    row = lax.broadcasted_iota(jnp.int32, a.shape, 1)
    for k in (1, 2, 4):
        if reverse:
            a_sh = pltpu.roll(a, SUBLANES - k, axis=1)
            b_sh = pltpu.roll(b, SUBLANES - k, axis=1)
            m = row < SUBLANES - k
        else:
            a_sh = pltpu.roll(a, k, axis=1)
            b_sh = pltpu.roll(b, k, axis=1)
            m = row >= k
        b = jnp.where(m, a * b_sh + b, b)
        a = jnp.where(m, a * a_sh, a)
    return a, b


def _sqrt_unit(x):
    return jnp.where(x > 0.0, x * lax.rsqrt(x), 0.0)


def _segment_scan(a, b, carry, reverse, scr_ref):
    tt = a.shape[0]
    seg = tt // SUBLANES
    pitch = seg + SEG_PAD
    for s in range(SUBLANES):
        rows = pl.ds(pitch * s, seg, stride=1)
        scr_ref[0, rows, :] = a[seg * s:seg * (s + 1), :]
        scr_ref[1, rows, :] = b[seg * s:seg * (s + 1), :]
    hs, ps = [None] * seg, [None] * seg
    h = p = None
    for j in (range(seg - 1, -1, -1) if reverse else range(seg)):
        step = pl.ds(j, SUBLANES, stride=pitch)
        a_j = scr_ref[0, step, :]
        b_j = scr_ref[1, step, :]
        h = b_j if h is None else a_j * h + b_j
        p = a_j if p is None else a_j * p
        hs[j], ps[j] = h, p
    g3, e3 = _group_scan(p[None], h[None], reverse)
    end = e3[0] + g3[0] * carry
    row = lax.broadcasted_iota(jnp.int32, end.shape, 0)
    if reverse:
        enter = jnp.where(row == SUBLANES - 1, carry, pltpu.roll(end, SUBLANES - 1, axis=0))
        carry_out = end[0:1, :]
    else:
        enter = jnp.where(row == 0, carry, pltpu.roll(end, 1, axis=0))
        carry_out = end[SUBLANES - 1:SUBLANES, :]
    for j in range(seg):
        scr_ref[2, pl.ds(j, SUBLANES, stride=pitch), :] = hs[j] + ps[j] * enter
    h_time = [scr_ref[2, pl.ds(pitch * s, seg, stride=1), :] for s in range(SUBLANES)]
    return jnp.concatenate(h_time, axis=0), carry_out


def _lru_block(u, wg, bg, sp, carry, reverse, scr_ref):
    g = jnp.dot(u.astype(BF16), wg, preferred_element_type=F32) + bg
    r = _sigmoid(g[:, :LANES])
    i = _sigmoid(g[:, LANES:])
    a = jnp.exp2(r * sp)
    b = _sqrt_unit(1.0 - a * a) * (i * u)
    return _segment_scan(a, b, carry, reverse, scr_ref)


def _rglru_kernel(xa_ref, ga_ref, xc_ref, cw_ref, cb_ref, wgf_ref, bgf_ref, wgb_ref, bgb_ref,
                  lam_ref, o_ref, xf_ref, xcf_ref, u_ref, hf_ref, scr_ref, *, tt):
    t_len = xa_ref.shape[1]
    tc_len = xc_ref.shape[1]
    nblk = xa_ref.shape[2] // LANES
    lanes = [slice(k * LANES, (k + 1) * LANES) for k in range(nblk)]
    zeros = jnp.zeros((SUBLANES, LANES), F32)
    for k, sl in enumerate(lanes):
        xf_ref[k, 0:SUBLANES, :] = zeros
        xf_ref[k, SUBLANES:SUBLANES + t_len, :] = xa_ref[0, :, sl].astype(F32)
        xf_ref[k, SUBLANES + t_len:, :] = zeros
        xcf_ref[k, 0:SUBLANES, :] = zeros
        xcf_ref[k, SUBLANES:SUBLANES + tc_len, :] = xc_ref[0, :, sl].astype(F32)
        xcf_ref[k, SUBLANES + tc_len:, :] = zeros

    sp_f = _softplus(-lam_ref[0:1, :]) * (LRU_C * NEG_LOG2E)
    sp_b = _softplus(-lam_ref[1:2, :]) * (LRU_C * NEG_LOG2E)

    def conv(src_ref, k, t0):
        u = cb_ref[:, lanes[k]]
        for tap in range(cw_ref.shape[0]):
            off = SUBLANES - CONV_PAD_L + tap
            if off % SUBLANES == 0:
                start = t0 + off
                rows = pl.ds(start if isinstance(start, int) else pl.multiple_of(start, SUBLANES), tt)
            else:
                rows = pl.ds(t0 + off, tt, stride=1)
            u = u + src_ref[k, rows, :] * cw_ref[tap:tap + 1, lanes[k]]
        return u

    def block(u, k, carry, reverse, slot):
        scr = scr_ref.at[k * TILES_PER_TRIP + slot]
        if reverse:
            return _lru_block(u, wgb_ref[k], bgb_ref[k], sp_b[:, lanes[k]], carry, True, scr)
        return _lru_block(u, wgf_ref[k], bgf_ref[k], sp_f[:, lanes[k]], carry, False, scr)

    n_lat = t_len // tt
    n_ctx = tc_len // tt
    assert n_lat % TILES_PER_TRIP == 0
    carry0 = tuple(jnp.zeros((1, LANES), F32) for _ in lanes)

    def tile_start(trip, slot, n_tiles, reverse):
        idx = trip * TILES_PER_TRIP + slot
        return pl.multiple_of((n_tiles - 1 - idx if reverse else idx) * tt, tt)

    def ctx_pass(reverse):
        carry = carry0
        for idx in range(n_ctx):
            t0 = (n_ctx - 1 - idx if reverse else idx) * tt
            carry = tuple(block(conv(xcf_ref, k, t0), k, carry[k], reverse, idx % TILES_PER_TRIP)[1]
                          for k in range(nblk))
        return carry

    def lat_f(trip, carry):
        for slot in range(TILES_PER_TRIP):
            t0 = tile_start(trip, slot, n_lat, False)
            rows = pl.ds(t0, tt)
            out = []
            for k, sl in enumerate(lanes):
                u = conv(xf_ref, k, t0)
                u_ref[rows, sl] = u
                h, c_out = block(u, k, carry[k], False, slot)
                hf_ref[rows, sl] = h
                out.append(c_out)
            carry = tuple(out)
        return carry

    lax.fori_loop(0, n_lat // TILES_PER_TRIP, lat_f, ctx_pass(False))

    def lat_b(trip, carry):
        for slot in range(TILES_PER_TRIP):
            rows = pl.ds(tile_start(trip, slot, n_lat, True), tt)
            out = []
            for k, sl in enumerate(lanes):
                h, c_out = block(u_ref[rows, sl], k, carry[k], True, slot)
                gate = ga_ref[0, rows, sl].astype(F32)
                o_ref[0, rows, sl] = ((hf_ref[rows, sl] + h) * gate).astype(o_ref.dtype)
                out.append(c_out)
            carry = tuple(out)
        return carry

    lax.fori_loop(0, n_lat // TILES_PER_TRIP, lat_b, ctx_pass(True))


def _rglru(p_lat, p_ctx, conv_w, conv_b, wg_f, bg_f, wg_b, bg_b, lam, d_a, cb, tt):
    bsz, t_len, _ = p_lat.shape
    tc_len = p_ctx.shape[1]
    nblk = cb // LANES
    ncb = d_a // cb
    kern = functools.partial(_rglru_kernel, tt=tt)
    return pl.pallas_call(
        kern,
        grid=(bsz, ncb),
        in_specs=[pl.BlockSpec((1, t_len, cb), lambda b, c: (b, 0, c)),
                  pl.BlockSpec((1, t_len, cb), lambda b, c: (b, 0, ncb + c)),
                  pl.BlockSpec((1, tc_len, cb), lambda b, c: (b, 0, c)),
                  pl.BlockSpec((conv_w.shape[0], cb), lambda b, c: (0, c)),
                  pl.BlockSpec((1, cb), lambda b, c: (0, c)),
                  pl.BlockSpec((nblk, LANES, 2 * LANES), lambda b, c: (c, 0, 0)),
                  pl.BlockSpec((nblk, 1, 2 * LANES), lambda b, c: (c, 0, 0)),
                  pl.BlockSpec((nblk, LANES, 2 * LANES), lambda b, c: (c, 0, 0)),
                  pl.BlockSpec((nblk, 1, 2 * LANES), lambda b, c: (c, 0, 0)),
                  pl.BlockSpec((2, cb), lambda b, c: (0, c))],
        out_specs=pl.BlockSpec((1, t_len, cb), lambda b, c: (b, 0, c)),
        out_shape=jax.ShapeDtypeStruct((bsz, t_len, d_a), BF16),
        scratch_shapes=[pltpu.VMEM((nblk, t_len + 2 * SUBLANES, LANES), F32),
                        pltpu.VMEM((nblk, tc_len + 2 * SUBLANES, LANES), F32),
                        pltpu.VMEM((t_len, cb), F32),
                        pltpu.VMEM((t_len, cb), F32),
                        pltpu.VMEM((nblk * TILES_PER_TRIP, 3, tt + SUBLANES * SEG_PAD, LANES),
                                   F32)],
        compiler_params=pltpu.CompilerParams(
            dimension_semantics=("parallel", "parallel"),
            vmem_limit_bytes=VMEM_LIMIT),
        name="rglru",
    )(p_lat, p_lat, p_ctx, conv_w, conv_b.reshape(1, d_a), wg_f, bg_f, wg_b, bg_b, lam)


_NT = (((1,), (1,)), ((), ()))
_TN = (((0,), (0,)), ((), ()))


class _HgrnDir:
    def __init__(self, d, q_ref, f_ref, v_ref, lb, st_ref, o_ref, scratch, reverse, n_heads, hd,
                 sub):
        self.d, self.q_ref, self.f_ref, self.v_ref, self.lb = d, q_ref, f_ref, v_ref, lb
        self.sub = sub
        self.st_ref, self.o_ref, self.reverse, self.n_heads, self.hd = st_ref, o_ref, reverse, n_heads, hd
        (self.hl_ref, self.kk_ref, self.cum_ref, self.qd_ref, self.kd_ref, self.qin_ref,
         self.kout_ref, self.dec_ref, self.sc_ref) = scratch
        self.c_len = q_ref.shape[2]
        r_i = lax.broadcasted_iota(jnp.int32, (self.c_len, self.c_len), 0)
        c_i = lax.broadcasted_iota(jnp.int32, (self.c_len, self.c_len), 1)
        self.tri = (r_i <= c_i) if reverse else (r_i >= c_i)

    def _slabs(self, width):
        total = self.n_heads * self.hd
        return [slice(s, s + width) for s in range(0, total, width)]


    def _gate_slab(self, sl):
        d, c_len = self.d, self.c_len
        lb = self.lb[:, sl]
        f = lb + (1.0 - lb) * _sigmoid(self.f_ref[0, self.sub, :, sl].astype(F32))
        logf = jnp.log2(f)
        self.kk_ref[d, :, sl] = (1.0 - f).astype(BF16)
        hi = logf.astype(BF16)
        self.hl_ref[d, 0:c_len, sl] = hi
        self.hl_ref[d, c_len:2 * c_len, sl] = (logf - hi.astype(F32)).astype(BF16)

    def _cumulate(self):
        tri_b = self.tri.astype(BF16)
        tri2 = jnp.concatenate([tri_b, tri_b], axis=1)
        self.cum_ref[self.d] = jnp.dot(tri2, self.hl_ref[self.d], preferred_element_type=F32)

    def _decay_slab(self, sl):
        d, c_len = self.d, self.c_len
        half = c_len // 2
        row_last = 0 if self.reverse else c_len - 1
        row_ref = half if self.reverse else half - 1
        cum = self.cum_ref[d, :, sl]
        last = cum[row_last:row_last + 1, :]
        ref = cum[row_ref:row_ref + 1, :]
        kk = self.kk_ref[d, :, sl]
        qs = self.q_ref[0, self.sub, :, sl]
        self.qd_ref[d, :, sl] = qs * jnp.exp2(jnp.minimum(cum - ref, EXP2_CLAMP)).astype(BF16)
        self.kd_ref[d, :, sl] = kk * jnp.exp2(jnp.minimum(ref - cum, EXP2_CLAMP)).astype(BF16)
        self.qin_ref[d, :, sl] = qs * jnp.exp2(cum).astype(BF16)
        self.kout_ref[d, :, sl] = kk * jnp.exp2(last - cum).astype(BF16)
        self.dec_ref[d, :, sl] = jnp.exp2(last)

    def _head(self, h):
        return slice(h * self.hd, (h + 1) * self.hd)

    def _scores(self, h):
        d, sl = self.d, self._head(h)
        s = lax.dot_general(self.qd_ref[d, :, sl], self.kd_ref[d, :, sl], _NT,
                            preferred_element_type=F32)
        self.sc_ref[d, h] = jnp.where(self.tri, s, 0.0).astype(BF16)

    def _output(self, h):
        d, sl = self.d, self._head(h)
        o = (jnp.dot(self.sc_ref[d, h], self.v_ref[0, self.sub, :, sl], preferred_element_type=F32)
             + lax.dot_general(self.qin_ref[d, :, sl], self.st_ref[h].astype(BF16), _NT,
                               preferred_element_type=F32))
        self.o_ref[0, self.sub, :, sl] = o.astype(self.o_ref.dtype)

    def _update(self, h):
        d, sl = self.d, self._head(h)
        self.st_ref[h] = (self.st_ref[h] * self.dec_ref[d, :, sl]
                          + lax.dot_general(self.v_ref[0, self.sub, :, sl], self.kout_ref[d, :, sl], _TN,
                                            preferred_element_type=F32))

    def stages(self):
        heads = range(self.n_heads)
        slabs = self._slabs(2 * LANES)
        out = [[functools.partial(self._gate_slab, sl) for sl in slabs], [self._cumulate],
               [functools.partial(self._decay_slab, sl) for sl in slabs]]
        if self.o_ref is not None:
            out += [[functools.partial(self._scores, h) for h in heads],
                    [functools.partial(self._output, h) for h in heads]]
        return out + [[functools.partial(self._update, h) for h in heads]]


def _lower_bounds(logits_ref, layer):
    out = []
    for d in range(2):
        rows = [logits_ref[d, l:l + 1, :] for l in range(logits_ref.shape[1])]
        m = functools.reduce(jnp.maximum, rows)
        e = [jnp.exp(r - m) for r in rows]
        out.append(sum(e[:layer + 1]) / sum(e))
    return out


def _hgrn_kernel(*refs, n_heads, hd, has_init, emit_o, emit_state):
    qf_ref, ff_ref, vf_ref, qb_ref, fb_ref, vb_ref, lg_ref = refs[:7]
    pos = 7
    if has_init:
        s0f_ref, s0b_ref = refs[pos:pos + 2]
        pos += 2
    if emit_o:
        of_ref, ob_ref = refs[pos:pos + 2]
        pos += 2
    if emit_state:
        sof_ref, sob_ref = refs[pos:pos + 2]
        pos += 2
    sf_ref, sb_ref = refs[pos:pos + 2]
    scratch = refs[pos + 2:]

    j = pl.program_id(1)

    @pl.when(j == 0)
    def _():
        if has_init:
            sf_ref[...] = s0f_ref[0]
            sb_ref[...] = s0b_ref[0]
        else:
            sf_ref[...] = jnp.zeros_like(sf_ref)
            sb_ref[...] = jnp.zeros_like(sb_ref)

    lb_f, lb_b = _lower_bounds(lg_ref, 0)
    n_sub = qf_ref.shape[1]
    for k in range(n_sub):
        dirs = [_HgrnDir(0, qf_ref, ff_ref, vf_ref, lb_f, sf_ref, of_ref if emit_o else None,
                         scratch, False, n_heads, hd, k),
                _HgrnDir(1, qb_ref, fb_ref, vb_ref, lb_b, sb_ref, ob_ref if emit_o else None,
                         scratch, True, n_heads, hd, n_sub - 1 - k)]
        for stage_f, stage_b in zip(dirs[0].stages(), dirs[1].stages()):
            for unit in stage_f + stage_b:
                unit()

    if emit_state:
        @pl.when(j == pl.num_programs(1) - 1)
        def _():
            sof_ref[0] = sf_ref[...]
            sob_ref[0] = sb_ref[...]


def _hgrn(p_view, logits, d_b, n_heads, first_group, init_states, emit_o):
    bsz, n_chunks = p_view.shape[:2]
    hd = d_b // n_heads
    has_init = init_states is not None
    emit_state = not emit_o

    cps = CHUNKS_PER_STEP if n_chunks % CHUNKS_PER_STEP == 0 else 1
    n_steps = n_chunks // cps

    def chunk(j, reverse):
        return n_steps - 1 - j if reverse else j

    def spec(group, reverse):
        def imap(b, j):
            return (b, chunk(j, reverse), 0, first_group + group)
        return pl.BlockSpec((1, cps, CHUNK, d_b), imap)

    in_specs = [spec(0, False), spec(1, False), spec(3, False),
                spec(0, True), spec(2, True), spec(3, True),
                pl.BlockSpec(logits.shape, lambda b, j: (0, 0, 0))]
    args = [p_view] * 6 + [logits]
    state_spec = pl.BlockSpec((1, n_heads, hd, hd), lambda b, j: (b, 0, 0, 0))
    state_shape = jax.ShapeDtypeStruct((bsz, n_heads, hd, hd), F32)
    if has_init:
        in_specs += [state_spec, state_spec]
        args += list(init_states)
    out_specs, out_shape = [], []
    if emit_o:
        o_shape = jax.ShapeDtypeStruct((bsz, n_chunks, CHUNK, d_b), BF16)
        out_specs += [pl.BlockSpec((1, cps, CHUNK, d_b), lambda b, j: (b, chunk(j, False), 0, 0)),
                      pl.BlockSpec((1, cps, CHUNK, d_b), lambda b, j: (b, chunk(j, True), 0, 0))]
        out_shape += [o_shape, o_shape]
    if emit_state:
        out_specs += [state_spec, state_spec]
        out_shape += [state_shape, state_shape]
    kern = functools.partial(_hgrn_kernel, n_heads=n_heads, hd=hd, has_init=has_init,
                             emit_o=emit_o, emit_state=emit_state)
    return pl.pallas_call(
        kern,
        grid=(bsz, n_steps),
        in_specs=in_specs,
        out_specs=out_specs,
        out_shape=out_shape,
        scratch_shapes=[pltpu.VMEM((n_heads, hd, hd), F32),
                        pltpu.VMEM((n_heads, hd, hd), F32),
                        pltpu.VMEM((2, 2 * CHUNK, d_b), BF16),
                        pltpu.VMEM((2, CHUNK, d_b), BF16),
                        pltpu.VMEM((2, CHUNK, d_b), F32),
                        pltpu.VMEM((2, CHUNK, d_b), BF16),
                        pltpu.VMEM((2, CHUNK, d_b), BF16),
                        pltpu.VMEM((2, CHUNK, d_b), BF16),
                        pltpu.VMEM((2, CHUNK, d_b), BF16),
                        pltpu.VMEM((2, 1, d_b), F32),
                        pltpu.VMEM((2, n_heads, CHUNK, CHUNK), BF16)],
        compiler_params=pltpu.CompilerParams(
            dimension_semantics=("parallel", "arbitrary"),
            vmem_limit_bytes=VMEM_LIMIT),
        name="hgrn2_latent" if emit_o else "hgrn2_context",
    )(*args)


def _outproj_kernel(ya_ref, of_ref, ob_ref, gb_ref, x_ref, gate_ref, hnw_ref, fnw_ref, w_ref,
                    o_ref, y_ref, *, n_heads, hd):
    _, rb, wb, d_a = ya_ref.shape
    rows = rb * wb
    d = x_ref.shape[3]
    y_ref[:, 0:d_a] = ya_ref[0].reshape(rows, d_a)
    hnw = hnw_ref[...]
    for h in range(n_heads):
        sl = slice(h * hd, (h + 1) * hd)
        o = of_ref[0, :, :, sl].astype(F32) + ob_ref[0, :, :, sl].astype(F32)
        ms = jnp.mean(o * o, axis=-1, keepdims=True)
        on = jnp.transpose(o * lax.rsqrt(ms + EPS) * hnw, (1, 0, 2))
        yb = on * gb_ref[0, :, :, sl].astype(F32)
        y_ref[:, d_a + h * hd:d_a + (h + 1) * hd] = yb.reshape(rows, hd).astype(BF16)
    acc = jnp.dot(y_ref[...], w_ref[...], preferred_element_type=F32)
    z = x_ref[0].reshape(rows, d) + gate_ref[0] * acc
    ms = jnp.mean(z * z, axis=-1, keepdims=True)
    o_ref[0] = (z * lax.rsqrt(ms + EPS) * fnw_ref[...]).reshape(rb, wb, d)


def _out_projection(ya4, of4, ob4, pa4, gb_block, x4, mod3, hnw, fnw, w_bf16, n_heads, rb, wb):
    bsz, n_r, n_w, d = x4.shape
    d_a = ya4.shape[3]
    d_b = of4.shape[3]
    hd = d_b // n_heads
    kern = functools.partial(_outproj_kernel, n_heads=n_heads, hd=hd)

    def raster(c, col=0):
        return pl.BlockSpec((1, rb, wb, c), lambda b, r, w: (b, r, w, col))

    def colmajor(c):
        return pl.BlockSpec((1, wb, rb, c), lambda b, r, w: (b, w, r, 0))

    return pl.pallas_call(
        kern,
        grid=(bsz, n_r // rb, n_w // wb),
        in_specs=[raster(d_a), colmajor(d_b), colmajor(d_b), raster(d_b, gb_block), raster(d),
                  pl.BlockSpec((1, 1, d), lambda b, r, w: (b, 0, 2)),
                  pl.BlockSpec((1, hd), lambda b, r, w: (0, 0)),
                  pl.BlockSpec((1, d), lambda b, r, w: (0, 0)),
                  pl.BlockSpec((d_a + d_b, d), lambda b, r, w: (0, 0),
                               pipeline_mode=pl.Buffered(1))],
        out_specs=raster(d),
        out_shape=jax.ShapeDtypeStruct((bsz, n_r, n_w, d), F32),
        scratch_shapes=[pltpu.VMEM((rb * wb, d_a + d_b), BF16)],
        compiler_params=pltpu.CompilerParams(
            dimension_semantics=("parallel", "parallel", "parallel"),
            vmem_limit_bytes=VMEM_LIMIT),
        name="out_projection",
    )(ya4, of4, ob4, pa4, x4, mod3, hnw.reshape(1, hd), fnw.reshape(1, d), w_bf16)


def kernel(x, c, ctx, c_ctx, ada_w, ada_b, norm_w, w_in, conv_w, conv_b, lru_wr, lru_br, lru_wi,
           lru_bi, lru_lambda, hgrn_lb_logits, hgrn_norm_w, w_out, final_norm_w):
    bsz, t_len, d = x.shape
    tc_len = ctx.shape[1]
    assert ada_w.shape[0] == 1, "single-layer stack only"
    d_a = conv_w.shape[2]
    d_b = hgrn_lb_logits.shape[2]
    hd = hgrn_norm_w.shape[1]
    n_heads = d_b // hd
    n_blocks_a = lru_wr.shape[2]
    n_cols = w_in.shape[2]
    assert t_len == GRID_W * CHUNK and tc_len % CHUNK == 0
    assert d_a // n_blocks_a == LANES and hd == LANES
    assert (2 * d_a) % d_b == 0 and n_cols == 2 * d_a + 5 * d_b
    first_b_group = (2 * d_a) // d_b
    n_rows_grid = t_len // GRID_W

    n_rows = -(-(bsz + 1) // SUBLANES) * SUBLANES
    cc = jnp.zeros((n_rows, d), F32).at[:bsz].set(c).at[bsz].set(c_ctx)
    mod3 = _modulation(cc, ada_w[0], ada_b[0]).reshape(n_rows, 1, 3 * d)

    w_in_b = w_in[0]
    w_out_b = w_out[0].astype(BF16)

    tm = 1024 if t_len % 1024 == 0 else t_len
    tn = 1024 if d_a % 1024 == 0 and d_b % 1024 == 0 else min(d_a, d_b)
    tpb = t_len // tm
    a_tiles = (2 * d_a) // tn
    b_tiles = (4 * d_b) // tn
    q_tiles = d_b // tn
    p_a = _in_projection(x.reshape(bsz * t_len, d), mod3, norm_w[0], w_in_b,
                         lambda i: i // tpb, lambda j: jnp.where(j < a_tiles, j, j + b_tiles),
                         2 * d_a + d_b, tm, tn, (d_a // tn, (2 * d_a + d_b) // tn))
    wb = 16
    p_b = _in_projection_colmajor(x.reshape(bsz, n_rows_grid, GRID_W, d), mod3, norm_w[0], w_in_b,
                                  lambda j: j + a_tiles, 4 * d_b, wb, tn, (0, q_tiles))
    xa_tiles = d_a // tn
    ctx_cols = d_a + 4 * d_b
    p_ctx = _in_projection(ctx.reshape(bsz * tc_len, d), mod3, norm_w[0], w_in_b,
                           lambda i: bsz, lambda j: jnp.where(j < xa_tiles, j, j + xa_tiles),
                           ctx_cols, bsz * tc_len, tn, (xa_tiles, xa_tiles + q_tiles))
    p_lat = p_a.reshape(bsz, t_len, 2 * d_a + d_b)
    p_ctx = p_ctx.reshape(bsz, tc_len, ctx_cols)

    def gate_w(dirn):
        return jnp.concatenate([lru_wr[0, dirn], lru_wi[0, dirn]], axis=-1).astype(BF16)

    def gate_b(dirn):
        return jnp.concatenate([lru_br[0, dirn].reshape(n_blocks_a, 1, LANES),
                                lru_bi[0, dirn].reshape(n_blocks_a, 1, LANES)], axis=-1)

    cb = 256 if d_a % 256 == 0 else LANES
    ya = _rglru(p_lat, p_ctx, conv_w[0], conv_b[0], gate_w(0), gate_b(0), gate_w(1), gate_b(1),
                lru_lambda[0], d_a, cb, tt=128)

    assert d_a % d_b == 0
    states = _hgrn(p_ctx.reshape(bsz, tc_len // CHUNK, CHUNK, ctx_cols), hgrn_lb_logits, d_b,
                   n_heads, d_a // d_b, None, False)
    o_f, o_b = _hgrn(p_b, hgrn_lb_logits, d_b, n_heads, 0, states, True)

    grid4 = lambda z: z.reshape(bsz, n_rows_grid, GRID_W, z.shape[-1])
    out = _out_projection(grid4(ya), o_f, o_b, grid4(p_lat), first_b_group, grid4(x), mod3,
                          hgrn_norm_w[0], final_norm_w, w_out_b, n_heads, 16, 16)
    return out.reshape(bsz, t_len, d)
```

```python
import functools

import jax
import jax.numpy as jnp
from jax import lax
from jax.experimental import pallas as pl
from jax.experimental.pallas import tpu as pltpu

GRID_W = 64
CHUNK = 64
LRU_C = 8.0
EPS = 1e-6
CONV_PAD_L = 2
LANES = 128
SUBLANES = 8
EXP2_CLAMP = 115.0
NEG_LOG2E = -1.4426950408889634
VMEM_LIMIT = 56 * 1024 * 1024
PROLOGUE_ROWS = 16
SEG_PAD = 4
CHUNKS_PER_STEP = 2
TILES_PER_TRIP = 4

F32 = jnp.float32
BF16 = jnp.bfloat16


def _sigmoid(z):
    return 1.0 / (1.0 + jnp.exp2(z * NEG_LOG2E))


def _silu(z):
    return z * _sigmoid(z)


def _softplus(z):
    return jnp.maximum(z, 0.0) + jnp.log1p(jnp.exp(-jnp.abs(z)))


def _mod_kernel(c_ref, w_ref, b_ref, o_ref):
    s = _silu(c_ref[...])
    o_ref[...] = jnp.dot(s.astype(BF16), w_ref[...].astype(BF16),
                         preferred_element_type=F32) + b_ref[...]


def _modulation(cc, w, b):
    rows, d = cc.shape
    n = w.shape[1]
    tn = 512 if n % 512 == 0 else n
    return pl.pallas_call(
        _mod_kernel,
        grid=(n // tn,),
        in_specs=[pl.BlockSpec((rows, d), lambda j: (0, 0)),
                  pl.BlockSpec((d, tn), lambda j: (0, j)),
                  pl.BlockSpec((1, tn), lambda j: (0, j))],
        out_specs=pl.BlockSpec((rows, tn), lambda j: (0, j)),
        out_shape=jax.ShapeDtypeStruct((rows, n), F32),
        compiler_params=pltpu.CompilerParams(
            dimension_semantics=("arbitrary",), vmem_limit_bytes=VMEM_LIMIT),
        name="adaln_modulation",
    )(cc, w, b.reshape(1, n))


def _store_projection(o_ref, res, j, silu_tiles):
    lo, hi = silu_tiles
    z = res.astype(o_ref.dtype)
    if hi <= lo:
        o_ref[...] = z
        return
    in_range = jnp.logical_and(j >= lo, j < hi)
    o_ref[...] = jnp.where(in_range, z * (0.5 * jnp.tanh(0.5 * z) + 0.5), z)


def _inproj_kernel(x_ref, shift_ref, scale_ref, nw_ref, w_ref, o_ref, h_ref, *, silu_tiles):
    @pl.when(pl.program_id(1) == 0)
    def _():
        rows = min(PROLOGUE_ROWS, x_ref.shape[0])
        gain = nw_ref[...] * (1.0 + scale_ref[0])

        def slab(s, carry):
            sl = pl.ds(pl.multiple_of(s * rows, rows), rows)
            x = x_ref[sl, :]
            rs = lax.rsqrt(jnp.mean(x * x, axis=-1, keepdims=True) + EPS)
            h_ref[sl, :] = (x_ref[sl, :] * rs * gain + shift_ref[0]).astype(BF16)
            return carry

        lax.fori_loop(0, x_ref.shape[0] // rows, slab, 0, unroll=4)

    res = jnp.dot(h_ref[...], w_ref[...].astype(BF16), preferred_element_type=F32)
    _store_projection(o_ref, res, pl.program_id(1), silu_tiles)


def _in_projection(x2d, mod3, norm_w, w_bf16, row_of_tile, wcol_of_tile, n_out, tm, tn, silu_tiles):
    m, d = x2d.shape
    return pl.pallas_call(
        functools.partial(_inproj_kernel, silu_tiles=silu_tiles),
        grid=(m // tm, n_out // tn),
        in_specs=[pl.BlockSpec((tm, d), lambda i, j: (i, 0)),
                  pl.BlockSpec((1, 1, d), lambda i, j: (row_of_tile(i), 0, 0)),
                  pl.BlockSpec((1, 1, d), lambda i, j: (row_of_tile(i), 0, 1)),
                  pl.BlockSpec((1, d), lambda i, j: (0, 0)),
                  pl.BlockSpec((d, tn), lambda i, j: (0, wcol_of_tile(j)))],
        out_specs=pl.BlockSpec((tm, tn), lambda i, j: (i, j)),
        out_shape=jax.ShapeDtypeStruct((m, n_out), BF16),
        scratch_shapes=[pltpu.VMEM((tm, d), BF16)],
        compiler_params=pltpu.CompilerParams(
            dimension_semantics=("parallel", "arbitrary"),
            vmem_limit_bytes=VMEM_LIMIT),
        name="in_projection",
    )(x2d, mod3, mod3, norm_w.reshape(1, d), w_bf16)


def _inproj_colmajor_kernel(x_ref, shift_ref, scale_ref, nw_ref, w_ref, o_ref, h_ref, *, slab,
                            silu_tiles):
    _, n_r, n_w, d = x_ref.shape

    rr = 2 * SUBLANES
    lanes = [slice(s * slab, (s + 1) * slab) for s in range(d // slab)]

    @pl.when(pl.program_id(2) == 0)
    def _():
        def row_group(g, carry):
            rows = pl.ds(pl.multiple_of(g * rr, rr), rr)
            sq = jnp.zeros((rr, n_w, slab), F32)
            for sl in lanes:
                xs = x_ref[0, rows, :, sl]
                sq = sq + xs * xs
            rs = lax.rsqrt(jnp.sum(sq, axis=-1, keepdims=True) * (1.0 / d) + EPS)
            for sl in lanes:
                y = x_ref[0, rows, :, sl] * rs * nw_ref[:, sl]
                y = y * (1.0 + scale_ref[0][:, sl]) + shift_ref[0][:, sl]
                h_ref[:, rows, sl] = jnp.transpose(y, (1, 0, 2)).astype(BF16)
            return carry

        lax.fori_loop(0, n_r // rr, row_group, 0)

    res = jnp.dot(h_ref[...].reshape(n_w * n_r, d), w_ref[...].astype(BF16),
                  preferred_element_type=F32)
    _store_projection(o_ref, res.reshape(1, n_w, n_r, res.shape[1]), pl.program_id(2), silu_tiles)


def _in_projection_colmajor(x4d, mod3, norm_w, w_bf16, wcol_of_tile, n_out, wb, tn, silu_tiles):
    bsz, n_r, n_w, d = x4d.shape
    kern = functools.partial(_inproj_colmajor_kernel, slab=2 * LANES, silu_tiles=silu_tiles)
    return pl.pallas_call(
        kern,
        grid=(bsz, n_w // wb, n_out // tn),
        in_specs=[pl.BlockSpec((1, n_r, wb, d), lambda b, w, j: (b, 0, w, 0)),
                  pl.BlockSpec((1, 1, d), lambda b, w, j: (b, 0, 0)),
                  pl.BlockSpec((1, 1, d), lambda b, w, j: (b, 0, 1)),
                  pl.BlockSpec((1, d), lambda b, w, j: (0, 0)),
                  pl.BlockSpec((d, tn), lambda b, w, j: (0, wcol_of_tile(j)))],
        out_specs=pl.BlockSpec((1, wb, n_r, tn), lambda b, w, j: (b, w, 0, j)),
        out_shape=jax.ShapeDtypeStruct((bsz, n_w, n_r, n_out), BF16),
        scratch_shapes=[pltpu.VMEM((wb, n_r, d), BF16)],
        compiler_params=pltpu.CompilerParams(
            dimension_semantics=("parallel", "parallel", "arbitrary"),
            vmem_limit_bytes=VMEM_LIMIT),
        name="in_projection_colmajor",
    )(x4d, mod3, mod3, norm_w.reshape(1, d), w_bf16)


def _group_scan(a, b, reverse):
    row = lax.broadcasted_iota(jnp.int32, a.shape, 1)
    for k in (1, 2, 4):
        if reverse:
            a_sh = pltpu.roll(a, SUBLANES - k, axis=1)
            b_sh = pltpu.roll(b, SUBLANES - k, axis=1)
            m = row < SUBLANES - k
        else:
            a_sh = pltpu.roll(a, k, axis=1)
            b_sh = pltpu.roll(b, k, axis=1)
            m = row >= k
        b = jnp.where(m, a * b_sh + b, b)
        a = jnp.where(m, a * a_sh, a)
    return a, b


def _sqrt_unit(x):
    return jnp.where(x > 0.0, x * lax.rsqrt(x), 0.0)


def _segment_scan(a, b, carry, reverse, scr_ref):
    tt = a.shape[0]
    seg = tt // SUBLANES
    pitch = seg + SEG_PAD
    for s in range(SUBLANES):
        rows = pl.ds(pitch * s, seg, stride=1)
        scr_ref[0, rows, :] = a[seg * s:seg * (s + 1), :]
        scr_ref[1, rows, :] = b[seg * s:seg * (s + 1), :]
    hs, ps = [None] * seg, [None] * seg
    h = p = None
    for j in (range(seg - 1, -1, -1) if reverse else range(seg)):
        step = pl.ds(j, SUBLANES, stride=pitch)
        a_j = scr_ref[0, step, :]
        b_j = scr_ref[1, step, :]
        h = b_j if h is None else a_j * h + b_j
        p = a_j if p is None else a_j * p
        hs[j], ps[j] = h, p
    g3, e3 = _group_scan(p[None], h[None], reverse)
    end = e3[0] + g3[0] * carry
    row = lax.broadcasted_iota(jnp.int32, end.shape, 0)
    if reverse:
        enter = jnp.where(row == SUBLANES - 1, carry, pltpu.roll(end, SUBLANES - 1, axis=0))
        carry_out = end[0:1, :]
    else:
        enter = jnp.where(row == 0, carry, pltpu.roll(end, 1, axis=0))
        carry_out = end[SUBLANES - 1:SUBLANES, :]
    for j in range(seg):
        scr_ref[2, pl.ds(j, SUBLANES, stride=pitch), :] = hs[j] + ps[j] * enter
    h_time = [scr_ref[2, pl.ds(pitch * s, seg, stride=1), :] for s in range(SUBLANES)]
    return jnp.concatenate(h_time, axis=0), carry_out


def _lru_block(u, wg, bg, sp, carry, reverse, scr_ref):
    g = jnp.dot(u.astype(BF16), wg, preferred_element_type=F32) + bg
    r = _sigmoid(g[:, :LANES])
    i = _sigmoid(g[:, LANES:])
    a = jnp.exp2(r * sp)
    b = _sqrt_unit(1.0 - a * a) * (i * u)
    return _segment_scan(a, b, carry, reverse, scr_ref)


def _rglru_kernel(xa_ref, ga_ref, xc_ref, cw_ref, cb_ref, wgf_ref, bgf_ref, wgb_ref, bgb_ref,
                  lam_ref, o_ref, xf_ref, xcf_ref, u_ref, hf_ref, scr_ref, *, tt):
    t_len = xa_ref.shape[1]
    tc_len = xc_ref.shape[1]
    nblk = xa_ref.shape[2] // LANES
    lanes = [slice(k * LANES, (k + 1) * LANES) for k in range(nblk)]
    zeros = jnp.zeros((SUBLANES, LANES), F32)
    for k, sl in enumerate(lanes):
        xf_ref[k, 0:SUBLANES, :] = zeros
        xf_ref[k, SUBLANES:SUBLANES + t_len, :] = xa_ref[0, :, sl].astype(F32)
        xf_ref[k, SUBLANES + t_len:, :] = zeros
        xcf_ref[k, 0:SUBLANES, :] = zeros
        xcf_ref[k, SUBLANES:SUBLANES + tc_len, :] = xc_ref[0, :, sl].astype(F32)
        xcf_ref[k, SUBLANES + tc_len:, :] = zeros

    sp_f = _softplus(-lam_ref[0:1, :]) * (LRU_C * NEG_LOG2E)
    sp_b = _softplus(-lam_ref[1:2, :]) * (LRU_C * NEG_LOG2E)

    def conv(src_ref, k, t0):
        u = cb_ref[:, lanes[k]]
        for tap in range(cw_ref.shape[0]):
            off = SUBLANES - CONV_PAD_L + tap
            if off % SUBLANES == 0:
                start = t0 + off
                rows = pl.ds(start if isinstance(start, int) else pl.multiple_of(start, SUBLANES), tt)
            else:
                rows = pl.ds(t0 + off, tt, stride=1)
            u = u + src_ref[k, rows, :] * cw_ref[tap:tap + 1, lanes[k]]
        return u

    def block(u, k, carry, reverse, slot):
        scr = scr_ref.at[k * TILES_PER_TRIP + slot]
        if reverse:
            return _lru_block(u, wgb_ref[k], bgb_ref[k], sp_b[:, lanes[k]], carry, True, scr)
        return _lru_block(u, wgf_ref[k], bgf_ref[k], sp_f[:, lanes[k]], carry, False, scr)

    n_lat = t_len // tt
    n_ctx = tc_len // tt
    assert n_lat % TILES_PER_TRIP == 0
    carry0 = tuple(jnp.zeros((1, LANES), F32) for _ in lanes)

    def tile_start(trip, slot, n_tiles, reverse):
        idx = trip * TILES_PER_TRIP + slot
        return pl.multiple_of((n_tiles - 1 - idx if reverse else idx) * tt, tt)

    def ctx_pass(reverse):
        carry = carry0
        for idx in range(n_ctx):
            t0 = (n_ctx - 1 - idx if reverse else idx) * tt
            carry = tuple(block(conv(xcf_ref, k, t0), k, carry[k], reverse, idx % TILES_PER_TRIP)[1]
                          for k in range(nblk))
        return carry

    def lat_f(trip, carry):
        for slot in range(TILES_PER_TRIP):
            t0 = tile_start(trip, slot, n_lat, False)
            rows = pl.ds(t0, tt)
            out = []
            for k, sl in enumerate(lanes):
                u = conv(xf_ref, k, t0)
                u_ref[rows, sl] = u
                h, c_out = block(u, k, carry[k], False, slot)
                hf_ref[rows, sl] = h
                out.append(c_out)
            carry = tuple(out)
        return carry

    lax.fori_loop(0, n_lat // TILES_PER_TRIP, lat_f, ctx_pass(False))

    def lat_b(trip, carry):
        for slot in range(TILES_PER_TRIP):
            rows = pl.ds(tile_start(trip, slot, n_lat, True), tt)
            out = []
            for k, sl in enumerate(lanes):
                h, c_out = block(u_ref[rows, sl], k, carry[k], True, slot)
                gate = ga_ref[0, rows, sl].astype(F32)
                o_ref[0, rows, sl] = ((hf_ref[rows, sl] + h) * gate).astype(o_ref.dtype)
                out.append(c_out)
            carry = tuple(out)
        return carry

    lax.fori_loop(0, n_lat // TILES_PER_TRIP, lat_b, ctx_pass(True))


def _rglru(p_lat, p_ctx, conv_w, conv_b, wg_f, bg_f, wg_b, bg_b, lam, d_a, cb, tt):
    bsz, t_len, _ = p_lat.shape
    tc_len = p_ctx.shape[1]
    nblk = cb // LANES
    ncb = d_a // cb
    kern = functools.partial(_rglru_kernel, tt=tt)
    return pl.pallas_call(
        kern,
        grid=(bsz, ncb),
        in_specs=[pl.BlockSpec((1, t_len, cb), lambda b, c: (b, 0, c)),
                  pl.BlockSpec((1, t_len, cb), lambda b, c: (b, 0, ncb + c)),
                  pl.BlockSpec((1, tc_len, cb), lambda b, c: (b, 0, c)),
                  pl.BlockSpec((conv_w.shape[0], cb), lambda b, c: (0, c)),
                  pl.BlockSpec((1, cb), lambda b, c: (0, c)),
                  pl.BlockSpec((nblk, LANES, 2 * LANES), lambda b, c: (c, 0, 0)),
                  pl.BlockSpec((nblk, 1, 2 * LANES), lambda b, c: (c, 0, 0)),
                  pl.BlockSpec((nblk, LANES, 2 * LANES), lambda b, c: (c, 0, 0)),
                  pl.BlockSpec((nblk, 1, 2 * LANES), lambda b, c: (c, 0, 0)),
                  pl.BlockSpec((2, cb), lambda b, c: (0, c))],
        out_specs=pl.BlockSpec((1, t_len, cb), lambda b, c: (b, 0, c)),
        out_shape=jax.ShapeDtypeStruct((bsz, t_len, d_a), BF16),
        scratch_shapes=[pltpu.VMEM((nblk, t_len + 2 * SUBLANES, LANES), F32),
                        pltpu.VMEM((nblk, tc_len + 2 * SUBLANES, LANES), F32),
                        pltpu.VMEM((t_len, cb), F32),
                        pltpu.VMEM((t_len, cb), F32),
                        pltpu.VMEM((nblk * TILES_PER_TRIP, 3, tt + SUBLANES * SEG_PAD, LANES),
                                   F32)],
        compiler_params=pltpu.CompilerParams(
            dimension_semantics=("parallel", "parallel"),
            vmem_limit_bytes=VMEM_LIMIT),
        name="rglru",
    )(p_lat, p_lat, p_ctx, conv_w, conv_b.reshape(1, d_a), wg_f, bg_f, wg_b, bg_b, lam)


_NT = (((1,), (1,)), ((), ()))
_TN = (((0,), (0,)), ((), ()))


class _HgrnDir:
    def __init__(self, d, q_ref, f_ref, v_ref, lb, st_ref, o_ref, scratch, reverse, n_heads, hd,
                 sub):
        self.d, self.q_ref, self.f_ref, self.v_ref, self.lb = d, q_ref, f_ref, v_ref, lb
        self.sub = sub
        self.st_ref, self.o_ref, self.reverse, self.n_heads, self.hd = st_ref, o_ref, reverse, n_heads, hd
        (self.hl_ref, self.kk_ref, self.cum_ref, self.qd_ref, self.kd_ref, self.qin_ref,
         self.kout_ref, self.dec_ref, self.sc_ref) = scratch
        self.c_len = q_ref.shape[2]
        r_i = lax.broadcasted_iota(jnp.int32, (self.c_len, self.c_len), 0)
        c_i = lax.broadcasted_iota(jnp.int32, (self.c_len, self.c_len), 1)
        self.tri = (r_i <= c_i) if reverse else (r_i >= c_i)

    def _slabs(self, width):
        total = self.n_heads * self.hd
        return [slice(s, s + width) for s in range(0, total, width)]


    def _gate_slab(self, sl):
        d, c_len = self.d, self.c_len
        lb = self.lb[:, sl]
        for r0 in (0, c_len // 2):
            rows = slice(r0, r0 + c_len // 2)
            f = lb + (1.0 - lb) * _sigmoid(self.f_ref[0, self.sub, rows, sl].astype(F32))
            logf = jnp.log2(f)
            self.kk_ref[d, rows, sl] = (1.0 - f).astype(BF16)
            hi = logf.astype(BF16)
            self.hl_ref[d, rows, sl] = hi
            self.hl_ref[d, c_len + r0:c_len + r0 + c_len // 2, sl] = (
                logf - hi.astype(F32)).astype(BF16)

    def _cumulate(self):
        tri_b = self.tri.astype(BF16)
        tri2 = jnp.concatenate([tri_b, tri_b], axis=1)
        self.cum_ref[self.d] = jnp.dot(tri2, self.hl_ref[self.d], preferred_element_type=F32)

    def _decay_slab(self, sl):
        d, c_len = self.d, self.c_len
        half = c_len // 2
        row_last = 0 if self.reverse else c_len - 1
        row_ref = half if self.reverse else half - 1
        last = self.cum_ref[d, row_last:row_last + 1, sl]
        ref = self.cum_ref[d, row_ref:row_ref + 1, sl]
        self.dec_ref[d, :, sl] = jnp.exp2(last)
        for rows in (slice(0, half), slice(half, c_len)):
            cum = self.cum_ref[d, rows, sl]
            kk = self.kk_ref[d, rows, sl]
            qs = self.q_ref[0, self.sub, rows, sl]
            self.qd_ref[d, rows, sl] = qs * jnp.exp2(jnp.minimum(cum - ref, EXP2_CLAMP)).astype(BF16)
            self.kd_ref[d, rows, sl] = kk * jnp.exp2(jnp.minimum(ref - cum, EXP2_CLAMP)).astype(BF16)
            self.qin_ref[d, rows, sl] = qs * jnp.exp2(cum).astype(BF16)
            self.kout_ref[d, rows, sl] = kk * jnp.exp2(last - cum).astype(BF16)

    def _head(self, h):
        return slice(h * self.hd, (h + 1) * self.hd)

    def _scores(self, h):
        d, sl = self.d, self._head(h)
        s = lax.dot_general(self.qd_ref[d, :, sl], self.kd_ref[d, :, sl], _NT,
                            preferred_element_type=F32)
        self.sc_ref[d, h] = jnp.where(self.tri, s, 0.0).astype(BF16)

    def _output(self, h):
        d, sl = self.d, self._head(h)
        o = (jnp.dot(self.sc_ref[d, h], self.v_ref[0, self.sub, :, sl], preferred_element_type=F32)
             + lax.dot_general(self.qin_ref[d, :, sl], self.st_ref[h].astype(BF16), _NT,
                               preferred_element_type=F32))
        self.o_ref[0, self.sub, :, sl] = o.astype(self.o_ref.dtype)

    def _update(self, h):
        d, sl = self.d, self._head(h)
        self.st_ref[h] = (self.st_ref[h] * self.dec_ref[d, :, sl]
                          + lax.dot_general(self.v_ref[0, self.sub, :, sl], self.kout_ref[d, :, sl], _TN,
                                            preferred_element_type=F32))

    def stages(self):
        heads = range(self.n_heads)
        slabs = self._slabs(LANES)
        out = [[functools.partial(self._gate_slab, sl) for sl in slabs], [self._cumulate],
               [functools.partial(self._decay_slab, sl) for sl in slabs]]
        if self.o_ref is not None:
            out += [[functools.partial(self._scores, h) for h in heads],
                    [functools.partial(self._output, h) for h in heads]]
        return out + [[functools.partial(self._update, h) for h in heads]]


def _lower_bounds(logits_ref, layer):
    out = []
    for d in range(2):
        rows = [logits_ref[d, l:l + 1, :] for l in range(logits_ref.shape[1])]
        m = functools.reduce(jnp.maximum, rows)
        e = [jnp.exp(r - m) for r in rows]
        out.append(sum(e[:layer + 1]) / sum(e))
    return out


def _hgrn_kernel(*refs, n_heads, hd, has_init, emit_o, emit_state):
    qf_ref, ff_ref, vf_ref, qb_ref, fb_ref, vb_ref, lg_ref = refs[:7]
    pos = 7
    if has_init:
        s0f_ref, s0b_ref = refs[pos:pos + 2]
        pos += 2
    if emit_o:
        of_ref, ob_ref = refs[pos:pos + 2]
        pos += 2
    if emit_state:
        sof_ref, sob_ref = refs[pos:pos + 2]
        pos += 2
    sf_ref, sb_ref = refs[pos:pos + 2]
    scratch = refs[pos + 2:]

    j = pl.program_id(1)

    @pl.when(j == 0)
    def _():
        if has_init:
            sf_ref[...] = s0f_ref[0]
            sb_ref[...] = s0b_ref[0]
        else:
            sf_ref[...] = jnp.zeros_like(sf_ref)
            sb_ref[...] = jnp.zeros_like(sb_ref)

    lb_f, lb_b = _lower_bounds(lg_ref, 0)
    n_sub = qf_ref.shape[1]
    for k in range(n_sub):
        dirs = [_HgrnDir(0, qf_ref, ff_ref, vf_ref, lb_f, sf_ref, of_ref if emit_o else None,
                         scratch, False, n_heads, hd, k),
                _HgrnDir(1, qb_ref, fb_ref, vb_ref, lb_b, sb_ref, ob_ref if emit_o else None,
                         scratch, True, n_heads, hd, n_sub - 1 - k)]
        for stage_f, stage_b in zip(dirs[0].stages(), dirs[1].stages()):
            for unit in stage_f + stage_b:
                unit()

    if emit_state:
        @pl.when(j == pl.num_programs(1) - 1)
        def _():
            sof_ref[0] = sf_ref[...]
            sob_ref[0] = sb_ref[...]


def _hgrn(p_view, logits, d_b, n_heads, first_group, init_states, emit_o):
    bsz, n_chunks = p_view.shape[:2]
    hd = d_b // n_heads
    has_init = init_states is not None
    emit_state = not emit_o

    cps = CHUNKS_PER_STEP if n_chunks % CHUNKS_PER_STEP == 0 else 1
    n_steps = n_chunks // cps

    def chunk(j, reverse):
        return n_steps - 1 - j if reverse else j

    def spec(group, reverse):
        def imap(b, j):
            return (b, chunk(j, reverse), 0, first_group + group)
        return pl.BlockSpec((1, cps, CHUNK, d_b), imap)

    in_specs = [spec(0, False), spec(1, False), spec(3, False),
                spec(0, True), spec(2, True), spec(3, True),
                pl.BlockSpec(logits.shape, lambda b, j: (0, 0, 0))]
    args = [p_view] * 6 + [logits]
    state_spec = pl.BlockSpec((1, n_heads, hd, hd), lambda b, j: (b, 0, 0, 0))
    state_shape = jax.ShapeDtypeStruct((bsz, n_heads, hd, hd), F32)
    if has_init:
        in_specs += [state_spec, state_spec]
        args += list(init_states)
    out_specs, out_shape = [], []
    if emit_o:
        o_shape = jax.ShapeDtypeStruct((bsz, n_chunks, CHUNK, d_b), BF16)
        out_specs += [pl.BlockSpec((1, cps, CHUNK, d_b), lambda b, j: (b, chunk(j, False), 0, 0)),
                      pl.BlockSpec((1, cps, CHUNK, d_b), lambda b, j: (b, chunk(j, True), 0, 0))]
        out_shape += [o_shape, o_shape]
    if emit_state:
        out_specs += [state_spec, state_spec]
        out_shape += [state_shape, state_shape]
    kern = functools.partial(_hgrn_kernel, n_heads=n_heads, hd=hd, has_init=has_init,
                             emit_o=emit_o, emit_state=emit_state)
    return pl.pallas_call(
        kern,
        grid=(bsz, n_steps),
        in_specs=in_specs,
        out_specs=out_specs,
        out_shape=out_shape,
        scratch_shapes=[pltpu.VMEM((n_heads, hd, hd), F32),
                        pltpu.VMEM((n_heads, hd, hd), F32),
                        pltpu.VMEM((2, 2 * CHUNK, d_b), BF16),
                        pltpu.VMEM((2, CHUNK, d_b), BF16),
                        pltpu.VMEM((2, CHUNK, d_b), F32),
                        pltpu.VMEM((2, CHUNK, d_b), BF16),
                        pltpu.VMEM((2, CHUNK, d_b), BF16),
                        pltpu.VMEM((2, CHUNK, d_b), BF16),
                        pltpu.VMEM((2, CHUNK, d_b), BF16),
                        pltpu.VMEM((2, 1, d_b), F32),
                        pltpu.VMEM((2, n_heads, CHUNK, CHUNK), BF16)],
        compiler_params=pltpu.CompilerParams(
            dimension_semantics=("parallel", "arbitrary"),
            vmem_limit_bytes=VMEM_LIMIT),
        name="hgrn2_latent" if emit_o else "hgrn2_context",
    )(*args)


def _outproj_kernel(ya_ref, of_ref, ob_ref, gb_ref, x_ref, gate_ref, hnw_ref, fnw_ref, w_ref,
                    o_ref, y_ref, *, n_heads, hd):
    _, rb, wb, d_a = ya_ref.shape
    rows = rb * wb
    d = x_ref.shape[3]
    y_ref[:, 0:d_a] = ya_ref[0].reshape(rows, d_a)
    hnw = hnw_ref[...]
    for h in range(n_heads):
        sl = slice(h * hd, (h + 1) * hd)
        o = of_ref[0, :, :, sl].astype(F32) + ob_ref[0, :, :, sl].astype(F32)
        ms = jnp.mean(o * o, axis=-1, keepdims=True)
        on = jnp.transpose(o * lax.rsqrt(ms + EPS) * hnw, (1, 0, 2))
        yb = on * gb_ref[0, :, :, sl].astype(F32)
        y_ref[:, d_a + h * hd:d_a + (h + 1) * hd] = yb.reshape(rows, hd).astype(BF16)
    acc = jnp.dot(y_ref[...], w_ref[...], preferred_element_type=F32)
    z = x_ref[0].reshape(rows, d) + gate_ref[0] * acc
    ms = jnp.mean(z * z, axis=-1, keepdims=True)
    o_ref[0] = (z * lax.rsqrt(ms + EPS) * fnw_ref[...]).reshape(rb, wb, d)


def _out_projection(ya4, of4, ob4, pa4, gb_block, x4, mod3, hnw, fnw, w_bf16, n_heads, rb, wb):
    bsz, n_r, n_w, d = x4.shape
    d_a = ya4.shape[3]
    d_b = of4.shape[3]
    hd = d_b // n_heads
    kern = functools.partial(_outproj_kernel, n_heads=n_heads, hd=hd)

    def raster(c, col=0):
        return pl.BlockSpec((1, rb, wb, c), lambda b, r, w: (b, r, w, col))

    def colmajor(c):
        return pl.BlockSpec((1, wb, rb, c), lambda b, r, w: (b, w, r, 0))

    return pl.pallas_call(
        kern,
        grid=(bsz, n_r // rb, n_w // wb),
        in_specs=[raster(d_a), colmajor(d_b), colmajor(d_b), raster(d_b, gb_block), raster(d),
                  pl.BlockSpec((1, 1, d), lambda b, r, w: (b, 0, 2)),
                  pl.BlockSpec((1, hd), lambda b, r, w: (0, 0)),
                  pl.BlockSpec((1, d), lambda b, r, w: (0, 0)),
                  pl.BlockSpec((d_a + d_b, d), lambda b, r, w: (0, 0),
                               pipeline_mode=pl.Buffered(1))],
        out_specs=raster(d),
        out_shape=jax.ShapeDtypeStruct((bsz, n_r, n_w, d), F32),
        scratch_shapes=[pltpu.VMEM((rb * wb, d_a + d_b), BF16)],
        compiler_params=pltpu.CompilerParams(
            dimension_semantics=("parallel", "parallel", "parallel"),
            vmem_limit_bytes=VMEM_LIMIT),
        name="out_projection",
    )(ya4, of4, ob4, pa4, x4, mod3, hnw.reshape(1, hd), fnw.reshape(1, d), w_bf16)


def kernel(x, c, ctx, c_ctx, ada_w, ada_b, norm_w, w_in, conv_w, conv_b, lru_wr, lru_br, lru_wi,
           lru_bi, lru_lambda, hgrn_lb_logits, hgrn_norm_w, w_out, final_norm_w):
    bsz, t_len, d = x.shape
    tc_len = ctx.shape[1]
    assert ada_w.shape[0] == 1, "single-layer stack only"
    d_a = conv_w.shape[2]
    d_b = hgrn_lb_logits.shape[2]
    hd = hgrn_norm_w.shape[1]
    n_heads = d_b // hd
    n_blocks_a = lru_wr.shape[2]
    n_cols = w_in.shape[2]
    assert t_len == GRID_W * CHUNK and tc_len % CHUNK == 0
    assert d_a // n_blocks_a == LANES and hd == LANES
    assert (2 * d_a) % d_b == 0 and n_cols == 2 * d_a + 5 * d_b
    first_b_group = (2 * d_a) // d_b
    n_rows_grid = t_len // GRID_W

    n_rows = -(-(bsz + 1) // SUBLANES) * SUBLANES
    cc = jnp.zeros((n_rows, d), F32).at[:bsz].set(c).at[bsz].set(c_ctx)
    mod3 = _modulation(cc, ada_w[0], ada_b[0]).reshape(n_rows, 1, 3 * d)

    w_in_b = w_in[0]
    w_out_b = w_out[0].astype(BF16)

    tm = 1024 if t_len % 1024 == 0 else t_len
    tn = 1024 if d_a % 1024 == 0 and d_b % 1024 == 0 else min(d_a, d_b)
    tpb = t_len // tm
    a_tiles = (2 * d_a) // tn
    b_tiles = (4 * d_b) // tn
    q_tiles = d_b // tn
    p_a = _in_projection(x.reshape(bsz * t_len, d), mod3, norm_w[0], w_in_b,
                         lambda i: i // tpb, lambda j: jnp.where(j < a_tiles, j, j + b_tiles),
                         2 * d_a + d_b, tm, tn, (d_a // tn, (2 * d_a + d_b) // tn))
    wb = 16
    p_b = _in_projection_colmajor(x.reshape(bsz, n_rows_grid, GRID_W, d), mod3, norm_w[0], w_in_b,
                                  lambda j: j + a_tiles, 4 * d_b, wb, tn, (0, q_tiles))
    xa_tiles = d_a // tn
    ctx_cols = d_a + 4 * d_b
    p_ctx = _in_projection(ctx.reshape(bsz * tc_len, d), mod3, norm_w[0], w_in_b,
                           lambda i: bsz, lambda j: jnp.where(j < xa_tiles, j, j + xa_tiles),
                           ctx_cols, bsz * tc_len, tn, (xa_tiles, xa_tiles + q_tiles))
    p_lat = p_a.reshape(bsz, t_len, 2 * d_a + d_b)
    p_ctx = p_ctx.reshape(bsz, tc_len, ctx_cols)

    def gate_w(dirn):
        return jnp.concatenate([lru_wr[0, dirn], lru_wi[0, dirn]], axis=-1).astype(BF16)

    def gate_b(dirn):
        return jnp.concatenate([lru_br[0, dirn].reshape(n_blocks_a, 1, LANES),
                                lru_bi[0, dirn].reshape(n_blocks_a, 1, LANES)], axis=-1)

    cb = 256 if d_a % 256 == 0 else LANES
    ya = _rglru(p_lat, p_ctx, conv_w[0], conv_b[0], gate_w(0), gate_b(0), gate_w(1), gate_b(1),
                lru_lambda[0], d_a, cb, tt=128)

    assert d_a % d_b == 0
    states = _hgrn(p_ctx.reshape(bsz, tc_len // CHUNK, CHUNK, ctx_cols), hgrn_lb_logits, d_b,
                   n_heads, d_a // d_b, None, False)
    o_f, o_b = _hgrn(p_b, hgrn_lb_logits, d_b, n_heads, 0, states, True)

    grid4 = lambda z: z.reshape(bsz, n_rows_grid, GRID_W, z.shape[-1])
    out = _out_projection(grid4(ya), o_f, o_b, grid4(p_lat), first_b_group, grid4(x), mod3,
                          hgrn_norm_w[0], final_norm_w, w_out_b, n_heads, 16, 16)
    return out.reshape(bsz, t_len, d)
```

```python
import functools

import jax
import jax.numpy as jnp
from jax import lax
from jax.experimental import pallas as pl
from jax.experimental.pallas import tpu as pltpu

GRID_W = 64
CHUNK = 64
LRU_C = 8.0
EPS = 1e-6
CONV_PAD_L = 2
LANES = 128
SUBLANES = 8
EXP2_CLAMP = 115.0
NEG_LOG2E = -1.4426950408889634
VMEM_LIMIT = 56 * 1024 * 1024
PROLOGUE_ROWS = 16
SEG_PAD = 4
CHUNKS_PER_STEP = 2
TILES_PER_TRIP = 4

F32 = jnp.float32
BF16 = jnp.bfloat16


def _sigmoid(z):
    return 1.0 / (1.0 + jnp.exp2(z * NEG_LOG2E))


def _silu(z):
    return z * _sigmoid(z)


def _softplus(z):
    return jnp.maximum(z, 0.0) + jnp.log1p(jnp.exp(-jnp.abs(z)))


def _mod_kernel(c_ref, w_ref, b_ref, o_ref):
    s = _silu(c_ref[...])
    o_ref[...] = jnp.dot(s.astype(BF16), w_ref[...].astype(BF16),
                         preferred_element_type=F32) + b_ref[...]


def _modulation(cc, w, b):
    rows, d = cc.shape
    n = w.shape[1]
    tn = 512 if n % 512 == 0 else n
    return pl.pallas_call(
        _mod_kernel,
        grid=(n // tn,),
        in_specs=[pl.BlockSpec((rows, d), lambda j: (0, 0)),
                  pl.BlockSpec((d, tn), lambda j: (0, j)),
                  pl.BlockSpec((1, tn), lambda j: (0, j))],
        out_specs=pl.BlockSpec((rows, tn), lambda j: (0, j)),
        out_shape=jax.ShapeDtypeStruct((rows, n), F32),
        compiler_params=pltpu.CompilerParams(
            dimension_semantics=("arbitrary",), vmem_limit_bytes=VMEM_LIMIT),
        name="adaln_modulation",
    )(cc, w, b.reshape(1, n))


def _store_projection(o_ref, res, j, silu_tiles):
    lo, hi = silu_tiles
    z = res.astype(o_ref.dtype)
    if hi <= lo:
        o_ref[...] = z
        return
    in_range = jnp.logical_and(j >= lo, j < hi)
    o_ref[...] = jnp.where(in_range, z * (0.5 * jnp.tanh(0.5 * z) + 0.5), z)


def _inproj_kernel(x_ref, shift_ref, scale_ref, nw_ref, w_ref, o_ref, h_ref, *, silu_tiles):
    @pl.when(pl.program_id(1) == 0)
    def _():
        rows = min(PROLOGUE_ROWS, x_ref.shape[0])
        gain = nw_ref[...] * (1.0 + scale_ref[0])

        def slab(s, carry):
            sl = pl.ds(pl.multiple_of(s * rows, rows), rows)
            x = x_ref[sl, :]
            rs = lax.rsqrt(jnp.mean(x * x, axis=-1, keepdims=True) + EPS)
            h_ref[sl, :] = (x_ref[sl, :] * rs * gain + shift_ref[0]).astype(BF16)
            return carry

        lax.fori_loop(0, x_ref.shape[0] // rows, slab, 0, unroll=4)

    res = jnp.dot(h_ref[...], w_ref[...].astype(BF16), preferred_element_type=F32)
    _store_projection(o_ref, res, pl.program_id(1), silu_tiles)


def _in_projection(x2d, mod3, norm_w, w_bf16, row_of_tile, wcol_of_tile, n_out, tm, tn, silu_tiles):
    m, d = x2d.shape
    return pl.pallas_call(
        functools.partial(_inproj_kernel, silu_tiles=silu_tiles),
        grid=(m // tm, n_out // tn),
        in_specs=[pl.BlockSpec((tm, d), lambda i, j: (i, 0)),
                  pl.BlockSpec((1, 1, d), lambda i, j: (row_of_tile(i), 0, 0)),
                  pl.BlockSpec((1, 1, d), lambda i, j: (row_of_tile(i), 0, 1)),
                  pl.BlockSpec((1, d), lambda i, j: (0, 0)),
                  pl.BlockSpec((d, tn), lambda i, j: (0, wcol_of_tile(j)))],
        out_specs=pl.BlockSpec((tm, tn), lambda i, j: (i, j)),
        out_shape=jax.ShapeDtypeStruct((m, n_out), BF16),
        scratch_shapes=[pltpu.VMEM((tm, d), BF16)],
        compiler_params=pltpu.CompilerParams(
            dimension_semantics=("parallel", "arbitrary"),
            vmem_limit_bytes=VMEM_LIMIT),
        name="in_projection",
    )(x2d, mod3, mod3, norm_w.reshape(1, d), w_bf16)


def _inproj_colmajor_kernel(x_ref, shift_ref, scale_ref, nw_ref, w_ref, o_ref, h_ref, *, slab,
                            silu_tiles):
    _, n_r, n_w, d = x_ref.shape

    rr = 2 * SUBLANES
    lanes = [slice(s * slab, (s + 1) * slab) for s in range(d // slab)]

    @pl.when(pl.program_id(2) == 0)
    def _():
        def row_group(g, carry):
            rows = pl.ds(pl.multiple_of(g * rr, rr), rr)
            sq = jnp.zeros((rr, n_w, slab), F32)
            for sl in lanes:
                xs = x_ref[0, rows, :, sl]
                sq = sq + xs * xs
            rs = lax.rsqrt(jnp.sum(sq, axis=-1, keepdims=True) * (1.0 / d) + EPS)
            for sl in lanes:
                y = x_ref[0, rows, :, sl] * rs * nw_ref[:, sl]
                y = y * (1.0 + scale_ref[0][:, sl]) + shift_ref[0][:, sl]
                h_ref[:, rows, sl] = jnp.transpose(y, (1, 0, 2)).astype(BF16)
            return carry

        lax.fori_loop(0, n_r // rr, row_group, 0)

    res = jnp.dot(h_ref[...].reshape(n_w * n_r, d), w_ref[...].astype(BF16),
                  preferred_element_type=F32)
    _store_projection(o_ref, res.reshape(1, n_w, n_r, res.shape[1]), pl.program_id(2), silu_tiles)


def _in_projection_colmajor(x4d, mod3, norm_w, w_bf16, wcol_of_tile, n_out, wb, tn, silu_tiles):
    bsz, n_r, n_w, d = x4d.shape
    kern = functools.partial(_inproj_colmajor_kernel, slab=2 * LANES, silu_tiles=silu_tiles)
    return pl.pallas_call(
        kern,
        grid=(bsz, n_w // wb, n_out // tn),
        in_specs=[pl.BlockSpec((1, n_r, wb, d), lambda b, w, j: (b, 0, w, 0)),
                  pl.BlockSpec((1, 1, d), lambda b, w, j: (b, 0, 0)),
                  pl.BlockSpec((1, 1, d), lambda b, w, j: (b, 0, 1)),
                  pl.BlockSpec((1, d), lambda b, w, j: (0, 0)),
                  pl.BlockSpec((d, tn), lambda b, w, j: (0, wcol_of_tile(j)))],
        out_specs=pl.BlockSpec((1, wb, n_r, tn), lambda b, w, j: (b, w, 0, j)),
        out_shape=jax.ShapeDtypeStruct((bsz, n_w, n_r, n_out), BF16),
        scratch_shapes=[pltpu.VMEM((wb, n_r, d), BF16)],
        compiler_params=pltpu.CompilerParams(
            dimension_semantics=("parallel", "parallel", "arbitrary"),
            vmem_limit_bytes=VMEM_LIMIT),
        name="in_projection_colmajor",
    )(x4d, mod3, mod3, norm_w.reshape(1, d), w_bf16)


def _group_scan(a, b, reverse):
    row = lax.broadcasted_iota(jnp.int32, a.shape, 1)
    for k in (1, 2, 4):
        if reverse:
            a_sh = pltpu.roll(a, SUBLANES - k, axis=1)
            b_sh = pltpu.roll(b, SUBLANES - k, axis=1)
            m = row < SUBLANES - k
        else:
            a_sh = pltpu.roll(a, k, axis=1)
            b_sh = pltpu.roll(b, k, axis=1)
            m = row >= k
        b = jnp.where(m, a * b_sh + b, b)
        a = jnp.where(m, a * a_sh, a)
    return a, b


def _sqrt_unit(x):
    return jnp.where(x > 0.0, x * lax.rsqrt(x), 0.0)


def _segment_scan(a, b, carry, reverse, scr_ref):
    tt = a.shape[0]
    seg = tt // SUBLANES
    pitch = seg + SEG_PAD
    for s in range(SUBLANES):
        rows = pl.ds(pitch * s, seg, stride=1)
        scr_ref[0, rows, :] = a[seg * s:seg * (s + 1), :]
        scr_ref[1, rows, :] = b[seg * s:seg * (s + 1), :]
    hs, ps = [None] * seg, [None] * seg
    h = p = None
    for j in (range(seg - 1, -1, -1) if reverse else range(seg)):
        step = pl.ds(j, SUBLANES, stride=pitch)
        a_j = scr_ref[0, step, :]
        b_j = scr_ref[1, step, :]
        h = b_j if h is None else a_j * h + b_j
        p = a_j if p is None else a_j * p
        hs[j], ps[j] = h, p
    g3, e3 = _group_scan(p[None], h[None], reverse)
    end = e3[0] + g3[0] * carry
    row = lax.broadcasted_iota(jnp.int32, end.shape, 0)
    if reverse:
        enter = jnp.where(row == SUBLANES - 1, carry, pltpu.roll(end, SUBLANES - 1, axis=0))
        carry_out = end[0:1, :]
    else:
        enter = jnp.where(row == 0, carry, pltpu.roll(end, 1, axis=0))
        carry_out = end[SUBLANES - 1:SUBLANES, :]
    for j in range(seg):
        scr_ref[2, pl.ds(j, SUBLANES, stride=pitch), :] = hs[j] + ps[j] * enter
    h_time = [scr_ref[2, pl.ds(pitch * s, seg, stride=1), :] for s in range(SUBLANES)]
    return jnp.concatenate(h_time, axis=0), carry_out


def _lru_block(u, wg, bg, sp, carry, reverse, scr_ref):
    g = jnp.dot(u.astype(BF16), wg, preferred_element_type=F32) + bg
    r = _sigmoid(g[:, :LANES])
    i = _sigmoid(g[:, LANES:])
    a = jnp.exp2(r * sp)
    b = _sqrt_unit(1.0 - a * a) * (i * u)
    return _segment_scan(a, b, carry, reverse, scr_ref)


def _rglru_kernel(xa_ref, ga_ref, xc_ref, cw_ref, cb_ref, wgf_ref, bgf_ref, wgb_ref, bgb_ref,
                  lam_ref, o_ref, xf_ref, xcf_ref, u_ref, hf_ref, scr_ref, *, tt):
    t_len = xa_ref.shape[1]
    tc_len = xc_ref.shape[1]
    nblk = xa_ref.shape[2] // LANES
    lanes = [slice(k * LANES, (k + 1) * LANES) for k in range(nblk)]
    zeros = jnp.zeros((SUBLANES, LANES), F32)
    for k, sl in enumerate(lanes):
        xf_ref[k, 0:SUBLANES, :] = zeros
        xf_ref[k, SUBLANES:SUBLANES + t_len, :] = xa_ref[0, :, sl].astype(F32)
        xf_ref[k, SUBLANES + t_len:, :] = zeros
        xcf_ref[k, 0:SUBLANES, :] = zeros
        xcf_ref[k, SUBLANES:SUBLANES + tc_len, :] = xc_ref[0, :, sl].astype(F32)
        xcf_ref[k, SUBLANES + tc_len:, :] = zeros

    sp_f = _softplus(-lam_ref[0:1, :]) * (LRU_C * NEG_LOG2E)
    sp_b = _softplus(-lam_ref[1:2, :]) * (LRU_C * NEG_LOG2E)

    def conv(src_ref, k, t0):
        u = cb_ref[:, lanes[k]]
        for tap in range(cw_ref.shape[0]):
            off = SUBLANES - CONV_PAD_L + tap
            if off % SUBLANES == 0:
                start = t0 + off
                rows = pl.ds(start if isinstance(start, int) else pl.multiple_of(start, SUBLANES), tt)
            else:
                rows = pl.ds(t0 + off, tt, stride=1)
            u = u + src_ref[k, rows, :] * cw_ref[tap:tap + 1, lanes[k]]
        return u

    def block(u, k, carry, reverse, slot):
        scr = scr_ref.at[k * TILES_PER_TRIP + slot]
        if reverse:
            return _lru_block(u, wgb_ref[k], bgb_ref[k], sp_b[:, lanes[k]], carry, True, scr)
        return _lru_block(u, wgf_ref[k], bgf_ref[k], sp_f[:, lanes[k]], carry, False, scr)

    n_lat = t_len // tt
    n_ctx = tc_len // tt
    assert n_lat % TILES_PER_TRIP == 0
    carry0 = tuple(jnp.zeros((1, LANES), F32) for _ in lanes)

    def tile_start(trip, slot, n_tiles, reverse):
        idx = trip * TILES_PER_TRIP + slot
        return pl.multiple_of((n_tiles - 1 - idx if reverse else idx) * tt, tt)

    def ctx_pass(reverse):
        carry = carry0
        for idx in range(n_ctx):
            t0 = (n_ctx - 1 - idx if reverse else idx) * tt
            carry = tuple(block(conv(xcf_ref, k, t0), k, carry[k], reverse, idx % TILES_PER_TRIP)[1]
                          for k in range(nblk))
        return carry

    def lat_f(trip, carry):
        for slot in range(TILES_PER_TRIP):
            t0 = tile_start(trip, slot, n_lat, False)
            rows = pl.ds(t0, tt)
            out = []
            for k, sl in enumerate(lanes):
                u = conv(xf_ref, k, t0)
                u_ref[rows, sl] = u
                h, c_out = block(u, k, carry[k], False, slot)
                hf_ref[rows, sl] = h
                out.append(c_out)
            carry = tuple(out)
        return carry

    lax.fori_loop(0, n_lat // TILES_PER_TRIP, lat_f, ctx_pass(False))

    def lat_b(trip, carry):
        for slot in range(TILES_PER_TRIP):
            rows = pl.ds(tile_start(trip, slot, n_lat, True), tt)
            out = []
            for k, sl in enumerate(lanes):
                h, c_out = block(u_ref[rows, sl], k, carry[k], True, slot)
                gate = ga_ref[0, rows, sl].astype(F32)
                o_ref[0, rows, sl] = ((hf_ref[rows, sl] + h) * gate).astype(o_ref.dtype)
                out.append(c_out)
            carry = tuple(out)
        return carry

    lax.fori_loop(0, n_lat // TILES_PER_TRIP, lat_b, ctx_pass(True))


def _rglru(p_lat, p_ctx, conv_w, conv_b, wg_f, bg_f, wg_b, bg_b, lam, d_a, cb, tt):
    bsz, t_len, _ = p_lat.shape
    tc_len = p_ctx.shape[1]
    nblk = cb // LANES
    ncb = d_a // cb
    kern = functools.partial(_rglru_kernel, tt=tt)
    return pl.pallas_call(
        kern,
        grid=(bsz, ncb),
        in_specs=[pl.BlockSpec((1, t_len, cb), lambda b, c: (b, 0, c)),
                  pl.BlockSpec((1, t_len, cb), lambda b, c: (b, 0, ncb + c)),
                  pl.BlockSpec((1, tc_len, cb), lambda b, c: (b, 0, c)),
                  pl.BlockSpec((conv_w.shape[0], cb), lambda b, c: (0, c)),
                  pl.BlockSpec((1, cb), lambda b, c: (0, c)),
                  pl.BlockSpec((nblk, LANES, 2 * LANES), lambda b, c: (c, 0, 0)),
                  pl.BlockSpec((nblk, 1, 2 * LANES), lambda b, c: (c, 0, 0)),
                  pl.BlockSpec((nblk, LANES, 2 * LANES), lambda b, c: (c, 0, 0)),
                  pl.BlockSpec((nblk, 1, 2 * LANES), lambda b, c: (c, 0, 0)),
                  pl.BlockSpec((2, cb), lambda b, c: (0, c))],
        out_specs=pl.BlockSpec((1, t_len, cb), lambda b, c: (b, 0, c)),
        out_shape=jax.ShapeDtypeStruct((bsz, t_len, d_a), BF16),
        scratch_shapes=[pltpu.VMEM((nblk, t_len + 2 * SUBLANES, LANES), F32),
                        pltpu.VMEM((nblk, tc_len + 2 * SUBLANES, LANES), F32),
                        pltpu.VMEM((t_len, cb), F32),
                        pltpu.VMEM((t_len, cb), F32),
                        pltpu.VMEM((nblk * TILES_PER_TRIP, 3, tt + SUBLANES * SEG_PAD, LANES),
                                   F32)],
        compiler_params=pltpu.CompilerParams(
            dimension_semantics=("parallel", "parallel"),
            vmem_limit_bytes=VMEM_LIMIT),
        name="rglru",
    )(p_lat, p_lat, p_ctx, conv_w, conv_b.reshape(1, d_a), wg_f, bg_f, wg_b, bg_b, lam)


_TN = (((0,), (0,)), ((), ()))
DEC_ROWS = 16


class _HgrnDir:
    def __init__(self, d, q_ref, f_ref, v_ref, lb, st_ref, o_ref, scratch, reverse, n_heads, hd,
                 sub):
        self.d, self.q_ref, self.f_ref, self.v_ref, self.lb = d, q_ref, f_ref, v_ref, lb
        self.sub = sub
        self.st_ref, self.o_ref, self.reverse, self.n_heads, self.hd = st_ref, o_ref, reverse, n_heads, hd
        (self.hl_ref, self.kk_ref, self.cum_ref, self.qd_ref, self.kdt_ref, self.qin_ref,
         self.kout_ref, self.dec_ref, self.sc_ref) = scratch
        self.c_len = q_ref.shape[2]
        r_i = lax.broadcasted_iota(jnp.int32, (self.c_len, self.c_len), 0)
        c_i = lax.broadcasted_iota(jnp.int32, (self.c_len, self.c_len), 1)
        self.tri = (r_i <= c_i) if reverse else (r_i >= c_i)

    def _slabs(self, width):
        total = self.n_heads * self.hd
        return [slice(s, s + width) for s in range(0, total, width)]


    def _gate_slab(self, sl):
        d, c_len = self.d, self.c_len
        lb = self.lb[:, sl]
        f = lb + (1.0 - lb) * _sigmoid(self.f_ref[0, self.sub, :, sl].astype(F32))
        logf = jnp.log2(f)
        self.kk_ref[d, :, sl] = (1.0 - f).astype(BF16)
        hi = logf.astype(BF16)
        self.hl_ref[d, 0:c_len, sl] = hi
        self.hl_ref[d, c_len:2 * c_len, sl] = (logf - hi.astype(F32)).astype(BF16)

    def _cumulate(self):
        tri_b = self.tri.astype(BF16)
        tri2 = jnp.concatenate([tri_b, tri_b], axis=1)
        self.cum_ref[self.d] = jnp.dot(tri2, self.hl_ref[self.d], preferred_element_type=F32)

    def _decay_slab(self, sl):
        d, c_len = self.d, self.c_len
        half = c_len // 2
        row_last = 0 if self.reverse else c_len - 1
        row_ref = half if self.reverse else half - 1
        cum = self.cum_ref[d, :, sl]
        last = cum[row_last:row_last + 1, :]
        ref = cum[row_ref:row_ref + 1, :]
        kk = self.kk_ref[d, :, sl]
        qs = self.q_ref[0, self.sub, :, sl]
        self.qd_ref[d, :, sl] = qs * jnp.exp2(jnp.minimum(cum - ref, EXP2_CLAMP)).astype(BF16)
        kd = kk * jnp.exp2(jnp.minimum(ref - cum, EXP2_CLAMP)).astype(BF16)
        for i in range((sl.stop - sl.start) // self.hd):
            h = sl.start // self.hd + i
            self.kdt_ref[d, h] = jnp.transpose(kd[:, i * self.hd:(i + 1) * self.hd])
        self.qin_ref[d, :, sl] = qs * jnp.exp2(cum).astype(BF16)
        self.kout_ref[d, :, sl] = kk * jnp.exp2(last - cum).astype(BF16)
        dec = jnp.exp2(last)
        hi = dec.astype(BF16)
        lo = (dec - hi.astype(F32)).astype(BF16)
        pad = jnp.zeros((DEC_ROWS - 2, dec.shape[1]), BF16)
        self.dec_ref[d, :, sl] = jnp.concatenate([hi, lo, pad], axis=0)

    def _head(self, h):
        return slice(h * self.hd, (h + 1) * self.hd)

    def _scores(self, h):
        d, sl = self.d, self._head(h)
        s = jnp.dot(self.qd_ref[d, :, sl], self.kdt_ref[d, h], preferred_element_type=F32)
        self.sc_ref[d, h] = jnp.where(self.tri, s, 0.0).astype(BF16)

    def _output(self, h):
        d, sl = self.d, self._head(h)
        o = (jnp.dot(self.sc_ref[d, h], self.v_ref[0, self.sub, :, sl], preferred_element_type=F32)
             + jnp.dot(self.qin_ref[d, :, sl], self.st_ref[h].astype(BF16),
                       preferred_element_type=F32))
        self.o_ref[0, self.sub, :, sl] = o.astype(self.o_ref.dtype)

    def _update(self, h):
        d, sl = self.d, self._head(h)
        row = lax.broadcasted_iota(jnp.int32, (DEC_ROWS, self.hd), 0)
        ones2 = jnp.where(row < 2, 1.0, 0.0).astype(BF16)
        decay = lax.dot_general(self.dec_ref[d, :, sl], ones2, _TN, preferred_element_type=F32)
        self.st_ref[h] = (self.st_ref[h] * decay
                          + lax.dot_general(self.kout_ref[d, :, sl], self.v_ref[0, self.sub, :, sl], _TN,
                                            preferred_element_type=F32))

    def stages(self):
        heads = range(self.n_heads)
        slabs = self._slabs(2 * LANES)
        out = [[functools.partial(self._gate_slab, sl) for sl in slabs], [self._cumulate],
               [functools.partial(self._decay_slab, sl) for sl in slabs]]
        if self.o_ref is not None:
            out += [[functools.partial(self._scores, h) for h in heads],
                    [functools.partial(self._output, h) for h in heads]]
        return out + [[functools.partial(self._update, h) for h in heads]]


def _lower_bounds(logits_ref, layer):
    out = []
    for d in range(2):
        rows = [logits_ref[d, l:l + 1, :] for l in range(logits_ref.shape[1])]
        m = functools.reduce(jnp.maximum, rows)
        e = [jnp.exp(r - m) for r in rows]
        out.append(sum(e[:layer + 1]) / sum(e))
    return out


def _hgrn_kernel(*refs, n_heads, hd, has_init, emit_o, emit_state):
    qf_ref, ff_ref, vf_ref, qb_ref, fb_ref, vb_ref, lg_ref = refs[:7]
    pos = 7
    if has_init:
        s0f_ref, s0b_ref = refs[pos:pos + 2]
        pos += 2
    if emit_o:
        of_ref, ob_ref = refs[pos:pos + 2]
        pos += 2
    if emit_state:
        sof_ref, sob_ref = refs[pos:pos + 2]
        pos += 2
    sf_ref, sb_ref = refs[pos:pos + 2]
    scratch = refs[pos + 2:]

    j = pl.program_id(1)

    @pl.when(j == 0)
    def _():
        if has_init:
            sf_ref[...] = s0f_ref[0]
            sb_ref[...] = s0b_ref[0]
        else:
            sf_ref[...] = jnp.zeros_like(sf_ref)
            sb_ref[...] = jnp.zeros_like(sb_ref)

    lb_f, lb_b = _lower_bounds(lg_ref, 0)
    n_sub = qf_ref.shape[1]
    for k in range(n_sub):
        dirs = [_HgrnDir(0, qf_ref, ff_ref, vf_ref, lb_f, sf_ref, of_ref if emit_o else None,
                         scratch, False, n_heads, hd, k),
                _HgrnDir(1, qb_ref, fb_ref, vb_ref, lb_b, sb_ref, ob_ref if emit_o else None,
                         scratch, True, n_heads, hd, n_sub - 1 - k)]
        for stage_f, stage_b in zip(dirs[0].stages(), dirs[1].stages()):
            for unit in stage_f + stage_b:
                unit()

    if emit_state:
        @pl.when(j == pl.num_programs(1) - 1)
        def _():
            sof_ref[0] = sf_ref[...]
            sob_ref[0] = sb_ref[...]


def _hgrn(p_view, logits, d_b, n_heads, first_group, init_states, emit_o):
    bsz, n_chunks = p_view.shape[:2]
    hd = d_b // n_heads
    has_init = init_states is not None
    emit_state = not emit_o

    cps = CHUNKS_PER_STEP if n_chunks % CHUNKS_PER_STEP == 0 else 1
    n_steps = n_chunks // cps

    def chunk(j, reverse):
        return n_steps - 1 - j if reverse else j

    def spec(group, reverse):
        def imap(b, j):
            return (b, chunk(j, reverse), 0, first_group + group)
        return pl.BlockSpec((1, cps, CHUNK, d_b), imap)

    in_specs = [spec(0, False), spec(1, False), spec(3, False),
                spec(0, True), spec(2, True), spec(3, True),
                pl.BlockSpec(logits.shape, lambda b, j: (0, 0, 0))]
    args = [p_view] * 6 + [logits]
    state_spec = pl.BlockSpec((1, n_heads, hd, hd), lambda b, j: (b, 0, 0, 0))
    state_shape = jax.ShapeDtypeStruct((bsz, n_heads, hd, hd), F32)
    if has_init:
        in_specs += [state_spec, state_spec]
        args += list(init_states)
    out_specs, out_shape = [], []
    if emit_o:
        o_shape = jax.ShapeDtypeStruct((bsz, n_chunks, CHUNK, d_b), BF16)
        out_specs += [pl.BlockSpec((1, cps, CHUNK, d_b), lambda b, j: (b, chunk(j, False), 0, 0)),
                      pl.BlockSpec((1, cps, CHUNK, d_b), lambda b, j: (b, chunk(j, True), 0, 0))]
        out_shape += [o_shape, o_shape]
    if emit_state:
        out_specs += [state_spec, state_spec]
        out_shape += [state_shape, state_shape]
    kern = functools.partial(_hgrn_kernel, n_heads=n_heads, hd=hd, has_init=has_init,
                             emit_o=emit_o, emit_state=emit_state)
    return pl.pallas_call(
        kern,
        grid=(bsz, n_steps),
        in_specs=in_specs,
        out_specs=out_specs,
        out_shape=out_shape,
        scratch_shapes=[pltpu.VMEM((n_heads, hd, hd), F32),
                        pltpu.VMEM((n_heads, hd, hd), F32),
                        pltpu.VMEM((2, 2 * CHUNK, d_b), BF16),
                        pltpu.VMEM((2, CHUNK, d_b), BF16),
                        pltpu.VMEM((2, CHUNK, d_b), F32),
                        pltpu.VMEM((2, CHUNK, d_b), BF16),
                        pltpu.VMEM((2, n_heads, hd, CHUNK), BF16),
                        pltpu.VMEM((2, CHUNK, d_b), BF16),
                        pltpu.VMEM((2, CHUNK, d_b), BF16),
                        pltpu.VMEM((2, DEC_ROWS, d_b), BF16),
                        pltpu.VMEM((2, n_heads, CHUNK, CHUNK), BF16)],
        compiler_params=pltpu.CompilerParams(
            dimension_semantics=("parallel", "arbitrary"),
            vmem_limit_bytes=VMEM_LIMIT),
        name="hgrn2_latent" if emit_o else "hgrn2_context",
    )(*args)


def _outproj_kernel(ya_ref, of_ref, ob_ref, gb_ref, x_ref, gate_ref, hnw_ref, fnw_ref, w_ref,
                    o_ref, y_ref, *, n_heads, hd):
    _, rb, wb, d_a = ya_ref.shape
    rows = rb * wb
    d = x_ref.shape[3]
    y_ref[:, 0:d_a] = ya_ref[0].reshape(rows, d_a)
    hnw = hnw_ref[...]
    for h in range(n_heads):
        sl = slice(h * hd, (h + 1) * hd)
        o = of_ref[0, :, :, sl].astype(F32) + ob_ref[0, :, :, sl].astype(F32)
        ms = jnp.mean(o * o, axis=-1, keepdims=True)
        on = jnp.transpose(o * lax.rsqrt(ms + EPS) * hnw, (1, 0, 2))
        yb = on * gb_ref[0, :, :, sl].astype(F32)
        y_ref[:, d_a + h * hd:d_a + (h + 1) * hd] = yb.reshape(rows, hd).astype(BF16)
    acc = jnp.dot(y_ref[...], w_ref[...], preferred_element_type=F32)
    z = x_ref[0].reshape(rows, d) + gate_ref[0] * acc
    ms = jnp.mean(z * z, axis=-1, keepdims=True)
    o_ref[0] = (z * lax.rsqrt(ms + EPS) * fnw_ref[...]).reshape(rb, wb, d)


def _out_projection(ya4, of4, ob4, pa4, gb_block, x4, mod3, hnw, fnw, w_bf16, n_heads, rb, wb):
    bsz, n_r, n_w, d = x4.shape
    d_a = ya4.shape[3]
    d_b = of4.shape[3]
    hd = d_b // n_heads
    kern = functools.partial(_outproj_kernel, n_heads=n_heads, hd=hd)

    def raster(c, col=0):
        return pl.BlockSpec((1, rb, wb, c), lambda b, r, w: (b, r, w, col))

    def colmajor(c):
        return pl.BlockSpec((1, wb, rb, c), lambda b, r, w: (b, w, r, 0))

    return pl.pallas_call(
        kern,
        grid=(bsz, n_r // rb, n_w // wb),
        in_specs=[raster(d_a), colmajor(d_b), colmajor(d_b), raster(d_b, gb_block), raster(d),
                  pl.BlockSpec((1, 1, d), lambda b, r, w: (b, 0, 2)),
                  pl.BlockSpec((1, hd), lambda b, r, w: (0, 0)),
                  pl.BlockSpec((1, d), lambda b, r, w: (0, 0)),
                  pl.BlockSpec((d_a + d_b, d), lambda b, r, w: (0, 0),
                               pipeline_mode=pl.Buffered(1))],
        out_specs=raster(d),
        out_shape=jax.ShapeDtypeStruct((bsz, n_r, n_w, d), F32),
        scratch_shapes=[pltpu.VMEM((rb * wb, d_a + d_b), BF16)],
        compiler_params=pltpu.CompilerParams(
            dimension_semantics=("parallel", "parallel", "parallel"),
            vmem_limit_bytes=VMEM_LIMIT),
        name="out_projection",
    )(ya4, of4, ob4, pa4, x4, mod3, hnw.reshape(1, hd), fnw.reshape(1, d), w_bf16)


def kernel(x, c, ctx, c_ctx, ada_w, ada_b, norm_w, w_in, conv_w, conv_b, lru_wr, lru_br, lru_wi,
           lru_bi, lru_lambda, hgrn_lb_logits, hgrn_norm_w, w_out, final_norm_w):
    bsz, t_len, d = x.shape
    tc_len = ctx.shape[1]
    assert ada_w.shape[0] == 1, "single-layer stack only"
    d_a = conv_w.shape[2]
    d_b = hgrn_lb_logits.shape[2]
    hd = hgrn_norm_w.shape[1]
    n_heads = d_b // hd
    n_blocks_a = lru_wr.shape[2]
    n_cols = w_in.shape[2]
    assert t_len == GRID_W * CHUNK and tc_len % CHUNK == 0
    assert d_a // n_blocks_a == LANES and hd == LANES
    assert (2 * d_a) % d_b == 0 and n_cols == 2 * d_a + 5 * d_b
    first_b_group = (2 * d_a) // d_b
    n_rows_grid = t_len // GRID_W

    n_rows = -(-(bsz + 1) // SUBLANES) * SUBLANES
    cc = jnp.zeros((n_rows, d), F32).at[:bsz].set(c).at[bsz].set(c_ctx)
    mod3 = _modulation(cc, ada_w[0], ada_b[0]).reshape(n_rows, 1, 3 * d)

    w_in_b = w_in[0]
    w_out_b = w_out[0].astype(BF16)

    tm = 1024 if t_len % 1024 == 0 else t_len
    tn = 1024 if d_a % 1024 == 0 and d_b % 1024 == 0 else min(d_a, d_b)
    tpb = t_len // tm
    a_tiles = (2 * d_a) // tn
    b_tiles = (4 * d_b) // tn
    q_tiles = d_b // tn
    p_a = _in_projection(x.reshape(bsz * t_len, d), mod3, norm_w[0], w_in_b,
                         lambda i: i // tpb, lambda j: jnp.where(j < a_tiles, j, j + b_tiles),
                         2 * d_a + d_b, tm, tn, (d_a // tn, (2 * d_a + d_b) // tn))
    wb = 16
    p_b = _in_projection_colmajor(x.reshape(bsz, n_rows_grid, GRID_W, d), mod3, norm_w[0], w_in_b,
                                  lambda j: j + a_tiles, 4 * d_b, wb, tn, (0, q_tiles))
    xa_tiles = d_a // tn
    ctx_cols = d_a + 4 * d_b
    p_ctx = _in_projection(ctx.reshape(bsz * tc_len, d), mod3, norm_w[0], w_in_b,
                           lambda i: bsz, lambda j: jnp.where(j < xa_tiles, j, j + xa_tiles),
                           ctx_cols, bsz * tc_len, tn, (xa_tiles, xa_tiles + q_tiles))
    p_lat = p_a.reshape(bsz, t_len, 2 * d_a + d_b)
    p_ctx = p_ctx.reshape(bsz, tc_len, ctx_cols)

    def gate_w(dirn):
        return jnp.concatenate([lru_wr[0, dirn], lru_wi[0, dirn]], axis=-1).astype(BF16)

    def gate_b(dirn):
        return jnp.concatenate([lru_br[0, dirn].reshape(n_blocks_a, 1, LANES),
                                lru_bi[0, dirn].reshape(n_blocks_a, 1, LANES)], axis=-1)

    cb = 256 if d_a % 256 == 0 else LANES
    ya = _rglru(p_lat, p_ctx, conv_w[0], conv_b[0], gate_w(0), gate_b(0), gate_w(1), gate_b(1),
                lru_lambda[0], d_a, cb, tt=128)

    assert d_a % d_b == 0
    states = _hgrn(p_ctx.reshape(bsz, tc_len // CHUNK, CHUNK, ctx_cols), hgrn_lb_logits, d_b,
                   n_heads, d_a // d_b, None, False)
    o_f, o_b = _hgrn(p_b, hgrn_lb_logits, d_b, n_heads, 0, states, True)

    grid4 = lambda z: z.reshape(bsz, n_rows_grid, GRID_W, z.shape[-1])
    out = _out_projection(grid4(ya), o_f, o_b, grid4(p_lat), first_b_group, grid4(x), mod3,
                          hgrn_norm_w[0], final_norm_w, w_out_b, n_heads, 16, 16)
    return out.reshape(bsz, t_len, d)
```

```python
import functools

import jax
import jax.numpy as jnp
from jax import lax
from jax.experimental import pallas as pl
from jax.experimental.pallas import tpu as pltpu

GRID_W = 64
CHUNK = 64
LRU_C = 8.0
EPS = 1e-6
CONV_PAD_L = 2
LANES = 128
SUBLANES = 8
EXP2_CLAMP = 115.0
NEG_LOG2E = -1.4426950408889634
VMEM_LIMIT = 56 * 1024 * 1024
PROLOGUE_ROWS = 16
SEG_PAD = 4
CHUNKS_PER_STEP = 2
TILES_PER_TRIP = 4

F32 = jnp.float32
BF16 = jnp.bfloat16


def _sigmoid(z):
    return 1.0 / (1.0 + jnp.exp2(z * NEG_LOG2E))


def _silu(z):
    return z * _sigmoid(z)


def _softplus(z):
    return jnp.maximum(z, 0.0) + jnp.log1p(jnp.exp(-jnp.abs(z)))


def _mod_kernel(c_ref, w_ref, b_ref, o_ref):
    s = _silu(c_ref[...])
    o_ref[...] = jnp.dot(s.astype(BF16), w_ref[...].astype(BF16),
                         preferred_element_type=F32) + b_ref[...]


def _modulation(cc, w, b):
    rows, d = cc.shape
    n = w.shape[1]
    tn = 512 if n % 512 == 0 else n
    return pl.pallas_call(
        _mod_kernel,
        grid=(n // tn,),
        in_specs=[pl.BlockSpec((rows, d), lambda j: (0, 0)),
                  pl.BlockSpec((d, tn), lambda j: (0, j)),
                  pl.BlockSpec((1, tn), lambda j: (0, j))],
        out_specs=pl.BlockSpec((rows, tn), lambda j: (0, j)),
        out_shape=jax.ShapeDtypeStruct((rows, n), F32),
        compiler_params=pltpu.CompilerParams(
            dimension_semantics=("arbitrary",), vmem_limit_bytes=VMEM_LIMIT),
        name="adaln_modulation",
    )(cc, w, b.reshape(1, n))


def _store_projection(o_ref, res, j, silu_tiles):
    lo, hi = silu_tiles
    z = res.astype(o_ref.dtype)
    if hi <= lo:
        o_ref[...] = z
        return
    in_range = jnp.logical_and(j >= lo, j < hi)
    o_ref[...] = jnp.where(in_range, z * (0.5 * jnp.tanh(0.5 * z) + 0.5), z)


def _inproj_kernel(x_ref, shift_ref, scale_ref, nw_ref, w_ref, o_ref, h_ref, *, silu_tiles):
    @pl.when(pl.program_id(1) == 0)
    def _():
        rows = min(PROLOGUE_ROWS, x_ref.shape[0])
        gain = nw_ref[...] * (1.0 + scale_ref[0])

        def slab(s, carry):
            sl = pl.ds(pl.multiple_of(s * rows, rows), rows)
            x = x_ref[sl, :]
            rs = lax.rsqrt(jnp.mean(x * x, axis=-1, keepdims=True) + EPS)
            h_ref[sl, :] = (x_ref[sl, :] * rs * gain + shift_ref[0]).astype(BF16)
            return carry

        lax.fori_loop(0, x_ref.shape[0] // rows, slab, 0, unroll=4)

    res = jnp.dot(h_ref[...], w_ref[...].astype(BF16), preferred_element_type=F32)
    _store_projection(o_ref, res, pl.program_id(1), silu_tiles)


def _in_projection(x2d, mod3, norm_w, w_bf16, row_of_tile, wcol_of_tile, n_out, tm, tn, silu_tiles):
    m, d = x2d.shape
    return pl.pallas_call(
        functools.partial(_inproj_kernel, silu_tiles=silu_tiles),
        grid=(m // tm, n_out // tn),
        in_specs=[pl.BlockSpec((tm, d), lambda i, j: (i, 0)),
                  pl.BlockSpec((1, 1, d), lambda i, j: (row_of_tile(i), 0, 0)),
                  pl.BlockSpec((1, 1, d), lambda i, j: (row_of_tile(i), 0, 1)),
                  pl.BlockSpec((1, d), lambda i, j: (0, 0)),
                  pl.BlockSpec((d, tn), lambda i, j: (0, wcol_of_tile(j)))],
        out_specs=pl.BlockSpec((tm, tn), lambda i, j: (i, j)),
        out_shape=jax.ShapeDtypeStruct((m, n_out), BF16),
        scratch_shapes=[pltpu.VMEM((tm, d), BF16)],
        compiler_params=pltpu.CompilerParams(
            dimension_semantics=("parallel", "arbitrary"),
            vmem_limit_bytes=VMEM_LIMIT),
        name="in_projection",
    )(x2d, mod3, mod3, norm_w.reshape(1, d), w_bf16)


def _inproj_colmajor_kernel(x_ref, shift_ref, scale_ref, nw_ref, w_ref, o_ref, h_ref, *, slab,
                            silu_tiles):
    _, n_r, n_w, d = x_ref.shape

    rr = 2 * SUBLANES
    lanes = [slice(s * slab, (s + 1) * slab) for s in range(d // slab)]

    @pl.when(pl.program_id(2) == 0)
    def _():
        def row_group(g, carry):
            rows = pl.ds(pl.multiple_of(g * rr, rr), rr)
            sq = jnp.zeros((rr, n_w, slab), F32)
            for sl in lanes:
                xs = x_ref[0, rows, :, sl]
                sq = sq + xs * xs
            rs = lax.rsqrt(jnp.sum(sq, axis=-1, keepdims=True) * (1.0 / d) + EPS)
            for sl in lanes:
                y = x_ref[0, rows, :, sl] * rs * nw_ref[:, sl]
                y = y * (1.0 + scale_ref[0][:, sl]) + shift_ref[0][:, sl]
                h_ref[:, rows, sl] = jnp.transpose(y, (1, 0, 2)).astype(BF16)
            return carry

        lax.fori_loop(0, n_r // rr, row_group, 0)

    res = jnp.dot(h_ref[...].reshape(n_w * n_r, d), w_ref[...].astype(BF16),
                  preferred_element_type=F32)
    _store_projection(o_ref, res.reshape(1, n_w, n_r, res.shape[1]), pl.program_id(2), silu_tiles)


def _in_projection_colmajor(x4d, mod3, norm_w, w_bf16, wcol_of_tile, n_out, wb, tn, silu_tiles):
    bsz, n_r, n_w, d = x4d.shape
    kern = functools.partial(_inproj_colmajor_kernel, slab=2 * LANES, silu_tiles=silu_tiles)
    return pl.pallas_call(
        kern,
        grid=(bsz, n_w // wb, n_out // tn),
        in_specs=[pl.BlockSpec((1, n_r, wb, d), lambda b, w, j: (b, 0, w, 0)),
                  pl.BlockSpec((1, 1, d), lambda b, w, j: (b, 0, 0)),
                  pl.BlockSpec((1, 1, d), lambda b, w, j: (b, 0, 1)),
                  pl.BlockSpec((1, d), lambda b, w, j: (0, 0)),
                  pl.BlockSpec((d, tn), lambda b, w, j: (0, wcol_of_tile(j)))],
        out_specs=pl.BlockSpec((1, wb, n_r, tn), lambda b, w, j: (b, w, 0, j)),
        out_shape=jax.ShapeDtypeStruct((bsz, n_w, n_r, n_out), BF16),
        scratch_shapes=[pltpu.VMEM((wb, n_r, d), BF16)],
        compiler_params=pltpu.CompilerParams(
            dimension_semantics=("parallel", "parallel", "arbitrary"),
            vmem_limit_bytes=VMEM_LIMIT),
        name="in_projection_colmajor",
    )(x4d, mod3, mod3, norm_w.reshape(1, d), w_bf16)


def _group_scan(a, b, reverse):
    row = lax.broadcasted_iota(jnp.int32, a.shape, 1)
    for k in (1, 2, 4):
        if reverse:
            a_sh = pltpu.roll(a, SUBLANES - k, axis=1)
            b_sh = pltpu.roll(b, SUBLANES - k, axis=1)
            m = row < SUBLANES - k
        else:
            a_sh = pltpu.roll(a, k, axis=1)
            b_sh = pltpu.roll(b, k, axis=1)
            m = row >= k
        b = jnp.where(m, a * b_sh + b, b)
        a = jnp.where(m, a * a_sh, a)
    return a, b


def _sqrt_unit(x):
    return jnp.where(x > 0.0, x * lax.rsqrt(x), 0.0)


def _segment_scan(a, b, carry, reverse, scr_ref):
    tt = a.shape[0]
    seg = tt // SUBLANES
    pitch = seg + SEG_PAD
    for s in range(SUBLANES):
        rows = pl.ds(pitch * s, seg, stride=1)
        scr_ref[0, rows, :] = a[seg * s:seg * (s + 1), :]
        scr_ref[1, rows, :] = b[seg * s:seg * (s + 1), :]
    hs, ps = [None] * seg, [None] * seg
    h = p = None
    for j in (range(seg - 1, -1, -1) if reverse else range(seg)):
        step = pl.ds(j, SUBLANES, stride=pitch)
        a_j = scr_ref[0, step, :]
        b_j = scr_ref[1, step, :]
        h = b_j if h is None else a_j * h + b_j
        p = a_j if p is None else a_j * p
        hs[j], ps[j] = h, p
    g3, e3 = _group_scan(p[None], h[None], reverse)
    end = e3[0] + g3[0] * carry
    row = lax.broadcasted_iota(jnp.int32, end.shape, 0)
    if reverse:
        enter = jnp.where(row == SUBLANES - 1, carry, pltpu.roll(end, SUBLANES - 1, axis=0))
        carry_out = end[0:1, :]
    else:
        enter = jnp.where(row == 0, carry, pltpu.roll(end, 1, axis=0))
        carry_out = end[SUBLANES - 1:SUBLANES, :]
    for j in range(seg):
        scr_ref[2, pl.ds(j, SUBLANES, stride=pitch), :] = hs[j] + ps[j] * enter
    h_time = [scr_ref[2, pl.ds(pitch * s, seg, stride=1), :] for s in range(SUBLANES)]
    return jnp.concatenate(h_time, axis=0), carry_out


def _lru_block(u, wg, bg, sp, carry, reverse, scr_ref):
    g = jnp.dot(u.astype(BF16), wg, preferred_element_type=F32) + bg
    r = _sigmoid(g[:, :LANES])
    i = _sigmoid(g[:, LANES:])
    a = jnp.exp2(r * sp)
    b = _sqrt_unit(1.0 - a * a) * (i * u)
    return _segment_scan(a, b, carry, reverse, scr_ref)


def _rglru_kernel(xa_ref, ga_ref, xc_ref, cw_ref, cb_ref, wgf_ref, bgf_ref, wgb_ref, bgb_ref,
                  lam_ref, o_ref, xf_ref, xcf_ref, u_ref, hf_ref, scr_ref, *, tt):
    t_len = xa_ref.shape[1]
    tc_len = xc_ref.shape[1]
    nblk = xa_ref.shape[2] // LANES
    lanes = [slice(k * LANES, (k + 1) * LANES) for k in range(nblk)]
    zeros = jnp.zeros((SUBLANES, LANES), F32)
    for k, sl in enumerate(lanes):
        xf_ref[k, 0:SUBLANES, :] = zeros
        xf_ref[k, SUBLANES:SUBLANES + t_len, :] = xa_ref[0, :, sl].astype(F32)
        xf_ref[k, SUBLANES + t_len:, :] = zeros
        xcf_ref[k, 0:SUBLANES, :] = zeros
        xcf_ref[k, SUBLANES:SUBLANES + tc_len, :] = xc_ref[0, :, sl].astype(F32)
        xcf_ref[k, SUBLANES + tc_len:, :] = zeros

    sp_f = _softplus(-lam_ref[0:1, :]) * (LRU_C * NEG_LOG2E)
    sp_b = _softplus(-lam_ref[1:2, :]) * (LRU_C * NEG_LOG2E)

    def conv(src_ref, k, t0):
        u = cb_ref[:, lanes[k]]
        for tap in range(cw_ref.shape[0]):
            off = SUBLANES - CONV_PAD_L + tap
            if off % SUBLANES == 0:
                start = t0 + off
                rows = pl.ds(start if isinstance(start, int) else pl.multiple_of(start, SUBLANES), tt)
            else:
                rows = pl.ds(t0 + off, tt, stride=1)
            u = u + src_ref[k, rows, :] * cw_ref[tap:tap + 1, lanes[k]]
        return u

    def block(u, k, carry, reverse, slot):
        scr = scr_ref.at[k * TILES_PER_TRIP + slot]
        if reverse:
            return _lru_block(u, wgb_ref[k], bgb_ref[k], sp_b[:, lanes[k]], carry, True, scr)
        return _lru_block(u, wgf_ref[k], bgf_ref[k], sp_f[:, lanes[k]], carry, False, scr)

    n_lat = t_len // tt
    n_ctx = tc_len // tt
    assert n_lat % TILES_PER_TRIP == 0
    carry0 = tuple(jnp.zeros((1, LANES), F32) for _ in lanes)

    def tile_start(trip, slot, n_tiles, reverse):
        idx = trip * TILES_PER_TRIP + slot
        return pl.multiple_of((n_tiles - 1 - idx if reverse else idx) * tt, tt)

    def ctx_pass(reverse):
        carry = carry0
        for idx in range(n_ctx):
            t0 = (n_ctx - 1 - idx if reverse else idx) * tt
            carry = tuple(block(conv(xcf_ref, k, t0), k, carry[k], reverse, idx % TILES_PER_TRIP)[1]
                          for k in range(nblk))
        return carry

    def lat_f(trip, carry):
        for slot in range(TILES_PER_TRIP):
            t0 = tile_start(trip, slot, n_lat, False)
            rows = pl.ds(t0, tt)
            out = []
            for k, sl in enumerate(lanes):
                u = conv(xf_ref, k, t0)
                u_ref[rows, sl] = u
                h, c_out = block(u, k, carry[k], False, slot)
                hf_ref[rows, sl] = h
                out.append(c_out)
            carry = tuple(out)
        return carry

    lax.fori_loop(0, n_lat // TILES_PER_TRIP, lat_f, ctx_pass(False))

    def lat_b(trip, carry):
        for slot in range(TILES_PER_TRIP):
            rows = pl.ds(tile_start(trip, slot, n_lat, True), tt)
            out = []
            for k, sl in enumerate(lanes):
                h, c_out = block(u_ref[rows, sl], k, carry[k], True, slot)
                gate = ga_ref[0, rows, sl].astype(F32)
                o_ref[0, rows, sl] = ((hf_ref[rows, sl] + h) * gate).astype(o_ref.dtype)
                out.append(c_out)
            carry = tuple(out)
        return carry

    lax.fori_loop(0, n_lat // TILES_PER_TRIP, lat_b, ctx_pass(True))


def _rglru(p_lat, p_ctx, conv_w, conv_b, wg_f, bg_f, wg_b, bg_b, lam, d_a, cb, tt):
    bsz, t_len, _ = p_lat.shape
    tc_len = p_ctx.shape[1]
    nblk = cb // LANES
    ncb = d_a // cb
    kern = functools.partial(_rglru_kernel, tt=tt)
    return pl.pallas_call(
        kern,
        grid=(bsz, ncb),
        in_specs=[pl.BlockSpec((1, t_len, cb), lambda b, c: (b, 0, c)),
                  pl.BlockSpec((1, t_len, cb), lambda b, c: (b, 0, ncb + c)),
                  pl.BlockSpec((1, tc_len, cb), lambda b, c: (b, 0, c)),
                  pl.BlockSpec((conv_w.shape[0], cb), lambda b, c: (0, c)),
                  pl.BlockSpec((1, cb), lambda b, c: (0, c)),
                  pl.BlockSpec((nblk, LANES, 2 * LANES), lambda b, c: (c, 0, 0)),
                  pl.BlockSpec((nblk, 1, 2 * LANES), lambda b, c: (c, 0, 0)),
                  pl.BlockSpec((nblk, LANES, 2 * LANES), lambda b, c: (c, 0, 0)),
                  pl.BlockSpec((nblk, 1, 2 * LANES), lambda b, c: (c, 0, 0)),
                  pl.BlockSpec((2, cb), lambda b, c: (0, c))],
        out_specs=pl.BlockSpec((1, t_len, cb), lambda b, c: (b, 0, c)),
        out_shape=jax.ShapeDtypeStruct((bsz, t_len, d_a), BF16),
        scratch_shapes=[pltpu.VMEM((nblk, t_len + 2 * SUBLANES, LANES), F32),
                        pltpu.VMEM((nblk, tc_len + 2 * SUBLANES, LANES), F32),
                        pltpu.VMEM((t_len, cb), F32),
                        pltpu.VMEM((t_len, cb), F32),
                        pltpu.VMEM((nblk * TILES_PER_TRIP, 3, tt + SUBLANES * SEG_PAD, LANES),
                                   F32)],
        compiler_params=pltpu.CompilerParams(
            dimension_semantics=("parallel", "parallel"),
            vmem_limit_bytes=VMEM_LIMIT),
        name="rglru",
    )(p_lat, p_lat, p_ctx, conv_w, conv_b.reshape(1, d_a), wg_f, bg_f, wg_b, bg_b, lam)


_NT = (((1,), (1,)), ((), ()))
_TN = (((0,), (0,)), ((), ()))


class _HgrnDir:
    def __init__(self, d, q_ref, f_ref, v_ref, lb, st_ref, o_ref, scratch, reverse, n_heads, hd,
                 sub):
        self.d, self.q_ref, self.f_ref, self.v_ref, self.lb = d, q_ref, f_ref, v_ref, lb
        self.sub = sub
        self.st_ref, self.o_ref, self.reverse, self.n_heads, self.hd = st_ref, o_ref, reverse, n_heads, hd
        (self.hl_ref, self.kk_ref, self.cum_ref, self.qd_ref, self.kd_ref, self.qin_ref,
         self.kout_ref, self.dec_ref, self.dect_ref, self.sc_ref) = scratch
        self.c_len = q_ref.shape[2]
        r_i = lax.broadcasted_iota(jnp.int32, (self.c_len, self.c_len), 0)
        c_i = lax.broadcasted_iota(jnp.int32, (self.c_len, self.c_len), 1)
        self.tri = (r_i <= c_i) if reverse else (r_i >= c_i)

    def _slabs(self, width):
        total = self.n_heads * self.hd
        return [slice(s, s + width) for s in range(0, total, width)]


    def _gate_slab(self, sl):
        d, c_len = self.d, self.c_len
        lb = self.lb[:, sl]
        f = lb + (1.0 - lb) * _sigmoid(self.f_ref[0, self.sub, :, sl].astype(F32))
        logf = jnp.log2(f)
        self.kk_ref[d, :, sl] = (1.0 - f).astype(BF16)
        hi = logf.astype(BF16)
        self.hl_ref[d, 0:c_len, sl] = hi
        self.hl_ref[d, c_len:2 * c_len, sl] = (logf - hi.astype(F32)).astype(BF16)

    def _cumulate(self):
        tri_b = self.tri.astype(BF16)
        tri2 = jnp.concatenate([tri_b, tri_b], axis=1)
        self.cum_ref[self.d] = jnp.dot(tri2, self.hl_ref[self.d], preferred_element_type=F32)

    def _decay_slab(self, sl):
        d, c_len = self.d, self.c_len
        half = c_len // 2
        row_last = 0 if self.reverse else c_len - 1
        row_ref = half if self.reverse else half - 1
        cum = self.cum_ref[d, :, sl]
        last = cum[row_last:row_last + 1, :]
        ref = cum[row_ref:row_ref + 1, :]
        kk = self.kk_ref[d, :, sl]
        qs = self.q_ref[0, self.sub, :, sl]
        self.qd_ref[d, :, sl] = qs * jnp.exp2(jnp.minimum(cum - ref, EXP2_CLAMP)).astype(BF16)
        self.kd_ref[d, :, sl] = kk * jnp.exp2(jnp.minimum(ref - cum, EXP2_CLAMP)).astype(BF16)
        self.qin_ref[d, :, sl] = qs * jnp.exp2(cum).astype(BF16)
        self.kout_ref[d, :, sl] = kk * jnp.exp2(last - cum).astype(BF16)
        dec = jnp.exp2(last)
        for i in range((sl.stop - sl.start) // self.hd):
            h = sl.start // self.hd + i
            self.dec_ref[d, h:h + 1, :] = dec[:, i * self.hd:(i + 1) * self.hd]

    def _decay_columns(self):
        self.dect_ref[self.d] = jnp.transpose(self.dec_ref[self.d])

    def _head(self, h):
        return slice(h * self.hd, (h + 1) * self.hd)

    def _scores(self, h):
        d, sl = self.d, self._head(h)
        s = lax.dot_general(self.qd_ref[d, :, sl], self.kd_ref[d, :, sl], _NT,
                            preferred_element_type=F32)
        self.sc_ref[d, h] = jnp.where(self.tri, s, 0.0).astype(BF16)

    def _output(self, h):
        d, sl = self.d, self._head(h)
        o = (jnp.dot(self.sc_ref[d, h], self.v_ref[0, self.sub, :, sl], preferred_element_type=F32)
             + jnp.dot(self.qin_ref[d, :, sl], self.st_ref[h].astype(BF16),
                       preferred_element_type=F32))
        self.o_ref[0, self.sub, :, sl] = o.astype(self.o_ref.dtype)

    def _update(self, h):
        d, sl = self.d, self._head(h)
        decay = jnp.broadcast_to(self.dect_ref[d, :, h:h + 1], (self.hd, self.hd))
        self.st_ref[h] = (self.st_ref[h] * decay
                          + lax.dot_general(self.kout_ref[d, :, sl], self.v_ref[0, self.sub, :, sl], _TN,
                                            preferred_element_type=F32))

    def stages(self):
        heads = range(self.n_heads)
        slabs = self._slabs(2 * LANES)
        out = [[functools.partial(self._gate_slab, sl) for sl in slabs], [self._cumulate],
               [functools.partial(self._decay_slab, sl) for sl in slabs], [self._decay_columns]]
        if self.o_ref is not None:
            out += [[functools.partial(self._scores, h) for h in heads],
                    [functools.partial(self._output, h) for h in heads]]
        return out + [[functools.partial(self._update, h) for h in heads]]


def _lower_bounds(logits_ref, layer):
    out = []
    for d in range(2):
        rows = [logits_ref[d, l:l + 1, :] for l in range(logits_ref.shape[1])]
        m = functools.reduce(jnp.maximum, rows)
        e = [jnp.exp(r - m) for r in rows]
        out.append(sum(e[:layer + 1]) / sum(e))
    return out


def _hgrn_kernel(*refs, n_heads, hd, has_init, emit_o, emit_state):
    qf_ref, ff_ref, vf_ref, qb_ref, fb_ref, vb_ref, lg_ref = refs[:7]
    pos = 7
    if has_init:
        s0f_ref, s0b_ref = refs[pos:pos + 2]
        pos += 2
    if emit_o:
        of_ref, ob_ref = refs[pos:pos + 2]
        pos += 2
    if emit_state:
        sof_ref, sob_ref = refs[pos:pos + 2]
        pos += 2
    sf_ref, sb_ref = refs[pos:pos + 2]
    scratch = refs[pos + 2:]

    j = pl.program_id(1)
    dec_ref = scratch[7]
    assert n_heads <= dec_ref.shape[1]

    @pl.when(j == 0)
    def _():
        if has_init:
            sf_ref[...] = s0f_ref[0]
            sb_ref[...] = s0b_ref[0]
        else:
            sf_ref[...] = jnp.zeros_like(sf_ref)
            sb_ref[...] = jnp.zeros_like(sb_ref)
        dec_ref[...] = jnp.zeros_like(dec_ref)

    lb_f, lb_b = _lower_bounds(lg_ref, 0)
    n_sub = qf_ref.shape[1]
    for k in range(n_sub):
        dirs = [_HgrnDir(0, qf_ref, ff_ref, vf_ref, lb_f, sf_ref, of_ref if emit_o else None,
                         scratch, False, n_heads, hd, k),
                _HgrnDir(1, qb_ref, fb_ref, vb_ref, lb_b, sb_ref, ob_ref if emit_o else None,
                         scratch, True, n_heads, hd, n_sub - 1 - k)]
        for stage_f, stage_b in zip(dirs[0].stages(), dirs[1].stages()):
            for unit in stage_f + stage_b:
                unit()

    if emit_state:
        @pl.when(j == pl.num_programs(1) - 1)
        def _():
            sof_ref[0] = sf_ref[...]
            sob_ref[0] = sb_ref[...]


def _hgrn(p_view, logits, d_b, n_heads, first_group, init_states, emit_o):
    bsz, n_chunks = p_view.shape[:2]
    hd = d_b // n_heads
    has_init = init_states is not None
    emit_state = not emit_o

    cps = CHUNKS_PER_STEP if n_chunks % CHUNKS_PER_STEP == 0 else 1
    n_steps = n_chunks // cps

    def chunk(j, reverse):
        return n_steps - 1 - j if reverse else j

    def spec(group, reverse):
        def imap(b, j):
            return (b, chunk(j, reverse), 0, first_group + group)
        return pl.BlockSpec((1, cps, CHUNK, d_b), imap)

    in_specs = [spec(0, False), spec(1, False), spec(3, False),
                spec(0, True), spec(2, True), spec(3, True),
                pl.BlockSpec(logits.shape, lambda b, j: (0, 0, 0))]
    args = [p_view] * 6 + [logits]
    state_spec = pl.BlockSpec((1, n_heads, hd, hd), lambda b, j: (b, 0, 0, 0))
    state_shape = jax.ShapeDtypeStruct((bsz, n_heads, hd, hd), F32)
    if has_init:
        in_specs += [state_spec, state_spec]
        args += list(init_states)
    out_specs, out_shape = [], []
    if emit_o:
        o_shape = jax.ShapeDtypeStruct((bsz, n_chunks, CHUNK, d_b), BF16)
        out_specs += [pl.BlockSpec((1, cps, CHUNK, d_b), lambda b, j: (b, chunk(j, False), 0, 0)),
                      pl.BlockSpec((1, cps, CHUNK, d_b), lambda b, j: (b, chunk(j, True), 0, 0))]
        out_shape += [o_shape, o_shape]
    if emit_state:
        out_specs += [state_spec, state_spec]
        out_shape += [state_shape, state_shape]
    kern = functools.partial(_hgrn_kernel, n_heads=n_heads, hd=hd, has_init=has_init,
                             emit_o=emit_o, emit_state=emit_state)
    return pl.pallas_call(
        kern,
        grid=(bsz, n_steps),
        in_specs=in_specs,
        out_specs=out_specs,
        out_shape=out_shape,
        scratch_shapes=[pltpu.VMEM((n_heads, hd, hd), F32),
                        pltpu.VMEM((n_heads, hd, hd), F32),
                        pltpu.VMEM((2, 2 * CHUNK, d_b), BF16),
                        pltpu.VMEM((2, CHUNK, d_b), BF16),
                        pltpu.VMEM((2, CHUNK, d_b), F32),
                        pltpu.VMEM((2, CHUNK, d_b), BF16),
                        pltpu.VMEM((2, CHUNK, d_b), BF16),
                        pltpu.VMEM((2, CHUNK, d_b), BF16),
                        pltpu.VMEM((2, CHUNK, d_b), BF16),
                        pltpu.VMEM((2, hd, hd), F32),
                        pltpu.VMEM((2, hd, hd), F32),
                        pltpu.VMEM((2, n_heads, CHUNK, CHUNK), BF16)],
        compiler_params=pltpu.CompilerParams(
            dimension_semantics=("parallel", "arbitrary"),
            vmem_limit_bytes=VMEM_LIMIT),
        name="hgrn2_latent" if emit_o else "hgrn2_context",
    )(*args)


def _outproj_kernel(ya_ref, of_ref, ob_ref, gb_ref, x_ref, gate_ref, hnw_ref, fnw_ref, w_ref,
                    o_ref, y_ref, *, n_heads, hd):
    _, rb, wb, d_a = ya_ref.shape
    rows = rb * wb
    d = x_ref.shape[3]
    y_ref[:, 0:d_a] = ya_ref[0].reshape(rows, d_a)
    hnw = hnw_ref[...]
    for h in range(n_heads):
        sl = slice(h * hd, (h + 1) * hd)
        o = of_ref[0, :, :, sl].astype(F32) + ob_ref[0, :, :, sl].astype(F32)
        ms = jnp.mean(o * o, axis=-1, keepdims=True)
        on = jnp.transpose(o * lax.rsqrt(ms + EPS) * hnw, (1, 0, 2))
        yb = on * gb_ref[0, :, :, sl].astype(F32)
        y_ref[:, d_a + h * hd:d_a + (h + 1) * hd] = yb.reshape(rows, hd).astype(BF16)
    acc = jnp.dot(y_ref[...], w_ref[...], preferred_element_type=F32)
    z = x_ref[0].reshape(rows, d) + gate_ref[0] * acc
    ms = jnp.mean(z * z, axis=-1, keepdims=True)
    o_ref[0] = (z * lax.rsqrt(ms + EPS) * fnw_ref[...]).reshape(rb, wb, d)


def _out_projection(ya4, of4, ob4, pa4, gb_block, x4, mod3, hnw, fnw, w_bf16, n_heads, rb, wb):
    bsz, n_r, n_w, d = x4.shape
    d_a = ya4.shape[3]
    d_b = of4.shape[3]
    hd = d_b // n_heads
    kern = functools.partial(_outproj_kernel, n_heads=n_heads, hd=hd)

    def raster(c, col=0):
        return pl.BlockSpec((1, rb, wb, c), lambda b, r, w: (b, r, w, col))

    def colmajor(c):
        return pl.BlockSpec((1, wb, rb, c), lambda b, r, w: (b, w, r, 0))

    return pl.pallas_call(
        kern,
        grid=(bsz, n_r // rb, n_w // wb),
        in_specs=[raster(d_a), colmajor(d_b), colmajor(d_b), raster(d_b, gb_block), raster(d),
                  pl.BlockSpec((1, 1, d), lambda b, r, w: (b, 0, 2)),
                  pl.BlockSpec((1, hd), lambda b, r, w: (0, 0)),
                  pl.BlockSpec((1, d), lambda b, r, w: (0, 0)),
                  pl.BlockSpec((d_a + d_b, d), lambda b, r, w: (0, 0),
                               pipeline_mode=pl.Buffered(1))],
        out_specs=raster(d),
        out_shape=jax.ShapeDtypeStruct((bsz, n_r, n_w, d), F32),
        scratch_shapes=[pltpu.VMEM((rb * wb, d_a + d_b), BF16)],
        compiler_params=pltpu.CompilerParams(
            dimension_semantics=("parallel", "parallel", "parallel"),
            vmem_limit_bytes=VMEM_LIMIT),
        name="out_projection",
    )(ya4, of4, ob4, pa4, x4, mod3, hnw.reshape(1, hd), fnw.reshape(1, d), w_bf16)


def kernel(x, c, ctx, c_ctx, ada_w, ada_b, norm_w, w_in, conv_w, conv_b, lru_wr, lru_br, lru_wi,
           lru_bi, lru_lambda, hgrn_lb_logits, hgrn_norm_w, w_out, final_norm_w):
    bsz, t_len, d = x.shape
    tc_len = ctx.shape[1]
    assert ada_w.shape[0] == 1, "single-layer stack only"
    d_a = conv_w.shape[2]
    d_b = hgrn_lb_logits.shape[2]
    hd = hgrn_norm_w.shape[1]
    n_heads = d_b // hd
    n_blocks_a = lru_wr.shape[2]
    n_cols = w_in.shape[2]
    assert t_len == GRID_W * CHUNK and tc_len % CHUNK == 0
    assert d_a // n_blocks_a == LANES and hd == LANES
    assert (2 * d_a) % d_b == 0 and n_cols == 2 * d_a + 5 * d_b
    first_b_group = (2 * d_a) // d_b
    n_rows_grid = t_len // GRID_W

    n_rows = -(-(bsz + 1) // SUBLANES) * SUBLANES
    cc = jnp.zeros((n_rows, d), F32).at[:bsz].set(c).at[bsz].set(c_ctx)
    mod3 = _modulation(cc, ada_w[0], ada_b[0]).reshape(n_rows, 1, 3 * d)

    w_in_b = w_in[0]
    w_out_b = w_out[0].astype(BF16)

    tm = 1024 if t_len % 1024 == 0 else t_len
    tn = 1024 if d_a % 1024 == 0 and d_b % 1024 == 0 else min(d_a, d_b)
    tpb = t_len // tm
    a_tiles = (2 * d_a) // tn
    b_tiles = (4 * d_b) // tn
    q_tiles = d_b // tn
    p_a = _in_projection(x.reshape(bsz * t_len, d), mod3, norm_w[0], w_in_b,
                         lambda i: i // tpb, lambda j: jnp.where(j < a_tiles, j, j + b_tiles),
                         2 * d_a + d_b, tm, tn, (d_a // tn, (2 * d_a + d_b) // tn))
    wb = 16
    p_b = _in_projection_colmajor(x.reshape(bsz, n_rows_grid, GRID_W, d), mod3, norm_w[0], w_in_b,
                                  lambda j: j + a_tiles, 4 * d_b, wb, tn, (0, q_tiles))
    xa_tiles = d_a // tn
    ctx_cols = d_a + 4 * d_b
    p_ctx = _in_projection(ctx.reshape(bsz * tc_len, d), mod3, norm_w[0], w_in_b,
                           lambda i: bsz, lambda j: jnp.where(j < xa_tiles, j, j + xa_tiles),
                           ctx_cols, bsz * tc_len, tn, (xa_tiles, xa_tiles + q_tiles))
    p_lat = p_a.reshape(bsz, t_len, 2 * d_a + d_b)
    p_ctx = p_ctx.reshape(bsz, tc_len, ctx_cols)

    def gate_w(dirn):
        return jnp.concatenate([lru_wr[0, dirn], lru_wi[0, dirn]], axis=-1).astype(BF16)

    def gate_b(dirn):
        return jnp.concatenate([lru_br[0, dirn].reshape(n_blocks_a, 1, LANES),
                                lru_bi[0, dirn].reshape(n_blocks_a, 1, LANES)], axis=-1)

    cb = 256 if d_a % 256 == 0 else LANES
    ya = _rglru(p_lat, p_ctx, conv_w[0], conv_b[0], gate_w(0), gate_b(0), gate_w(1), gate_b(1),
                lru_lambda[0], d_a, cb, tt=128)

    assert d_a % d_b == 0
    states = _hgrn(p_ctx.reshape(bsz, tc_len // CHUNK, CHUNK, ctx_cols), hgrn_lb_logits, d_b,
                   n_heads, d_a // d_b, None, False)
    o_f, o_b = _hgrn(p_b, hgrn_lb_logits, d_b, n_heads, 0, states, True)

    grid4 = lambda z: z.reshape(bsz, n_rows_grid, GRID_W, z.shape[-1])
    out = _out_projection(grid4(ya), o_f, o_b, grid4(p_lat), first_b_group, grid4(x), mod3,
                          hgrn_norm_w[0], final_norm_w, w_out_b, n_heads, 16, 16)
    return out.reshape(bsz, t_len, d)
```

```python
import functools

import jax
import jax.numpy as jnp
from jax import lax
from jax.experimental import pallas as pl
from jax.experimental.pallas import tpu as pltpu

GRID_W = 64
CHUNK = 64
LRU_C = 8.0
EPS = 1e-6
CONV_PAD_L = 2
LANES = 128
SUBLANES = 8
EXP2_CLAMP = 115.0
NEG_LOG2E = -1.4426950408889634
VMEM_LIMIT = 56 * 1024 * 1024
PROLOGUE_ROWS = 16
SEG_PAD = 4
CHUNKS_PER_STEP = 2
TILES_PER_TRIP = 4

F32 = jnp.float32
BF16 = jnp.bfloat16


def _sigmoid(z):
    return 1.0 / (1.0 + jnp.exp2(z * NEG_LOG2E))


def _silu(z):
    return z * _sigmoid(z)


def _softplus(z):
    return jnp.maximum(z, 0.0) + jnp.log1p(jnp.exp(-jnp.abs(z)))


def _mod_kernel(c_ref, w_ref, b_ref, o_ref):
    s = _silu(c_ref[...])
    o_ref[...] = jnp.dot(s.astype(BF16), w_ref[...].astype(BF16),
                         preferred_element_type=F32) + b_ref[...]


def _modulation(cc, w, b):
    rows, d = cc.shape
    n = w.shape[1]
    tn = 512 if n % 512 == 0 else n
    return pl.pallas_call(
        _mod_kernel,
        grid=(n // tn,),
        in_specs=[pl.BlockSpec((rows, d), lambda j: (0, 0)),
                  pl.BlockSpec((d, tn), lambda j: (0, j)),
                  pl.BlockSpec((1, tn), lambda j: (0, j))],
        out_specs=pl.BlockSpec((rows, tn), lambda j: (0, j)),
        out_shape=jax.ShapeDtypeStruct((rows, n), F32),
        compiler_params=pltpu.CompilerParams(
            dimension_semantics=("arbitrary",), vmem_limit_bytes=VMEM_LIMIT),
        name="adaln_modulation",
    )(cc, w, b.reshape(1, n))


def _store_projection(o_ref, res, j, silu_tiles):
    lo, hi = silu_tiles
    z = res.astype(o_ref.dtype)
    if hi <= lo:
        o_ref[...] = z
        return
    in_range = jnp.logical_and(j >= lo, j < hi)
    o_ref[...] = jnp.where(in_range, z * (0.5 * jnp.tanh(0.5 * z) + 0.5), z)


def _inproj_kernel(x_ref, shift_ref, scale_ref, nw_ref, w_ref, o_ref, h_ref, *, silu_tiles):
    @pl.when(pl.program_id(1) == 0)
    def _():
        rows = min(PROLOGUE_ROWS, x_ref.shape[0])
        gain = nw_ref[...] * (1.0 + scale_ref[0])

        def slab(s, carry):
            sl = pl.ds(pl.multiple_of(s * rows, rows), rows)
            x = x_ref[sl, :]
            rs = lax.rsqrt(jnp.mean(x * x, axis=-1, keepdims=True) + EPS)
            h_ref[sl, :] = (x_ref[sl, :] * rs * gain + shift_ref[0]).astype(BF16)
            return carry

        lax.fori_loop(0, x_ref.shape[0] // rows, slab, 0, unroll=4)

    res = jnp.dot(h_ref[...], w_ref[...].astype(BF16), preferred_element_type=F32)
    _store_projection(o_ref, res, pl.program_id(1), silu_tiles)


def _in_projection(x2d, mod3, norm_w, w_bf16, row_of_tile, wcol_of_tile, n_out, tm, tn, silu_tiles):
    m, d = x2d.shape
    return pl.pallas_call(
        functools.partial(_inproj_kernel, silu_tiles=silu_tiles),
        grid=(m // tm, n_out // tn),
        in_specs=[pl.BlockSpec((tm, d), lambda i, j: (i, 0)),
                  pl.BlockSpec((1, 1, d), lambda i, j: (row_of_tile(i), 0, 0)),
                  pl.BlockSpec((1, 1, d), lambda i, j: (row_of_tile(i), 0, 1)),
                  pl.BlockSpec((1, d), lambda i, j: (0, 0)),
                  pl.BlockSpec((d, tn), lambda i, j: (0, wcol_of_tile(j)))],
        out_specs=pl.BlockSpec((tm, tn), lambda i, j: (i, j)),
        out_shape=jax.ShapeDtypeStruct((m, n_out), BF16),
        scratch_shapes=[pltpu.VMEM((tm, d), BF16)],
        compiler_params=pltpu.CompilerParams(
            dimension_semantics=("parallel", "arbitrary"),
            vmem_limit_bytes=VMEM_LIMIT),
        name="in_projection",
    )(x2d, mod3, mod3, norm_w.reshape(1, d), w_bf16)


def _inproj_colmajor_kernel(x_ref, shift_ref, scale_ref, nw_ref, w_ref, o_ref, h_ref, *, slab,
                            silu_tiles):
    _, n_r, n_w, d = x_ref.shape

    rr = 2 * SUBLANES
    lanes = [slice(s * slab, (s + 1) * slab) for s in range(d // slab)]

    @pl.when(pl.program_id(2) == 0)
    def _():
        def row_group(g, carry):
            rows = pl.ds(pl.multiple_of(g * rr, rr), rr)
            sq = jnp.zeros((rr, n_w, slab), F32)
            for sl in lanes:
                xs = x_ref[0, rows, :, sl]
                sq = sq + xs * xs
            rs = lax.rsqrt(jnp.sum(sq, axis=-1, keepdims=True) * (1.0 / d) + EPS)
            for sl in lanes:
                y = x_ref[0, rows, :, sl] * rs * nw_ref[:, sl]
                y = y * (1.0 + scale_ref[0][:, sl]) + shift_ref[0][:, sl]
                h_ref[:, rows, sl] = jnp.transpose(y, (1, 0, 2)).astype(BF16)
            return carry

        lax.fori_loop(0, n_r // rr, row_group, 0)

    res = jnp.dot(h_ref[...].reshape(n_w * n_r, d), w_ref[...].astype(BF16),
                  preferred_element_type=F32)
    _store_projection(o_ref, res.reshape(1, n_w, n_r, res.shape[1]), pl.program_id(2), silu_tiles)


def _in_projection_colmajor(x4d, mod3, norm_w, w_bf16, wcol_of_tile, n_out, wb, tn, silu_tiles):
    bsz, n_r, n_w, d = x4d.shape
    kern = functools.partial(_inproj_colmajor_kernel, slab=2 * LANES, silu_tiles=silu_tiles)
    return pl.pallas_call(
        kern,
        grid=(bsz, n_w // wb, n_out // tn),
        in_specs=[pl.BlockSpec((1, n_r, wb, d), lambda b, w, j: (b, 0, w, 0)),
                  pl.BlockSpec((1, 1, d), lambda b, w, j: (b, 0, 0)),
                  pl.BlockSpec((1, 1, d), lambda b, w, j: (b, 0, 1)),
                  pl.BlockSpec((1, d), lambda b, w, j: (0, 0)),
                  pl.BlockSpec((d, tn), lambda b, w, j: (0, wcol_of_tile(j)))],
        out_specs=pl.BlockSpec((1, wb, n_r, tn), lambda b, w, j: (b, w, 0, j)),
        out_shape=jax.ShapeDtypeStruct((bsz, n_w, n_r, n_out), BF16),
        scratch_shapes=[pltpu.VMEM((wb, n_r, d), BF16)],
        compiler_params=pltpu.CompilerParams(
            dimension_semantics=("parallel", "parallel", "arbitrary"),
            vmem_limit_bytes=VMEM_LIMIT),
        name="in_projection_colmajor",
    )(x4d, mod3, mod3, norm_w.reshape(1, d), w_bf16)


def _group_scan(a, b, reverse):
    row = lax.broadcasted_iota(jnp.int32, a.shape, 1)
    for k in (1, 2, 4):
        if reverse:
            a_sh = pltpu.roll(a, SUBLANES - k, axis=1)
            b_sh = pltpu.roll(b, SUBLANES - k, axis=1)
            m = row < SUBLANES - k
        else:
            a_sh = pltpu.roll(a, k, axis=1)
            b_sh = pltpu.roll(b, k, axis=1)
            m = row >= k
        b = jnp.where(m, a * b_sh + b, b)
        a = jnp.where(m, a * a_sh, a)
    return a, b


def _sqrt_unit(x):
    return jnp.where(x > 0.0, x * lax.rsqrt(x), 0.0)


def _segment_scan(a, b, carry, reverse, scr_ref):
    tt = a.shape[0]
    seg = tt // SUBLANES
    pitch = seg + SEG_PAD
    for s in range(SUBLANES):
        rows = pl.ds(pitch * s, seg, stride=1)
        scr_ref[0, rows, :] = a[seg * s:seg * (s + 1), :]
        scr_ref[1, rows, :] = b[seg * s:seg * (s + 1), :]
    hs, ps = [None] * seg, [None] * seg
    h = p = None
    for j in (range(seg - 1, -1, -1) if reverse else range(seg)):
        step = pl.ds(j, SUBLANES, stride=pitch)
        a_j = scr_ref[0, step, :]
        b_j = scr_ref[1, step, :]
        h = b_j if h is None else a_j * h + b_j
        p = a_j if p is None else a_j * p
        hs[j], ps[j] = h, p
    g3, e3 = _group_scan(p[None], h[None], reverse)
    end = e3[0] + g3[0] * carry
    row = lax.broadcasted_iota(jnp.int32, end.shape, 0)
    if reverse:
        enter = jnp.where(row == SUBLANES - 1, carry, pltpu.roll(end, SUBLANES - 1, axis=0))
        carry_out = end[0:1, :]
    else:
        enter = jnp.where(row == 0, carry, pltpu.roll(end, 1, axis=0))
        carry_out = end[SUBLANES - 1:SUBLANES, :]
    for j in range(seg):
        scr_ref[2, pl.ds(j, SUBLANES, stride=pitch), :] = hs[j] + ps[j] * enter
    h_time = [scr_ref[2, pl.ds(pitch * s, seg, stride=1), :] for s in range(SUBLANES)]
    return jnp.concatenate(h_time, axis=0), carry_out


def _lru_block(u, wg, bg, sp, carry, reverse, scr_ref):
    g = jnp.dot(u.astype(BF16), wg, preferred_element_type=F32) + bg
    r = _sigmoid(g[:, :LANES])
    i = _sigmoid(g[:, LANES:])
    a = jnp.exp2(r * sp)
    b = _sqrt_unit(1.0 - a * a) * (i * u)
    return _segment_scan(a, b, carry, reverse, scr_ref)


def _rglru_kernel(xa_ref, ga_ref, xc_ref, cw_ref, cb_ref, wgf_ref, bgf_ref, wgb_ref, bgb_ref,
                  lam_ref, wcast_ref, o_ref, wcast_out_ref, xf_ref, xcf_ref, u_ref, hf_ref, scr_ref,
                  *, tt):
    wcast_out_ref[...] = wcast_ref[...].astype(BF16)
    t_len = xa_ref.shape[1]
    tc_len = xc_ref.shape[1]
    nblk = xa_ref.shape[2] // LANES
    lanes = [slice(k * LANES, (k + 1) * LANES) for k in range(nblk)]
    zeros = jnp.zeros((SUBLANES, LANES), F32)
    for k, sl in enumerate(lanes):
        xf_ref[k, 0:SUBLANES, :] = zeros
        xf_ref[k, SUBLANES:SUBLANES + t_len, :] = xa_ref[0, :, sl].astype(F32)
        xf_ref[k, SUBLANES + t_len:, :] = zeros
        xcf_ref[k, 0:SUBLANES, :] = zeros
        xcf_ref[k, SUBLANES:SUBLANES + tc_len, :] = xc_ref[0, :, sl].astype(F32)
        xcf_ref[k, SUBLANES + tc_len:, :] = zeros

    sp_f = _softplus(-lam_ref[0:1, :]) * (LRU_C * NEG_LOG2E)
    sp_b = _softplus(-lam_ref[1:2, :]) * (LRU_C * NEG_LOG2E)

    def conv(src_ref, k, t0):
        u = cb_ref[:, lanes[k]]
        for tap in range(cw_ref.shape[0]):
            off = SUBLANES - CONV_PAD_L + tap
            if off % SUBLANES == 0:
                start = t0 + off
                rows = pl.ds(start if isinstance(start, int) else pl.multiple_of(start, SUBLANES), tt)
            else:
                rows = pl.ds(t0 + off, tt, stride=1)
            u = u + src_ref[k, rows, :] * cw_ref[tap:tap + 1, lanes[k]]
        return u

    def block(u, k, carry, reverse, slot):
        scr = scr_ref.at[k * TILES_PER_TRIP + slot]
        if reverse:
            return _lru_block(u, wgb_ref[k], bgb_ref[k], sp_b[:, lanes[k]], carry, True, scr)
        return _lru_block(u, wgf_ref[k], bgf_ref[k], sp_f[:, lanes[k]], carry, False, scr)

    n_lat = t_len // tt
    n_ctx = tc_len // tt
    assert n_lat % TILES_PER_TRIP == 0
    carry0 = tuple(jnp.zeros((1, LANES), F32) for _ in lanes)

    def tile_start(trip, slot, n_tiles, reverse):
        idx = trip * TILES_PER_TRIP + slot
        return pl.multiple_of((n_tiles - 1 - idx if reverse else idx) * tt, tt)

    def ctx_pass(reverse):
        carry = carry0
        for idx in range(n_ctx):
            t0 = (n_ctx - 1 - idx if reverse else idx) * tt
            carry = tuple(block(conv(xcf_ref, k, t0), k, carry[k], reverse, idx % TILES_PER_TRIP)[1]
                          for k in range(nblk))
        return carry

    def lat_f(trip, carry):
        for slot in range(TILES_PER_TRIP):
            t0 = tile_start(trip, slot, n_lat, False)
            rows = pl.ds(t0, tt)
            out = []
            for k, sl in enumerate(lanes):
                u = conv(xf_ref, k, t0)
                u_ref[rows, sl] = u
                h, c_out = block(u, k, carry[k], False, slot)
                hf_ref[rows, sl] = h
                out.append(c_out)
            carry = tuple(out)
        return carry

    lax.fori_loop(0, n_lat // TILES_PER_TRIP, lat_f, ctx_pass(False))

    def lat_b(trip, carry):
        for slot in range(TILES_PER_TRIP):
            rows = pl.ds(tile_start(trip, slot, n_lat, True), tt)
            out = []
            for k, sl in enumerate(lanes):
                h, c_out = block(u_ref[rows, sl], k, carry[k], True, slot)
                gate = ga_ref[0, rows, sl].astype(F32)
                o_ref[0, rows, sl] = ((hf_ref[rows, sl] + h) * gate).astype(o_ref.dtype)
                out.append(c_out)
            carry = tuple(out)
        return carry

    lax.fori_loop(0, n_lat // TILES_PER_TRIP, lat_b, ctx_pass(True))


def _rglru(p_lat, p_ctx, conv_w, conv_b, wg_f, bg_f, wg_b, bg_b, lam, w_cast, d_a, cb, tt):
    bsz, t_len, _ = p_lat.shape
    tc_len = p_ctx.shape[1]
    nblk = cb // LANES
    ncb = d_a // cb
    wrows, wcols = w_cast.shape
    assert wrows % (bsz * ncb) == 0
    wr = wrows // (bsz * ncb)
    wspec = pl.BlockSpec((wr, wcols), lambda b, c: (b * ncb + c, 0))
    kern = functools.partial(_rglru_kernel, tt=tt)
    return pl.pallas_call(
        kern,
        grid=(bsz, ncb),
        in_specs=[pl.BlockSpec((1, t_len, cb), lambda b, c: (b, 0, c)),
                  pl.BlockSpec((1, t_len, cb), lambda b, c: (b, 0, ncb + c)),
                  pl.BlockSpec((1, tc_len, cb), lambda b, c: (b, 0, c)),
                  pl.BlockSpec((conv_w.shape[0], cb), lambda b, c: (0, c)),
                  pl.BlockSpec((1, cb), lambda b, c: (0, c)),
                  pl.BlockSpec((nblk, LANES, 2 * LANES), lambda b, c: (c, 0, 0)),
                  pl.BlockSpec((nblk, 1, 2 * LANES), lambda b, c: (c, 0, 0)),
                  pl.BlockSpec((nblk, LANES, 2 * LANES), lambda b, c: (c, 0, 0)),
                  pl.BlockSpec((nblk, 1, 2 * LANES), lambda b, c: (c, 0, 0)),
                  pl.BlockSpec((2, cb), lambda b, c: (0, c)),
                  wspec],
        out_specs=[pl.BlockSpec((1, t_len, cb), lambda b, c: (b, 0, c)), wspec],
        out_shape=[jax.ShapeDtypeStruct((bsz, t_len, d_a), BF16),
                   jax.ShapeDtypeStruct((wrows, wcols), BF16)],
        scratch_shapes=[pltpu.VMEM((nblk, t_len + 2 * SUBLANES, LANES), F32),
                        pltpu.VMEM((nblk, tc_len + 2 * SUBLANES, LANES), F32),
                        pltpu.VMEM((t_len, cb), F32),
                        pltpu.VMEM((t_len, cb), F32),
                        pltpu.VMEM((nblk * TILES_PER_TRIP, 3, tt + SUBLANES * SEG_PAD, LANES),
                                   F32)],
        compiler_params=pltpu.CompilerParams(
            dimension_semantics=("parallel", "parallel"),
            vmem_limit_bytes=VMEM_LIMIT),
        name="rglru",
    )(p_lat, p_lat, p_ctx, conv_w, conv_b.reshape(1, d_a), wg_f, bg_f, wg_b, bg_b, lam, w_cast)


_NT = (((1,), (1,)), ((), ()))
_TN = (((0,), (0,)), ((), ()))


class _HgrnDir:
    def __init__(self, d, q_ref, f_ref, v_ref, lb, st_ref, o_ref, scratch, reverse, n_heads, hd,
                 sub):
        self.d, self.q_ref, self.f_ref, self.v_ref, self.lb = d, q_ref, f_ref, v_ref, lb
        self.sub = sub
        self.st_ref, self.o_ref, self.reverse, self.n_heads, self.hd = st_ref, o_ref, reverse, n_heads, hd
        (self.hl_ref, self.kk_ref, self.cum_ref, self.qd_ref, self.kd_ref, self.qin_ref,
         self.kout_ref, self.dec_ref, self.dect_ref, self.sc_ref) = scratch
        self.c_len = q_ref.shape[2]
        r_i = lax.broadcasted_iota(jnp.int32, (self.c_len, self.c_len), 0)
        c_i = lax.broadcasted_iota(jnp.int32, (self.c_len, self.c_len), 1)
        self.tri = (r_i <= c_i) if reverse else (r_i >= c_i)

    def _slabs(self, width):
        total = self.n_heads * self.hd
        return [slice(s, s + width) for s in range(0, total, width)]


    def _gate_slab(self, sl):
        d, c_len = self.d, self.c_len
        lb = self.lb[:, sl]
        f = lb + (1.0 - lb) * _sigmoid(self.f_ref[0, self.sub, :, sl].astype(F32))
        logf = jnp.log2(f)
        self.kk_ref[d, :, sl] = (1.0 - f).astype(BF16)
        hi = logf.astype(BF16)
        self.hl_ref[d, 0:c_len, sl] = hi
        self.hl_ref[d, c_len:2 * c_len, sl] = (logf - hi.astype(F32)).astype(BF16)

    def _cumulate(self):
        tri_b = self.tri.astype(BF16)
        tri2 = jnp.concatenate([tri_b, tri_b], axis=1)
        self.cum_ref[self.d] = jnp.dot(tri2, self.hl_ref[self.d], preferred_element_type=F32)

    def _decay_slab(self, sl):
        d, c_len = self.d, self.c_len
        half = c_len // 2
        row_last = 0 if self.reverse else c_len - 1
        row_ref = half if self.reverse else half - 1
        cum = self.cum_ref[d, :, sl]
        last = cum[row_last:row_last + 1, :]
        ref = cum[row_ref:row_ref + 1, :]
        kk = self.kk_ref[d, :, sl]
        qs = self.q_ref[0, self.sub, :, sl]
        self.qd_ref[d, :, sl] = qs * jnp.exp2(jnp.minimum(cum - ref, EXP2_CLAMP)).astype(BF16)
        self.kd_ref[d, :, sl] = kk * jnp.exp2(jnp.minimum(ref - cum, EXP2_CLAMP)).astype(BF16)
        self.qin_ref[d, :, sl] = qs * jnp.exp2(cum).astype(BF16)
        self.kout_ref[d, :, sl] = kk * jnp.exp2(last - cum).astype(BF16)
        dec = jnp.exp2(last)
        for i in range((sl.stop - sl.start) // self.hd):
            h = sl.start // self.hd + i
            self.dec_ref[d, h:h + 1, :] = dec[:, i * self.hd:(i + 1) * self.hd]

    def _decay_columns(self):
        self.dect_ref[self.d] = jnp.transpose(self.dec_ref[self.d])

    def _head(self, h):
        return slice(h * self.hd, (h + 1) * self.hd)

    def _scores(self, h):
        d, sl = self.d, self._head(h)
        s = lax.dot_general(self.qd_ref[d, :, sl], self.kd_ref[d, :, sl], _NT,
                            preferred_element_type=F32)
        self.sc_ref[d, h] = jnp.where(self.tri, s, 0.0).astype(BF16)

    def _output(self, h):
        d, sl = self.d, self._head(h)
        o = (jnp.dot(self.sc_ref[d, h], self.v_ref[0, self.sub, :, sl], preferred_element_type=F32)
             + jnp.dot(self.qin_ref[d, :, sl], self.st_ref[h].astype(BF16),
                       preferred_element_type=F32))
        self.o_ref[0, self.sub, :, sl] = o.astype(self.o_ref.dtype)

    def _update(self, h):
        d, sl = self.d, self._head(h)
        decay = jnp.broadcast_to(self.dect_ref[d, :, h:h + 1], (self.hd, self.hd))
        self.st_ref[h] = (self.st_ref[h] * decay
                          + lax.dot_general(self.kout_ref[d, :, sl], self.v_ref[0, self.sub, :, sl], _TN,
                                            preferred_element_type=F32))

    def stages(self):
        heads = range(self.n_heads)
        slabs = self._slabs(2 * LANES)
        out = [[functools.partial(self._gate_slab, sl) for sl in slabs], [self._cumulate],
               [functools.partial(self._decay_slab, sl) for sl in slabs], [self._decay_columns]]
        if self.o_ref is not None:
            out += [[functools.partial(self._scores, h) for h in heads],
                    [functools.partial(self._output, h) for h in heads]]
        return out + [[functools.partial(self._update, h) for h in heads]]


def _lower_bounds(logits_ref, layer):
    out = []
    for d in range(2):
        rows = [logits_ref[d, l:l + 1, :] for l in range(logits_ref.shape[1])]
        m = functools.reduce(jnp.maximum, rows)
        e = [jnp.exp(r - m) for r in rows]
        out.append(sum(e[:layer + 1]) / sum(e))
    return out


def _hgrn_kernel(*refs, n_heads, hd, has_init, emit_o, emit_state):
    qf_ref, ff_ref, vf_ref, qb_ref, fb_ref, vb_ref, lg_ref = refs[:7]
    pos = 7
    if has_init:
        s0f_ref, s0b_ref = refs[pos:pos + 2]
        pos += 2
    if emit_o:
        of_ref, ob_ref = refs[pos:pos + 2]
        pos += 2
    if emit_state:
        sof_ref, sob_ref = refs[pos:pos + 2]
        pos += 2
    sf_ref, sb_ref = refs[pos:pos + 2]
    scratch = refs[pos + 2:]

    j = pl.program_id(1)
    dec_ref = scratch[7]
    assert n_heads <= dec_ref.shape[1]

    @pl.when(j == 0)
    def _():
        if has_init:
            sf_ref[...] = s0f_ref[0]
            sb_ref[...] = s0b_ref[0]
        else:
            sf_ref[...] = jnp.zeros_like(sf_ref)
            sb_ref[...] = jnp.zeros_like(sb_ref)
        dec_ref[...] = jnp.zeros_like(dec_ref)

    lb_f, lb_b = _lower_bounds(lg_ref, 0)
    n_sub = qf_ref.shape[1]
    for k in range(n_sub):
        dirs = [_HgrnDir(0, qf_ref, ff_ref, vf_ref, lb_f, sf_ref, of_ref if emit_o else None,
                         scratch, False, n_heads, hd, k),
                _HgrnDir(1, qb_ref, fb_ref, vb_ref, lb_b, sb_ref, ob_ref if emit_o else None,
                         scratch, True, n_heads, hd, n_sub - 1 - k)]
        for stage_f, stage_b in zip(dirs[0].stages(), dirs[1].stages()):
            for unit in stage_f + stage_b:
                unit()

    if emit_state:
        @pl.when(j == pl.num_programs(1) - 1)
        def _():
            sof_ref[0] = sf_ref[...]
            sob_ref[0] = sb_ref[...]


def _hgrn(p_view, logits, d_b, n_heads, first_group, init_states, emit_o):
    bsz, n_chunks = p_view.shape[:2]
    hd = d_b // n_heads
    has_init = init_states is not None
    emit_state = not emit_o

    cps = CHUNKS_PER_STEP if n_chunks % CHUNKS_PER_STEP == 0 else 1
    n_steps = n_chunks // cps

    def chunk(j, reverse):
        return n_steps - 1 - j if reverse else j

    def spec(group, reverse):
        def imap(b, j):
            return (b, chunk(j, reverse), 0, first_group + group)
        return pl.BlockSpec((1, cps, CHUNK, d_b), imap)

    in_specs = [spec(0, False), spec(1, False), spec(3, False),
                spec(0, True), spec(2, True), spec(3, True),
                pl.BlockSpec(logits.shape, lambda b, j: (0, 0, 0))]
    args = [p_view] * 6 + [logits]
    state_spec = pl.BlockSpec((1, n_heads, hd, hd), lambda b, j: (b, 0, 0, 0))
    state_shape = jax.ShapeDtypeStruct((bsz, n_heads, hd, hd), F32)
    if has_init:
        in_specs += [state_spec, state_spec]
        args += list(init_states)
    out_specs, out_shape = [], []
    if emit_o:
        o_shape = jax.ShapeDtypeStruct((bsz, n_chunks, CHUNK, d_b), BF16)
        out_specs += [pl.BlockSpec((1, cps, CHUNK, d_b), lambda b, j: (b, chunk(j, False), 0, 0)),
                      pl.BlockSpec((1, cps, CHUNK, d_b), lambda b, j: (b, chunk(j, True), 0, 0))]
        out_shape += [o_shape, o_shape]
    if emit_state:
        out_specs += [state_spec, state_spec]
        out_shape += [state_shape, state_shape]
    kern = functools.partial(_hgrn_kernel, n_heads=n_heads, hd=hd, has_init=has_init,
                             emit_o=emit_o, emit_state=emit_state)
    return pl.pallas_call(
        kern,
        grid=(bsz, n_steps),
        in_specs=in_specs,
        out_specs=out_specs,
        out_shape=out_shape,
        scratch_shapes=[pltpu.VMEM((n_heads, hd, hd), F32),
                        pltpu.VMEM((n_heads, hd, hd), F32),
                        pltpu.VMEM((2, 2 * CHUNK, d_b), BF16),
                        pltpu.VMEM((2, CHUNK, d_b), BF16),
                        pltpu.VMEM((2, CHUNK, d_b), F32),
                        pltpu.VMEM((2, CHUNK, d_b), BF16),
                        pltpu.VMEM((2, CHUNK, d_b), BF16),
                        pltpu.VMEM((2, CHUNK, d_b), BF16),
                        pltpu.VMEM((2, CHUNK, d_b), BF16),
                        pltpu.VMEM((2, hd, hd), F32),
                        pltpu.VMEM((2, hd, hd), F32),
                        pltpu.VMEM((2, n_heads, CHUNK, CHUNK), BF16)],
        compiler_params=pltpu.CompilerParams(
            dimension_semantics=("parallel", "arbitrary"),
            vmem_limit_bytes=VMEM_LIMIT),
        name="hgrn2_latent" if emit_o else "hgrn2_context",
    )(*args)


def _outproj_kernel(ya_ref, of_ref, ob_ref, gb_ref, x_ref, gate_ref, hnw_ref, fnw_ref, w_ref,
                    o_ref, y_ref, *, n_heads, hd):
    _, rb, wb, d_a = ya_ref.shape
    rows = rb * wb
    d = x_ref.shape[3]
    y_ref[:, 0:d_a] = ya_ref[0].reshape(rows, d_a)
    hnw = hnw_ref[...]
    for h in range(n_heads):
        sl = slice(h * hd, (h + 1) * hd)
        o = of_ref[0, :, :, sl].astype(F32) + ob_ref[0, :, :, sl].astype(F32)
        ms = jnp.mean(o * o, axis=-1, keepdims=True)
        on = jnp.transpose(o * lax.rsqrt(ms + EPS) * hnw, (1, 0, 2))
        yb = on * gb_ref[0, :, :, sl].astype(F32)
        y_ref[:, d_a + h * hd:d_a + (h + 1) * hd] = yb.reshape(rows, hd).astype(BF16)
    acc = jnp.dot(y_ref[...], w_ref[...], preferred_element_type=F32)
    z = x_ref[0].reshape(rows, d) + gate_ref[0] * acc
    ms = jnp.mean(z * z, axis=-1, keepdims=True)
    o_ref[0] = (z * lax.rsqrt(ms + EPS) * fnw_ref[...]).reshape(rb, wb, d)


def _out_projection(ya4, of4, ob4, pa4, gb_block, x4, mod3, hnw, fnw, w_bf16, n_heads, rb, wb):
    bsz, n_r, n_w, d = x4.shape
    d_a = ya4.shape[3]
    d_b = of4.shape[3]
    hd = d_b // n_heads
    kern = functools.partial(_outproj_kernel, n_heads=n_heads, hd=hd)

    def raster(c, col=0):
        return pl.BlockSpec((1, rb, wb, c), lambda b, r, w: (b, r, w, col))

    def colmajor(c):
        return pl.BlockSpec((1, wb, rb, c), lambda b, r, w: (b, w, r, 0))

    return pl.pallas_call(
        kern,
        grid=(bsz, n_r // rb, n_w // wb),
        in_specs=[raster(d_a), colmajor(d_b), colmajor(d_b), raster(d_b, gb_block), raster(d),
                  pl.BlockSpec((1, 1, d), lambda b, r, w: (b, 0, 2)),
                  pl.BlockSpec((1, hd), lambda b, r, w: (0, 0)),
                  pl.BlockSpec((1, d), lambda b, r, w: (0, 0)),
                  pl.BlockSpec((d_a + d_b, d), lambda b, r, w: (0, 0),
                               pipeline_mode=pl.Buffered(1))],
        out_specs=raster(d),
        out_shape=jax.ShapeDtypeStruct((bsz, n_r, n_w, d), F32),
        scratch_shapes=[pltpu.VMEM((rb * wb, d_a + d_b), BF16)],
        compiler_params=pltpu.CompilerParams(
            dimension_semantics=("parallel", "parallel", "parallel"),
            vmem_limit_bytes=VMEM_LIMIT),
        name="out_projection",
    )(ya4, of4, ob4, pa4, x4, mod3, hnw.reshape(1, hd), fnw.reshape(1, d), w_bf16)


def kernel(x, c, ctx, c_ctx, ada_w, ada_b, norm_w, w_in, conv_w, conv_b, lru_wr, lru_br, lru_wi,
           lru_bi, lru_lambda, hgrn_lb_logits, hgrn_norm_w, w_out, final_norm_w):
    bsz, t_len, d = x.shape
    tc_len = ctx.shape[1]
    assert ada_w.shape[0] == 1, "single-layer stack only"
    d_a = conv_w.shape[2]
    d_b = hgrn_lb_logits.shape[2]
    hd = hgrn_norm_w.shape[1]
    n_heads = d_b // hd
    n_blocks_a = lru_wr.shape[2]
    n_cols = w_in.shape[2]
    assert t_len == GRID_W * CHUNK and tc_len % CHUNK == 0
    assert d_a // n_blocks_a == LANES and hd == LANES
    assert (2 * d_a) % d_b == 0 and n_cols == 2 * d_a + 5 * d_b
    first_b_group = (2 * d_a) // d_b
    n_rows_grid = t_len // GRID_W

    n_rows = -(-(bsz + 1) // SUBLANES) * SUBLANES
    cc = jnp.zeros((n_rows, d), F32).at[:bsz].set(c).at[bsz].set(c_ctx)
    mod3 = _modulation(cc, ada_w[0], ada_b[0]).reshape(n_rows, 1, 3 * d)

    w_in_b = w_in[0]

    tm = 1024 if t_len % 1024 == 0 else t_len
    tn = 1024 if d_a % 1024 == 0 and d_b % 1024 == 0 else min(d_a, d_b)
    tpb = t_len // tm
    a_tiles = (2 * d_a) // tn
    b_tiles = (4 * d_b) // tn
    q_tiles = d_b // tn
    p_a = _in_projection(x.reshape(bsz * t_len, d), mod3, norm_w[0], w_in_b,
                         lambda i: i // tpb, lambda j: jnp.where(j < a_tiles, j, j + b_tiles),
                         2 * d_a + d_b, tm, tn, (d_a // tn, (2 * d_a + d_b) // tn))
    wb = 16
    p_b = _in_projection_colmajor(x.reshape(bsz, n_rows_grid, GRID_W, d), mod3, norm_w[0], w_in_b,
                                  lambda j: j + a_tiles, 4 * d_b, wb, tn, (0, q_tiles))
    xa_tiles = d_a // tn
    ctx_cols = d_a + 4 * d_b
    p_ctx = _in_projection(ctx.reshape(bsz * tc_len, d), mod3, norm_w[0], w_in_b,
                           lambda i: bsz, lambda j: jnp.where(j < xa_tiles, j, j + xa_tiles),
                           ctx_cols, bsz * tc_len, tn, (xa_tiles, xa_tiles + q_tiles))
    p_lat = p_a.reshape(bsz, t_len, 2 * d_a + d_b)
    p_ctx = p_ctx.reshape(bsz, tc_len, ctx_cols)

    def gate_w(dirn):
        return jnp.concatenate([lru_wr[0, dirn], lru_wi[0, dirn]], axis=-1).astype(BF16)

    def gate_b(dirn):
        return jnp.concatenate([lru_br[0, dirn].reshape(n_blocks_a, 1, LANES),
                                lru_bi[0, dirn].reshape(n_blocks_a, 1, LANES)], axis=-1)

    cb = 256 if d_a % 256 == 0 else LANES
    ya, w_out_b = _rglru(p_lat, p_ctx, conv_w[0], conv_b[0], gate_w(0), gate_b(0), gate_w(1),
                         gate_b(1), lru_lambda[0], w_out[0], d_a, cb, tt=128)

    assert d_a % d_b == 0
    states = _hgrn(p_ctx.reshape(bsz, tc_len // CHUNK, CHUNK, ctx_cols), hgrn_lb_logits, d_b,
                   n_heads, d_a // d_b, None, False)
    o_f, o_b = _hgrn(p_b, hgrn_lb_logits, d_b, n_heads, 0, states, True)

    grid4 = lambda z: z.reshape(bsz, n_rows_grid, GRID_W, z.shape[-1])
    out = _out_projection(grid4(ya), o_f, o_b, grid4(p_lat), first_b_group, grid4(x), mod3,
                          hgrn_norm_w[0], final_norm_w, w_out_b, n_heads, 16, 16)
    return out.reshape(bsz, t_len, d)
```

```python
import functools

import jax
import jax.numpy as jnp
from jax import lax
from jax.experimental import pallas as pl
from jax.experimental.pallas import tpu as pltpu

GRID_W = 64
CHUNK = 64
LRU_C = 8.0
EPS = 1e-6
CONV_PAD_L = 2
LANES = 128
SUBLANES = 8
EXP2_CLAMP = 115.0
NEG_LOG2E = -1.4426950408889634
VMEM_LIMIT = 56 * 1024 * 1024
PROLOGUE_ROWS = 16
SEG_PAD = 4
CHUNKS_PER_STEP = 2
TILES_PER_TRIP = 4

F32 = jnp.float32
BF16 = jnp.bfloat16


def _sigmoid(z):
    return 1.0 / (1.0 + jnp.exp2(z * NEG_LOG2E))


def _silu(z):
    return z * _sigmoid(z)


def _softplus(z):
    return jnp.maximum(z, 0.0) + jnp.log1p(jnp.exp(-jnp.abs(z)))


def _mod_kernel(c_ref, w_ref, b_ref, o_ref):
    s = _silu(c_ref[...])
    o_ref[...] = jnp.dot(s.astype(BF16), w_ref[...].astype(BF16),
                         preferred_element_type=F32) + b_ref[...]


def _modulation(cc, w, b):
    rows, d = cc.shape
    n = w.shape[1]
    tn = 512 if n % 512 == 0 else n
    return pl.pallas_call(
        _mod_kernel,
        grid=(n // tn,),
        in_specs=[pl.BlockSpec((rows, d), lambda j: (0, 0)),
                  pl.BlockSpec((d, tn), lambda j: (0, j)),
                  pl.BlockSpec((1, tn), lambda j: (0, j))],
        out_specs=pl.BlockSpec((rows, tn), lambda j: (0, j)),
        out_shape=jax.ShapeDtypeStruct((rows, n), F32),
        compiler_params=pltpu.CompilerParams(
            dimension_semantics=("arbitrary",), vmem_limit_bytes=VMEM_LIMIT),
        name="adaln_modulation",
    )(cc, w, b.reshape(1, n))


def _store_projection(o_ref, res, j, silu_tiles):
    lo, hi = silu_tiles
    z = res.astype(o_ref.dtype)
    if hi <= lo:
        o_ref[...] = z
        return
    in_range = jnp.logical_and(j >= lo, j < hi)
    o_ref[...] = jnp.where(in_range, z * (0.5 * jnp.tanh(0.5 * z) + 0.5), z)


def _inproj_kernel(x_ref, shift_ref, scale_ref, nw_ref, w_ref, o_ref, h_ref, *, silu_tiles):
    @pl.when(pl.program_id(1) == 0)
    def _():
        rows = min(PROLOGUE_ROWS, x_ref.shape[0])
        gain = nw_ref[...] * (1.0 + scale_ref[0])

        def slab(s, carry):
            sl = pl.ds(pl.multiple_of(s * rows, rows), rows)
            x = x_ref[sl, :]
            rs = lax.rsqrt(jnp.mean(x * x, axis=-1, keepdims=True) + EPS)
            h_ref[sl, :] = (x_ref[sl, :] * rs * gain + shift_ref[0]).astype(BF16)
            return carry

        lax.fori_loop(0, x_ref.shape[0] // rows, slab, 0, unroll=4)

    res = jnp.dot(h_ref[...], w_ref[...].astype(BF16), preferred_element_type=F32)
    _store_projection(o_ref, res, pl.program_id(1), silu_tiles)


def _in_projection(x2d, mod3, norm_w, w_bf16, row_of_tile, wcol_of_tile, n_out, tm, tn, silu_tiles):
    m, d = x2d.shape
    return pl.pallas_call(
        functools.partial(_inproj_kernel, silu_tiles=silu_tiles),
        grid=(m // tm, n_out // tn),
        in_specs=[pl.BlockSpec((tm, d), lambda i, j: (i, 0)),
                  pl.BlockSpec((1, 1, d), lambda i, j: (row_of_tile(i), 0, 0)),
                  pl.BlockSpec((1, 1, d), lambda i, j: (row_of_tile(i), 0, 1)),
                  pl.BlockSpec((1, d), lambda i, j: (0, 0)),
                  pl.BlockSpec((d, tn), lambda i, j: (0, wcol_of_tile(j)))],
        out_specs=pl.BlockSpec((tm, tn), lambda i, j: (i, j)),
        out_shape=jax.ShapeDtypeStruct((m, n_out), BF16),
        scratch_shapes=[pltpu.VMEM((tm, d), BF16)],
        compiler_params=pltpu.CompilerParams(
            dimension_semantics=("parallel", "arbitrary"),
            vmem_limit_bytes=VMEM_LIMIT),
        name="in_projection",
    )(x2d, mod3, mod3, norm_w.reshape(1, d), w_bf16)


def _matmul_kernel(h_ref, w_ref, o_ref, *, silu_tiles):
    res = jnp.dot(h_ref[...], w_ref[...].astype(BF16), preferred_element_type=F32)
    _store_projection(o_ref, res, pl.program_id(1), silu_tiles)


def _projection_from_h(h2d, w_f32, wcol_of_tile, n_out, tm, tn, silu_tiles):
    m, d = h2d.shape
    return pl.pallas_call(
        functools.partial(_matmul_kernel, silu_tiles=silu_tiles),
        grid=(m // tm, n_out // tn),
        in_specs=[pl.BlockSpec((tm, d), lambda i, j: (i, 0)),
                  pl.BlockSpec((d, tn), lambda i, j: (0, wcol_of_tile(j)))],
        out_specs=pl.BlockSpec((tm, tn), lambda i, j: (i, j)),
        out_shape=jax.ShapeDtypeStruct((m, n_out), BF16),
        compiler_params=pltpu.CompilerParams(
            dimension_semantics=("parallel", "arbitrary"),
            vmem_limit_bytes=VMEM_LIMIT),
        name="projection_from_h",
    )(h2d, w_f32)


def _inproj_colmajor_kernel(x_ref, shift_ref, scale_ref, nw_ref, w_ref, o_ref, hr_ref, h_ref, *,
                            slab, silu_tiles):
    _, n_r, n_w, d = x_ref.shape

    rr = 2 * SUBLANES
    lanes = [slice(s * slab, (s + 1) * slab) for s in range(d // slab)]

    @pl.when(pl.program_id(2) == 0)
    def _():
        def row_group(g, carry):
            rows = pl.ds(pl.multiple_of(g * rr, rr), rr)
            sq = jnp.zeros((rr, n_w, slab), F32)
            for sl in lanes:
                xs = x_ref[0, rows, :, sl]
                sq = sq + xs * xs
            rs = lax.rsqrt(jnp.sum(sq, axis=-1, keepdims=True) * (1.0 / d) + EPS)
            for sl in lanes:
                y = x_ref[0, rows, :, sl] * rs * nw_ref[:, sl]
                y = y * (1.0 + scale_ref[0][:, sl]) + shift_ref[0][:, sl]
                hr_ref[0, rows, :, sl] = y.astype(BF16)
                h_ref[:, rows, sl] = jnp.transpose(y, (1, 0, 2)).astype(BF16)
            return carry

        lax.fori_loop(0, n_r // rr, row_group, 0)

    res = jnp.dot(h_ref[...].reshape(n_w * n_r, d), w_ref[...].astype(BF16),
                  preferred_element_type=F32)
    _store_projection(o_ref, res.reshape(1, n_w, n_r, res.shape[1]), pl.program_id(2), silu_tiles)


def _in_projection_colmajor(x4d, mod3, norm_w, w_bf16, wcol_of_tile, n_out, wb, tn, silu_tiles):
    bsz, n_r, n_w, d = x4d.shape
    kern = functools.partial(_inproj_colmajor_kernel, slab=2 * LANES, silu_tiles=silu_tiles)
    return pl.pallas_call(
        kern,
        grid=(bsz, n_w // wb, n_out // tn),
        in_specs=[pl.BlockSpec((1, n_r, wb, d), lambda b, w, j: (b, 0, w, 0)),
                  pl.BlockSpec((1, 1, d), lambda b, w, j: (b, 0, 0)),
                  pl.BlockSpec((1, 1, d), lambda b, w, j: (b, 0, 1)),
                  pl.BlockSpec((1, d), lambda b, w, j: (0, 0)),
                  pl.BlockSpec((d, tn), lambda b, w, j: (0, wcol_of_tile(j)))],
        out_specs=[pl.BlockSpec((1, wb, n_r, tn), lambda b, w, j: (b, w, 0, j)),
                   pl.BlockSpec((1, n_r, wb, d), lambda b, w, j: (b, 0, w, 0))],
        out_shape=[jax.ShapeDtypeStruct((bsz, n_w, n_r, n_out), BF16),
                   jax.ShapeDtypeStruct((bsz, n_r, n_w, d), BF16)],
        scratch_shapes=[pltpu.VMEM((wb, n_r, d), BF16)],
        compiler_params=pltpu.CompilerParams(
            dimension_semantics=("parallel", "parallel", "arbitrary"),
            vmem_limit_bytes=VMEM_LIMIT),
        name="in_projection_colmajor",
    )(x4d, mod3, mod3, norm_w.reshape(1, d), w_bf16)


def _group_scan(a, b, reverse):
    row = lax.broadcasted_iota(jnp.int32, a.shape, 1)
    for k in (1, 2, 4):
        if reverse:
            a_sh = pltpu.roll(a, SUBLANES - k, axis=1)
            b_sh = pltpu.roll(b, SUBLANES - k, axis=1)
            m = row < SUBLANES - k
        else:
            a_sh = pltpu.roll(a, k, axis=1)
            b_sh = pltpu.roll(b, k, axis=1)
            m = row >= k
        b = jnp.where(m, a * b_sh + b, b)
        a = jnp.where(m, a * a_sh, a)
    return a, b


def _sqrt_unit(x):
    return jnp.where(x > 0.0, x * lax.rsqrt(x), 0.0)


def _segment_scan(a, b, carry, reverse, scr_ref):
    tt = a.shape[0]
    seg = tt // SUBLANES
    pitch = seg + SEG_PAD
    for s in range(SUBLANES):
        rows = pl.ds(pitch * s, seg, stride=1)
        scr_ref[0, rows, :] = a[seg * s:seg * (s + 1), :]
        scr_ref[1, rows, :] = b[seg * s:seg * (s + 1), :]
    hs, ps = [None] * seg, [None] * seg
    h = p = None
    for j in (range(seg - 1, -1, -1) if reverse else range(seg)):
        step = pl.ds(j, SUBLANES, stride=pitch)
        a_j = scr_ref[0, step, :]
        b_j = scr_ref[1, step, :]
        h = b_j if h is None else a_j * h + b_j
        p = a_j if p is None else a_j * p
        hs[j], ps[j] = h, p
    g3, e3 = _group_scan(p[None], h[None], reverse)
    end = e3[0] + g3[0] * carry
    row = lax.broadcasted_iota(jnp.int32, end.shape, 0)
    if reverse:
        enter = jnp.where(row == SUBLANES - 1, carry, pltpu.roll(end, SUBLANES - 1, axis=0))
        carry_out = end[0:1, :]
    else:
        enter = jnp.where(row == 0, carry, pltpu.roll(end, 1, axis=0))
        carry_out = end[SUBLANES - 1:SUBLANES, :]
    for j in range(seg):
        scr_ref[2, pl.ds(j, SUBLANES, stride=pitch), :] = hs[j] + ps[j] * enter
    h_time = [scr_ref[2, pl.ds(pitch * s, seg, stride=1), :] for s in range(SUBLANES)]
    return jnp.concatenate(h_time, axis=0), carry_out


def _lru_block(u, wg, bg, sp, carry, reverse, scr_ref):
    g = jnp.dot(u.astype(BF16), wg, preferred_element_type=F32) + bg
    r = _sigmoid(g[:, :LANES])
    i = _sigmoid(g[:, LANES:])
    a = jnp.exp2(r * sp)
    b = _sqrt_unit(1.0 - a * a) * (i * u)
    return _segment_scan(a, b, carry, reverse, scr_ref)


def _rglru_kernel(xa_ref, ga_ref, xc_ref, cw_ref, cb_ref, wgf_ref, bgf_ref, wgb_ref, bgb_ref,
                  lam_ref, wcast_ref, o_ref, wcast_out_ref, xf_ref, xcf_ref, u_ref, hf_ref, scr_ref,
                  *, tt):
    wcast_out_ref[...] = wcast_ref[...].astype(BF16)
    t_len = xa_ref.shape[1]
    tc_len = xc_ref.shape[1]
    nblk = xa_ref.shape[2] // LANES
    lanes = [slice(k * LANES, (k + 1) * LANES) for k in range(nblk)]
    zeros = jnp.zeros((SUBLANES, LANES), F32)
    for k, sl in enumerate(lanes):
        xf_ref[k, 0:SUBLANES, :] = zeros
        xf_ref[k, SUBLANES:SUBLANES + t_len, :] = xa_ref[0, :, sl].astype(F32)
        xf_ref[k, SUBLANES + t_len:, :] = zeros
        xcf_ref[k, 0:SUBLANES, :] = zeros
        xcf_ref[k, SUBLANES:SUBLANES + tc_len, :] = xc_ref[0, :, sl].astype(F32)
        xcf_ref[k, SUBLANES + tc_len:, :] = zeros

    sp_f = _softplus(-lam_ref[0:1, :]) * (LRU_C * NEG_LOG2E)
    sp_b = _softplus(-lam_ref[1:2, :]) * (LRU_C * NEG_LOG2E)

    def conv(src_ref, k, t0):
        u = cb_ref[:, lanes[k]]
        for tap in range(cw_ref.shape[0]):
            off = SUBLANES - CONV_PAD_L + tap
            if off % SUBLANES == 0:
                start = t0 + off
                rows = pl.ds(start if isinstance(start, int) else pl.multiple_of(start, SUBLANES), tt)
            else:
                rows = pl.ds(t0 + off, tt, stride=1)
            u = u + src_ref[k, rows, :] * cw_ref[tap:tap + 1, lanes[k]]
        return u

    def block(u, k, carry, reverse, slot):
        scr = scr_ref.at[k * TILES_PER_TRIP + slot]
        if reverse:
            return _lru_block(u, wgb_ref[k], bgb_ref[k], sp_b[:, lanes[k]], carry, True, scr)
        return _lru_block(u, wgf_ref[k], bgf_ref[k], sp_f[:, lanes[k]], carry, False, scr)

    n_lat = t_len // tt
    n_ctx = tc_len // tt
    assert n_lat % TILES_PER_TRIP == 0
    carry0 = tuple(jnp.zeros((1, LANES), F32) for _ in lanes)

    def tile_start(trip, slot, n_tiles, reverse):
        idx = trip * TILES_PER_TRIP + slot
        return pl.multiple_of((n_tiles - 1 - idx if reverse else idx) * tt, tt)

    def ctx_pass(reverse):
        carry = carry0
        for idx in range(n_ctx):
            t0 = (n_ctx - 1 - idx if reverse else idx) * tt
            carry = tuple(block(conv(xcf_ref, k, t0), k, carry[k], reverse, idx % TILES_PER_TRIP)[1]
                          for k in range(nblk))
        return carry

    def lat_f(trip, carry):
        for slot in range(TILES_PER_TRIP):
            t0 = tile_start(trip, slot, n_lat, False)
            rows = pl.ds(t0, tt)
            out = []
            for k, sl in enumerate(lanes):
                u = conv(xf_ref, k, t0)
                u_ref[rows, sl] = u
                h, c_out = block(u, k, carry[k], False, slot)
                hf_ref[rows, sl] = h
                out.append(c_out)
            carry = tuple(out)
        return carry

    lax.fori_loop(0, n_lat // TILES_PER_TRIP, lat_f, ctx_pass(False))

    def lat_b(trip, carry):
        for slot in range(TILES_PER_TRIP):
            rows = pl.ds(tile_start(trip, slot, n_lat, True), tt)
            out = []
            for k, sl in enumerate(lanes):
                h, c_out = block(u_ref[rows, sl], k, carry[k], True, slot)
                gate = ga_ref[0, rows, sl].astype(F32)
                o_ref[0, rows, sl] = ((hf_ref[rows, sl] + h) * gate).astype(o_ref.dtype)
                out.append(c_out)
            carry = tuple(out)
        return carry

    lax.fori_loop(0, n_lat // TILES_PER_TRIP, lat_b, ctx_pass(True))


def _rglru(p_lat, p_ctx, conv_w, conv_b, wg_f, bg_f, wg_b, bg_b, lam, w_cast, d_a, cb, tt):
    bsz, t_len, _ = p_lat.shape
    tc_len = p_ctx.shape[1]
    nblk = cb // LANES
    ncb = d_a // cb
    wrows, wcols = w_cast.shape
    assert wrows % (bsz * ncb) == 0
    wr = wrows // (bsz * ncb)
    wspec = pl.BlockSpec((wr, wcols), lambda b, c: (b * ncb + c, 0))
    kern = functools.partial(_rglru_kernel, tt=tt)
    return pl.pallas_call(
        kern,
        grid=(bsz, ncb),
        in_specs=[pl.BlockSpec((1, t_len, cb), lambda b, c: (b, 0, c)),
                  pl.BlockSpec((1, t_len, cb), lambda b, c: (b, 0, ncb + c)),
                  pl.BlockSpec((1, tc_len, cb), lambda b, c: (b, 0, c)),
                  pl.BlockSpec((conv_w.shape[0], cb), lambda b, c: (0, c)),
                  pl.BlockSpec((1, cb), lambda b, c: (0, c)),
                  pl.BlockSpec((nblk, LANES, 2 * LANES), lambda b, c: (c, 0, 0)),
                  pl.BlockSpec((nblk, 1, 2 * LANES), lambda b, c: (c, 0, 0)),
                  pl.BlockSpec((nblk, LANES, 2 * LANES), lambda b, c: (c, 0, 0)),
                  pl.BlockSpec((nblk, 1, 2 * LANES), lambda b, c: (c, 0, 0)),
                  pl.BlockSpec((2, cb), lambda b, c: (0, c)),
                  wspec],
        out_specs=[pl.BlockSpec((1, t_len, cb), lambda b, c: (b, 0, c)), wspec],
        out_shape=[jax.ShapeDtypeStruct((bsz, t_len, d_a), BF16),
                   jax.ShapeDtypeStruct((wrows, wcols), BF16)],
        scratch_shapes=[pltpu.VMEM((nblk, t_len + 2 * SUBLANES, LANES), F32),
                        pltpu.VMEM((nblk, tc_len + 2 * SUBLANES, LANES), F32),
                        pltpu.VMEM((t_len, cb), F32),
                        pltpu.VMEM((t_len, cb), F32),
                        pltpu.VMEM((nblk * TILES_PER_TRIP, 3, tt + SUBLANES * SEG_PAD, LANES),
                                   F32)],
        compiler_params=pltpu.CompilerParams(
            dimension_semantics=("parallel", "parallel"),
            vmem_limit_bytes=VMEM_LIMIT),
        name="rglru",
    )(p_lat, p_lat, p_ctx, conv_w, conv_b.reshape(1, d_a), wg_f, bg_f, wg_b, bg_b, lam, w_cast)


_NT = (((1,), (1,)), ((), ()))
_TN = (((0,), (0,)), ((), ()))


class _HgrnDir:
    def __init__(self, d, q_ref, f_ref, v_ref, lb, st_ref, o_ref, scratch, reverse, n_heads, hd,
                 sub):
        self.d, self.q_ref, self.f_ref, self.v_ref, self.lb = d, q_ref, f_ref, v_ref, lb
        self.sub = sub
        self.st_ref, self.o_ref, self.reverse, self.n_heads, self.hd = st_ref, o_ref, reverse, n_heads, hd
        (self.hl_ref, self.kk_ref, self.cum_ref, self.qd_ref, self.kd_ref, self.qin_ref,
         self.kout_ref, self.dec_ref, self.dect_ref, self.sc_ref) = scratch
        self.c_len = q_ref.shape[2]
        r_i = lax.broadcasted_iota(jnp.int32, (self.c_len, self.c_len), 0)
        c_i = lax.broadcasted_iota(jnp.int32, (self.c_len, self.c_len), 1)
        self.tri = (r_i <= c_i) if reverse else (r_i >= c_i)

    def _slabs(self, width):
        total = self.n_heads * self.hd
        return [slice(s, s + width) for s in range(0, total, width)]


    def _gate_slab(self, sl):
        d, c_len = self.d, self.c_len
        lb = self.lb[:, sl]
        f = lb + (1.0 - lb) * _sigmoid(self.f_ref[0, self.sub, :, sl].astype(F32))
        logf = jnp.log2(f)
        self.kk_ref[d, :, sl] = (1.0 - f).astype(BF16)
        hi = logf.astype(BF16)
        self.hl_ref[d, 0:c_len, sl] = hi
        self.hl_ref[d, c_len:2 * c_len, sl] = (logf - hi.astype(F32)).astype(BF16)

    def _cumulate(self):
        tri_b = self.tri.astype(BF16)
        tri2 = jnp.concatenate([tri_b, tri_b], axis=1)
        self.cum_ref[self.d] = jnp.dot(tri2, self.hl_ref[self.d], preferred_element_type=F32)

    def _decay_slab(self, sl):
        d, c_len = self.d, self.c_len
        half = c_len // 2
        row_last = 0 if self.reverse else c_len - 1
        row_ref = half if self.reverse else half - 1
        cum = self.cum_ref[d, :, sl]
        last = cum[row_last:row_last + 1, :]
        ref = cum[row_ref:row_ref + 1, :]
        kk = self.kk_ref[d, :, sl]
        qs = self.q_ref[0, self.sub, :, sl]
        self.qd_ref[d, :, sl] = qs * jnp.exp2(jnp.minimum(cum - ref, EXP2_CLAMP)).astype(BF16)
        self.kd_ref[d, :, sl] = kk * jnp.exp2(jnp.minimum(ref - cum, EXP2_CLAMP)).astype(BF16)
        self.qin_ref[d, :, sl] = qs * jnp.exp2(cum).astype(BF16)
        self.kout_ref[d, :, sl] = kk * jnp.exp2(last - cum).astype(BF16)
        dec = jnp.exp2(last)
        for i in range((sl.stop - sl.start) // self.hd):
            h = sl.start // self.hd + i
            self.dec_ref[d, h:h + 1, :] = dec[:, i * self.hd:(i + 1) * self.hd]

    def _decay_columns(self):
        self.dect_ref[self.d] = jnp.transpose(self.dec_ref[self.d])

    def _head(self, h):
        return slice(h * self.hd, (h + 1) * self.hd)

    def _scores(self, h):
        d, sl = self.d, self._head(h)
        s = lax.dot_general(self.qd_ref[d, :, sl], self.kd_ref[d, :, sl], _NT,
                            preferred_element_type=F32)
        self.sc_ref[d, h] = jnp.where(self.tri, s, 0.0).astype(BF16)

    def _output(self, h):
        d, sl = self.d, self._head(h)
        o = (jnp.dot(self.sc_ref[d, h], self.v_ref[0, self.sub, :, sl], preferred_element_type=F32)
             + jnp.dot(self.qin_ref[d, :, sl], self.st_ref[h].astype(BF16),
                       preferred_element_type=F32))
        self.o_ref[0, self.sub, :, sl] = o.astype(self.o_ref.dtype)

    def _update(self, h):
        d, sl = self.d, self._head(h)
        decay = jnp.broadcast_to(self.dect_ref[d, :, h:h + 1], (self.hd, self.hd))
        self.st_ref[h] = (self.st_ref[h] * decay
                          + lax.dot_general(self.kout_ref[d, :, sl], self.v_ref[0, self.sub, :, sl], _TN,
                                            preferred_element_type=F32))

    def stages(self):
        heads = range(self.n_heads)
        slabs = self._slabs(2 * LANES)
        out = [[functools.partial(self._gate_slab, sl) for sl in slabs], [self._cumulate],
               [functools.partial(self._decay_slab, sl) for sl in slabs], [self._decay_columns]]
        if self.o_ref is not None:
            out += [[functools.partial(self._scores, h) for h in heads],
                    [functools.partial(self._output, h) for h in heads]]
        return out + [[functools.partial(self._update, h) for h in heads]]


def _lower_bounds(logits_ref, layer):
    out = []
    for d in range(2):
        rows = [logits_ref[d, l:l + 1, :] for l in range(logits_ref.shape[1])]
        m = functools.reduce(jnp.maximum, rows)
        e = [jnp.exp(r - m) for r in rows]
        out.append(sum(e[:layer + 1]) / sum(e))
    return out


def _hgrn_kernel(*refs, n_heads, hd, has_init, emit_o, emit_state):
    qf_ref, ff_ref, vf_ref, qb_ref, fb_ref, vb_ref, lg_ref = refs[:7]
    pos = 7
    if has_init:
        s0f_ref, s0b_ref = refs[pos:pos + 2]
        pos += 2
    if emit_o:
        of_ref, ob_ref = refs[pos:pos + 2]
        pos += 2
    if emit_state:
        sof_ref, sob_ref = refs[pos:pos + 2]
        pos += 2
    sf_ref, sb_ref = refs[pos:pos + 2]
    scratch = refs[pos + 2:]

    j = pl.program_id(1)
    dec_ref = scratch[7]
    assert n_heads <= dec_ref.shape[1]

    @pl.when(j == 0)
    def _():
        if has_init:
            sf_ref[...] = s0f_ref[0]
            sb_ref[...] = s0b_ref[0]
        else:
            sf_ref[...] = jnp.zeros_like(sf_ref)
            sb_ref[...] = jnp.zeros_like(sb_ref)
        dec_ref[...] = jnp.zeros_like(dec_ref)

    lb_f, lb_b = _lower_bounds(lg_ref, 0)
    n_sub = qf_ref.shape[1]
    for k in range(n_sub):
        dirs = [_HgrnDir(0, qf_ref, ff_ref, vf_ref, lb_f, sf_ref, of_ref if emit_o else None,
                         scratch, False, n_heads, hd, k),
                _HgrnDir(1, qb_ref, fb_ref, vb_ref, lb_b, sb_ref, ob_ref if emit_o else None,
                         scratch, True, n_heads, hd, n_sub - 1 - k)]
        for stage_f, stage_b in zip(dirs[0].stages(), dirs[1].stages()):
            for unit in stage_f + stage_b:
                unit()

    if emit_state:
        @pl.when(j == pl.num_programs(1) - 1)
        def _():
            sof_ref[0] = sf_ref[...]
            sob_ref[0] = sb_ref[...]


def _hgrn(p_view, logits, d_b, n_heads, first_group, init_states, emit_o):
    bsz, n_chunks = p_view.shape[:2]
    hd = d_b // n_heads
    has_init = init_states is not None
    emit_state = not emit_o

    cps = CHUNKS_PER_STEP if n_chunks % CHUNKS_PER_STEP == 0 else 1
    n_steps = n_chunks // cps

    def chunk(j, reverse):
        return n_steps - 1 - j if reverse else j

    def spec(group, reverse):
        def imap(b, j):
            return (b, chunk(j, reverse), 0, first_group + group)
        return pl.BlockSpec((1, cps, CHUNK, d_b), imap)

    in_specs = [spec(0, False), spec(1, False), spec(3, False),
                spec(0, True), spec(2, True), spec(3, True),
                pl.BlockSpec(logits.shape, lambda b, j: (0, 0, 0))]
    args = [p_view] * 6 + [logits]
    state_spec = pl.BlockSpec((1, n_heads, hd, hd), lambda b, j: (b, 0, 0, 0))
    state_shape = jax.ShapeDtypeStruct((bsz, n_heads, hd, hd), F32)
    if has_init:
        in_specs += [state_spec, state_spec]
        args += list(init_states)
    out_specs, out_shape = [], []
    if emit_o:
        o_shape = jax.ShapeDtypeStruct((bsz, n_chunks, CHUNK, d_b), BF16)
        out_specs += [pl.BlockSpec((1, cps, CHUNK, d_b), lambda b, j: (b, chunk(j, False), 0, 0)),
                      pl.BlockSpec((1, cps, CHUNK, d_b), lambda b, j: (b, chunk(j, True), 0, 0))]
        out_shape += [o_shape, o_shape]
    if emit_state:
        out_specs += [state_spec, state_spec]
        out_shape += [state_shape, state_shape]
    kern = functools.partial(_hgrn_kernel, n_heads=n_heads, hd=hd, has_init=has_init,
                             emit_o=emit_o, emit_state=emit_state)
    return pl.pallas_call(
        kern,
        grid=(bsz, n_steps),
        in_specs=in_specs,
        out_specs=out_specs,
        out_shape=out_shape,
        scratch_shapes=[pltpu.VMEM((n_heads, hd, hd), F32),
                        pltpu.VMEM((n_heads, hd, hd), F32),
                        pltpu.VMEM((2, 2 * CHUNK, d_b), BF16),
                        pltpu.VMEM((2, CHUNK, d_b), BF16),
                        pltpu.VMEM((2, CHUNK, d_b), F32),
                        pltpu.VMEM((2, CHUNK, d_b), BF16),
                        pltpu.VMEM((2, CHUNK, d_b), BF16),
                        pltpu.VMEM((2, CHUNK, d_b), BF16),
                        pltpu.VMEM((2, CHUNK, d_b), BF16),
                        pltpu.VMEM((2, hd, hd), F32),
                        pltpu.VMEM((2, hd, hd), F32),
                        pltpu.VMEM((2, n_heads, CHUNK, CHUNK), BF16)],
        compiler_params=pltpu.CompilerParams(
            dimension_semantics=("parallel", "arbitrary"),
            vmem_limit_bytes=VMEM_LIMIT),
        name="hgrn2_latent" if emit_o else "hgrn2_context",
    )(*args)


def _outproj_kernel(ya_ref, of_ref, ob_ref, gb_ref, x_ref, gate_ref, hnw_ref, fnw_ref, w_ref,
                    o_ref, y_ref, *, n_heads, hd):
    _, rb, wb, d_a = ya_ref.shape
    rows = rb * wb
    d = x_ref.shape[3]
    y_ref[:, 0:d_a] = ya_ref[0].reshape(rows, d_a)
    hnw = hnw_ref[...]
    for h in range(n_heads):
        sl = slice(h * hd, (h + 1) * hd)
        o = of_ref[0, :, :, sl].astype(F32) + ob_ref[0, :, :, sl].astype(F32)
        ms = jnp.mean(o * o, axis=-1, keepdims=True)
        on = jnp.transpose(o * lax.rsqrt(ms + EPS) * hnw, (1, 0, 2))
        yb = on * gb_ref[0, :, :, sl].astype(F32)
        y_ref[:, d_a + h * hd:d_a + (h + 1) * hd] = yb.reshape(rows, hd).astype(BF16)
    acc = jnp.dot(y_ref[...], w_ref[...], preferred_element_type=F32)
    z = x_ref[0].reshape(rows, d) + gate_ref[0] * acc
    ms = jnp.mean(z * z, axis=-1, keepdims=True)
    o_ref[0] = (z * lax.rsqrt(ms + EPS) * fnw_ref[...]).reshape(rb, wb, d)


def _out_projection(ya4, of4, ob4, pa4, gb_block, x4, mod3, hnw, fnw, w_bf16, n_heads, rb, wb):
    bsz, n_r, n_w, d = x4.shape
    d_a = ya4.shape[3]
    d_b = of4.shape[3]
    hd = d_b // n_heads
    kern = functools.partial(_outproj_kernel, n_heads=n_heads, hd=hd)

    def raster(c, col=0):
        return pl.BlockSpec((1, rb, wb, c), lambda b, r, w: (b, r, w, col))

    def colmajor(c):
        return pl.BlockSpec((1, wb, rb, c), lambda b, r, w: (b, w, r, 0))

    return pl.pallas_call(
        kern,
        grid=(bsz, n_r // rb, n_w // wb),
        in_specs=[raster(d_a), colmajor(d_b), colmajor(d_b), raster(d_b, gb_block), raster(d),
                  pl.BlockSpec((1, 1, d), lambda b, r, w: (b, 0, 2)),
                  pl.BlockSpec((1, hd), lambda b, r, w: (0, 0)),
                  pl.BlockSpec((1, d), lambda b, r, w: (0, 0)),
                  pl.BlockSpec((d_a + d_b, d), lambda b, r, w: (0, 0),
                               pipeline_mode=pl.Buffered(1))],
        out_specs=raster(d),
        out_shape=jax.ShapeDtypeStruct((bsz, n_r, n_w, d), F32),
        scratch_shapes=[pltpu.VMEM((rb * wb, d_a + d_b), BF16)],
        compiler_params=pltpu.CompilerParams(
            dimension_semantics=("parallel", "parallel", "parallel"),
            vmem_limit_bytes=VMEM_LIMIT),
        name="out_projection",
    )(ya4, of4, ob4, pa4, x4, mod3, hnw.reshape(1, hd), fnw.reshape(1, d), w_bf16)


def kernel(x, c, ctx, c_ctx, ada_w, ada_b, norm_w, w_in, conv_w, conv_b, lru_wr, lru_br, lru_wi,
           lru_bi, lru_lambda, hgrn_lb_logits, hgrn_norm_w, w_out, final_norm_w):
    bsz, t_len, d = x.shape
    tc_len = ctx.shape[1]
    assert ada_w.shape[0] == 1, "single-layer stack only"
    d_a = conv_w.shape[2]
    d_b = hgrn_lb_logits.shape[2]
    hd = hgrn_norm_w.shape[1]
    n_heads = d_b // hd
    n_blocks_a = lru_wr.shape[2]
    n_cols = w_in.shape[2]
    assert t_len == GRID_W * CHUNK and tc_len % CHUNK == 0
    assert d_a // n_blocks_a == LANES and hd == LANES
    assert (2 * d_a) % d_b == 0 and n_cols == 2 * d_a + 5 * d_b
    first_b_group = (2 * d_a) // d_b
    n_rows_grid = t_len // GRID_W

    n_rows = -(-(bsz + 1) // SUBLANES) * SUBLANES
    cc = jnp.zeros((n_rows, d), F32).at[:bsz].set(c).at[bsz].set(c_ctx)
    mod3 = _modulation(cc, ada_w[0], ada_b[0]).reshape(n_rows, 1, 3 * d)

    w_in_b = w_in[0]

    tm = 1024 if t_len % 1024 == 0 else t_len
    tn = 1024 if d_a % 1024 == 0 and d_b % 1024 == 0 else min(d_a, d_b)
    a_tiles = (2 * d_a) // tn
    b_tiles = (4 * d_b) // tn
    q_tiles = d_b // tn
    wb = 16
    p_b, h_lat = _in_projection_colmajor(x.reshape(bsz, n_rows_grid, GRID_W, d), mod3, norm_w[0],
                                         w_in_b, lambda j: j + a_tiles, 4 * d_b, wb, tn,
                                         (0, q_tiles))
    p_a = _projection_from_h(h_lat.reshape(bsz * t_len, d), w_in_b,
                             lambda j: jnp.where(j < a_tiles, j, j + b_tiles),
                             2 * d_a + d_b, tm, tn, (d_a // tn, (2 * d_a + d_b) // tn))
    xa_tiles = d_a // tn
    ctx_cols = d_a + 4 * d_b
    p_ctx = _in_projection(ctx.reshape(bsz * tc_len, d), mod3, norm_w[0], w_in_b,
                           lambda i: bsz, lambda j: jnp.where(j < xa_tiles, j, j + xa_tiles),
                           ctx_cols, bsz * tc_len, tn, (xa_tiles, xa_tiles + q_tiles))
    p_lat = p_a.reshape(bsz, t_len, 2 * d_a + d_b)
    p_ctx = p_ctx.reshape(bsz, tc_len, ctx_cols)

    def gate_w(dirn):
        return jnp.concatenate([lru_wr[0, dirn], lru_wi[0, dirn]], axis=-1).astype(BF16)

    def gate_b(dirn):
        return jnp.concatenate([lru_br[0, dirn].reshape(n_blocks_a, 1, LANES),
                                lru_bi[0, dirn].reshape(n_blocks_a, 1, LANES)], axis=-1)

    cb = 256 if d_a % 256 == 0 else LANES
    ya, w_out_b = _rglru(p_lat, p_ctx, conv_w[0], conv_b[0], gate_w(0), gate_b(0), gate_w(1),
                         gate_b(1), lru_lambda[0], w_out[0], d_a, cb, tt=128)

    assert d_a % d_b == 0
    states = _hgrn(p_ctx.reshape(bsz, tc_len // CHUNK, CHUNK, ctx_cols), hgrn_lb_logits, d_b,
                   n_heads, d_a // d_b, None, False)
    o_f, o_b = _hgrn(p_b, hgrn_lb_logits, d_b, n_heads, 0, states, True)

    grid4 = lambda z: z.reshape(bsz, n_rows_grid, GRID_W, z.shape[-1])
    out = _out_projection(grid4(ya), o_f, o_b, grid4(p_lat), first_b_group, grid4(x), mod3,
                          hgrn_norm_w[0], final_norm_w, w_out_b, n_heads, 16, 16)
    return out.reshape(bsz, t_len, d)
```

```python
import functools

import jax
import jax.numpy as jnp
from jax import lax
from jax.experimental import pallas as pl
from jax.experimental.pallas import tpu as pltpu

GRID_W = 64
CHUNK = 64
LRU_C = 8.0
EPS = 1e-6
CONV_PAD_L = 2
LANES = 128
SUBLANES = 8
EXP2_CLAMP = 115.0
NEG_LOG2E = -1.4426950408889634
VMEM_LIMIT = 56 * 1024 * 1024
PROLOGUE_ROWS = 16
SEG_PAD = 4
CHUNKS_PER_STEP = 2
TILES_PER_TRIP = 4

F32 = jnp.float32
BF16 = jnp.bfloat16


def _sigmoid(z):
    return 1.0 / (1.0 + jnp.exp2(z * NEG_LOG2E))


def _silu(z):
    return z * _sigmoid(z)


def _softplus(z):
    return jnp.maximum(z, 0.0) + jnp.log1p(jnp.exp(-jnp.abs(z)))


def _mod_kernel(c_ref, w_ref, b_ref, o_ref):
    s = _silu(c_ref[...])
    o_ref[...] = jnp.dot(s.astype(BF16), w_ref[...].astype(BF16),
                         preferred_element_type=F32) + b_ref[...]


def _modulation(cc, w, b):
    rows, d = cc.shape
    n = w.shape[1]
    tn = 512 if n % 512 == 0 else n
    return pl.pallas_call(
        _mod_kernel,
        grid=(n // tn,),
        in_specs=[pl.BlockSpec((rows, d), lambda j: (0, 0)),
                  pl.BlockSpec((d, tn), lambda j: (0, j)),
                  pl.BlockSpec((1, tn), lambda j: (0, j))],
        out_specs=pl.BlockSpec((rows, tn), lambda j: (0, j)),
        out_shape=jax.ShapeDtypeStruct((rows, n), F32),
        compiler_params=pltpu.CompilerParams(
            dimension_semantics=("arbitrary",), vmem_limit_bytes=VMEM_LIMIT),
        name="adaln_modulation",
    )(cc, w, b.reshape(1, n))


def _store_projection(o_ref, res, j, silu_tiles):
    lo, hi = silu_tiles
    z = res.astype(o_ref.dtype)
    if hi <= lo:
        o_ref[...] = z
        return
    in_range = jnp.logical_and(j >= lo, j < hi)
    o_ref[...] = jnp.where(in_range, z * (0.5 * jnp.tanh(0.5 * z) + 0.5), z)


def _inproj_kernel(x_ref, shift_ref, scale_ref, nw_ref, w_ref, o_ref, h_ref, *, silu_tiles):
    @pl.when(pl.program_id(1) == 0)
    def _():
        rows = min(PROLOGUE_ROWS, x_ref.shape[0])
        gain = nw_ref[...] * (1.0 + scale_ref[0])

        def slab(s, carry):
            sl = pl.ds(pl.multiple_of(s * rows, rows), rows)
            x = x_ref[sl, :]
            rs = lax.rsqrt(jnp.mean(x * x, axis=-1, keepdims=True) + EPS)
            h_ref[sl, :] = (x_ref[sl, :] * rs * gain + shift_ref[0]).astype(BF16)
            return carry

        lax.fori_loop(0, x_ref.shape[0] // rows, slab, 0, unroll=4)

    res = jnp.dot(h_ref[...], w_ref[...].astype(BF16), preferred_element_type=F32)
    _store_projection(o_ref, res, pl.program_id(1), silu_tiles)


def _in_projection(x2d, mod3, norm_w, w_bf16, row_of_tile, wcol_of_tile, n_out, tm, tn, silu_tiles):
    m, d = x2d.shape
    return pl.pallas_call(
        functools.partial(_inproj_kernel, silu_tiles=silu_tiles),
        grid=(m // tm, n_out // tn),
        in_specs=[pl.BlockSpec((tm, d), lambda i, j: (i, 0)),
                  pl.BlockSpec((1, 1, d), lambda i, j: (row_of_tile(i), 0, 0)),
                  pl.BlockSpec((1, 1, d), lambda i, j: (row_of_tile(i), 0, 1)),
                  pl.BlockSpec((1, d), lambda i, j: (0, 0)),
                  pl.BlockSpec((d, tn), lambda i, j: (0, wcol_of_tile(j)))],
        out_specs=pl.BlockSpec((tm, tn), lambda i, j: (i, j)),
        out_shape=jax.ShapeDtypeStruct((m, n_out), BF16),
        scratch_shapes=[pltpu.VMEM((tm, d), BF16)],
        compiler_params=pltpu.CompilerParams(
            dimension_semantics=("parallel", "arbitrary"),
            vmem_limit_bytes=VMEM_LIMIT),
        name="in_projection",
    )(x2d, mod3, mod3, norm_w.reshape(1, d), w_bf16)


def _matmul_kernel(h_ref, w_ref, o_ref, *, silu_tiles):
    res = jnp.dot(h_ref[...], w_ref[...].astype(BF16), preferred_element_type=F32)
    _store_projection(o_ref, res, pl.program_id(1), silu_tiles)


def _projection_from_h(h2d, w_f32, wcol_of_tile, n_out, tm, tn, silu_tiles):
    m, d = h2d.shape
    return pl.pallas_call(
        functools.partial(_matmul_kernel, silu_tiles=silu_tiles),
        grid=(m // tm, n_out // tn),
        in_specs=[pl.BlockSpec((tm, d), lambda i, j: (i, 0)),
                  pl.BlockSpec((d, tn), lambda i, j: (0, wcol_of_tile(j)))],
        out_specs=pl.BlockSpec((tm, tn), lambda i, j: (i, j)),
        out_shape=jax.ShapeDtypeStruct((m, n_out), BF16),
        compiler_params=pltpu.CompilerParams(
            dimension_semantics=("parallel", "arbitrary"),
            vmem_limit_bytes=VMEM_LIMIT),
        name="projection_from_h",
    )(h2d, w_f32)


def _inproj_colmajor_kernel(x_ref, shift_ref, scale_ref, nw_ref, w_ref, o_ref, hr_ref, h_ref, *,
                            slab, silu_tiles):
    _, n_r, n_w, d = x_ref.shape

    rr = 2 * SUBLANES
    lanes = [slice(s * slab, (s + 1) * slab) for s in range(d // slab)]

    @pl.when(pl.program_id(2) == 0)
    def _():
        def row_group(g, carry):
            rows = pl.ds(pl.multiple_of(g * rr, rr), rr)
            sq = jnp.zeros((rr, n_w, slab), F32)
            for sl in lanes:
                xs = x_ref[0, rows, :, sl]
                sq = sq + xs * xs
            rs = lax.rsqrt(jnp.sum(sq, axis=-1, keepdims=True) * (1.0 / d) + EPS)
            for sl in lanes:
                y = x_ref[0, rows, :, sl] * rs * nw_ref[:, sl]
                y = y * (1.0 + scale_ref[0][:, sl]) + shift_ref[0][:, sl]
                hr_ref[0, rows, :, sl] = y.astype(BF16)
                h_ref[:, rows, sl] = jnp.transpose(y, (1, 0, 2)).astype(BF16)
            return carry

        lax.fori_loop(0, n_r // rr, row_group, 0)

    res = jnp.dot(h_ref[...].reshape(n_w * n_r, d), w_ref[...].astype(BF16),
                  preferred_element_type=F32)
    _store_projection(o_ref, res.reshape(1, n_w, n_r, res.shape[1]), pl.program_id(2), silu_tiles)


def _in_projection_colmajor(x4d, mod3, norm_w, w_bf16, wcol_of_tile, n_out, wb, tn, silu_tiles):
    bsz, n_r, n_w, d = x4d.shape
    kern = functools.partial(_inproj_colmajor_kernel, slab=2 * LANES, silu_tiles=silu_tiles)
    return pl.pallas_call(
        kern,
        grid=(bsz, n_w // wb, n_out // tn),
        in_specs=[pl.BlockSpec((1, n_r, wb, d), lambda b, w, j: (b, 0, w, 0)),
                  pl.BlockSpec((1, 1, d), lambda b, w, j: (b, 0, 0)),
                  pl.BlockSpec((1, 1, d), lambda b, w, j: (b, 0, 1)),
                  pl.BlockSpec((1, d), lambda b, w, j: (0, 0)),
                  pl.BlockSpec((d, tn), lambda b, w, j: (0, wcol_of_tile(j)))],
        out_specs=[pl.BlockSpec((1, wb, n_r, tn), lambda b, w, j: (b, w, 0, j)),
                   pl.BlockSpec((1, n_r, wb, d), lambda b, w, j: (b, 0, w, 0))],
        out_shape=[jax.ShapeDtypeStruct((bsz, n_w, n_r, n_out), BF16),
                   jax.ShapeDtypeStruct((bsz, n_r, n_w, d), BF16)],
        scratch_shapes=[pltpu.VMEM((wb, n_r, d), BF16)],
        compiler_params=pltpu.CompilerParams(
            dimension_semantics=("parallel", "parallel", "arbitrary"),
            vmem_limit_bytes=VMEM_LIMIT),
        name="in_projection_colmajor",
    )(x4d, mod3, mod3, norm_w.reshape(1, d), w_bf16)


def _group_scan(a, b, reverse):
    row = lax.broadcasted_iota(jnp.int32, a.shape, 1)
    for k in (1, 2, 4):
        if reverse:
            a_sh = pltpu.roll(a, SUBLANES - k, axis=1)
            b_sh = pltpu.roll(b, SUBLANES - k, axis=1)
            m = row < SUBLANES - k
        else:
            a_sh = pltpu.roll(a, k, axis=1)
            b_sh = pltpu.roll(b, k, axis=1)
            m = row >= k
        b = jnp.where(m, a * b_sh + b, b)
        a = jnp.where(m, a * a_sh, a)
    return a, b


def _sqrt_unit(x):
    return jnp.where(x > 0.0, x * lax.rsqrt(x), 0.0)


def _segment_scan(a, b, carry, reverse, scr_ref):
    tt = a.shape[0]
    seg = tt // SUBLANES
    pitch = seg + SEG_PAD
    for s in range(SUBLANES):
        rows = pl.ds(pitch * s, seg, stride=1)
        scr_ref[0, rows, :] = a[seg * s:seg * (s + 1), :]
        scr_ref[1, rows, :] = b[seg * s:seg * (s + 1), :]
    hs, ps = [None] * seg, [None] * seg
    h = p = None
    for j in (range(seg - 1, -1, -1) if reverse else range(seg)):
        step = pl.ds(j, SUBLANES, stride=pitch)
        a_j = scr_ref[0, step, :]
        b_j = scr_ref[1, step, :]
        h = b_j if h is None else a_j * h + b_j
        p = a_j if p is None else a_j * p
        hs[j], ps[j] = h, p
    g3, e3 = _group_scan(p[None], h[None], reverse)
    end = e3[0] + g3[0] * carry
    row = lax.broadcasted_iota(jnp.int32, end.shape, 0)
    if reverse:
        enter = jnp.where(row == SUBLANES - 1, carry, pltpu.roll(end, SUBLANES - 1, axis=0))
        carry_out = end[0:1, :]
    else:
        enter = jnp.where(row == 0, carry, pltpu.roll(end, 1, axis=0))
        carry_out = end[SUBLANES - 1:SUBLANES, :]
    for j in range(seg):
        scr_ref[2, pl.ds(j, SUBLANES, stride=pitch), :] = hs[j] + ps[j] * enter
    h_time = [scr_ref[2, pl.ds(pitch * s, seg, stride=1), :] for s in range(SUBLANES)]
    return jnp.concatenate(h_time, axis=0), carry_out


def _lru_block(u, wg, bg, sp, carry, reverse, scr_ref):
    g = jnp.dot(u.astype(BF16), wg, preferred_element_type=F32) + bg
    r = _sigmoid(g[:, :LANES])
    i = _sigmoid(g[:, LANES:])
    a = jnp.exp2(r * sp)
    b = _sqrt_unit(1.0 - a * a) * (i * u)
    return _segment_scan(a, b, carry, reverse, scr_ref)


def _rglru_kernel(xa_ref, ga_ref, xc_ref, cw_ref, cb_ref, wgf_ref, bgf_ref, wgb_ref, bgb_ref,
                  lam_ref, wcast_ref, o_ref, wcast_out_ref, xf_ref, xcf_ref, u_ref, hf_ref, scr_ref,
                  *, tt):
    wcast_out_ref[...] = wcast_ref[...].astype(BF16)
    t_len = xa_ref.shape[1]
    tc_len = xc_ref.shape[1]
    nblk = xa_ref.shape[2] // LANES
    lanes = [slice(k * LANES, (k + 1) * LANES) for k in range(nblk)]
    zeros = jnp.zeros((SUBLANES, LANES), F32)
    for k, sl in enumerate(lanes):
        xf_ref[k, 0:SUBLANES, :] = zeros
        xf_ref[k, SUBLANES:SUBLANES + t_len, :] = xa_ref[0, :, sl].astype(F32)
        xf_ref[k, SUBLANES + t_len:, :] = zeros
        xcf_ref[k, 0:SUBLANES, :] = zeros
        xcf_ref[k, SUBLANES:SUBLANES + tc_len, :] = xc_ref[0, :, sl].astype(F32)
        xcf_ref[k, SUBLANES + tc_len:, :] = zeros

    sp_f = _softplus(-lam_ref[0:1, :]) * (LRU_C * NEG_LOG2E)
    sp_b = _softplus(-lam_ref[1:2, :]) * (LRU_C * NEG_LOG2E)

    def conv(src_ref, k, t0):
        u = cb_ref[:, lanes[k]]
        for tap in range(cw_ref.shape[0]):
            off = SUBLANES - CONV_PAD_L + tap
            if off % SUBLANES == 0:
                start = t0 + off
                rows = pl.ds(start if isinstance(start, int) else pl.multiple_of(start, SUBLANES), tt)
            else:
                rows = pl.ds(t0 + off, tt, stride=1)
            u = u + src_ref[k, rows, :] * cw_ref[tap:tap + 1, lanes[k]]
        return u

    def block(u, k, carry, reverse, slot):
        scr = scr_ref.at[k * TILES_PER_TRIP + slot]
        if reverse:
            return _lru_block(u, wgb_ref[k], bgb_ref[k], sp_b[:, lanes[k]], carry, True, scr)
        return _lru_block(u, wgf_ref[k], bgf_ref[k], sp_f[:, lanes[k]], carry, False, scr)

    n_lat = t_len // tt
    n_ctx = tc_len // tt
    assert n_lat % TILES_PER_TRIP == 0
    carry0 = tuple(jnp.zeros((1, LANES), F32) for _ in lanes)

    def tile_start(trip, slot, n_tiles, reverse):
        idx = trip * TILES_PER_TRIP + slot
        return pl.multiple_of((n_tiles - 1 - idx if reverse else idx) * tt, tt)

    def ctx_pass(reverse):
        carry = carry0
        for idx in range(n_ctx):
            t0 = (n_ctx - 1 - idx if reverse else idx) * tt
            carry = tuple(block(conv(xcf_ref, k, t0), k, carry[k], reverse, idx % TILES_PER_TRIP)[1]
                          for k in range(nblk))
        return carry

    def lat_f(trip, carry):
        for slot in range(TILES_PER_TRIP):
            t0 = tile_start(trip, slot, n_lat, False)
            rows = pl.ds(t0, tt)
            out = []
            for k, sl in enumerate(lanes):
                u = conv(xf_ref, k, t0)
                u_ref[rows, sl] = u
                h, c_out = block(u, k, carry[k], False, slot)
                hf_ref[rows, sl] = h
                out.append(c_out)
            carry = tuple(out)
        return carry

    lax.fori_loop(0, n_lat // TILES_PER_TRIP, lat_f, ctx_pass(False))

    def lat_b(trip, carry):
        for slot in range(TILES_PER_TRIP):
            rows = pl.ds(tile_start(trip, slot, n_lat, True), tt)
            out = []
            for k, sl in enumerate(lanes):
                h, c_out = block(u_ref[rows, sl], k, carry[k], True, slot)
                gate = ga_ref[0, rows, sl].astype(F32)
                o_ref[0, rows, sl] = ((hf_ref[rows, sl] + h) * gate).astype(o_ref.dtype)
                out.append(c_out)
            carry = tuple(out)
        return carry

    lax.fori_loop(0, n_lat // TILES_PER_TRIP, lat_b, ctx_pass(True))


def _rglru(p_lat, p_ctx, conv_w, conv_b, wg_f, bg_f, wg_b, bg_b, lam, w_cast, d_a, cb, tt):
    bsz, t_len, _ = p_lat.shape
    tc_len = p_ctx.shape[1]
    nblk = cb // LANES
    ncb = d_a // cb
    wrows, wcols = w_cast.shape
    assert wrows % (bsz * ncb) == 0
    wr = wrows // (bsz * ncb)
    wspec = pl.BlockSpec((wr, wcols), lambda b, c: (b * ncb + c, 0))
    kern = functools.partial(_rglru_kernel, tt=tt)
    return pl.pallas_call(
        kern,
        grid=(bsz, ncb),
        in_specs=[pl.BlockSpec((1, t_len, cb), lambda b, c: (b, 0, c)),
                  pl.BlockSpec((1, t_len, cb), lambda b, c: (b, 0, ncb + c)),
                  pl.BlockSpec((1, tc_len, cb), lambda b, c: (b, 0, c)),
                  pl.BlockSpec((conv_w.shape[0], cb), lambda b, c: (0, c)),
                  pl.BlockSpec((1, cb), lambda b, c: (0, c)),
                  pl.BlockSpec((nblk, LANES, 2 * LANES), lambda b, c: (c, 0, 0)),
                  pl.BlockSpec((nblk, 1, 2 * LANES), lambda b, c: (c, 0, 0)),
                  pl.BlockSpec((nblk, LANES, 2 * LANES), lambda b, c: (c, 0, 0)),
                  pl.BlockSpec((nblk, 1, 2 * LANES), lambda b, c: (c, 0, 0)),
                  pl.BlockSpec((2, cb), lambda b, c: (0, c)),
                  wspec],
        out_specs=[pl.BlockSpec((1, t_len, cb), lambda b, c: (b, 0, c)), wspec],
        out_shape=[jax.ShapeDtypeStruct((bsz, t_len, d_a), BF16),
                   jax.ShapeDtypeStruct((wrows, wcols), BF16)],
        scratch_shapes=[pltpu.VMEM((nblk, t_len + 2 * SUBLANES, LANES), F32),
                        pltpu.VMEM((nblk, tc_len + 2 * SUBLANES, LANES), F32),
                        pltpu.VMEM((t_len, cb), F32),
                        pltpu.VMEM((t_len, cb), F32),
                        pltpu.VMEM((nblk * TILES_PER_TRIP, 3, tt + SUBLANES * SEG_PAD, LANES),
                                   F32)],
        compiler_params=pltpu.CompilerParams(
            dimension_semantics=("parallel", "parallel"),
            vmem_limit_bytes=VMEM_LIMIT),
        name="rglru",
    )(p_lat, p_lat, p_ctx, conv_w, conv_b.reshape(1, d_a), wg_f, bg_f, wg_b, bg_b, lam, w_cast)


_NT = (((1,), (1,)), ((), ()))
_TN = (((0,), (0,)), ((), ()))


class _HgrnDir:
    def __init__(self, d, q_ref, f_ref, v_ref, lb, st_ref, o_ref, scratch, reverse, n_heads, hd,
                 sub):
        self.d, self.q_ref, self.f_ref, self.v_ref, self.lb = d, q_ref, f_ref, v_ref, lb
        self.sub = sub
        self.st_ref, self.o_ref, self.reverse, self.n_heads, self.hd = st_ref, o_ref, reverse, n_heads, hd
        (self.hl_ref, self.kk_ref, self.cum_ref, self.qd_ref, self.kd_ref, self.qin_ref,
         self.kout_ref, self.dec_ref, self.dect_ref, self.sc_ref) = scratch
        self.c_len = q_ref.shape[2]
        r_i = lax.broadcasted_iota(jnp.int32, (self.c_len, self.c_len), 0)
        c_i = lax.broadcasted_iota(jnp.int32, (self.c_len, self.c_len), 1)
        self.tri = (r_i <= c_i) if reverse else (r_i >= c_i)

    def _slabs(self, width):
        total = self.n_heads * self.hd
        return [slice(s, s + width) for s in range(0, total, width)]


    def _gate_slab(self, sl):
        d, c_len = self.d, self.c_len
        lb = self.lb[:, sl]
        f = lb + (1.0 - lb) * _sigmoid(self.f_ref[0, self.sub, :, sl].astype(F32))
        logf = jnp.log2(f)
        self.kk_ref[d, :, sl] = (1.0 - f).astype(BF16)
        hi = logf.astype(BF16)
        self.hl_ref[d, 0:c_len, sl] = hi
        self.hl_ref[d, c_len:2 * c_len, sl] = (logf - hi.astype(F32)).astype(BF16)

    def _cumulate(self):
        tri_b = self.tri.astype(BF16)
        tri2 = jnp.concatenate([tri_b, tri_b], axis=1)
        self.cum_ref[self.d] = jnp.dot(tri2, self.hl_ref[self.d], preferred_element_type=F32)

    def _decay_slab(self, sl):
        d, c_len = self.d, self.c_len
        half = c_len // 2
        row_last = 0 if self.reverse else c_len - 1
        row_ref = half if self.reverse else half - 1
        cum = self.cum_ref[d, :, sl]
        last = cum[row_last:row_last + 1, :]
        ref = cum[row_ref:row_ref + 1, :]
        kk = self.kk_ref[d, :, sl]
        qs = self.q_ref[0, self.sub, :, sl]
        self.qd_ref[d, :, sl] = qs * jnp.exp2(jnp.minimum(cum - ref, EXP2_CLAMP)).astype(BF16)
        self.kd_ref[d, :, sl] = kk * jnp.exp2(jnp.minimum(ref - cum, EXP2_CLAMP)).astype(BF16)
        self.qin_ref[d, :, sl] = qs * jnp.exp2(cum).astype(BF16)
        self.kout_ref[d, :, sl] = kk * jnp.exp2(last - cum).astype(BF16)
        dec = jnp.exp2(last)
        for i in range((sl.stop - sl.start) // self.hd):
            h = sl.start // self.hd + i
            self.dec_ref[d, h:h + 1, :] = dec[:, i * self.hd:(i + 1) * self.hd]

    def _decay_columns(self):
        self.dect_ref[self.d] = jnp.transpose(self.dec_ref[self.d])

    def _head(self, h):
        return slice(h * self.hd, (h + 1) * self.hd)

    def _scores(self, h):
        d, sl = self.d, self._head(h)
        s = lax.dot_general(self.qd_ref[d, :, sl], self.kd_ref[d, :, sl], _NT,
                            preferred_element_type=F32)
        self.sc_ref[d, h] = jnp.where(self.tri, s, 0.0).astype(BF16)

    def _output(self, h):
        d, sl = self.d, self._head(h)
        o = (jnp.dot(self.sc_ref[d, h], self.v_ref[0, self.sub, :, sl], preferred_element_type=F32)
             + jnp.dot(self.qin_ref[d, :, sl], self.st_ref[h].astype(BF16),
                       preferred_element_type=F32))
        self.o_ref[0, self.sub, :, sl] = o.astype(self.o_ref.dtype)

    def _update(self, h):
        d, sl = self.d, self._head(h)
        decay = jnp.broadcast_to(self.dect_ref[d, :, h:h + 1], (self.hd, self.hd))
        self.st_ref[h] = (self.st_ref[h] * decay
                          + lax.dot_general(self.kout_ref[d, :, sl], self.v_ref[0, self.sub, :, sl], _TN,
                                            preferred_element_type=F32))

    def stages(self):
        heads = range(self.n_heads)
        slabs = self._slabs(2 * LANES)
        out = [[functools.partial(self._gate_slab, sl) for sl in slabs], [self._cumulate],
               [functools.partial(self._decay_slab, sl) for sl in slabs], [self._decay_columns]]
        if self.o_ref is not None:
            out += [[functools.partial(self._scores, h) for h in heads],
                    [functools.partial(self._output, h) for h in heads]]
        return out + [[functools.partial(self._update, h) for h in heads]]


def _lower_bounds(logits_ref, layer):
    out = []
    for d in range(2):
        rows = [logits_ref[d, l:l + 1, :] for l in range(logits_ref.shape[1])]
        m = functools.reduce(jnp.maximum, rows)
        e = [jnp.exp(r - m) for r in rows]
        out.append(sum(e[:layer + 1]) / sum(e))
    return out


def _hgrn_kernel(*refs, n_heads, hd, has_init, emit_o, emit_state):
    qf_ref, ff_ref, vf_ref, qb_ref, fb_ref, vb_ref, lg_ref = refs[:7]
    pos = 7
    if has_init:
        s0f_ref, s0b_ref = refs[pos:pos + 2]
        pos += 2
    if emit_o:
        of_ref, ob_ref = refs[pos:pos + 2]
        pos += 2
    if emit_state:
        sof_ref, sob_ref = refs[pos:pos + 2]
        pos += 2
    sf_ref, sb_ref = refs[pos:pos + 2]
    scratch = refs[pos + 2:]

    j = pl.program_id(1)
    dec_ref = scratch[7]
    assert n_heads <= dec_ref.shape[1]

    @pl.when(j == 0)
    def _():
        if has_init:
            sf_ref[...] = s0f_ref[0]
            sb_ref[...] = s0b_ref[0]
        else:
            sf_ref[...] = jnp.zeros_like(sf_ref)
            sb_ref[...] = jnp.zeros_like(sb_ref)
        dec_ref[...] = jnp.zeros_like(dec_ref)

    lb_f, lb_b = _lower_bounds(lg_ref, 0)
    n_sub = qf_ref.shape[1]
    for k in range(n_sub):
        dirs = [_HgrnDir(0, qf_ref, ff_ref, vf_ref, lb_f, sf_ref, of_ref if emit_o else None,
                         scratch, False, n_heads, hd, k),
                _HgrnDir(1, qb_ref, fb_ref, vb_ref, lb_b, sb_ref, ob_ref if emit_o else None,
                         scratch, True, n_heads, hd, n_sub - 1 - k)]
        for stage_f, stage_b in zip(dirs[0].stages(), dirs[1].stages()):
            for unit in stage_f + stage_b:
                unit()

    if emit_state:
        @pl.when(j == pl.num_programs(1) - 1)
        def _():
            sof_ref[0] = sf_ref[...]
            sob_ref[0] = sb_ref[...]


def _hgrn(p_view, logits, d_b, n_heads, first_group, init_states, emit_o):
    bsz, n_chunks = p_view.shape[:2]
    hd = d_b // n_heads
    has_init = init_states is not None
    emit_state = not emit_o

    cps = CHUNKS_PER_STEP if n_chunks % CHUNKS_PER_STEP == 0 else 1
    n_steps = n_chunks // cps

    def chunk(j, reverse):
        return n_steps - 1 - j if reverse else j

    def spec(group, reverse):
        def imap(b, j):
            return (b, chunk(j, reverse), 0, first_group + group)
        return pl.BlockSpec((1, cps, CHUNK, d_b), imap)

    in_specs = [spec(0, False), spec(1, False), spec(3, False),
                spec(0, True), spec(2, True), spec(3, True),
                pl.BlockSpec(logits.shape, lambda b, j: (0, 0, 0))]
    args = [p_view] * 6 + [logits]
    state_spec = pl.BlockSpec((1, n_heads, hd, hd), lambda b, j: (b, 0, 0, 0))
    state_shape = jax.ShapeDtypeStruct((bsz, n_heads, hd, hd), F32)
    if has_init:
        in_specs += [state_spec, state_spec]
        args += list(init_states)
    out_specs, out_shape = [], []
    if emit_o:
        o_shape = jax.ShapeDtypeStruct((bsz, n_chunks, CHUNK, d_b), BF16)
        out_specs += [pl.BlockSpec((1, cps, CHUNK, d_b), lambda b, j: (b, chunk(j, False), 0, 0)),
                      pl.BlockSpec((1, cps, CHUNK, d_b), lambda b, j: (b, chunk(j, True), 0, 0))]
        out_shape += [o_shape, o_shape]
    if emit_state:
        out_specs += [state_spec, state_spec]
        out_shape += [state_shape, state_shape]
    kern = functools.partial(_hgrn_kernel, n_heads=n_heads, hd=hd, has_init=has_init,
                             emit_o=emit_o, emit_state=emit_state)
    return pl.pallas_call(
        kern,
        grid=(bsz, n_steps),
        in_specs=in_specs,
        out_specs=out_specs,
        out_shape=out_shape,
        scratch_shapes=[pltpu.VMEM((n_heads, hd, hd), F32),
                        pltpu.VMEM((n_heads, hd, hd), F32),
                        pltpu.VMEM((2, 2 * CHUNK, d_b), BF16),
                        pltpu.VMEM((2, CHUNK, d_b), BF16),
                        pltpu.VMEM((2, CHUNK, d_b), F32),
                        pltpu.VMEM((2, CHUNK, d_b), BF16),
                        pltpu.VMEM((2, CHUNK, d_b), BF16),
                        pltpu.VMEM((2, CHUNK, d_b), BF16),
                        pltpu.VMEM((2, CHUNK, d_b), BF16),
                        pltpu.VMEM((2, hd, hd), F32),
                        pltpu.VMEM((2, hd, hd), F32),
                        pltpu.VMEM((2, n_heads, CHUNK, CHUNK), BF16)],
        compiler_params=pltpu.CompilerParams(
            dimension_semantics=("parallel", "arbitrary"),
            vmem_limit_bytes=VMEM_LIMIT),
        name="hgrn2_latent" if emit_o else "hgrn2_context",
    )(*args)


def _outproj_kernel(ya_ref, of_ref, ob_ref, gb_ref, x_ref, gate_ref, hnw_ref, fnw_ref, w_ref,
                    o_ref, y_ref, *, n_heads, hd):
    _, rb, wb, d_a = ya_ref.shape
    rows = rb * wb
    d = x_ref.shape[3]
    y_ref[:, 0:d_a] = ya_ref[0].reshape(rows, d_a)
    hnw = hnw_ref[...]
    for h in range(n_heads):
        sl = slice(h * hd, (h + 1) * hd)
        o = of_ref[0, :, :, sl].astype(F32) + ob_ref[0, :, :, sl].astype(F32)
        ms = jnp.mean(o * o, axis=-1, keepdims=True)
        on = jnp.transpose(o * lax.rsqrt(ms + EPS) * hnw, (1, 0, 2))
        yb = on * gb_ref[0, :, :, sl].astype(F32)
        y_ref[:, d_a + h * hd:d_a + (h + 1) * hd] = yb.reshape(rows, hd).astype(BF16)
    acc = jnp.dot(y_ref[...], w_ref[...], preferred_element_type=F32)
    z = x_ref[0].reshape(rows, d) + gate_ref[0] * acc
    ms = jnp.mean(z * z, axis=-1, keepdims=True)
    o_ref[0] = (z * lax.rsqrt(ms + EPS) * fnw_ref[...]).reshape(rb, wb, d)


def _out_projection(ya4, of4, ob4, pa4, gb_block, x4, mod3, hnw, fnw, w_bf16, n_heads, rb, wb):
    bsz, n_r, n_w, d = x4.shape
    d_a = ya4.shape[3]
    d_b = of4.shape[3]
    hd = d_b // n_heads
    kern = functools.partial(_outproj_kernel, n_heads=n_heads, hd=hd)

    def raster(c, col=0):
        return pl.BlockSpec((1, rb, wb, c), lambda b, r, w: (b, r, w, col))

    def colmajor(c):
        return pl.BlockSpec((1, wb, rb, c), lambda b, r, w: (b, w, r, 0))

    return pl.pallas_call(
        kern,
        grid=(bsz, n_r // rb, n_w // wb),
        in_specs=[raster(d_a), colmajor(d_b), colmajor(d_b), raster(d_b, gb_block), raster(d),
                  pl.BlockSpec((1, 1, d), lambda b, r, w: (b, 0, 2)),
                  pl.BlockSpec((1, hd), lambda b, r, w: (0, 0)),
                  pl.BlockSpec((1, d), lambda b, r, w: (0, 0)),
                  pl.BlockSpec((d_a + d_b, d), lambda b, r, w: (0, 0),
                               pipeline_mode=pl.Buffered(1))],
        out_specs=raster(d),
        out_shape=jax.ShapeDtypeStruct((bsz, n_r, n_w, d), F32),
        scratch_shapes=[pltpu.VMEM((rb * wb, d_a + d_b), BF16)],
        compiler_params=pltpu.CompilerParams(
            dimension_semantics=("parallel", "parallel", "parallel"),
            vmem_limit_bytes=VMEM_LIMIT),
        name="out_projection",
    )(ya4, of4, ob4, pa4, x4, mod3, hnw.reshape(1, hd), fnw.reshape(1, d), w_bf16)


def kernel(x, c, ctx, c_ctx, ada_w, ada_b, norm_w, w_in, conv_w, conv_b, lru_wr, lru_br, lru_wi,
           lru_bi, lru_lambda, hgrn_lb_logits, hgrn_norm_w, w_out, final_norm_w):
    bsz, t_len, d = x.shape
    tc_len = ctx.shape[1]
    assert ada_w.shape[0] == 1, "single-layer stack only"
    d_a = conv_w.shape[2]
    d_b = hgrn_lb_logits.shape[2]
    hd = hgrn_norm_w.shape[1]
    n_heads = d_b // hd
    n_blocks_a = lru_wr.shape[2]
    n_cols = w_in.shape[2]
    assert t_len == GRID_W * CHUNK and tc_len % CHUNK == 0
    assert d_a // n_blocks_a == LANES and hd == LANES
    assert (2 * d_a) % d_b == 0 and n_cols == 2 * d_a + 5 * d_b
    first_b_group = (2 * d_a) // d_b
    n_rows_grid = t_len // GRID_W

    n_rows = -(-(bsz + 1) // SUBLANES) * SUBLANES
    cc = jnp.zeros((n_rows, d), F32).at[:bsz].set(c).at[bsz].set(c_ctx)
    mod3 = _modulation(cc, ada_w[0], ada_b[0]).reshape(n_rows, 1, 3 * d)

    w_in_b = w_in[0]

    tm = 1024 if t_len % 1024 == 0 else t_len
    tn = 1024 if d_a % 1024 == 0 and d_b % 1024 == 0 else min(d_a, d_b)
    a_tiles = (2 * d_a) // tn
    b_tiles = (4 * d_b) // tn
    q_tiles = d_b // tn
    wb = 16
    p_b, h_lat = _in_projection_colmajor(x.reshape(bsz, n_rows_grid, GRID_W, d), mod3, norm_w[0],
                                         w_in_b, lambda j: j + a_tiles, 4 * d_b, wb, tn,
                                         (0, q_tiles))
    tm_h = 2 * tm if (bsz * t_len) % (2 * tm) == 0 else tm
    p_a = _projection_from_h(h_lat.reshape(bsz * t_len, d), w_in_b,
                             lambda j: jnp.where(j < a_tiles, j, j + b_tiles),
                             2 * d_a + d_b, tm_h, tn, (d_a // tn, (2 * d_a + d_b) // tn))
    xa_tiles = d_a // tn
    ctx_cols = d_a + 4 * d_b
    p_ctx = _in_projection(ctx.reshape(bsz * tc_len, d), mod3, norm_w[0], w_in_b,
                           lambda i: bsz, lambda j: jnp.where(j < xa_tiles, j, j + xa_tiles),
                           ctx_cols, bsz * tc_len, tn, (xa_tiles, xa_tiles + q_tiles))
    p_lat = p_a.reshape(bsz, t_len, 2 * d_a + d_b)
    p_ctx = p_ctx.reshape(bsz, tc_len, ctx_cols)

    def gate_w(dirn):
        return jnp.concatenate([lru_wr[0, dirn], lru_wi[0, dirn]], axis=-1).astype(BF16)

    def gate_b(dirn):
        return jnp.concatenate([lru_br[0, dirn].reshape(n_blocks_a, 1, LANES),
                                lru_bi[0, dirn].reshape(n_blocks_a, 1, LANES)], axis=-1)

    cb = 256 if d_a % 256 == 0 else LANES
    ya, w_out_b = _rglru(p_lat, p_ctx, conv_w[0], conv_b[0], gate_w(0), gate_b(0), gate_w(1),
                         gate_b(1), lru_lambda[0], w_out[0], d_a, cb, tt=256)

    assert d_a % d_b == 0
    states = _hgrn(p_ctx.reshape(bsz, tc_len // CHUNK, CHUNK, ctx_cols), hgrn_lb_logits, d_b,
                   n_heads, d_a // d_b, None, False)
    o_f, o_b = _hgrn(p_b, hgrn_lb_logits, d_b, n_heads, 0, states, True)

    grid4 = lambda z: z.reshape(bsz, n_rows_grid, GRID_W, z.shape[-1])
    out = _out_projection(grid4(ya), o_f, o_b, grid4(p_lat), first_b_group, grid4(x), mod3,
                          hgrn_norm_w[0], final_norm_w, w_out_b, n_heads, 16, 16)
    return out.reshape(bsz, t_len, d)
```

```python
import functools

import jax
import jax.numpy as jnp
from jax import lax
from jax.experimental import pallas as pl
from jax.experimental.pallas import tpu as pltpu

GRID_W = 64
CHUNK = 64
LRU_C = 8.0
EPS = 1e-6
CONV_PAD_L = 2
LANES = 128
SUBLANES = 8
EXP2_CLAMP = 115.0
NEG_LOG2E = -1.4426950408889634
VMEM_LIMIT = 56 * 1024 * 1024
PROLOGUE_ROWS = 16
SEG_PAD = 4
CHUNKS_PER_STEP = 2
TILES_PER_TRIP = 8

F32 = jnp.float32
BF16 = jnp.bfloat16


def _sigmoid(z):
    return 1.0 / (1.0 + jnp.exp2(z * NEG_LOG2E))


def _silu(z):
    return z * _sigmoid(z)


def _softplus(z):
    return jnp.maximum(z, 0.0) + jnp.log1p(jnp.exp(-jnp.abs(z)))


def _mod_kernel(c_ref, w_ref, b_ref, o_ref):
    s = _silu(c_ref[...])
    o_ref[...] = jnp.dot(s.astype(BF16), w_ref[...].astype(BF16),
                         preferred_element_type=F32) + b_ref[...]


def _modulation(cc, w, b):
    rows, d = cc.shape
    n = w.shape[1]
    tn = 512 if n % 512 == 0 else n
    return pl.pallas_call(
        _mod_kernel,
        grid=(n // tn,),
        in_specs=[pl.BlockSpec((rows, d), lambda j: (0, 0)),
                  pl.BlockSpec((d, tn), lambda j: (0, j)),
                  pl.BlockSpec((1, tn), lambda j: (0, j))],
        out_specs=pl.BlockSpec((rows, tn), lambda j: (0, j)),
        out_shape=jax.ShapeDtypeStruct((rows, n), F32),
        compiler_params=pltpu.CompilerParams(
            dimension_semantics=("arbitrary",), vmem_limit_bytes=VMEM_LIMIT),
        name="adaln_modulation",
    )(cc, w, b.reshape(1, n))


def _store_projection(o_ref, res, j, silu_tiles):
    lo, hi = silu_tiles
    z = res.astype(o_ref.dtype)
    if hi <= lo:
        o_ref[...] = z
        return
    in_range = jnp.logical_and(j >= lo, j < hi)
    o_ref[...] = jnp.where(in_range, z * (0.5 * jnp.tanh(0.5 * z) + 0.5), z)


def _inproj_kernel(x_ref, shift_ref, scale_ref, nw_ref, w_ref, o_ref, h_ref, *, silu_tiles):
    @pl.when(pl.program_id(1) == 0)
    def _():
        rows = min(PROLOGUE_ROWS, x_ref.shape[0])
        gain = nw_ref[...] * (1.0 + scale_ref[0])

        def slab(s, carry):
            sl = pl.ds(pl.multiple_of(s * rows, rows), rows)
            x = x_ref[sl, :]
            rs = lax.rsqrt(jnp.mean(x * x, axis=-1, keepdims=True) + EPS)
            h_ref[sl, :] = (x_ref[sl, :] * rs * gain + shift_ref[0]).astype(BF16)
            return carry

        lax.fori_loop(0, x_ref.shape[0] // rows, slab, 0, unroll=4)

    res = jnp.dot(h_ref[...], w_ref[...].astype(BF16), preferred_element_type=F32)
    _store_projection(o_ref, res, pl.program_id(1), silu_tiles)


def _in_projection(x2d, mod3, norm_w, w_bf16, row_of_tile, wcol_of_tile, n_out, tm, tn, silu_tiles):
    m, d = x2d.shape
    return pl.pallas_call(
        functools.partial(_inproj_kernel, silu_tiles=silu_tiles),
        grid=(m // tm, n_out // tn),
        in_specs=[pl.BlockSpec((tm, d), lambda i, j: (i, 0)),
                  pl.BlockSpec((1, 1, d), lambda i, j: (row_of_tile(i), 0, 0)),
                  pl.BlockSpec((1, 1, d), lambda i, j: (row_of_tile(i), 0, 1)),
                  pl.BlockSpec((1, d), lambda i, j: (0, 0)),
                  pl.BlockSpec((d, tn), lambda i, j: (0, wcol_of_tile(j)))],
        out_specs=pl.BlockSpec((tm, tn), lambda i, j: (i, j)),
        out_shape=jax.ShapeDtypeStruct((m, n_out), BF16),
        scratch_shapes=[pltpu.VMEM((tm, d), BF16)],
        compiler_params=pltpu.CompilerParams(
            dimension_semantics=("parallel", "arbitrary"),
            vmem_limit_bytes=VMEM_LIMIT),
        name="in_projection",
    )(x2d, mod3, mod3, norm_w.reshape(1, d), w_bf16)


def _matmul_kernel(h_ref, w_ref, o_ref, *, silu_tiles):
    res = jnp.dot(h_ref[...], w_ref[...].astype(BF16), preferred_element_type=F32)
    _store_projection(o_ref, res, pl.program_id(1), silu_tiles)


def _projection_from_h(h2d, w_f32, wcol_of_tile, n_out, tm, tn, silu_tiles):
    m, d = h2d.shape
    return pl.pallas_call(
        functools.partial(_matmul_kernel, silu_tiles=silu_tiles),
        grid=(m // tm, n_out // tn),
        in_specs=[pl.BlockSpec((tm, d), lambda i, j: (i, 0)),
                  pl.BlockSpec((d, tn), lambda i, j: (0, wcol_of_tile(j)))],
        out_specs=pl.BlockSpec((tm, tn), lambda i, j: (i, j)),
        out_shape=jax.ShapeDtypeStruct((m, n_out), BF16),
        compiler_params=pltpu.CompilerParams(
            dimension_semantics=("parallel", "arbitrary"),
            vmem_limit_bytes=VMEM_LIMIT),
        name="projection_from_h",
    )(h2d, w_f32)


def _inproj_colmajor_kernel(x_ref, shift_ref, scale_ref, nw_ref, w_ref, o_ref, hr_ref, h_ref, *,
                            slab, silu_tiles):
    _, n_r, n_w, d = x_ref.shape

    rr = 2 * SUBLANES
    lanes = [slice(s * slab, (s + 1) * slab) for s in range(d // slab)]

    @pl.when(pl.program_id(2) == 0)
    def _():
        def row_group(g, carry):
            rows = pl.ds(pl.multiple_of(g * rr, rr), rr)
            sq = jnp.zeros((rr, n_w, slab), F32)
            for sl in lanes:
                xs = x_ref[0, rows, :, sl]
                sq = sq + xs * xs
            rs = lax.rsqrt(jnp.sum(sq, axis=-1, keepdims=True) * (1.0 / d) + EPS)
            for sl in lanes:
                y = x_ref[0, rows, :, sl] * rs * nw_ref[:, sl]
                y = y * (1.0 + scale_ref[0][:, sl]) + shift_ref[0][:, sl]
                hr_ref[0, rows, :, sl] = y.astype(BF16)
                h_ref[:, rows, sl] = jnp.transpose(y, (1, 0, 2)).astype(BF16)
            return carry

        lax.fori_loop(0, n_r // rr, row_group, 0)

    res = jnp.dot(h_ref[...].reshape(n_w * n_r, d), w_ref[...].astype(BF16),
                  preferred_element_type=F32)
    _store_projection(o_ref, res.reshape(1, n_w, n_r, res.shape[1]), pl.program_id(2), silu_tiles)


def _in_projection_colmajor(x4d, mod3, norm_w, w_bf16, wcol_of_tile, n_out, wb, tn, silu_tiles):
    bsz, n_r, n_w, d = x4d.shape
    kern = functools.partial(_inproj_colmajor_kernel, slab=2 * LANES, silu_tiles=silu_tiles)
    return pl.pallas_call(
        kern,
        grid=(bsz, n_w // wb, n_out // tn),
        in_specs=[pl.BlockSpec((1, n_r, wb, d), lambda b, w, j: (b, 0, w, 0)),
                  pl.BlockSpec((1, 1, d), lambda b, w, j: (b, 0, 0)),
                  pl.BlockSpec((1, 1, d), lambda b, w, j: (b, 0, 1)),
                  pl.BlockSpec((1, d), lambda b, w, j: (0, 0)),
                  pl.BlockSpec((d, tn), lambda b, w, j: (0, wcol_of_tile(j)))],
        out_specs=[pl.BlockSpec((1, wb, n_r, tn), lambda b, w, j: (b, w, 0, j)),
                   pl.BlockSpec((1, n_r, wb, d), lambda b, w, j: (b, 0, w, 0))],
        out_shape=[jax.ShapeDtypeStruct((bsz, n_w, n_r, n_out), BF16),
                   jax.ShapeDtypeStruct((bsz, n_r, n_w, d), BF16)],
        scratch_shapes=[pltpu.VMEM((wb, n_r, d), BF16)],
        compiler_params=pltpu.CompilerParams(
            dimension_semantics=("parallel", "parallel", "arbitrary"),
            vmem_limit_bytes=VMEM_LIMIT),
        name="in_projection_colmajor",
    )(x4d, mod3, mod3, norm_w.reshape(1, d), w_bf16)


def _group_scan(a, b, reverse):
    row = lax.broadcasted_iota(jnp.int32, a.shape, 1)
    for k in (1, 2, 4):
        if reverse:
            a_sh = pltpu.roll(a, SUBLANES - k, axis=1)
            b_sh = pltpu.roll(b, SUBLANES - k, axis=1)
            m = row < SUBLANES - k
        else:
            a_sh = pltpu.roll(a, k, axis=1)
            b_sh = pltpu.roll(b, k, axis=1)
            m = row >= k
        b = jnp.where(m, a * b_sh + b, b)
        a = jnp.where(m, a * a_sh, a)
    return a, b


def _sqrt_unit(x):
    return jnp.where(x > 0.0, x * lax.rsqrt(x), 0.0)


def _segment_scan(a, b, carry, reverse, scr_ref):
    tt = a.shape[0]
    seg = tt // SUBLANES
    pitch = seg + SEG_PAD
    for s in range(SUBLANES):
        rows = pl.ds(pitch * s, seg, stride=1)
        scr_ref[0, rows, :] = a[seg * s:seg * (s + 1), :]
        scr_ref[1, rows, :] = b[seg * s:seg * (s + 1), :]
    hs, ps = [None] * seg, [None] * seg
    h = p = None
    for j in (range(seg - 1, -1, -1) if reverse else range(seg)):
        step = pl.ds(j, SUBLANES, stride=pitch)
        a_j = scr_ref[0, step, :]
        b_j = scr_ref[1, step, :]
        h = b_j if h is None else a_j * h + b_j
        p = a_j if p is None else a_j * p
        hs[j], ps[j] = h, p
    g3, e3 = _group_scan(p[None], h[None], reverse)
    end = e3[0] + g3[0] * carry
    row = lax.broadcasted_iota(jnp.int32, end.shape, 0)
    if reverse:
        enter = jnp.where(row == SUBLANES - 1, carry, pltpu.roll(end, SUBLANES - 1, axis=0))
        carry_out = end[0:1, :]
    else:
        enter = jnp.where(row == 0, carry, pltpu.roll(end, 1, axis=0))
        carry_out = end[SUBLANES - 1:SUBLANES, :]
    for j in range(seg):
        scr_ref[2, pl.ds(j, SUBLANES, stride=pitch), :] = hs[j] + ps[j] * enter
    h_time = [scr_ref[2, pl.ds(pitch * s, seg, stride=1), :] for s in range(SUBLANES)]
    return jnp.concatenate(h_time, axis=0), carry_out


def _lru_block(u, wg, bg, sp, carry, reverse, scr_ref):
    g = jnp.dot(u.astype(BF16), wg, preferred_element_type=F32) + bg
    r = _sigmoid(g[:, :LANES])
    i = _sigmoid(g[:, LANES:])
    a = jnp.exp2(r * sp)
    b = _sqrt_unit(1.0 - a * a) * (i * u)
    return _segment_scan(a, b, carry, reverse, scr_ref)


def _rglru_kernel(xa_ref, ga_ref, xc_ref, cw_ref, cb_ref, wgf_ref, bgf_ref, wgb_ref, bgb_ref,
                  lam_ref, wcast_ref, o_ref, wcast_out_ref, xf_ref, xcf_ref, u_ref, hf_ref, scr_ref,
                  *, tt):
    wcast_out_ref[...] = wcast_ref[...].astype(BF16)
    t_len = xa_ref.shape[1]
    tc_len = xc_ref.shape[1]
    nblk = xa_ref.shape[2] // LANES
    lanes = [slice(k * LANES, (k + 1) * LANES) for k in range(nblk)]
    zeros = jnp.zeros((SUBLANES, LANES), F32)
    for k, sl in enumerate(lanes):
        xf_ref[k, 0:SUBLANES, :] = zeros
        xf_ref[k, SUBLANES:SUBLANES + t_len, :] = xa_ref[0, :, sl].astype(F32)
        xf_ref[k, SUBLANES + t_len:, :] = zeros
        xcf_ref[k, 0:SUBLANES, :] = zeros
        xcf_ref[k, SUBLANES:SUBLANES + tc_len, :] = xc_ref[0, :, sl].astype(F32)
        xcf_ref[k, SUBLANES + tc_len:, :] = zeros

    sp_f = _softplus(-lam_ref[0:1, :]) * (LRU_C * NEG_LOG2E)
    sp_b = _softplus(-lam_ref[1:2, :]) * (LRU_C * NEG_LOG2E)

    def conv(src_ref, k, t0):
        u = cb_ref[:, lanes[k]]
        for tap in range(cw_ref.shape[0]):
            off = SUBLANES - CONV_PAD_L + tap
            if off % SUBLANES == 0:
                start = t0 + off
                rows = pl.ds(start if isinstance(start, int) else pl.multiple_of(start, SUBLANES), tt)
            else:
                rows = pl.ds(t0 + off, tt, stride=1)
            u = u + src_ref[k, rows, :] * cw_ref[tap:tap + 1, lanes[k]]
        return u

    def block(u, k, carry, reverse, slot):
        scr = scr_ref.at[k * TILES_PER_TRIP + slot]
        if reverse:
            return _lru_block(u, wgb_ref[k], bgb_ref[k], sp_b[:, lanes[k]], carry, True, scr)
        return _lru_block(u, wgf_ref[k], bgf_ref[k], sp_f[:, lanes[k]], carry, False, scr)

    n_lat = t_len // tt
    n_ctx = tc_len // tt
    assert n_lat % TILES_PER_TRIP == 0
    carry0 = tuple(jnp.zeros((1, LANES), F32) for _ in lanes)

    def tile_start(trip, slot, n_tiles, reverse):
        idx = trip * TILES_PER_TRIP + slot
        return pl.multiple_of((n_tiles - 1 - idx if reverse else idx) * tt, tt)

    def ctx_pass(reverse):
        carry = carry0
        for idx in range(n_ctx):
            t0 = (n_ctx - 1 - idx if reverse else idx) * tt
            carry = tuple(block(conv(xcf_ref, k, t0), k, carry[k], reverse, idx % TILES_PER_TRIP)[1]
                          for k in range(nblk))
        return carry

    def lat_f(trip, carry):
        for slot in range(TILES_PER_TRIP):
            t0 = tile_start(trip, slot, n_lat, False)
            rows = pl.ds(t0, tt)
            out = []
            for k, sl in enumerate(lanes):
                u = conv(xf_ref, k, t0)
                u_ref[rows, sl] = u
                h, c_out = block(u, k, carry[k], False, slot)
                hf_ref[rows, sl] = h
                out.append(c_out)
            carry = tuple(out)
        return carry

    lax.fori_loop(0, n_lat // TILES_PER_TRIP, lat_f, ctx_pass(False))

    def lat_b(trip, carry):
        for slot in range(TILES_PER_TRIP):
            rows = pl.ds(tile_start(trip, slot, n_lat, True), tt)
            out = []
            for k, sl in enumerate(lanes):
                h, c_out = block(u_ref[rows, sl], k, carry[k], True, slot)
                gate = ga_ref[0, rows, sl].astype(F32)
                o_ref[0, rows, sl] = ((hf_ref[rows, sl] + h) * gate).astype(o_ref.dtype)
                out.append(c_out)
            carry = tuple(out)
        return carry

    lax.fori_loop(0, n_lat // TILES_PER_TRIP, lat_b, ctx_pass(True))


def _rglru(p_lat, p_ctx, conv_w, conv_b, wg_f, bg_f, wg_b, bg_b, lam, w_cast, d_a, cb, tt):
    bsz, t_len, _ = p_lat.shape
    tc_len = p_ctx.shape[1]
    nblk = cb // LANES
    ncb = d_a // cb
    wrows, wcols = w_cast.shape
    assert wrows % (bsz * ncb) == 0
    wr = wrows // (bsz * ncb)
    wspec = pl.BlockSpec((wr, wcols), lambda b, c: (b * ncb + c, 0))
    kern = functools.partial(_rglru_kernel, tt=tt)
    return pl.pallas_call(
        kern,
        grid=(bsz, ncb),
        in_specs=[pl.BlockSpec((1, t_len, cb), lambda b, c: (b, 0, c)),
                  pl.BlockSpec((1, t_len, cb), lambda b, c: (b, 0, ncb + c)),
                  pl.BlockSpec((1, tc_len, cb), lambda b, c: (b, 0, c)),
                  pl.BlockSpec((conv_w.shape[0], cb), lambda b, c: (0, c)),
                  pl.BlockSpec((1, cb), lambda b, c: (0, c)),
                  pl.BlockSpec((nblk, LANES, 2 * LANES), lambda b, c: (c, 0, 0)),
                  pl.BlockSpec((nblk, 1, 2 * LANES), lambda b, c: (c, 0, 0)),
                  pl.BlockSpec((nblk, LANES, 2 * LANES), lambda b, c: (c, 0, 0)),
                  pl.BlockSpec((nblk, 1, 2 * LANES), lambda b, c: (c, 0, 0)),
                  pl.BlockSpec((2, cb), lambda b, c: (0, c)),
                  wspec],
        out_specs=[pl.BlockSpec((1, t_len, cb), lambda b, c: (b, 0, c)), wspec],
        out_shape=[jax.ShapeDtypeStruct((bsz, t_len, d_a), BF16),
                   jax.ShapeDtypeStruct((wrows, wcols), BF16)],
        scratch_shapes=[pltpu.VMEM((nblk, t_len + 2 * SUBLANES, LANES), F32),
                        pltpu.VMEM((nblk, tc_len + 2 * SUBLANES, LANES), F32),
                        pltpu.VMEM((t_len, cb), F32),
                        pltpu.VMEM((t_len, cb), F32),
                        pltpu.VMEM((nblk * TILES_PER_TRIP, 3, tt + SUBLANES * SEG_PAD, LANES),
                                   F32)],
        compiler_params=pltpu.CompilerParams(
            dimension_semantics=("parallel", "parallel"),
            vmem_limit_bytes=VMEM_LIMIT),
        name="rglru",
    )(p_lat, p_lat, p_ctx, conv_w, conv_b.reshape(1, d_a), wg_f, bg_f, wg_b, bg_b, lam, w_cast)


_NT = (((1,), (1,)), ((), ()))
_TN = (((0,), (0,)), ((), ()))


class _HgrnDir:
    def __init__(self, d, q_ref, f_ref, v_ref, lb, st_ref, o_ref, scratch, reverse, n_heads, hd,
                 sub):
        self.d, self.q_ref, self.f_ref, self.v_ref, self.lb = d, q_ref, f_ref, v_ref, lb
        self.sub = sub
        self.st_ref, self.o_ref, self.reverse, self.n_heads, self.hd = st_ref, o_ref, reverse, n_heads, hd
        (self.hl_ref, self.kk_ref, self.cum_ref, self.qd_ref, self.kd_ref, self.qin_ref,
         self.kout_ref, self.dec_ref, self.dect_ref, self.sc_ref) = scratch
        self.c_len = q_ref.shape[2]
        r_i = lax.broadcasted_iota(jnp.int32, (self.c_len, self.c_len), 0)
        c_i = lax.broadcasted_iota(jnp.int32, (self.c_len, self.c_len), 1)
        self.tri = (r_i <= c_i) if reverse else (r_i >= c_i)

    def _slabs(self, width):
        total = self.n_heads * self.hd
        return [slice(s, s + width) for s in range(0, total, width)]


    def _gate_slab(self, sl):
        d, c_len = self.d, self.c_len
        lb = self.lb[:, sl]
        f = lb + (1.0 - lb) * _sigmoid(self.f_ref[0, self.sub, :, sl].astype(F32))
        logf = jnp.log2(f)
        self.kk_ref[d, :, sl] = (1.0 - f).astype(BF16)
        hi = logf.astype(BF16)
        self.hl_ref[d, 0:c_len, sl] = hi
        self.hl_ref[d, c_len:2 * c_len, sl] = (logf - hi.astype(F32)).astype(BF16)

    def _cumulate(self):
        tri_b = self.tri.astype(BF16)
        tri2 = jnp.concatenate([tri_b, tri_b], axis=1)
        self.cum_ref[self.d] = jnp.dot(tri2, self.hl_ref[self.d], preferred_element_type=F32)

    def _decay_slab(self, sl):
        d, c_len = self.d, self.c_len
        half = c_len // 2
        row_last = 0 if self.reverse else c_len - 1
        row_ref = half if self.reverse else half - 1
        cum = self.cum_ref[d, :, sl]
        last = cum[row_last:row_last + 1, :]
        ref = cum[row_ref:row_ref + 1, :]
        kk = self.kk_ref[d, :, sl]
        qs = self.q_ref[0, self.sub, :, sl]
        self.qd_ref[d, :, sl] = qs * jnp.exp2(jnp.minimum(cum - ref, EXP2_CLAMP)).astype(BF16)
        self.kd_ref[d, :, sl] = kk * jnp.exp2(jnp.minimum(ref - cum, EXP2_CLAMP)).astype(BF16)
        self.qin_ref[d, :, sl] = qs * jnp.exp2(cum).astype(BF16)
        self.kout_ref[d, :, sl] = kk * jnp.exp2(last - cum).astype(BF16)
        dec = jnp.exp2(last)
        for i in range((sl.stop - sl.start) // self.hd):
            h = sl.start // self.hd + i
            self.dec_ref[d, h:h + 1, :] = dec[:, i * self.hd:(i + 1) * self.hd]

    def _decay_columns(self):
        self.dect_ref[self.d] = jnp.transpose(self.dec_ref[self.d])

    def _head(self, h):
        return slice(h * self.hd, (h + 1) * self.hd)

    def _scores(self, h):
        d, sl = self.d, self._head(h)
        s = lax.dot_general(self.qd_ref[d, :, sl], self.kd_ref[d, :, sl], _NT,
                            preferred_element_type=F32)
        self.sc_ref[d, h] = jnp.where(self.tri, s, 0.0).astype(BF16)

    def _output(self, h):
        d, sl = self.d, self._head(h)
        o = (jnp.dot(self.sc_ref[d, h], self.v_ref[0, self.sub, :, sl], preferred_element_type=F32)
             + jnp.dot(self.qin_ref[d, :, sl], self.st_ref[h].astype(BF16),
                       preferred_element_type=F32))
        self.o_ref[0, self.sub, :, sl] = o.astype(self.o_ref.dtype)

    def _update(self, h):
        d, sl = self.d, self._head(h)
        decay = jnp.broadcast_to(self.dect_ref[d, :, h:h + 1], (self.hd, self.hd))
        self.st_ref[h] = (self.st_ref[h] * decay
                          + lax.dot_general(self.kout_ref[d, :, sl], self.v_ref[0, self.sub, :, sl], _TN,
                                            preferred_element_type=F32))

    def stages(self):
        heads = range(self.n_heads)
        slabs = self._slabs(2 * LANES)
        out = [[functools.partial(self._gate_slab, sl) for sl in slabs], [self._cumulate],
               [functools.partial(self._decay_slab, sl) for sl in slabs], [self._decay_columns]]
        if self.o_ref is not None:
            out += [[functools.partial(self._scores, h) for h in heads],
                    [functools.partial(self._output, h) for h in heads]]
        return out + [[functools.partial(self._update, h) for h in heads]]


def _lower_bounds(logits_ref, layer):
    out = []
    for d in range(2):
        rows = [logits_ref[d, l:l + 1, :] for l in range(logits_ref.shape[1])]
        m = functools.reduce(jnp.maximum, rows)
        e = [jnp.exp(r - m) for r in rows]
        out.append(sum(e[:layer + 1]) / sum(e))
    return out


def _hgrn_kernel(*refs, n_heads, hd, has_init, emit_o, emit_state):
    qf_ref, ff_ref, vf_ref, qb_ref, fb_ref, vb_ref, lg_ref = refs[:7]
    pos = 7
    if has_init:
        s0f_ref, s0b_ref = refs[pos:pos + 2]
        pos += 2
    if emit_o:
        of_ref, ob_ref = refs[pos:pos + 2]
        pos += 2
    if emit_state:
        sof_ref, sob_ref = refs[pos:pos + 2]
        pos += 2
    sf_ref, sb_ref = refs[pos:pos + 2]
    scratch = refs[pos + 2:]

    j = pl.program_id(1)
    dec_ref = scratch[7]
    assert n_heads <= dec_ref.shape[1]

    @pl.when(j == 0)
    def _():
        if has_init:
            sf_ref[...] = s0f_ref[0]
            sb_ref[...] = s0b_ref[0]
        else:
            sf_ref[...] = jnp.zeros_like(sf_ref)
            sb_ref[...] = jnp.zeros_like(sb_ref)
        dec_ref[...] = jnp.zeros_like(dec_ref)

    lb_f, lb_b = _lower_bounds(lg_ref, 0)
    n_sub = qf_ref.shape[1]
    for k in range(n_sub):
        dirs = [_HgrnDir(0, qf_ref, ff_ref, vf_ref, lb_f, sf_ref, of_ref if emit_o else None,
                         scratch, False, n_heads, hd, k),
                _HgrnDir(1, qb_ref, fb_ref, vb_ref, lb_b, sb_ref, ob_ref if emit_o else None,
                         scratch, True, n_heads, hd, n_sub - 1 - k)]
        for stage_f, stage_b in zip(dirs[0].stages(), dirs[1].stages()):
            for unit in stage_f + stage_b:
                unit()

    if emit_state:
        @pl.when(j == pl.num_programs(1) - 1)
        def _():
            sof_ref[0] = sf_ref[...]
            sob_ref[0] = sb_ref[...]


def _hgrn(p_view, logits, d_b, n_heads, first_group, init_states, emit_o):
    bsz, n_chunks = p_view.shape[:2]
    hd = d_b // n_heads
    has_init = init_states is not None
    emit_state = not emit_o

    cps = CHUNKS_PER_STEP if n_chunks % CHUNKS_PER_STEP == 0 else 1
    n_steps = n_chunks // cps

    def chunk(j, reverse):
        return n_steps - 1 - j if reverse else j

    def spec(group, reverse):
        def imap(b, j):
            return (b, chunk(j, reverse), 0, first_group + group)
        return pl.BlockSpec((1, cps, CHUNK, d_b), imap)

    in_specs = [spec(0, False), spec(1, False), spec(3, False),
                spec(0, True), spec(2, True), spec(3, True),
                pl.BlockSpec(logits.shape, lambda b, j: (0, 0, 0))]
    args = [p_view] * 6 + [logits]
    state_spec = pl.BlockSpec((1, n_heads, hd, hd), lambda b, j: (b, 0, 0, 0))
    state_shape = jax.ShapeDtypeStruct((bsz, n_heads, hd, hd), F32)
    if has_init:
        in_specs += [state_spec, state_spec]
        args += list(init_states)
    out_specs, out_shape = [], []
    if emit_o:
        o_shape = jax.ShapeDtypeStruct((bsz, n_chunks, CHUNK, d_b), BF16)
        out_specs += [pl.BlockSpec((1, cps, CHUNK, d_b), lambda b, j: (b, chunk(j, False), 0, 0)),
                      pl.BlockSpec((1, cps, CHUNK, d_b), lambda b, j: (b, chunk(j, True), 0, 0))]
        out_shape += [o_shape, o_shape]
    if emit_state:
        out_specs += [state_spec, state_spec]
        out_shape += [state_shape, state_shape]
    kern = functools.partial(_hgrn_kernel, n_heads=n_heads, hd=hd, has_init=has_init,
                             emit_o=emit_o, emit_state=emit_state)
    return pl.pallas_call(
        kern,
        grid=(bsz, n_steps),
        in_specs=in_specs,
        out_specs=out_specs,
        out_shape=out_shape,
        scratch_shapes=[pltpu.VMEM((n_heads, hd, hd), F32),
                        pltpu.VMEM((n_heads, hd, hd), F32),
                        pltpu.VMEM((2, 2 * CHUNK, d_b), BF16),
                        pltpu.VMEM((2, CHUNK, d_b), BF16),
                        pltpu.VMEM((2, CHUNK, d_b), F32),
                        pltpu.VMEM((2, CHUNK, d_b), BF16),
                        pltpu.VMEM((2, CHUNK, d_b), BF16),
                        pltpu.VMEM((2, CHUNK, d_b), BF16),
                        pltpu.VMEM((2, CHUNK, d_b), BF16),
                        pltpu.VMEM((2, hd, hd), F32),
                        pltpu.VMEM((2, hd, hd), F32),
                        pltpu.VMEM((2, n_heads, CHUNK, CHUNK), BF16)],
        compiler_params=pltpu.CompilerParams(
            dimension_semantics=("parallel", "arbitrary"),
            vmem_limit_bytes=VMEM_LIMIT),
        name="hgrn2_latent" if emit_o else "hgrn2_context",
    )(*args)


def _outproj_kernel(ya_ref, of_ref, ob_ref, gb_ref, x_ref, gate_ref, hnw_ref, fnw_ref, w_ref,
                    o_ref, y_ref, *, n_heads, hd):
    _, rb, wb, d_a = ya_ref.shape
    rows = rb * wb
    d = x_ref.shape[3]
    y_ref[:, 0:d_a] = ya_ref[0].reshape(rows, d_a)
    hnw = hnw_ref[...]
    for h in range(n_heads):
        sl = slice(h * hd, (h + 1) * hd)
        o = of_ref[0, :, :, sl].astype(F32) + ob_ref[0, :, :, sl].astype(F32)
        ms = jnp.mean(o * o, axis=-1, keepdims=True)
        on = jnp.transpose(o * lax.rsqrt(ms + EPS) * hnw, (1, 0, 2))
        yb = on * gb_ref[0, :, :, sl].astype(F32)
        y_ref[:, d_a + h * hd:d_a + (h + 1) * hd] = yb.reshape(rows, hd).astype(BF16)
    acc = jnp.dot(y_ref[...], w_ref[...], preferred_element_type=F32)
    z = x_ref[0].reshape(rows, d) + gate_ref[0] * acc
    ms = jnp.mean(z * z, axis=-1, keepdims=True)
    o_ref[0] = (z * lax.rsqrt(ms + EPS) * fnw_ref[...]).reshape(rb, wb, d)


def _out_projection(ya4, of4, ob4, pa4, gb_block, x4, mod3, hnw, fnw, w_bf16, n_heads, rb, wb):
    bsz, n_r, n_w, d = x4.shape
    d_a = ya4.shape[3]
    d_b = of4.shape[3]
    hd = d_b // n_heads
    kern = functools.partial(_outproj_kernel, n_heads=n_heads, hd=hd)

    def raster(c, col=0):
        return pl.BlockSpec((1, rb, wb, c), lambda b, r, w: (b, r, w, col))

    def colmajor(c):
        return pl.BlockSpec((1, wb, rb, c), lambda b, r, w: (b, w, r, 0))

    return pl.pallas_call(
        kern,
        grid=(bsz, n_r // rb, n_w // wb),
        in_specs=[raster(d_a), colmajor(d_b), colmajor(d_b), raster(d_b, gb_block), raster(d),
                  pl.BlockSpec((1, 1, d), lambda b, r, w: (b, 0, 2)),
                  pl.BlockSpec((1, hd), lambda b, r, w: (0, 0)),
                  pl.BlockSpec((1, d), lambda b, r, w: (0, 0)),
                  pl.BlockSpec((d_a + d_b, d), lambda b, r, w: (0, 0),
                               pipeline_mode=pl.Buffered(1))],
        out_specs=raster(d),
        out_shape=jax.ShapeDtypeStruct((bsz, n_r, n_w, d), F32),
        scratch_shapes=[pltpu.VMEM((rb * wb, d_a + d_b), BF16)],
        compiler_params=pltpu.CompilerParams(
            dimension_semantics=("parallel", "parallel", "parallel"),
            vmem_limit_bytes=VMEM_LIMIT),
        name="out_projection",
    )(ya4, of4, ob4, pa4, x4, mod3, hnw.reshape(1, hd), fnw.reshape(1, d), w_bf16)


def kernel(x, c, ctx, c_ctx, ada_w, ada_b, norm_w, w_in, conv_w, conv_b, lru_wr, lru_br, lru_wi,
           lru_bi, lru_lambda, hgrn_lb_logits, hgrn_norm_w, w_out, final_norm_w):
    bsz, t_len, d = x.shape
    tc_len = ctx.shape[1]
    assert ada_w.shape[0] == 1, "single-layer stack only"
    d_a = conv_w.shape[2]
    d_b = hgrn_lb_logits.shape[2]
    hd = hgrn_norm_w.shape[1]
    n_heads = d_b // hd
    n_blocks_a = lru_wr.shape[2]
    n_cols = w_in.shape[2]
    assert t_len == GRID_W * CHUNK and tc_len % CHUNK == 0
    assert d_a // n_blocks_a == LANES and hd == LANES
    assert (2 * d_a) % d_b == 0 and n_cols == 2 * d_a + 5 * d_b
    first_b_group = (2 * d_a) // d_b
    n_rows_grid = t_len // GRID_W

    n_rows = -(-(bsz + 1) // SUBLANES) * SUBLANES
    cc = jnp.zeros((n_rows, d), F32).at[:bsz].set(c).at[bsz].set(c_ctx)
    mod3 = _modulation(cc, ada_w[0], ada_b[0]).reshape(n_rows, 1, 3 * d)

    w_in_b = w_in[0]

    tm = 1024 if t_len % 1024 == 0 else t_len
    tn = 1024 if d_a % 1024 == 0 and d_b % 1024 == 0 else min(d_a, d_b)
    a_tiles = (2 * d_a) // tn
    b_tiles = (4 * d_b) // tn
    q_tiles = d_b // tn
    wb = 16
    p_b, h_lat = _in_projection_colmajor(x.reshape(bsz, n_rows_grid, GRID_W, d), mod3, norm_w[0],
                                         w_in_b, lambda j: j + a_tiles, 4 * d_b, wb, tn,
                                         (0, q_tiles))
    tm_h = 2 * tm if (bsz * t_len) % (2 * tm) == 0 else tm
    p_a = _projection_from_h(h_lat.reshape(bsz * t_len, d), w_in_b,
                             lambda j: jnp.where(j < a_tiles, j, j + b_tiles),
                             2 * d_a + d_b, tm_h, tn, (d_a // tn, (2 * d_a + d_b) // tn))
    xa_tiles = d_a // tn
    ctx_cols = d_a + 4 * d_b
    p_ctx = _in_projection(ctx.reshape(bsz * tc_len, d), mod3, norm_w[0], w_in_b,
                           lambda i: bsz, lambda j: jnp.where(j < xa_tiles, j, j + xa_tiles),
                           ctx_cols, bsz * tc_len, tn, (xa_tiles, xa_tiles + q_tiles))
    p_lat = p_a.reshape(bsz, t_len, 2 * d_a + d_b)
    p_ctx = p_ctx.reshape(bsz, tc_len, ctx_cols)

    def gate_w(dirn):
        return jnp.concatenate([lru_wr[0, dirn], lru_wi[0, dirn]], axis=-1).astype(BF16)

    def gate_b(dirn):
        return jnp.concatenate([lru_br[0, dirn].reshape(n_blocks_a, 1, LANES),
                                lru_bi[0, dirn].reshape(n_blocks_a, 1, LANES)], axis=-1)

    cb = 256 if d_a % 256 == 0 else LANES
    ya, w_out_b = _rglru(p_lat, p_ctx, conv_w[0], conv_b[0], gate_w(0), gate_b(0), gate_w(1),
                         gate_b(1), lru_lambda[0], w_out[0], d_a, cb, tt=256)

    assert d_a % d_b == 0
    states = _hgrn(p_ctx.reshape(bsz, tc_len // CHUNK, CHUNK, ctx_cols), hgrn_lb_logits, d_b,
                   n_heads, d_a // d_b, None, False)
    o_f, o_b = _hgrn(p_b, hgrn_lb_logits, d_b, n_heads, 0, states, True)

    grid4 = lambda z: z.reshape(bsz, n_rows_grid, GRID_W, z.shape[-1])
    out = _out_projection(grid4(ya), o_f, o_b, grid4(p_lat), first_b_group, grid4(x), mod3,
                          hgrn_norm_w[0], final_norm_w, w_out_b, n_heads, 16, 16)
    return out.reshape(bsz, t_len, d)
```

```python
import functools

import jax
import jax.numpy as jnp
from jax import lax
from jax.experimental import pallas as pl
from jax.experimental.pallas import tpu as pltpu

GRID_W = 64
CHUNK = 64
LRU_C = 8.0
EPS = 1e-6
CONV_PAD_L = 2
LANES = 128
SUBLANES = 8
EXP2_CLAMP = 115.0
NEG_LOG2E = -1.4426950408889634
VMEM_LIMIT = 56 * 1024 * 1024
PROLOGUE_ROWS = 16
SEG_PAD = 4
CHUNKS_PER_STEP = 2
TILES_PER_TRIP = 4

F32 = jnp.float32
BF16 = jnp.bfloat16


def _sigmoid(z):
    return 1.0 / (1.0 + jnp.exp2(z * NEG_LOG2E))


def _silu(z):
    return z * _sigmoid(z)


def _softplus(z):
    return jnp.maximum(z, 0.0) + jnp.log1p(jnp.exp(-jnp.abs(z)))


def _mod_kernel(c_ref, w_ref, b_ref, o_ref):
    s = _silu(c_ref[...])
    o_ref[...] = jnp.dot(s.astype(BF16), w_ref[...].astype(BF16),
                         preferred_element_type=F32) + b_ref[...]


def _modulation(cc, w, b):
    rows, d = cc.shape
    n = w.shape[1]
    tn = 512 if n % 512 == 0 else n
    return pl.pallas_call(
        _mod_kernel,
        grid=(n // tn,),
        in_specs=[pl.BlockSpec((rows, d), lambda j: (0, 0)),
                  pl.BlockSpec((d, tn), lambda j: (0, j)),
                  pl.BlockSpec((1, tn), lambda j: (0, j))],
        out_specs=pl.BlockSpec((rows, tn), lambda j: (0, j)),
        out_shape=jax.ShapeDtypeStruct((rows, n), F32),
        compiler_params=pltpu.CompilerParams(
            dimension_semantics=("arbitrary",), vmem_limit_bytes=VMEM_LIMIT),
        name="adaln_modulation",
    )(cc, w, b.reshape(1, n))


def _store_projection(o_ref, res, j, silu_tiles):
    lo, hi = silu_tiles
    z = res.astype(o_ref.dtype)
    if hi <= lo:
        o_ref[...] = z
        return
    in_range = jnp.logical_and(j >= lo, j < hi)
    o_ref[...] = jnp.where(in_range, z * (0.5 * jnp.tanh(0.5 * z) + 0.5), z)


def _inproj_kernel(x_ref, shift_ref, scale_ref, nw_ref, w_ref, o_ref, h_ref, *, silu_tiles):
    @pl.when(pl.program_id(1) == 0)
    def _():
        rows = min(PROLOGUE_ROWS, x_ref.shape[0])
        gain = nw_ref[...] * (1.0 + scale_ref[0])

        def slab(s, carry):
            sl = pl.ds(pl.multiple_of(s * rows, rows), rows)
            x = x_ref[sl, :]
            rs = lax.rsqrt(jnp.mean(x * x, axis=-1, keepdims=True) + EPS)
            h_ref[sl, :] = (x_ref[sl, :] * rs * gain + shift_ref[0]).astype(BF16)
            return carry

        lax.fori_loop(0, x_ref.shape[0] // rows, slab, 0, unroll=4)

    res = jnp.dot(h_ref[...], w_ref[...].astype(BF16), preferred_element_type=F32)
    _store_projection(o_ref, res, pl.program_id(1), silu_tiles)


def _in_projection(x2d, mod3, norm_w, w_bf16, row_of_tile, wcol_of_tile, n_out, tm, tn, silu_tiles):
    m, d = x2d.shape
    return pl.pallas_call(
        functools.partial(_inproj_kernel, silu_tiles=silu_tiles),
        grid=(m // tm, n_out // tn),
        in_specs=[pl.BlockSpec((tm, d), lambda i, j: (i, 0)),
                  pl.BlockSpec((1, 1, d), lambda i, j: (row_of_tile(i), 0, 0)),
                  pl.BlockSpec((1, 1, d), lambda i, j: (row_of_tile(i), 0, 1)),
                  pl.BlockSpec((1, d), lambda i, j: (0, 0)),
                  pl.BlockSpec((d, tn), lambda i, j: (0, wcol_of_tile(j)))],
        out_specs=pl.BlockSpec((tm, tn), lambda i, j: (i, j)),
        out_shape=jax.ShapeDtypeStruct((m, n_out), BF16),
        scratch_shapes=[pltpu.VMEM((tm, d), BF16)],
        compiler_params=pltpu.CompilerParams(
            dimension_semantics=("parallel", "arbitrary"),
            vmem_limit_bytes=VMEM_LIMIT),
        name="in_projection",
    )(x2d, mod3, mod3, norm_w.reshape(1, d), w_bf16)


def _matmul_kernel(h_ref, w_ref, o_ref, *, silu_tiles):
    res = jnp.dot(h_ref[...], w_ref[...].astype(BF16), preferred_element_type=F32)
    _store_projection(o_ref, res, pl.program_id(1), silu_tiles)


def _projection_from_h(h2d, w_f32, wcol_of_tile, n_out, tm, tn, silu_tiles):
    m, d = h2d.shape
    return pl.pallas_call(
        functools.partial(_matmul_kernel, silu_tiles=silu_tiles),
        grid=(m // tm, n_out // tn),
        in_specs=[pl.BlockSpec((tm, d), lambda i, j: (i, 0)),
                  pl.BlockSpec((d, tn), lambda i, j: (0, wcol_of_tile(j)))],
        out_specs=pl.BlockSpec((tm, tn), lambda i, j: (i, j)),
        out_shape=jax.ShapeDtypeStruct((m, n_out), BF16),
        compiler_params=pltpu.CompilerParams(
            dimension_semantics=("parallel", "arbitrary"),
            vmem_limit_bytes=VMEM_LIMIT),
        name="projection_from_h",
    )(h2d, w_f32)


def _normalise_kernel(x_ref, shift_ref, scale_ref, nw_ref, hr_ref, hc_ref, *, slab):
    _, n_r, n_w, d = x_ref.shape
    rr = 2 * SUBLANES
    lanes = [slice(s * slab, (s + 1) * slab) for s in range(d // slab)]

    def row_group(g, carry):
        rows = pl.ds(pl.multiple_of(g * rr, rr), rr)
        sq = jnp.zeros((rr, n_w, slab), F32)
        for sl in lanes:
            xs = x_ref[0, rows, :, sl]
            sq = sq + xs * xs
        rs = lax.rsqrt(jnp.sum(sq, axis=-1, keepdims=True) * (1.0 / d) + EPS)
        for sl in lanes:
            y = x_ref[0, rows, :, sl] * rs * nw_ref[:, sl]
            y = y * (1.0 + scale_ref[0][:, sl]) + shift_ref[0][:, sl]
            hr_ref[0, rows, :, sl] = y.astype(BF16)
            hc_ref[0, :, rows, sl] = jnp.transpose(y, (1, 0, 2)).astype(BF16)
        return carry

    lax.fori_loop(0, n_r // rr, row_group, 0)


def _normalise(x4d, mod3, norm_w, wb):
    bsz, n_r, n_w, d = x4d.shape
    return pl.pallas_call(
        functools.partial(_normalise_kernel, slab=2 * LANES),
        grid=(bsz, n_w // wb),
        in_specs=[pl.BlockSpec((1, n_r, wb, d), lambda b, w: (b, 0, w, 0)),
                  pl.BlockSpec((1, 1, d), lambda b, w: (b, 0, 0)),
                  pl.BlockSpec((1, 1, d), lambda b, w: (b, 0, 1)),
                  pl.BlockSpec((1, d), lambda b, w: (0, 0))],
        out_specs=[pl.BlockSpec((1, n_r, wb, d), lambda b, w: (b, 0, w, 0)),
                   pl.BlockSpec((1, wb, n_r, d), lambda b, w: (b, w, 0, 0))],
        out_shape=[jax.ShapeDtypeStruct((bsz, n_r, n_w, d), BF16),
                   jax.ShapeDtypeStruct((bsz, n_w, n_r, d), BF16)],
        compiler_params=pltpu.CompilerParams(
            dimension_semantics=("parallel", "parallel"), vmem_limit_bytes=VMEM_LIMIT),
        name="normalise_rows",
    )(x4d, mod3, mod3, norm_w.reshape(1, d))


def _group_scan(a, b, reverse):
    row = lax.broadcasted_iota(jnp.int32, a.shape, 1)
    for k in (1, 2, 4):
        if reverse:
            a_sh = pltpu.roll(a, SUBLANES - k, axis=1)
            b_sh = pltpu.roll(b, SUBLANES - k, axis=1)
            m = row < SUBLANES - k
        else:
            a_sh = pltpu.roll(a, k, axis=1)
            b_sh = pltpu.roll(b, k, axis=1)
            m = row >= k
        b = jnp.where(m, a * b_sh + b, b)
        a = jnp.where(m, a * a_sh, a)
    return a, b


def _sqrt_unit(x):
    return jnp.where(x > 0.0, x * lax.rsqrt(x), 0.0)


def _segment_scan(a, b, carry, reverse, scr_ref):
    tt = a.shape[0]
    seg = tt // SUBLANES
    pitch = seg + SEG_PAD
    for s in range(SUBLANES):
        rows = pl.ds(pitch * s, seg, stride=1)
        scr_ref[0, rows, :] = a[seg * s:seg * (s + 1), :]
        scr_ref[1, rows, :] = b[seg * s:seg * (s + 1), :]
    hs, ps = [None] * seg, [None] * seg
    h = p = None
    for j in (range(seg - 1, -1, -1) if reverse else range(seg)):
        step = pl.ds(j, SUBLANES, stride=pitch)
        a_j = scr_ref[0, step, :]
        b_j = scr_ref[1, step, :]
        h = b_j if h is None else a_j * h + b_j
        p = a_j if p is None else a_j * p
        hs[j], ps[j] = h, p
    g3, e3 = _group_scan(p[None], h[None], reverse)
    end = e3[0] + g3[0] * carry
    row = lax.broadcasted_iota(jnp.int32, end.shape, 0)
    if reverse:
        enter = jnp.where(row == SUBLANES - 1, carry, pltpu.roll(end, SUBLANES - 1, axis=0))
        carry_out = end[0:1, :]
    else:
        enter = jnp.where(row == 0, carry, pltpu.roll(end, 1, axis=0))
        carry_out = end[SUBLANES - 1:SUBLANES, :]
    for j in range(seg):
        scr_ref[2, pl.ds(j, SUBLANES, stride=pitch), :] = hs[j] + ps[j] * enter
    h_time = [scr_ref[2, pl.ds(pitch * s, seg, stride=1), :] for s in range(SUBLANES)]
    return jnp.concatenate(h_time, axis=0), carry_out


def _lru_block(u, wg, bg, sp, carry, reverse, scr_ref):
    g = jnp.dot(u.astype(BF16), wg, preferred_element_type=F32) + bg
    r = _sigmoid(g[:, :LANES])
    i = _sigmoid(g[:, LANES:])
    a = jnp.exp2(r * sp)
    b = _sqrt_unit(1.0 - a * a) * (i * u)
    return _segment_scan(a, b, carry, reverse, scr_ref)


def _rglru_kernel(xa_ref, ga_ref, xc_ref, cw_ref, cb_ref, wgf_ref, bgf_ref, wgb_ref, bgb_ref,
                  lam_ref, wcast_ref, o_ref, wcast_out_ref, xf_ref, xcf_ref, u_ref, hf_ref, scr_ref,
                  *, tt):
    wcast_out_ref[...] = wcast_ref[...].astype(BF16)
    t_len = xa_ref.shape[1]
    tc_len = xc_ref.shape[1]
    nblk = xa_ref.shape[2] // LANES
    lanes = [slice(k * LANES, (k + 1) * LANES) for k in range(nblk)]
    zeros = jnp.zeros((SUBLANES, LANES), F32)
    for k, sl in enumerate(lanes):
        xf_ref[k, 0:SUBLANES, :] = zeros
        xf_ref[k, SUBLANES:SUBLANES + t_len, :] = xa_ref[0, :, sl].astype(F32)
        xf_ref[k, SUBLANES + t_len:, :] = zeros
        xcf_ref[k, 0:SUBLANES, :] = zeros
        xcf_ref[k, SUBLANES:SUBLANES + tc_len, :] = xc_ref[0, :, sl].astype(F32)
        xcf_ref[k, SUBLANES + tc_len:, :] = zeros

    sp_f = _softplus(-lam_ref[0:1, :]) * (LRU_C * NEG_LOG2E)
    sp_b = _softplus(-lam_ref[1:2, :]) * (LRU_C * NEG_LOG2E)

    def conv(src_ref, k, t0):
        u = cb_ref[:, lanes[k]]
        for tap in range(cw_ref.shape[0]):
            off = SUBLANES - CONV_PAD_L + tap
            if off % SUBLANES == 0:
                start = t0 + off
                rows = pl.ds(start if isinstance(start, int) else pl.multiple_of(start, SUBLANES), tt)
            else:
                rows = pl.ds(t0 + off, tt, stride=1)
            u = u + src_ref[k, rows, :] * cw_ref[tap:tap + 1, lanes[k]]
        return u

    def block(u, k, carry, reverse, slot):
        scr = scr_ref.at[k * TILES_PER_TRIP + slot]
        if reverse:
            return _lru_block(u, wgb_ref[k], bgb_ref[k], sp_b[:, lanes[k]], carry, True, scr)
        return _lru_block(u, wgf_ref[k], bgf_ref[k], sp_f[:, lanes[k]], carry, False, scr)

    n_lat = t_len // tt
    n_ctx = tc_len // tt
    assert n_lat % TILES_PER_TRIP == 0
    carry0 = tuple(jnp.zeros((1, LANES), F32) for _ in lanes)

    def tile_start(trip, slot, n_tiles, reverse):
        idx = trip * TILES_PER_TRIP + slot
        return pl.multiple_of((n_tiles - 1 - idx if reverse else idx) * tt, tt)

    def ctx_pass(reverse):
        carry = carry0
        for idx in range(n_ctx):
            t0 = (n_ctx - 1 - idx if reverse else idx) * tt
            carry = tuple(block(conv(xcf_ref, k, t0), k, carry[k], reverse, idx % TILES_PER_TRIP)[1]
                          for k in range(nblk))
        return carry

    def lat_f(trip, carry):
        for slot in range(TILES_PER_TRIP):
            t0 = tile_start(trip, slot, n_lat, False)
            rows = pl.ds(t0, tt)
            out = []
            for k, sl in enumerate(lanes):
                u = conv(xf_ref, k, t0)
                u_ref[rows, sl] = u
                h, c_out = block(u, k, carry[k], False, slot)
                hf_ref[rows, sl] = h
                out.append(c_out)
            carry = tuple(out)
        return carry

    lax.fori_loop(0, n_lat // TILES_PER_TRIP, lat_f, ctx_pass(False))

    def lat_b(trip, carry):
        for slot in range(TILES_PER_TRIP):
            rows = pl.ds(tile_start(trip, slot, n_lat, True), tt)
            out = []
            for k, sl in enumerate(lanes):
                h, c_out = block(u_ref[rows, sl], k, carry[k], True, slot)
                gate = ga_ref[0, rows, sl].astype(F32)
                o_ref[0, rows, sl] = ((hf_ref[rows, sl] + h) * gate).astype(o_ref.dtype)
                out.append(c_out)
            carry = tuple(out)
        return carry

    lax.fori_loop(0, n_lat // TILES_PER_TRIP, lat_b, ctx_pass(True))


def _rglru(p_lat, p_ctx, conv_w, conv_b, wg_f, bg_f, wg_b, bg_b, lam, w_cast, d_a, cb, tt):
    bsz, t_len, _ = p_lat.shape
    tc_len = p_ctx.shape[1]
    nblk = cb // LANES
    ncb = d_a // cb
    wrows, wcols = w_cast.shape
    assert wrows % (bsz * ncb) == 0
    wr = wrows // (bsz * ncb)
    wspec = pl.BlockSpec((wr, wcols), lambda b, c: (b * ncb + c, 0))
    kern = functools.partial(_rglru_kernel, tt=tt)
    return pl.pallas_call(
        kern,
        grid=(bsz, ncb),
        in_specs=[pl.BlockSpec((1, t_len, cb), lambda b, c: (b, 0, c)),
                  pl.BlockSpec((1, t_len, cb), lambda b, c: (b, 0, ncb + c)),
                  pl.BlockSpec((1, tc_len, cb), lambda b, c: (b, 0, c)),
                  pl.BlockSpec((conv_w.shape[0], cb), lambda b, c: (0, c)),
                  pl.BlockSpec((1, cb), lambda b, c: (0, c)),
                  pl.BlockSpec((nblk, LANES, 2 * LANES), lambda b, c: (c, 0, 0)),
                  pl.BlockSpec((nblk, 1, 2 * LANES), lambda b, c: (c, 0, 0)),
                  pl.BlockSpec((nblk, LANES, 2 * LANES), lambda b, c: (c, 0, 0)),
                  pl.BlockSpec((nblk, 1, 2 * LANES), lambda b, c: (c, 0, 0)),
                  pl.BlockSpec((2, cb), lambda b, c: (0, c)),
                  wspec],
        out_specs=[pl.BlockSpec((1, t_len, cb), lambda b, c: (b, 0, c)), wspec],
        out_shape=[jax.ShapeDtypeStruct((bsz, t_len, d_a), BF16),
                   jax.ShapeDtypeStruct((wrows, wcols), BF16)],
        scratch_shapes=[pltpu.VMEM((nblk, t_len + 2 * SUBLANES, LANES), F32),
                        pltpu.VMEM((nblk, tc_len + 2 * SUBLANES, LANES), F32),
                        pltpu.VMEM((t_len, cb), F32),
                        pltpu.VMEM((t_len, cb), F32),
                        pltpu.VMEM((nblk * TILES_PER_TRIP, 3, tt + SUBLANES * SEG_PAD, LANES),
                                   F32)],
        compiler_params=pltpu.CompilerParams(
            dimension_semantics=("parallel", "parallel"),
            vmem_limit_bytes=VMEM_LIMIT),
        name="rglru",
    )(p_lat, p_lat, p_ctx, conv_w, conv_b.reshape(1, d_a), wg_f, bg_f, wg_b, bg_b, lam, w_cast)


_NT = (((1,), (1,)), ((), ()))
_TN = (((0,), (0,)), ((), ()))


class _HgrnDir:
    def __init__(self, d, q_ref, f_ref, v_ref, lb, st_ref, o_ref, scratch, reverse, n_heads, hd,
                 sub):
        self.d, self.q_ref, self.f_ref, self.v_ref, self.lb = d, q_ref, f_ref, v_ref, lb
        self.sub = sub
        self.st_ref, self.o_ref, self.reverse, self.n_heads, self.hd = st_ref, o_ref, reverse, n_heads, hd
        (self.hl_ref, self.kk_ref, self.cum_ref, self.qd_ref, self.kd_ref, self.qin_ref,
         self.kout_ref, self.dec_ref, self.dect_ref, self.sc_ref) = scratch
        self.c_len = q_ref.shape[2]
        r_i = lax.broadcasted_iota(jnp.int32, (self.c_len, self.c_len), 0)
        c_i = lax.broadcasted_iota(jnp.int32, (self.c_len, self.c_len), 1)
        self.tri = (r_i <= c_i) if reverse else (r_i >= c_i)

    def _slabs(self, width):
        total = self.n_heads * self.hd
        return [slice(s, s + width) for s in range(0, total, width)]


    def _gate_slab(self, sl):
        d, c_len = self.d, self.c_len
        lb = self.lb[:, sl]
        f = lb + (1.0 - lb) * _sigmoid(self.f_ref[0, self.sub, :, sl].astype(F32))
        logf = jnp.log2(f)
        self.kk_ref[d, :, sl] = (1.0 - f).astype(BF16)
        hi = logf.astype(BF16)
        self.hl_ref[d, 0:c_len, sl] = hi
        self.hl_ref[d, c_len:2 * c_len, sl] = (logf - hi.astype(F32)).astype(BF16)

    def _cumulate(self):
        tri_b = self.tri.astype(BF16)
        tri2 = jnp.concatenate([tri_b, tri_b], axis=1)
        self.cum_ref[self.d] = jnp.dot(tri2, self.hl_ref[self.d], preferred_element_type=F32)

    def _decay_slab(self, sl):
        d, c_len = self.d, self.c_len
        half = c_len // 2
        row_last = 0 if self.reverse else c_len - 1
        row_ref = half if self.reverse else half - 1
        cum = self.cum_ref[d, :, sl]
        last = cum[row_last:row_last + 1, :]
        ref = cum[row_ref:row_ref + 1, :]
        kk = self.kk_ref[d, :, sl]
        qs = self.q_ref[0, self.sub, :, sl]
        self.qd_ref[d, :, sl] = qs * jnp.exp2(jnp.minimum(cum - ref, EXP2_CLAMP)).astype(BF16)
        self.kd_ref[d, :, sl] = kk * jnp.exp2(jnp.minimum(ref - cum, EXP2_CLAMP)).astype(BF16)
        self.qin_ref[d, :, sl] = qs * jnp.exp2(cum).astype(BF16)
        self.kout_ref[d, :, sl] = kk * jnp.exp2(last - cum).astype(BF16)
        dec = jnp.exp2(last)
        for i in range((sl.stop - sl.start) // self.hd):
            h = sl.start // self.hd + i
            self.dec_ref[d, h:h + 1, :] = dec[:, i * self.hd:(i + 1) * self.hd]

    def _decay_columns(self):
        self.dect_ref[self.d] = jnp.transpose(self.dec_ref[self.d])

    def _head(self, h):
        return slice(h * self.hd, (h + 1) * self.hd)

    def _scores(self, h):
        d, sl = self.d, self._head(h)
        s = lax.dot_general(self.qd_ref[d, :, sl], self.kd_ref[d, :, sl], _NT,
                            preferred_element_type=F32)
        self.sc_ref[d, h] = jnp.where(self.tri, s, 0.0).astype(BF16)

    def _output(self, h):
        d, sl = self.d, self._head(h)
        o = (jnp.dot(self.sc_ref[d, h], self.v_ref[0, self.sub, :, sl], preferred_element_type=F32)
             + jnp.dot(self.qin_ref[d, :, sl], self.st_ref[h].astype(BF16),
                       preferred_element_type=F32))
        self.o_ref[0, self.sub, :, sl] = o.astype(self.o_ref.dtype)

    def _update(self, h):
        d, sl = self.d, self._head(h)
        decay = jnp.broadcast_to(self.dect_ref[d, :, h:h + 1], (self.hd, self.hd))
        self.st_ref[h] = (self.st_ref[h] * decay
                          + lax.dot_general(self.kout_ref[d, :, sl], self.v_ref[0, self.sub, :, sl], _TN,
                                            preferred_element_type=F32))

    def stages(self):
        heads = range(self.n_heads)
        slabs = self._slabs(2 * LANES)
        out = [[functools.partial(self._gate_slab, sl) for sl in slabs], [self._cumulate],
               [functools.partial(self._decay_slab, sl) for sl in slabs], [self._decay_columns]]
        if self.o_ref is not None:
            out += [[functools.partial(self._scores, h) for h in heads],
                    [functools.partial(self._output, h) for h in heads]]
        return out + [[functools.partial(self._update, h) for h in heads]]


def _lower_bounds(logits_ref, layer):
    out = []
    for d in range(2):
        rows = [logits_ref[d, l:l + 1, :] for l in range(logits_ref.shape[1])]
        m = functools.reduce(jnp.maximum, rows)
        e = [jnp.exp(r - m) for r in rows]
        out.append(sum(e[:layer + 1]) / sum(e))
    return out


def _hgrn_kernel(*refs, n_heads, hd, has_init, emit_o, emit_state):
    qf_ref, ff_ref, vf_ref, qb_ref, fb_ref, vb_ref, lg_ref = refs[:7]
    pos = 7
    if has_init:
        s0f_ref, s0b_ref = refs[pos:pos + 2]
        pos += 2
    if emit_o:
        of_ref, ob_ref = refs[pos:pos + 2]
        pos += 2
    if emit_state:
        sof_ref, sob_ref = refs[pos:pos + 2]
        pos += 2
    sf_ref, sb_ref = refs[pos:pos + 2]
    scratch = refs[pos + 2:]

    j = pl.program_id(1)
    dec_ref = scratch[7]
    assert n_heads <= dec_ref.shape[1]

    @pl.when(j == 0)
    def _():
        if has_init:
            sf_ref[...] = s0f_ref[0]
            sb_ref[...] = s0b_ref[0]
        else:
            sf_ref[...] = jnp.zeros_like(sf_ref)
            sb_ref[...] = jnp.zeros_like(sb_ref)
        dec_ref[...] = jnp.zeros_like(dec_ref)

    lb_f, lb_b = _lower_bounds(lg_ref, 0)
    n_sub = qf_ref.shape[1]
    for k in range(n_sub):
        dirs = [_HgrnDir(0, qf_ref, ff_ref, vf_ref, lb_f, sf_ref, of_ref if emit_o else None,
                         scratch, False, n_heads, hd, k),
                _HgrnDir(1, qb_ref, fb_ref, vb_ref, lb_b, sb_ref, ob_ref if emit_o else None,
                         scratch, True, n_heads, hd, n_sub - 1 - k)]
        for stage_f, stage_b in zip(dirs[0].stages(), dirs[1].stages()):
            for unit in stage_f + stage_b:
                unit()

    if emit_state:
        @pl.when(j == pl.num_programs(1) - 1)
        def _():
            sof_ref[0] = sf_ref[...]
            sob_ref[0] = sb_ref[...]


def _hgrn(p_view, logits, d_b, n_heads, first_group, init_states, emit_o):
    bsz, n_chunks = p_view.shape[:2]
    hd = d_b // n_heads
    has_init = init_states is not None
    emit_state = not emit_o

    cps = CHUNKS_PER_STEP if n_chunks % CHUNKS_PER_STEP == 0 else 1
    n_steps = n_chunks // cps

    def chunk(j, reverse):
        return n_steps - 1 - j if reverse else j

    def spec(group, reverse):
        def imap(b, j):
            return (b, chunk(j, reverse), 0, first_group + group)
        return pl.BlockSpec((1, cps, CHUNK, d_b), imap)

    in_specs = [spec(0, False), spec(1, False), spec(3, False),
                spec(0, True), spec(2, True), spec(3, True),
                pl.BlockSpec(logits.shape, lambda b, j: (0, 0, 0))]
    args = [p_view] * 6 + [logits]
    state_spec = pl.BlockSpec((1, n_heads, hd, hd), lambda b, j: (b, 0, 0, 0))
    state_shape = jax.ShapeDtypeStruct((bsz, n_heads, hd, hd), F32)
    if has_init:
        in_specs += [state_spec, state_spec]
        args += list(init_states)
    out_specs, out_shape = [], []
    if emit_o:
        o_shape = jax.ShapeDtypeStruct((bsz, n_chunks, CHUNK, d_b), BF16)
        out_specs += [pl.BlockSpec((1, cps, CHUNK, d_b), lambda b, j: (b, chunk(j, False), 0, 0)),
                      pl.BlockSpec((1, cps, CHUNK, d_b), lambda b, j: (b, chunk(j, True), 0, 0))]
        out_shape += [o_shape, o_shape]
    if emit_state:
        out_specs += [state_spec, state_spec]
        out_shape += [state_shape, state_shape]
    kern = functools.partial(_hgrn_kernel, n_heads=n_heads, hd=hd, has_init=has_init,
                             emit_o=emit_o, emit_state=emit_state)
    return pl.pallas_call(
        kern,
        grid=(bsz, n_steps),
        in_specs=in_specs,
        out_specs=out_specs,
        out_shape=out_shape,
        scratch_shapes=[pltpu.VMEM((n_heads, hd, hd), F32),
                        pltpu.VMEM((n_heads, hd, hd), F32),
                        pltpu.VMEM((2, 2 * CHUNK, d_b), BF16),
                        pltpu.VMEM((2, CHUNK, d_b), BF16),
                        pltpu.VMEM((2, CHUNK, d_b), F32),
                        pltpu.VMEM((2, CHUNK, d_b), BF16),
                        pltpu.VMEM((2, CHUNK, d_b), BF16),
                        pltpu.VMEM((2, CHUNK, d_b), BF16),
                        pltpu.VMEM((2, CHUNK, d_b), BF16),
                        pltpu.VMEM((2, hd, hd), F32),
                        pltpu.VMEM((2, hd, hd), F32),
                        pltpu.VMEM((2, n_heads, CHUNK, CHUNK), BF16)],
        compiler_params=pltpu.CompilerParams(
            dimension_semantics=("parallel", "arbitrary"),
            vmem_limit_bytes=VMEM_LIMIT),
        name="hgrn2_latent" if emit_o else "hgrn2_context",
    )(*args)


def _outproj_kernel(ya_ref, of_ref, ob_ref, gb_ref, x_ref, gate_ref, hnw_ref, fnw_ref, w_ref,
                    o_ref, y_ref, *, n_heads, hd):
    _, rb, wb, d_a = ya_ref.shape
    rows = rb * wb
    d = x_ref.shape[3]
    y_ref[:, 0:d_a] = ya_ref[0].reshape(rows, d_a)
    hnw = hnw_ref[...]
    for h in range(n_heads):
        sl = slice(h * hd, (h + 1) * hd)
        o = of_ref[0, :, :, sl].astype(F32) + ob_ref[0, :, :, sl].astype(F32)
        ms = jnp.mean(o * o, axis=-1, keepdims=True)
        on = jnp.transpose(o * lax.rsqrt(ms + EPS) * hnw, (1, 0, 2))
        yb = on * gb_ref[0, :, :, sl].astype(F32)
        y_ref[:, d_a + h * hd:d_a + (h + 1) * hd] = yb.reshape(rows, hd).astype(BF16)
    acc = jnp.dot(y_ref[...], w_ref[...], preferred_element_type=F32)
    z = x_ref[0].reshape(rows, d) + gate_ref[0] * acc
    ms = jnp.mean(z * z, axis=-1, keepdims=True)
    o_ref[0] = (z * lax.rsqrt(ms + EPS) * fnw_ref[...]).reshape(rb, wb, d)


def _out_projection(ya4, of4, ob4, pa4, gb_block, x4, mod3, hnw, fnw, w_bf16, n_heads, rb, wb):
    bsz, n_r, n_w, d = x4.shape
    d_a = ya4.shape[3]
    d_b = of4.shape[3]
    hd = d_b // n_heads
    kern = functools.partial(_outproj_kernel, n_heads=n_heads, hd=hd)

    def raster(c, col=0):
        return pl.BlockSpec((1, rb, wb, c), lambda b, r, w: (b, r, w, col))

    def colmajor(c):
        return pl.BlockSpec((1, wb, rb, c), lambda b, r, w: (b, w, r, 0))

    return pl.pallas_call(
        kern,
        grid=(bsz, n_r // rb, n_w // wb),
        in_specs=[raster(d_a), colmajor(d_b), colmajor(d_b), raster(d_b, gb_block), raster(d),
                  pl.BlockSpec((1, 1, d), lambda b, r, w: (b, 0, 2)),
                  pl.BlockSpec((1, hd), lambda b, r, w: (0, 0)),
                  pl.BlockSpec((1, d), lambda b, r, w: (0, 0)),
                  pl.BlockSpec((d_a + d_b, d), lambda b, r, w: (0, 0),
                               pipeline_mode=pl.Buffered(1))],
        out_specs=raster(d),
        out_shape=jax.ShapeDtypeStruct((bsz, n_r, n_w, d), F32),
        scratch_shapes=[pltpu.VMEM((rb * wb, d_a + d_b), BF16)],
        compiler_params=pltpu.CompilerParams(
            dimension_semantics=("parallel", "parallel", "parallel"),
            vmem_limit_bytes=VMEM_LIMIT),
        name="out_projection",
    )(ya4, of4, ob4, pa4, x4, mod3, hnw.reshape(1, hd), fnw.reshape(1, d), w_bf16)


def kernel(x, c, ctx, c_ctx, ada_w, ada_b, norm_w, w_in, conv_w, conv_b, lru_wr, lru_br, lru_wi,
           lru_bi, lru_lambda, hgrn_lb_logits, hgrn_norm_w, w_out, final_norm_w):
    bsz, t_len, d = x.shape
    tc_len = ctx.shape[1]
    assert ada_w.shape[0] == 1, "single-layer stack only"
    d_a = conv_w.shape[2]
    d_b = hgrn_lb_logits.shape[2]
    hd = hgrn_norm_w.shape[1]
    n_heads = d_b // hd
    n_blocks_a = lru_wr.shape[2]
    n_cols = w_in.shape[2]
    assert t_len == GRID_W * CHUNK and tc_len % CHUNK == 0
    assert d_a // n_blocks_a == LANES and hd == LANES
    assert (2 * d_a) % d_b == 0 and n_cols == 2 * d_a + 5 * d_b
    first_b_group = (2 * d_a) // d_b
    n_rows_grid = t_len // GRID_W

    n_rows = -(-(bsz + 1) // SUBLANES) * SUBLANES
    cc = jnp.zeros((n_rows, d), F32).at[:bsz].set(c).at[bsz].set(c_ctx)
    mod3 = _modulation(cc, ada_w[0], ada_b[0]).reshape(n_rows, 1, 3 * d)

    w_in_b = w_in[0]

    tm = 1024 if t_len % 1024 == 0 else t_len
    tn = 1024 if d_a % 1024 == 0 and d_b % 1024 == 0 else min(d_a, d_b)
    a_tiles = (2 * d_a) // tn
    b_tiles = (4 * d_b) // tn
    q_tiles = d_b // tn
    h_lat, h_col = _normalise(x.reshape(bsz, n_rows_grid, GRID_W, d), mod3, norm_w[0], 16)
    tm_h = 2 * tm if (bsz * t_len) % (2 * tm) == 0 else tm
    p_b = _projection_from_h(h_col.reshape(bsz * t_len, d), w_in_b, lambda j: j + a_tiles,
                             4 * d_b, tm_h, tn, (0, q_tiles))
    p_b = p_b.reshape(bsz, GRID_W, n_rows_grid, 4 * d_b)
    p_a = _projection_from_h(h_lat.reshape(bsz * t_len, d), w_in_b,
                             lambda j: jnp.where(j < a_tiles, j, j + b_tiles),
                             2 * d_a + d_b, tm_h, tn, (d_a // tn, (2 * d_a + d_b) // tn))
    xa_tiles = d_a // tn
    ctx_cols = d_a + 4 * d_b
    p_ctx = _in_projection(ctx.reshape(bsz * tc_len, d), mod3, norm_w[0], w_in_b,
                           lambda i: bsz, lambda j: jnp.where(j < xa_tiles, j, j + xa_tiles),
                           ctx_cols, bsz * tc_len, tn, (xa_tiles, xa_tiles + q_tiles))
    p_lat = p_a.reshape(bsz, t_len, 2 * d_a + d_b)
    p_ctx = p_ctx.reshape(bsz, tc_len, ctx_cols)

    def gate_w(dirn):
        return jnp.concatenate([lru_wr[0, dirn], lru_wi[0, dirn]], axis=-1).astype(BF16)

    def gate_b(dirn):
        return jnp.concatenate([lru_br[0, dirn].reshape(n_blocks_a, 1, LANES),
                                lru_bi[0, dirn].reshape(n_blocks_a, 1, LANES)], axis=-1)

    cb = 256 if d_a % 256 == 0 else LANES
    ya, w_out_b = _rglru(p_lat, p_ctx, conv_w[0], conv_b[0], gate_w(0), gate_b(0), gate_w(1),
                         gate_b(1), lru_lambda[0], w_out[0], d_a, cb, tt=256)

    assert d_a % d_b == 0
    states = _hgrn(p_ctx.reshape(bsz, tc_len // CHUNK, CHUNK, ctx_cols), hgrn_lb_logits, d_b,
                   n_heads, d_a // d_b, None, False)
    o_f, o_b = _hgrn(p_b, hgrn_lb_logits, d_b, n_heads, 0, states, True)

    grid4 = lambda z: z.reshape(bsz, n_rows_grid, GRID_W, z.shape[-1])
    out = _out_projection(grid4(ya), o_f, o_b, grid4(p_lat), first_b_group, grid4(x), mod3,
                          hgrn_norm_w[0], final_norm_w, w_out_b, n_heads, 16, 16)
    return out.reshape(bsz, t_len, d)
```

```python
import functools

import jax
import jax.numpy as jnp
from jax import lax
from jax.experimental import pallas as pl
from jax.experimental.pallas import tpu as pltpu

GRID_W = 64
CHUNK = 64
LRU_C = 8.0
EPS = 1e-6
CONV_PAD_L = 2
LANES = 128
SUBLANES = 8
EXP2_CLAMP = 115.0
NEG_LOG2E = -1.4426950408889634
VMEM_LIMIT = 56 * 1024 * 1024
PROLOGUE_ROWS = 16
SEG_PAD = 4
CHUNKS_PER_STEP = 2
TILES_PER_TRIP = 4

F32 = jnp.float32
BF16 = jnp.bfloat16


def _sigmoid(z):
    return 1.0 / (1.0 + jnp.exp2(z * NEG_LOG2E))


def _silu(z):
    return z * _sigmoid(z)


def _softplus(z):
    return jnp.maximum(z, 0.0) + jnp.log1p(jnp.exp(-jnp.abs(z)))


def _mod_kernel(c_ref, w_ref, b_ref, o_ref):
    s = _silu(c_ref[...])
    o_ref[...] = jnp.dot(s.astype(BF16), w_ref[...].astype(BF16),
                         preferred_element_type=F32) + b_ref[...]


def _modulation(cc, w, b):
    rows, d = cc.shape
    n = w.shape[1]
    tn = 512 if n % 512 == 0 else n
    return pl.pallas_call(
        _mod_kernel,
        grid=(n // tn,),
        in_specs=[pl.BlockSpec((rows, d), lambda j: (0, 0)),
                  pl.BlockSpec((d, tn), lambda j: (0, j)),
                  pl.BlockSpec((1, tn), lambda j: (0, j))],
        out_specs=pl.BlockSpec((rows, tn), lambda j: (0, j)),
        out_shape=jax.ShapeDtypeStruct((rows, n), F32),
        compiler_params=pltpu.CompilerParams(
            dimension_semantics=("arbitrary",), vmem_limit_bytes=VMEM_LIMIT),
        name="adaln_modulation",
    )(cc, w, b.reshape(1, n))


def _store_projection(o_ref, res, j, silu_tiles):
    lo, hi = silu_tiles
    z = res.astype(o_ref.dtype)
    if hi <= lo:
        o_ref[...] = z
        return
    in_range = jnp.logical_and(j >= lo, j < hi)
    o_ref[...] = jnp.where(in_range, z * (0.5 * jnp.tanh(0.5 * z) + 0.5), z)


def _inproj_kernel(x_ref, shift_ref, scale_ref, nw_ref, w_ref, o_ref, h_ref, *, silu_tiles):
    @pl.when(pl.program_id(1) == 0)
    def _():
        rows = min(PROLOGUE_ROWS, x_ref.shape[0])
        gain = nw_ref[...] * (1.0 + scale_ref[0])

        def slab(s, carry):
            sl = pl.ds(pl.multiple_of(s * rows, rows), rows)
            x = x_ref[sl, :]
            rs = lax.rsqrt(jnp.mean(x * x, axis=-1, keepdims=True) + EPS)
            h_ref[sl, :] = (x_ref[sl, :] * rs * gain + shift_ref[0]).astype(BF16)
            return carry

        lax.fori_loop(0, x_ref.shape[0] // rows, slab, 0, unroll=4)

    res = jnp.dot(h_ref[...], w_ref[...].astype(BF16), preferred_element_type=F32)
    _store_projection(o_ref, res, pl.program_id(1), silu_tiles)


def _in_projection(x2d, mod3, norm_w, w_bf16, row_of_tile, wcol_of_tile, n_out, tm, tn, silu_tiles):
    m, d = x2d.shape
    return pl.pallas_call(
        functools.partial(_inproj_kernel, silu_tiles=silu_tiles),
        grid=(m // tm, n_out // tn),
        in_specs=[pl.BlockSpec((tm, d), lambda i, j: (i, 0)),
                  pl.BlockSpec((1, 1, d), lambda i, j: (row_of_tile(i), 0, 0)),
                  pl.BlockSpec((1, 1, d), lambda i, j: (row_of_tile(i), 0, 1)),
                  pl.BlockSpec((1, d), lambda i, j: (0, 0)),
                  pl.BlockSpec((d, tn), lambda i, j: (0, wcol_of_tile(j)))],
        out_specs=pl.BlockSpec((tm, tn), lambda i, j: (i, j)),
        out_shape=jax.ShapeDtypeStruct((m, n_out), BF16),
        scratch_shapes=[pltpu.VMEM((tm, d), BF16)],
        compiler_params=pltpu.CompilerParams(
            dimension_semantics=("parallel", "arbitrary"),
            vmem_limit_bytes=VMEM_LIMIT),
        name="in_projection",
    )(x2d, mod3, mod3, norm_w.reshape(1, d), w_bf16)


def _matmul_kernel(h_ref, w_ref, o_ref, *, silu_tiles):
    res = jnp.dot(h_ref[...], w_ref[...].astype(BF16), preferred_element_type=F32)
    _store_projection(o_ref, res, pl.program_id(1), silu_tiles)


def _projection_from_h(h2d, w_f32, wcol_of_tile, n_out, tm, tn, silu_tiles):
    m, d = h2d.shape
    return pl.pallas_call(
        functools.partial(_matmul_kernel, silu_tiles=silu_tiles),
        grid=(m // tm, n_out // tn),
        in_specs=[pl.BlockSpec((tm, d), lambda i, j: (i, 0)),
                  pl.BlockSpec((d, tn), lambda i, j: (0, wcol_of_tile(j)))],
        out_specs=pl.BlockSpec((tm, tn), lambda i, j: (i, j)),
        out_shape=jax.ShapeDtypeStruct((m, n_out), BF16),
        compiler_params=pltpu.CompilerParams(
            dimension_semantics=("parallel", "arbitrary"),
            vmem_limit_bytes=VMEM_LIMIT),
        name="projection_from_h",
    )(h2d, w_f32)


def _normalise_kernel(x_ref, shift_ref, scale_ref, nw_ref, hr_ref, hc_ref, *, slab):
    _, n_r, n_w, d = x_ref.shape
    rr = 2 * SUBLANES
    lanes = [slice(s * slab, (s + 1) * slab) for s in range(d // slab)]

    def row_group(g, carry):
        rows = pl.ds(pl.multiple_of(g * rr, rr), rr)
        sq = jnp.zeros((rr, n_w, slab), F32)
        for sl in lanes:
            xs = x_ref[0, rows, :, sl]
            sq = sq + xs * xs
        rs = lax.rsqrt(jnp.sum(sq, axis=-1, keepdims=True) * (1.0 / d) + EPS)
        for sl in lanes:
            y = x_ref[0, rows, :, sl] * rs * nw_ref[:, sl]
            y = y * (1.0 + scale_ref[0][:, sl]) + shift_ref[0][:, sl]
            hr_ref[0, rows, :, sl] = y.astype(BF16)
            hc_ref[0, :, rows, sl] = jnp.transpose(y, (1, 0, 2)).astype(BF16)
        return carry

    lax.fori_loop(0, n_r // rr, row_group, 0)


def _normalise(x4d, mod3, norm_w, wb):
    bsz, n_r, n_w, d = x4d.shape
    return pl.pallas_call(
        functools.partial(_normalise_kernel, slab=2 * LANES),
        grid=(bsz, n_w // wb),
        in_specs=[pl.BlockSpec((1, n_r, wb, d), lambda b, w: (b, 0, w, 0)),
                  pl.BlockSpec((1, 1, d), lambda b, w: (b, 0, 0)),
                  pl.BlockSpec((1, 1, d), lambda b, w: (b, 0, 1)),
                  pl.BlockSpec((1, d), lambda b, w: (0, 0))],
        out_specs=[pl.BlockSpec((1, n_r, wb, d), lambda b, w: (b, 0, w, 0)),
                   pl.BlockSpec((1, wb, n_r, d), lambda b, w: (b, w, 0, 0))],
        out_shape=[jax.ShapeDtypeStruct((bsz, n_r, n_w, d), BF16),
                   jax.ShapeDtypeStruct((bsz, n_w, n_r, d), BF16)],
        compiler_params=pltpu.CompilerParams(
            dimension_semantics=("parallel", "parallel"), vmem_limit_bytes=VMEM_LIMIT),
        name="normalise_rows",
    )(x4d, mod3, mod3, norm_w.reshape(1, d))


def _group_scan(a, b, reverse):
    row = lax.broadcasted_iota(jnp.int32, a.shape, 1)
    for k in (1, 2, 4):
        if reverse:
            a_sh = pltpu.roll(a, SUBLANES - k, axis=1)
            b_sh = pltpu.roll(b, SUBLANES - k, axis=1)
            m = row < SUBLANES - k
        else:
            a_sh = pltpu.roll(a, k, axis=1)
            b_sh = pltpu.roll(b, k, axis=1)
            m = row >= k
        b = jnp.where(m, a * b_sh + b, b)
        a = jnp.where(m, a * a_sh, a)
    return a, b


def _sqrt_unit(x):
    return jnp.where(x > 0.0, x * lax.rsqrt(x), 0.0)


def _segment_scan(a, b, carry, reverse, scr_ref):
    tt = a.shape[0]
    seg = tt // SUBLANES
    pitch = seg + SEG_PAD
    for s in range(SUBLANES):
        rows = pl.ds(pitch * s, seg, stride=1)
        scr_ref[0, rows, :] = a[seg * s:seg * (s + 1), :]
        scr_ref[1, rows, :] = b[seg * s:seg * (s + 1), :]
    hs, ps = [None] * seg, [None] * seg
    h = p = None
    for j in (range(seg - 1, -1, -1) if reverse else range(seg)):
        step = pl.ds(j, SUBLANES, stride=pitch)
        a_j = scr_ref[0, step, :]
        b_j = scr_ref[1, step, :]
        h = b_j if h is None else a_j * h + b_j
        p = a_j if p is None else a_j * p
        hs[j], ps[j] = h, p
    g3, e3 = _group_scan(p[None], h[None], reverse)
    end = e3[0] + g3[0] * carry
    row = lax.broadcasted_iota(jnp.int32, end.shape, 0)
    if reverse:
        enter = jnp.where(row == SUBLANES - 1, carry, pltpu.roll(end, SUBLANES - 1, axis=0))
        carry_out = end[0:1, :]
    else:
        enter = jnp.where(row == 0, carry, pltpu.roll(end, 1, axis=0))
        carry_out = end[SUBLANES - 1:SUBLANES, :]
    for j in range(seg):
        scr_ref[2, pl.ds(j, SUBLANES, stride=pitch), :] = hs[j] + ps[j] * enter
    h_time = [scr_ref[2, pl.ds(pitch * s, seg, stride=1), :] for s in range(SUBLANES)]
    return jnp.concatenate(h_time, axis=0), carry_out


def _lru_block(u, wg, bg, sp, carry, reverse, scr_ref):
    g = jnp.dot(u.astype(BF16), wg, preferred_element_type=F32) + bg
    r = _sigmoid(g[:, :LANES])
    i = _sigmoid(g[:, LANES:])
    a = jnp.exp2(r * sp)
    b = _sqrt_unit(1.0 - a * a) * (i * u)
    return _segment_scan(a, b, carry, reverse, scr_ref)


def _rglru_kernel(xa_ref, ga_ref, xc_ref, cw_ref, cb_ref, wgf_ref, bgf_ref, wgb_ref, bgb_ref,
                  lam_ref, wcast_ref, o_ref, wcast_out_ref, xf_ref, xcf_ref, u_ref, hf_ref, scr_ref,
                  *, tt):
    wcast_out_ref[...] = wcast_ref[...].astype(BF16)
    t_len = xa_ref.shape[1]
    tc_len = xc_ref.shape[1]
    nblk = xa_ref.shape[2] // LANES
    lanes = [slice(k * LANES, (k + 1) * LANES) for k in range(nblk)]
    zeros = jnp.zeros((SUBLANES, LANES), F32)
    for k, sl in enumerate(lanes):
        xf_ref[k, 0:SUBLANES, :] = zeros
        xf_ref[k, SUBLANES:SUBLANES + t_len, :] = xa_ref[0, :, sl].astype(F32)
        xf_ref[k, SUBLANES + t_len:, :] = zeros
        xcf_ref[k, 0:SUBLANES, :] = zeros
        xcf_ref[k, SUBLANES:SUBLANES + tc_len, :] = xc_ref[0, :, sl].astype(F32)
        xcf_ref[k, SUBLANES + tc_len:, :] = zeros

    sp_f = _softplus(-lam_ref[0:1, :]) * (LRU_C * NEG_LOG2E)
    sp_b = _softplus(-lam_ref[1:2, :]) * (LRU_C * NEG_LOG2E)

    def conv(src_ref, k, t0):
        u = cb_ref[:, lanes[k]]
        for tap in range(cw_ref.shape[0]):
            off = SUBLANES - CONV_PAD_L + tap
            if off % SUBLANES == 0:
                start = t0 + off
                rows = pl.ds(start if isinstance(start, int) else pl.multiple_of(start, SUBLANES), tt)
            else:
                rows = pl.ds(t0 + off, tt, stride=1)
            u = u + src_ref[k, rows, :] * cw_ref[tap:tap + 1, lanes[k]]
        return u

    def block(u, k, carry, reverse, slot):
        scr = scr_ref.at[k * TILES_PER_TRIP + slot]
        if reverse:
            return _lru_block(u, wgb_ref[k], bgb_ref[k], sp_b[:, lanes[k]], carry, True, scr)
        return _lru_block(u, wgf_ref[k], bgf_ref[k], sp_f[:, lanes[k]], carry, False, scr)

    n_lat = t_len // tt
    n_ctx = tc_len // tt
    assert n_lat % TILES_PER_TRIP == 0
    carry0 = tuple(jnp.zeros((1, LANES), F32) for _ in lanes)

    def tile_start(trip, slot, n_tiles, reverse):
        idx = trip * TILES_PER_TRIP + slot
        return pl.multiple_of((n_tiles - 1 - idx if reverse else idx) * tt, tt)

    def ctx_pass(reverse):
        carry = carry0
        for idx in range(n_ctx):
            t0 = (n_ctx - 1 - idx if reverse else idx) * tt
            carry = tuple(block(conv(xcf_ref, k, t0), k, carry[k], reverse, idx % TILES_PER_TRIP)[1]
                          for k in range(nblk))
        return carry

    def lat_f(trip, carry):
        for slot in range(TILES_PER_TRIP):
            t0 = tile_start(trip, slot, n_lat, False)
            rows = pl.ds(t0, tt)
            out = []
            for k, sl in enumerate(lanes):
                u = conv(xf_ref, k, t0)
                u_ref[rows, sl] = u
                h, c_out = block(u, k, carry[k], False, slot)
                hf_ref[rows, sl] = h
                out.append(c_out)
            carry = tuple(out)
        return carry

    lax.fori_loop(0, n_lat // TILES_PER_TRIP, lat_f, ctx_pass(False))

    def lat_b(trip, carry):
        for slot in range(TILES_PER_TRIP):
            rows = pl.ds(tile_start(trip, slot, n_lat, True), tt)
            out = []
            for k, sl in enumerate(lanes):
                h, c_out = block(u_ref[rows, sl], k, carry[k], True, slot)
                gate = ga_ref[0, rows, sl].astype(F32)
                o_ref[0, rows, sl] = ((hf_ref[rows, sl] + h) * gate).astype(o_ref.dtype)
                out.append(c_out)
            carry = tuple(out)
        return carry

    lax.fori_loop(0, n_lat // TILES_PER_TRIP, lat_b, ctx_pass(True))


def _rglru(p_lat, p_ctx, conv_w, conv_b, wg_f, bg_f, wg_b, bg_b, lam, w_cast, d_a, cb, tt):
    bsz, t_len, _ = p_lat.shape
    tc_len = p_ctx.shape[1]
    nblk = cb // LANES
    ncb = d_a // cb
    wrows, wcols = w_cast.shape
    assert wrows % (bsz * ncb) == 0
    wr = wrows // (bsz * ncb)
    wspec = pl.BlockSpec((wr, wcols), lambda b, c: (b * ncb + c, 0))
    kern = functools.partial(_rglru_kernel, tt=tt)
    return pl.pallas_call(
        kern,
        grid=(bsz, ncb),
        in_specs=[pl.BlockSpec((1, t_len, cb), lambda b, c: (b, 0, c)),
                  pl.BlockSpec((1, t_len, cb), lambda b, c: (b, 0, ncb + c)),
                  pl.BlockSpec((1, tc_len, cb), lambda b, c: (b, 0, c)),
                  pl.BlockSpec((conv_w.shape[0], cb), lambda b, c: (0, c)),
                  pl.BlockSpec((1, cb), lambda b, c: (0, c)),
                  pl.BlockSpec((nblk, LANES, 2 * LANES), lambda b, c: (c, 0, 0)),
                  pl.BlockSpec((nblk, 1, 2 * LANES), lambda b, c: (c, 0, 0)),
                  pl.BlockSpec((nblk, LANES, 2 * LANES), lambda b, c: (c, 0, 0)),
                  pl.BlockSpec((nblk, 1, 2 * LANES), lambda b, c: (c, 0, 0)),
                  pl.BlockSpec((2, cb), lambda b, c: (0, c)),
                  wspec],
        out_specs=[pl.BlockSpec((1, t_len, cb), lambda b, c: (b, 0, c)), wspec],
        out_shape=[jax.ShapeDtypeStruct((bsz, t_len, d_a), BF16),
                   jax.ShapeDtypeStruct((wrows, wcols), BF16)],
        scratch_shapes=[pltpu.VMEM((nblk, t_len + 2 * SUBLANES, LANES), F32),
                        pltpu.VMEM((nblk, tc_len + 2 * SUBLANES, LANES), F32),
                        pltpu.VMEM((t_len, cb), F32),
                        pltpu.VMEM((t_len, cb), F32),
                        pltpu.VMEM((nblk * TILES_PER_TRIP, 3, tt + SUBLANES * SEG_PAD, LANES),
                                   F32)],
        compiler_params=pltpu.CompilerParams(
            dimension_semantics=("parallel", "parallel"),
            vmem_limit_bytes=VMEM_LIMIT),
        name="rglru",
    )(p_lat, p_lat, p_ctx, conv_w, conv_b.reshape(1, d_a), wg_f, bg_f, wg_b, bg_b, lam, w_cast)


class _HgrnDir:
    def __init__(self, d, q_ref, f_ref, v_ref, lb, st_ref, o_ref, scratch, reverse, n_heads, hd,
                 sub):
        self.d, self.q_ref, self.f_ref, self.v_ref, self.lb = d, q_ref, f_ref, v_ref, lb
        self.sub = sub
        self.st_ref, self.o_ref, self.reverse, self.n_heads, self.hd = st_ref, o_ref, reverse, n_heads, hd
        (self.hl_ref, self.kk_ref, self.cum_ref, self.qd_ref, self.kdt_ref, self.qin_ref,
         self.dec_ref, self.dect_ref, self.sc_ref) = scratch
        self.c_len = q_ref.shape[2]
        r_i = lax.broadcasted_iota(jnp.int32, (self.c_len, self.c_len), 0)
        c_i = lax.broadcasted_iota(jnp.int32, (self.c_len, self.c_len), 1)
        self.tri = (r_i <= c_i) if reverse else (r_i >= c_i)

    def _slabs(self, width):
        total = self.n_heads * self.hd
        return [slice(s, s + width) for s in range(0, total, width)]


    def _gate_slab(self, sl):
        d, c_len = self.d, self.c_len
        lb = self.lb[:, sl]
        f = lb + (1.0 - lb) * _sigmoid(self.f_ref[0, self.sub, :, sl].astype(F32))
        logf = jnp.log2(f)
        self.kk_ref[d, :, sl] = (1.0 - f).astype(BF16)
        hi = logf.astype(BF16)
        self.hl_ref[d, 0:c_len, sl] = hi
        self.hl_ref[d, c_len:2 * c_len, sl] = (logf - hi.astype(F32)).astype(BF16)

    def _cumulate(self):
        tri_b = self.tri.astype(BF16)
        tri2 = jnp.concatenate([tri_b, tri_b], axis=1)
        self.cum_ref[self.d] = jnp.dot(tri2, self.hl_ref[self.d], preferred_element_type=F32)

    def _decay_slab(self, sl):
        d, c_len = self.d, self.c_len
        half = c_len // 2
        row_last = 0 if self.reverse else c_len - 1
        row_ref = half if self.reverse else half - 1
        cum = self.cum_ref[d, :, sl]
        last = cum[row_last:row_last + 1, :]
        ref = cum[row_ref:row_ref + 1, :]
        kk = self.kk_ref[d, :, sl]
        qs = self.q_ref[0, self.sub, :, sl]
        self.qd_ref[d, :, sl] = qs * jnp.exp2(jnp.minimum(cum - ref, EXP2_CLAMP)).astype(BF16)
        kd = kk * jnp.exp2(jnp.minimum(ref - cum, EXP2_CLAMP)).astype(BF16)
        self.qin_ref[d, :, sl] = qs * jnp.exp2(cum).astype(BF16)
        dec = jnp.exp2(last)
        fac = jnp.exp2(last - ref)
        for i in range((sl.stop - sl.start) // self.hd):
            h = sl.start // self.hd + i
            lanes = slice(i * self.hd, (i + 1) * self.hd)
            self.kdt_ref[d, h] = jnp.transpose(kd[:, lanes])
            self.dec_ref[d, h:h + 1, :] = dec[:, lanes]
            self.dec_ref[d, self.n_heads + h:self.n_heads + h + 1, :] = fac[:, lanes]

    def _decay_columns(self):
        self.dect_ref[self.d] = jnp.transpose(self.dec_ref[self.d])

    def _head(self, h):
        return slice(h * self.hd, (h + 1) * self.hd)

    def _scores(self, h):
        d, sl = self.d, self._head(h)
        s = jnp.dot(self.qd_ref[d, :, sl], self.kdt_ref[d, h], preferred_element_type=F32)
        self.sc_ref[d, h] = jnp.where(self.tri, s, 0.0).astype(BF16)

    def _output(self, h):
        d, sl = self.d, self._head(h)
        o = (jnp.dot(self.sc_ref[d, h], self.v_ref[0, self.sub, :, sl], preferred_element_type=F32)
             + jnp.dot(self.qin_ref[d, :, sl], self.st_ref[h].astype(BF16),
                       preferred_element_type=F32))
        self.o_ref[0, self.sub, :, sl] = o.astype(self.o_ref.dtype)

    def _update(self, h):
        d, sl = self.d, self._head(h)
        decay = jnp.broadcast_to(self.dect_ref[d, :, h:h + 1], (self.hd, self.hd))
        fac = self.dect_ref[d, :, self.n_heads + h:self.n_heads + h + 1].astype(BF16)
        kout_t = self.kdt_ref[d, h] * jnp.broadcast_to(fac, (self.hd, self.c_len))
        self.st_ref[h] = (self.st_ref[h] * decay
                          + jnp.dot(kout_t, self.v_ref[0, self.sub, :, sl],
                                    preferred_element_type=F32))

    def stages(self):
        heads = range(self.n_heads)
        slabs = self._slabs(2 * LANES)
        out = [[functools.partial(self._gate_slab, sl) for sl in slabs], [self._cumulate],
               [functools.partial(self._decay_slab, sl) for sl in slabs], [self._decay_columns]]
        if self.o_ref is not None:
            out += [[functools.partial(self._scores, h) for h in heads],
                    [functools.partial(self._output, h) for h in heads]]
        return out + [[functools.partial(self._update, h) for h in heads]]


def _lower_bounds(logits_ref, layer):
    out = []
    for d in range(2):
        rows = [logits_ref[d, l:l + 1, :] for l in range(logits_ref.shape[1])]
        m = functools.reduce(jnp.maximum, rows)
        e = [jnp.exp(r - m) for r in rows]
        out.append(sum(e[:layer + 1]) / sum(e))
    return out


def _hgrn_kernel(*refs, n_heads, hd, has_init, emit_o, emit_state):
    qf_ref, ff_ref, vf_ref, qb_ref, fb_ref, vb_ref, lg_ref = refs[:7]
    pos = 7
    if has_init:
        s0f_ref, s0b_ref = refs[pos:pos + 2]
        pos += 2
    if emit_o:
        of_ref, ob_ref = refs[pos:pos + 2]
        pos += 2
    if emit_state:
        sof_ref, sob_ref = refs[pos:pos + 2]
        pos += 2
    sf_ref, sb_ref = refs[pos:pos + 2]
    scratch = refs[pos + 2:]

    j = pl.program_id(1)
    dec_ref = scratch[6]
    assert 2 * n_heads <= dec_ref.shape[1]

    @pl.when(j == 0)
    def _():
        if has_init:
            sf_ref[...] = s0f_ref[0]
            sb_ref[...] = s0b_ref[0]
        else:
            sf_ref[...] = jnp.zeros_like(sf_ref)
            sb_ref[...] = jnp.zeros_like(sb_ref)
        dec_ref[...] = jnp.zeros_like(dec_ref)

    lb_f, lb_b = _lower_bounds(lg_ref, 0)
    n_sub = qf_ref.shape[1]
    for k in range(n_sub):
        dirs = [_HgrnDir(0, qf_ref, ff_ref, vf_ref, lb_f, sf_ref, of_ref if emit_o else None,
                         scratch, False, n_heads, hd, k),
                _HgrnDir(1, qb_ref, fb_ref, vb_ref, lb_b, sb_ref, ob_ref if emit_o else None,
                         scratch, True, n_heads, hd, n_sub - 1 - k)]
        for stage_f, stage_b in zip(dirs[0].stages(), dirs[1].stages()):
            for unit in stage_f + stage_b:
                unit()

    if emit_state:
        @pl.when(j == pl.num_programs(1) - 1)
        def _():
            sof_ref[0] = sf_ref[...]
            sob_ref[0] = sb_ref[...]


def _hgrn(p_view, logits, d_b, n_heads, first_group, init_states, emit_o):
    bsz, n_chunks = p_view.shape[:2]
    hd = d_b // n_heads
    has_init = init_states is not None
    emit_state = not emit_o

    cps = CHUNKS_PER_STEP if n_chunks % CHUNKS_PER_STEP == 0 else 1
    n_steps = n_chunks // cps

    def chunk(j, reverse):
        return n_steps - 1 - j if reverse else j

    def spec(group, reverse):
        def imap(b, j):
            return (b, chunk(j, reverse), 0, first_group + group)
        return pl.BlockSpec((1, cps, CHUNK, d_b), imap)

    in_specs = [spec(0, False), spec(1, False), spec(3, False),
                spec(0, True), spec(2, True), spec(3, True),
                pl.BlockSpec(logits.shape, lambda b, j: (0, 0, 0))]
    args = [p_view] * 6 + [logits]
    state_spec = pl.BlockSpec((1, n_heads, hd, hd), lambda b, j: (b, 0, 0, 0))
    state_shape = jax.ShapeDtypeStruct((bsz, n_heads, hd, hd), F32)
    if has_init:
        in_specs += [state_spec, state_spec]
        args += list(init_states)
    out_specs, out_shape = [], []
    if emit_o:
        o_shape = jax.ShapeDtypeStruct((bsz, n_chunks, CHUNK, d_b), BF16)
        out_specs += [pl.BlockSpec((1, cps, CHUNK, d_b), lambda b, j: (b, chunk(j, False), 0, 0)),
                      pl.BlockSpec((1, cps, CHUNK, d_b), lambda b, j: (b, chunk(j, True), 0, 0))]
        out_shape += [o_shape, o_shape]
    if emit_state:
        out_specs += [state_spec, state_spec]
        out_shape += [state_shape, state_shape]
    kern = functools.partial(_hgrn_kernel, n_heads=n_heads, hd=hd, has_init=has_init,
                             emit_o=emit_o, emit_state=emit_state)
    return pl.pallas_call(
        kern,
        grid=(bsz, n_steps),
        in_specs=in_specs,
        out_specs=out_specs,
        out_shape=out_shape,
        scratch_shapes=[pltpu.VMEM((n_heads, hd, hd), F32),
                        pltpu.VMEM((n_heads, hd, hd), F32),
                        pltpu.VMEM((2, 2 * CHUNK, d_b), BF16),
                        pltpu.VMEM((2, CHUNK, d_b), BF16),
                        pltpu.VMEM((2, CHUNK, d_b), F32),
                        pltpu.VMEM((2, CHUNK, d_b), BF16),
                        pltpu.VMEM((2, n_heads, hd, CHUNK), BF16),
                        pltpu.VMEM((2, CHUNK, d_b), BF16),
                        pltpu.VMEM((2, hd, hd), F32),
                        pltpu.VMEM((2, hd, hd), F32),
                        pltpu.VMEM((2, n_heads, CHUNK, CHUNK), BF16)],
        compiler_params=pltpu.CompilerParams(
            dimension_semantics=("parallel", "arbitrary"),
            vmem_limit_bytes=VMEM_LIMIT),
        name="hgrn2_latent" if emit_o else "hgrn2_context",
    )(*args)


def _outproj_kernel(ya_ref, of_ref, ob_ref, gb_ref, x_ref, gate_ref, hnw_ref, fnw_ref, w_ref,
                    o_ref, y_ref, *, n_heads, hd):
    _, rb, wb, d_a = ya_ref.shape
    rows = rb * wb
    d = x_ref.shape[3]
    y_ref[:, 0:d_a] = ya_ref[0].reshape(rows, d_a)
    hnw = hnw_ref[...]
    for h in range(n_heads):
        sl = slice(h * hd, (h + 1) * hd)
        o = of_ref[0, :, :, sl].astype(F32) + ob_ref[0, :, :, sl].astype(F32)
        ms = jnp.mean(o * o, axis=-1, keepdims=True)
        on = jnp.transpose(o * lax.rsqrt(ms + EPS) * hnw, (1, 0, 2))
        yb = on * gb_ref[0, :, :, sl].astype(F32)
        y_ref[:, d_a + h * hd:d_a + (h + 1) * hd] = yb.reshape(rows, hd).astype(BF16)
    acc = jnp.dot(y_ref[...], w_ref[...], preferred_element_type=F32)
    z = x_ref[0].reshape(rows, d) + gate_ref[0] * acc
    ms = jnp.mean(z * z, axis=-1, keepdims=True)
    o_ref[0] = (z * lax.rsqrt(ms + EPS) * fnw_ref[...]).reshape(rb, wb, d)


def _out_projection(ya4, of4, ob4, pa4, gb_block, x4, mod3, hnw, fnw, w_bf16, n_heads, rb, wb):
    bsz, n_r, n_w, d = x4.shape
    d_a = ya4.shape[3]
    d_b = of4.shape[3]
    hd = d_b // n_heads
    kern = functools.partial(_outproj_kernel, n_heads=n_heads, hd=hd)

    def raster(c, col=0):
        return pl.BlockSpec((1, rb, wb, c), lambda b, r, w: (b, r, w, col))

    def colmajor(c):
        return pl.BlockSpec((1, wb, rb, c), lambda b, r, w: (b, w, r, 0))

    return pl.pallas_call(
        kern,
        grid=(bsz, n_r // rb, n_w // wb),
        in_specs=[raster(d_a), colmajor(d_b), colmajor(d_b), raster(d_b, gb_block), raster(d),
                  pl.BlockSpec((1, 1, d), lambda b, r, w: (b, 0, 2)),
                  pl.BlockSpec((1, hd), lambda b, r, w: (0, 0)),
                  pl.BlockSpec((1, d), lambda b, r, w: (0, 0)),
                  pl.BlockSpec((d_a + d_b, d), lambda b, r, w: (0, 0),
                               pipeline_mode=pl.Buffered(1))],
        out_specs=raster(d),
        out_shape=jax.ShapeDtypeStruct((bsz, n_r, n_w, d), F32),
        scratch_shapes=[pltpu.VMEM((rb * wb, d_a + d_b), BF16)],
        compiler_params=pltpu.CompilerParams(
            dimension_semantics=("parallel", "parallel", "parallel"),
            vmem_limit_bytes=VMEM_LIMIT),
        name="out_projection",
    )(ya4, of4, ob4, pa4, x4, mod3, hnw.reshape(1, hd), fnw.reshape(1, d), w_bf16)


def kernel(x, c, ctx, c_ctx, ada_w, ada_b, norm_w, w_in, conv_w, conv_b, lru_wr, lru_br, lru_wi,
           lru_bi, lru_lambda, hgrn_lb_logits, hgrn_norm_w, w_out, final_norm_w):
    bsz, t_len, d = x.shape
    tc_len = ctx.shape[1]
    assert ada_w.shape[0] == 1, "single-layer stack only"
    d_a = conv_w.shape[2]
    d_b = hgrn_lb_logits.shape[2]
    hd = hgrn_norm_w.shape[1]
    n_heads = d_b // hd
    n_blocks_a = lru_wr.shape[2]
    n_cols = w_in.shape[2]
    assert t_len == GRID_W * CHUNK and tc_len % CHUNK == 0
    assert d_a // n_blocks_a == LANES and hd == LANES
    assert (2 * d_a) % d_b == 0 and n_cols == 2 * d_a + 5 * d_b
    first_b_group = (2 * d_a) // d_b
    n_rows_grid = t_len // GRID_W

    n_rows = -(-(bsz + 1) // SUBLANES) * SUBLANES
    cc = jnp.zeros((n_rows, d), F32).at[:bsz].set(c).at[bsz].set(c_ctx)
    mod3 = _modulation(cc, ada_w[0], ada_b[0]).reshape(n_rows, 1, 3 * d)

    w_in_b = w_in[0]

    tm = 1024 if t_len % 1024 == 0 else t_len
    tn = 1024 if d_a % 1024 == 0 and d_b % 1024 == 0 else min(d_a, d_b)
    a_tiles = (2 * d_a) // tn
    b_tiles = (4 * d_b) // tn
    q_tiles = d_b // tn
    h_lat, h_col = _normalise(x.reshape(bsz, n_rows_grid, GRID_W, d), mod3, norm_w[0], 16)
    tm_h = 2 * tm if (bsz * t_len) % (2 * tm) == 0 else tm
    p_b = _projection_from_h(h_col.reshape(bsz * t_len, d), w_in_b, lambda j: j + a_tiles,
                             4 * d_b, tm_h, tn, (0, q_tiles))
    p_b = p_b.reshape(bsz, GRID_W, n_rows_grid, 4 * d_b)
    p_a = _projection_from_h(h_lat.reshape(bsz * t_len, d), w_in_b,
                             lambda j: jnp.where(j < a_tiles, j, j + b_tiles),
                             2 * d_a + d_b, tm_h, tn, (d_a // tn, (2 * d_a + d_b) // tn))
    xa_tiles = d_a // tn
    ctx_cols = d_a + 4 * d_b
    p_ctx = _in_projection(ctx.reshape(bsz * tc_len, d), mod3, norm_w[0], w_in_b,
                           lambda i: bsz, lambda j: jnp.where(j < xa_tiles, j, j + xa_tiles),
                           ctx_cols, bsz * tc_len, tn, (xa_tiles, xa_tiles + q_tiles))
    p_lat = p_a.reshape(bsz, t_len, 2 * d_a + d_b)
    p_ctx = p_ctx.reshape(bsz, tc_len, ctx_cols)

    def gate_w(dirn):
        return jnp.concatenate([lru_wr[0, dirn], lru_wi[0, dirn]], axis=-1).astype(BF16)

    def gate_b(dirn):
        return jnp.concatenate([lru_br[0, dirn].reshape(n_blocks_a, 1, LANES),
                                lru_bi[0, dirn].reshape(n_blocks_a, 1, LANES)], axis=-1)

    cb = 256 if d_a % 256 == 0 else LANES
    ya, w_out_b = _rglru(p_lat, p_ctx, conv_w[0], conv_b[0], gate_w(0), gate_b(0), gate_w(1),
                         gate_b(1), lru_lambda[0], w_out[0], d_a, cb, tt=256)

    assert d_a % d_b == 0
    states = _hgrn(p_ctx.reshape(bsz, tc_len // CHUNK, CHUNK, ctx_cols), hgrn_lb_logits, d_b,
                   n_heads, d_a // d_b, None, False)
    o_f, o_b = _hgrn(p_b, hgrn_lb_logits, d_b, n_heads, 0, states, True)

    grid4 = lambda z: z.reshape(bsz, n_rows_grid, GRID_W, z.shape[-1])
    out = _out_projection(grid4(ya), o_f, o_b, grid4(p_lat), first_b_group, grid4(x), mod3,
                          hgrn_norm_w[0], final_norm_w, w_out_b, n_heads, 16, 16)
    return out.reshape(bsz, t_len, d)
```

```python
import functools

import jax
import jax.numpy as jnp
from jax import lax
from jax.experimental import pallas as pl
from jax.experimental.pallas import tpu as pltpu

GRID_W = 64
CHUNK = 64
LRU_C = 8.0
EPS = 1e-6
CONV_PAD_L = 2
LANES = 128
SUBLANES = 8
EXP2_CLAMP = 115.0
NEG_LOG2E = -1.4426950408889634
VMEM_LIMIT = 56 * 1024 * 1024
PROLOGUE_ROWS = 16
SEG_PAD = 4
CHUNKS_PER_STEP = 2
TILES_PER_TRIP = 8

F32 = jnp.float32
BF16 = jnp.bfloat16


def _sigmoid(z):
    return 1.0 / (1.0 + jnp.exp2(z * NEG_LOG2E))


def _silu(z):
    return z * _sigmoid(z)


def _softplus(z):
    return jnp.maximum(z, 0.0) + jnp.log1p(jnp.exp(-jnp.abs(z)))


def _mod_kernel(c_ref, w_ref, b_ref, o_ref):
    s = _silu(c_ref[...])
    o_ref[...] = jnp.dot(s.astype(BF16), w_ref[...].astype(BF16),
                         preferred_element_type=F32) + b_ref[...]


def _modulation(cc, w, b):
    rows, d = cc.shape
    n = w.shape[1]
    tn = 512 if n % 512 == 0 else n
    return pl.pallas_call(
        _mod_kernel,
        grid=(n // tn,),
        in_specs=[pl.BlockSpec((rows, d), lambda j: (0, 0)),
                  pl.BlockSpec((d, tn), lambda j: (0, j)),
                  pl.BlockSpec((1, tn), lambda j: (0, j))],
        out_specs=pl.BlockSpec((rows, tn), lambda j: (0, j)),
        out_shape=jax.ShapeDtypeStruct((rows, n), F32),
        compiler_params=pltpu.CompilerParams(
            dimension_semantics=("arbitrary",), vmem_limit_bytes=VMEM_LIMIT),
        name="adaln_modulation",
    )(cc, w, b.reshape(1, n))


def _store_projection(o_ref, res, j, silu_tiles):
    lo, hi = silu_tiles
    z = res.astype(o_ref.dtype)
    if hi <= lo:
        o_ref[...] = z
        return
    in_range = jnp.logical_and(j >= lo, j < hi)
    o_ref[...] = jnp.where(in_range, z * (0.5 * jnp.tanh(0.5 * z) + 0.5), z)


def _inproj_kernel(x_ref, shift_ref, scale_ref, nw_ref, w_ref, o_ref, h_ref, *, silu_tiles):
    @pl.when(pl.program_id(1) == 0)
    def _():
        rows = min(PROLOGUE_ROWS, x_ref.shape[0])
        gain = nw_ref[...] * (1.0 + scale_ref[0])

        def slab(s, carry):
            sl = pl.ds(pl.multiple_of(s * rows, rows), rows)
            x = x_ref[sl, :]
            rs = lax.rsqrt(jnp.mean(x * x, axis=-1, keepdims=True) + EPS)
            h_ref[sl, :] = (x_ref[sl, :] * rs * gain + shift_ref[0]).astype(BF16)
            return carry

        lax.fori_loop(0, x_ref.shape[0] // rows, slab, 0, unroll=4)

    res = jnp.dot(h_ref[...], w_ref[...].astype(BF16), preferred_element_type=F32)
    _store_projection(o_ref, res, pl.program_id(1), silu_tiles)


def _in_projection(x2d, mod3, norm_w, w_bf16, row_of_tile, wcol_of_tile, n_out, tm, tn, silu_tiles):
    m, d = x2d.shape
    return pl.pallas_call(
        functools.partial(_inproj_kernel, silu_tiles=silu_tiles),
        grid=(m // tm, n_out // tn),
        in_specs=[pl.BlockSpec((tm, d), lambda i, j: (i, 0)),
                  pl.BlockSpec((1, 1, d), lambda i, j: (row_of_tile(i), 0, 0)),
                  pl.BlockSpec((1, 1, d), lambda i, j: (row_of_tile(i), 0, 1)),
                  pl.BlockSpec((1, d), lambda i, j: (0, 0)),
                  pl.BlockSpec((d, tn), lambda i, j: (0, wcol_of_tile(j)))],
        out_specs=pl.BlockSpec((tm, tn), lambda i, j: (i, j)),
        out_shape=jax.ShapeDtypeStruct((m, n_out), BF16),
        scratch_shapes=[pltpu.VMEM((tm, d), BF16)],
        compiler_params=pltpu.CompilerParams(
            dimension_semantics=("parallel", "arbitrary"),
            vmem_limit_bytes=VMEM_LIMIT),
        name="in_projection",
    )(x2d, mod3, mod3, norm_w.reshape(1, d), w_bf16)


def _matmul_kernel(h_ref, w_ref, o_ref, *, silu_tiles):
    res = jnp.dot(h_ref[...], w_ref[...].astype(BF16), preferred_element_type=F32)
    _store_projection(o_ref, res, pl.program_id(1), silu_tiles)


def _projection_from_h(h2d, w_f32, wcol_of_tile, n_out, tm, tn, silu_tiles):
    m, d = h2d.shape
    return pl.pallas_call(
        functools.partial(_matmul_kernel, silu_tiles=silu_tiles),
        grid=(m // tm, n_out // tn),
        in_specs=[pl.BlockSpec((tm, d), lambda i, j: (i, 0)),
                  pl.BlockSpec((d, tn), lambda i, j: (0, wcol_of_tile(j)))],
        out_specs=pl.BlockSpec((tm, tn), lambda i, j: (i, j)),
        out_shape=jax.ShapeDtypeStruct((m, n_out), BF16),
        compiler_params=pltpu.CompilerParams(
            dimension_semantics=("parallel", "arbitrary"),
            vmem_limit_bytes=VMEM_LIMIT),
        name="projection_from_h",
    )(h2d, w_f32)


def _normalise_kernel(x_ref, shift_ref, scale_ref, nw_ref, hr_ref, hc_ref, *, slab):
    _, n_r, n_w, d = x_ref.shape
    rr = 2 * SUBLANES
    lanes = [slice(s * slab, (s + 1) * slab) for s in range(d // slab)]

    def row_group(g, carry):
        rows = pl.ds(pl.multiple_of(g * rr, rr), rr)
        sq = jnp.zeros((rr, n_w, slab), F32)
        for sl in lanes:
            xs = x_ref[0, rows, :, sl]
            sq = sq + xs * xs
        rs = lax.rsqrt(jnp.sum(sq, axis=-1, keepdims=True) * (1.0 / d) + EPS)
        for sl in lanes:
            gain = nw_ref[:, sl] * (1.0 + scale_ref[0][:, sl])
            y = x_ref[0, rows, :, sl] * rs * gain + shift_ref[0][:, sl]
            hr_ref[0, rows, :, sl] = y.astype(BF16)
            hc_ref[0, :, rows, sl] = jnp.transpose(y, (1, 0, 2)).astype(BF16)
        return carry

    lax.fori_loop(0, n_r // rr, row_group, 0)


def _normalise(x4d, mod3, norm_w, wb):
    bsz, n_r, n_w, d = x4d.shape
    return pl.pallas_call(
        functools.partial(_normalise_kernel, slab=2 * LANES),
        grid=(bsz, n_w // wb),
        in_specs=[pl.BlockSpec((1, n_r, wb, d), lambda b, w: (b, 0, w, 0)),
                  pl.BlockSpec((1, 1, d), lambda b, w: (b, 0, 0)),
                  pl.BlockSpec((1, 1, d), lambda b, w: (b, 0, 1)),
                  pl.BlockSpec((1, d), lambda b, w: (0, 0))],
        out_specs=[pl.BlockSpec((1, n_r, wb, d), lambda b, w: (b, 0, w, 0)),
                   pl.BlockSpec((1, wb, n_r, d), lambda b, w: (b, w, 0, 0))],
        out_shape=[jax.ShapeDtypeStruct((bsz, n_r, n_w, d), BF16),
                   jax.ShapeDtypeStruct((bsz, n_w, n_r, d), BF16)],
        compiler_params=pltpu.CompilerParams(
            dimension_semantics=("parallel", "parallel"), vmem_limit_bytes=VMEM_LIMIT),
        name="normalise_rows",
    )(x4d, mod3, mod3, norm_w.reshape(1, d))


def _group_scan(a, b, reverse):
    row = lax.broadcasted_iota(jnp.int32, a.shape, 1)
    for k in (1, 2, 4):
        if reverse:
            a_sh = pltpu.roll(a, SUBLANES - k, axis=1)
            b_sh = pltpu.roll(b, SUBLANES - k, axis=1)
            m = row < SUBLANES - k
        else:
            a_sh = pltpu.roll(a, k, axis=1)
            b_sh = pltpu.roll(b, k, axis=1)
            m = row >= k
        b = jnp.where(m, a * b_sh + b, b)
        a = jnp.where(m, a * a_sh, a)
    return a, b


def _sqrt_unit(x):
    return jnp.where(x > 0.0, x * lax.rsqrt(x), 0.0)


def _segment_scan(a, b, carry, reverse, scr_ref):
    tt = a.shape[0]
    seg = tt // SUBLANES
    pitch = seg + SEG_PAD
    for s in range(SUBLANES):
        rows = pl.ds(pitch * s, seg, stride=1)
        scr_ref[0, rows, :] = a[seg * s:seg * (s + 1), :]
        scr_ref[1, rows, :] = b[seg * s:seg * (s + 1), :]
    hs, ps = [None] * seg, [None] * seg
    h = p = None
    for j in (range(seg - 1, -1, -1) if reverse else range(seg)):
        step = pl.ds(j, SUBLANES, stride=pitch)
        a_j = scr_ref[0, step, :]
        b_j = scr_ref[1, step, :]
        h = b_j if h is None else a_j * h + b_j
        p = a_j if p is None else a_j * p
        hs[j], ps[j] = h, p
    g3, e3 = _group_scan(p[None], h[None], reverse)
    end = e3[0] + g3[0] * carry
    row = lax.broadcasted_iota(jnp.int32, end.shape, 0)
    if reverse:
        enter = jnp.where(row == SUBLANES - 1, carry, pltpu.roll(end, SUBLANES - 1, axis=0))
        carry_out = end[0:1, :]
    else:
        enter = jnp.where(row == 0, carry, pltpu.roll(end, 1, axis=0))
        carry_out = end[SUBLANES - 1:SUBLANES, :]
    for j in range(seg):
        scr_ref[2, pl.ds(j, SUBLANES, stride=pitch), :] = hs[j] + ps[j] * enter
    h_time = [scr_ref[2, pl.ds(pitch * s, seg, stride=1), :] for s in range(SUBLANES)]
    return jnp.concatenate(h_time, axis=0), carry_out


def _lru_block(u, wg, bg, sp, carry, reverse, scr_ref):
    g = jnp.dot(u.astype(BF16), wg, preferred_element_type=F32) + bg
    r = _sigmoid(g[:, :LANES])
    i = _sigmoid(g[:, LANES:])
    a = jnp.exp2(r * sp)
    b = _sqrt_unit(1.0 - a * a) * (i * u)
    return _segment_scan(a, b, carry, reverse, scr_ref)


def _rglru_kernel(xa_ref, ga_ref, xc_ref, cw_ref, cb_ref, wgf_ref, bgf_ref, wgb_ref, bgb_ref,
                  lam_ref, wcast_ref, o_ref, wcast_out_ref, xf_ref, xcf_ref, u_ref, hf_ref, scr_ref,
                  *, tt):
    wcast_out_ref[...] = wcast_ref[...].astype(BF16)
    t_len = xa_ref.shape[1]
    tc_len = xc_ref.shape[1]
    nblk = xa_ref.shape[2] // LANES
    lanes = [slice(k * LANES, (k + 1) * LANES) for k in range(nblk)]
    zeros = jnp.zeros((SUBLANES, LANES), F32)
    for k, sl in enumerate(lanes):
        xf_ref[k, 0:SUBLANES, :] = zeros
        xf_ref[k, SUBLANES:SUBLANES + t_len, :] = xa_ref[0, :, sl].astype(F32)
        xf_ref[k, SUBLANES + t_len:, :] = zeros
        xcf_ref[k, 0:SUBLANES, :] = zeros
        xcf_ref[k, SUBLANES:SUBLANES + tc_len, :] = xc_ref[0, :, sl].astype(F32)
        xcf_ref[k, SUBLANES + tc_len:, :] = zeros

    sp_f = _softplus(-lam_ref[0:1, :]) * (LRU_C * NEG_LOG2E)
    sp_b = _softplus(-lam_ref[1:2, :]) * (LRU_C * NEG_LOG2E)

    def conv(src_ref, k, t0):
        u = cb_ref[:, lanes[k]]
        for tap in range(cw_ref.shape[0]):
            off = SUBLANES - CONV_PAD_L + tap
            if off % SUBLANES == 0:
                start = t0 + off
                rows = pl.ds(start if isinstance(start, int) else pl.multiple_of(start, SUBLANES), tt)
            else:
                rows = pl.ds(t0 + off, tt, stride=1)
            u = u + src_ref[k, rows, :] * cw_ref[tap:tap + 1, lanes[k]]
        return u

    def block(u, k, carry, reverse, slot):
        scr = scr_ref.at[k * TILES_PER_TRIP + slot]
        if reverse:
            return _lru_block(u, wgb_ref[k], bgb_ref[k], sp_b[:, lanes[k]], carry, True, scr)
        return _lru_block(u, wgf_ref[k], bgf_ref[k], sp_f[:, lanes[k]], carry, False, scr)

    n_lat = t_len // tt
    n_ctx = tc_len // tt
    assert n_lat % TILES_PER_TRIP == 0
    carry0 = tuple(jnp.zeros((1, LANES), F32) for _ in lanes)

    def tile_start(trip, slot, n_tiles, reverse):
        idx = trip * TILES_PER_TRIP + slot
        return pl.multiple_of((n_tiles - 1 - idx if reverse else idx) * tt, tt)

    def ctx_pass(reverse):
        carry = carry0
        for idx in range(n_ctx):
            t0 = (n_ctx - 1 - idx if reverse else idx) * tt
            carry = tuple(block(conv(xcf_ref, k, t0), k, carry[k], reverse, idx % TILES_PER_TRIP)[1]
                          for k in range(nblk))
        return carry

    def lat_f(trip, carry):
        for slot in range(TILES_PER_TRIP):
            t0 = tile_start(trip, slot, n_lat, False)
            rows = pl.ds(t0, tt)
            out = []
            for k, sl in enumerate(lanes):
                u = conv(xf_ref, k, t0)
                u_ref[rows, sl] = u
                h, c_out = block(u, k, carry[k], False, slot)
                hf_ref[rows, sl] = h
                out.append(c_out)
            carry = tuple(out)
        return carry

    lax.fori_loop(0, n_lat // TILES_PER_TRIP, lat_f, ctx_pass(False))

    def lat_b(trip, carry):
        for slot in range(TILES_PER_TRIP):
            rows = pl.ds(tile_start(trip, slot, n_lat, True), tt)
            out = []
            for k, sl in enumerate(lanes):
                h, c_out = block(u_ref[rows, sl], k, carry[k], True, slot)
                gate = ga_ref[0, rows, sl].astype(F32)
                o_ref[0, rows, sl] = ((hf_ref[rows, sl] + h) * gate).astype(o_ref.dtype)
                out.append(c_out)
            carry = tuple(out)
        return carry

    lax.fori_loop(0, n_lat // TILES_PER_TRIP, lat_b, ctx_pass(True))


def _rglru(p_lat, p_ctx, conv_w, conv_b, wg_f, bg_f, wg_b, bg_b, lam, w_cast, d_a, cb, tt):
    bsz, t_len, _ = p_lat.shape
    tc_len = p_ctx.shape[1]
    nblk = cb // LANES
    ncb = d_a // cb
    wrows, wcols = w_cast.shape
    assert wrows % (bsz * ncb) == 0
    wr = wrows // (bsz * ncb)
    wspec = pl.BlockSpec((wr, wcols), lambda b, c: (b * ncb + c, 0))
    kern = functools.partial(_rglru_kernel, tt=tt)
    return pl.pallas_call(
        kern,
        grid=(bsz, ncb),
        in_specs=[pl.BlockSpec((1, t_len, cb), lambda b, c: (b, 0, c)),
                  pl.BlockSpec((1, t_len, cb), lambda b, c: (b, 0, ncb + c)),
                  pl.BlockSpec((1, tc_len, cb), lambda b, c: (b, 0, c)),
                  pl.BlockSpec((conv_w.shape[0], cb), lambda b, c: (0, c)),
                  pl.BlockSpec((1, cb), lambda b, c: (0, c)),
                  pl.BlockSpec((nblk, LANES, 2 * LANES), lambda b, c: (c, 0, 0)),
                  pl.BlockSpec((nblk, 1, 2 * LANES), lambda b, c: (c, 0, 0)),
                  pl.BlockSpec((nblk, LANES, 2 * LANES), lambda b, c: (c, 0, 0)),
                  pl.BlockSpec((nblk, 1, 2 * LANES), lambda b, c: (c, 0, 0)),
                  pl.BlockSpec((2, cb), lambda b, c: (0, c)),
                  wspec],
        out_specs=[pl.BlockSpec((1, t_len, cb), lambda b, c: (b, 0, c)), wspec],
        out_shape=[jax.ShapeDtypeStruct((bsz, t_len, d_a), BF16),
                   jax.ShapeDtypeStruct((wrows, wcols), BF16)],
        scratch_shapes=[pltpu.VMEM((nblk, t_len + 2 * SUBLANES, LANES), F32),
                        pltpu.VMEM((nblk, tc_len + 2 * SUBLANES, LANES), F32),
                        pltpu.VMEM((t_len, cb), F32),
                        pltpu.VMEM((t_len, cb), F32),
                        pltpu.VMEM((nblk * TILES_PER_TRIP, 3, tt + SUBLANES * SEG_PAD, LANES),
                                   F32)],
        compiler_params=pltpu.CompilerParams(
            dimension_semantics=("parallel", "parallel"),
            vmem_limit_bytes=VMEM_LIMIT),
        name="rglru",
    )(p_lat, p_lat, p_ctx, conv_w, conv_b.reshape(1, d_a), wg_f, bg_f, wg_b, bg_b, lam, w_cast)


class _HgrnDir:
    def __init__(self, d, q_ref, f_ref, v_ref, lb, st_ref, o_ref, scratch, reverse, n_heads, hd,
                 sub):
        self.d, self.q_ref, self.f_ref, self.v_ref, self.lb = d, q_ref, f_ref, v_ref, lb
        self.sub = sub
        self.st_ref, self.o_ref, self.reverse, self.n_heads, self.hd = st_ref, o_ref, reverse, n_heads, hd
        (self.hl_ref, self.kk_ref, self.cum_ref, self.qd_ref, self.kdt_ref, self.qin_ref,
         self.dec_ref, self.dect_ref, self.sc_ref) = scratch
        self.c_len = q_ref.shape[2]
        r_i = lax.broadcasted_iota(jnp.int32, (self.c_len, self.c_len), 0)
        c_i = lax.broadcasted_iota(jnp.int32, (self.c_len, self.c_len), 1)
        self.tri = (r_i <= c_i) if reverse else (r_i >= c_i)

    def _slabs(self, width):
        total = self.n_heads * self.hd
        return [slice(s, s + width) for s in range(0, total, width)]


    def _gate_slab(self, sl):
        d, c_len = self.d, self.c_len
        lb = self.lb[:, sl]
        f = lb + (1.0 - lb) * _sigmoid(self.f_ref[0, self.sub, :, sl].astype(F32))
        logf = jnp.log2(f)
        self.kk_ref[d, :, sl] = (1.0 - f).astype(BF16)
        hi = logf.astype(BF16)
        self.hl_ref[d, 0:c_len, sl] = hi
        self.hl_ref[d, c_len:2 * c_len, sl] = (logf - hi.astype(F32)).astype(BF16)

    def _cumulate(self):
        tri_b = self.tri.astype(BF16)
        tri2 = jnp.concatenate([tri_b, tri_b], axis=1)
        self.cum_ref[self.d] = jnp.dot(tri2, self.hl_ref[self.d], preferred_element_type=F32)

    def _decay_slab(self, sl):
        d, c_len = self.d, self.c_len
        half = c_len // 2
        row_last = 0 if self.reverse else c_len - 1
        row_ref = half if self.reverse else half - 1
        cum = self.cum_ref[d, :, sl]
        last = cum[row_last:row_last + 1, :]
        ref = cum[row_ref:row_ref + 1, :]
        kk = self.kk_ref[d, :, sl]
        qs = self.q_ref[0, self.sub, :, sl]
        self.qd_ref[d, :, sl] = qs * jnp.exp2(jnp.minimum(cum - ref, EXP2_CLAMP)).astype(BF16)
        kd = kk * jnp.exp2(jnp.minimum(ref - cum, EXP2_CLAMP)).astype(BF16)
        self.qin_ref[d, :, sl] = qs * jnp.exp2(cum).astype(BF16)
        dec = jnp.exp2(last)
        fac = jnp.exp2(last - ref)
        for i in range((sl.stop - sl.start) // self.hd):
            h = sl.start // self.hd + i
            lanes = slice(i * self.hd, (i + 1) * self.hd)
            self.kdt_ref[d, h] = jnp.transpose(kd[:, lanes])
            self.dec_ref[d, h:h + 1, :] = dec[:, lanes]
            self.dec_ref[d, self.n_heads + h:self.n_heads + h + 1, :] = fac[:, lanes]

    def _decay_columns(self):
        self.dect_ref[self.d] = jnp.transpose(self.dec_ref[self.d])

    def _head(self, h):
        return slice(h * self.hd, (h + 1) * self.hd)

    def _scores(self, h):
        d, sl = self.d, self._head(h)
        s = jnp.dot(self.qd_ref[d, :, sl], self.kdt_ref[d, h], preferred_element_type=F32)
        self.sc_ref[d, h] = jnp.where(self.tri, s, 0.0).astype(BF16)

    def _output(self, h):
        d, sl = self.d, self._head(h)
        o = (jnp.dot(self.sc_ref[d, h], self.v_ref[0, self.sub, :, sl], preferred_element_type=F32)
             + jnp.dot(self.qin_ref[d, :, sl], self.st_ref[h].astype(BF16),
                       preferred_element_type=F32))
        self.o_ref[0, self.sub, :, sl] = o.astype(self.o_ref.dtype)

    def _update(self, h):
        d, sl = self.d, self._head(h)
        decay = jnp.broadcast_to(self.dect_ref[d, :, h:h + 1], (self.hd, self.hd))
        fac = self.dect_ref[d, :, self.n_heads + h:self.n_heads + h + 1].astype(BF16)
        kout_t = self.kdt_ref[d, h] * jnp.broadcast_to(fac, (self.hd, self.c_len))
        self.st_ref[h] = (self.st_ref[h] * decay
                          + jnp.dot(kout_t, self.v_ref[0, self.sub, :, sl],
                                    preferred_element_type=F32))

    def stages(self):
        heads = range(self.n_heads)
        slabs = self._slabs(2 * LANES)
        out = [[functools.partial(self._gate_slab, sl) for sl in slabs], [self._cumulate],
               [functools.partial(self._decay_slab, sl) for sl in slabs], [self._decay_columns]]
        if self.o_ref is not None:
            out += [[functools.partial(self._scores, h) for h in heads],
                    [functools.partial(self._output, h) for h in heads]]
        return out + [[functools.partial(self._update, h) for h in heads]]


def _lower_bounds(logits_ref, layer):
    out = []
    for d in range(2):
        rows = [logits_ref[d, l:l + 1, :] for l in range(logits_ref.shape[1])]
        m = functools.reduce(jnp.maximum, rows)
        e = [jnp.exp(r - m) for r in rows]
        out.append(sum(e[:layer + 1]) / sum(e))
    return out


def _hgrn_kernel(*refs, n_heads, hd, has_init, emit_o, emit_state):
    qf_ref, ff_ref, vf_ref, qb_ref, fb_ref, vb_ref, lg_ref = refs[:7]
    pos = 7
    if has_init:
        s0f_ref, s0b_ref = refs[pos:pos + 2]
        pos += 2
    if emit_o:
        of_ref, ob_ref = refs[pos:pos + 2]
        pos += 2
    if emit_state:
        sof_ref, sob_ref = refs[pos:pos + 2]
        pos += 2
    sf_ref, sb_ref = refs[pos:pos + 2]
    scratch = refs[pos + 2:]

    j = pl.program_id(1)
    dec_ref = scratch[6]
    assert 2 * n_heads <= dec_ref.shape[1]

    @pl.when(j == 0)
    def _():
        if has_init:
            sf_ref[...] = s0f_ref[0]
            sb_ref[...] = s0b_ref[0]
        else:
            sf_ref[...] = jnp.zeros_like(sf_ref)
            sb_ref[...] = jnp.zeros_like(sb_ref)
        dec_ref[...] = jnp.zeros_like(dec_ref)

    lb_f, lb_b = _lower_bounds(lg_ref, 0)
    n_sub = qf_ref.shape[1]
    for k in range(n_sub):
        dirs = [_HgrnDir(0, qf_ref, ff_ref, vf_ref, lb_f, sf_ref, of_ref if emit_o else None,
                         scratch, False, n_heads, hd, k),
                _HgrnDir(1, qb_ref, fb_ref, vb_ref, lb_b, sb_ref, ob_ref if emit_o else None,
                         scratch, True, n_heads, hd, n_sub - 1 - k)]
        for stage_f, stage_b in zip(dirs[0].stages(), dirs[1].stages()):
            for unit in stage_f + stage_b:
                unit()

    if emit_state:
        @pl.when(j == pl.num_programs(1) - 1)
        def _():
            sof_ref[0] = sf_ref[...]
            sob_ref[0] = sb_ref[...]


def _hgrn(p_view, logits, d_b, n_heads, first_group, init_states, emit_o):
    bsz, n_chunks = p_view.shape[:2]
    hd = d_b // n_heads
    has_init = init_states is not None
    emit_state = not emit_o

    cps = CHUNKS_PER_STEP if n_chunks % CHUNKS_PER_STEP == 0 else 1
    n_steps = n_chunks // cps

    def chunk(j, reverse):
        return n_steps - 1 - j if reverse else j

    def spec(group, reverse):
        def imap(b, j):
            return (b, chunk(j, reverse), 0, first_group + group)
        return pl.BlockSpec((1, cps, CHUNK, d_b), imap)

    in_specs = [spec(0, False), spec(1, False), spec(3, False),
                spec(0, True), spec(2, True), spec(3, True),
                pl.BlockSpec(logits.shape, lambda b, j: (0, 0, 0))]
    args = [p_view] * 6 + [logits]
    state_spec = pl.BlockSpec((1, n_heads, hd, hd), lambda b, j: (b, 0, 0, 0))
    state_shape = jax.ShapeDtypeStruct((bsz, n_heads, hd, hd), F32)
    if has_init:
        in_specs += [state_spec, state_spec]
        args += list(init_states)
    out_specs, out_shape = [], []
    if emit_o:
        o_shape = jax.ShapeDtypeStruct((bsz, n_chunks, CHUNK, d_b), BF16)
        out_specs += [pl.BlockSpec((1, cps, CHUNK, d_b), lambda b, j: (b, chunk(j, False), 0, 0)),
                      pl.BlockSpec((1, cps, CHUNK, d_b), lambda b, j: (b, chunk(j, True), 0, 0))]
        out_shape += [o_shape, o_shape]
    if emit_state:
        out_specs += [state_spec, state_spec]
        out_shape += [state_shape, state_shape]
    kern = functools.partial(_hgrn_kernel, n_heads=n_heads, hd=hd, has_init=has_init,
                             emit_o=emit_o, emit_state=emit_state)
    return pl.pallas_call(
        kern,
        grid=(bsz, n_steps),
        in_specs=in_specs,
        out_specs=out_specs,
        out_shape=out_shape,
        scratch_shapes=[pltpu.VMEM((n_heads, hd, hd), F32),
                        pltpu.VMEM((n_heads, hd, hd), F32),
                        pltpu.VMEM((2, 2 * CHUNK, d_b), BF16),
                        pltpu.VMEM((2, CHUNK, d_b), BF16),
                        pltpu.VMEM((2, CHUNK, d_b), F32),
                        pltpu.VMEM((2, CHUNK, d_b), BF16),
                        pltpu.VMEM((2, n_heads, hd, CHUNK), BF16),
                        pltpu.VMEM((2, CHUNK, d_b), BF16),
                        pltpu.VMEM((2, hd, hd), F32),
                        pltpu.VMEM((2, hd, hd), F32),
                        pltpu.VMEM((2, n_heads, CHUNK, CHUNK), BF16)],
        compiler_params=pltpu.CompilerParams(
            dimension_semantics=("parallel", "arbitrary"),
            vmem_limit_bytes=VMEM_LIMIT),
        name="hgrn2_latent" if emit_o else "hgrn2_context",
    )(*args)


def _outproj_kernel(ya_ref, of_ref, ob_ref, gb_ref, x_ref, gate_ref, hnw_ref, fnw_ref, w_ref,
                    o_ref, y_ref, *, n_heads, hd):
    _, rb, wb, d_a = ya_ref.shape
    rows = rb * wb
    d = x_ref.shape[3]
    y_ref[:, 0:d_a] = ya_ref[0].reshape(rows, d_a)
    hnw = hnw_ref[...]
    for h in range(n_heads):
        sl = slice(h * hd, (h + 1) * hd)
        o = of_ref[0, :, :, sl].astype(F32) + ob_ref[0, :, :, sl].astype(F32)
        ms = jnp.mean(o * o, axis=-1, keepdims=True)
        on = jnp.transpose(o * lax.rsqrt(ms + EPS) * hnw, (1, 0, 2))
        yb = on * gb_ref[0, :, :, sl].astype(F32)
        y_ref[:, d_a + h * hd:d_a + (h + 1) * hd] = yb.reshape(rows, hd).astype(BF16)
    acc = jnp.dot(y_ref[...], w_ref[...], preferred_element_type=F32)
    z = x_ref[0].reshape(rows, d) + gate_ref[0] * acc
    ms = jnp.mean(z * z, axis=-1, keepdims=True)
    o_ref[0] = (z * lax.rsqrt(ms + EPS) * fnw_ref[...]).reshape(rb, wb, d)


def _out_projection(ya4, of4, ob4, pa4, gb_block, x4, mod3, hnw, fnw, w_bf16, n_heads, rb, wb):
    bsz, n_r, n_w, d = x4.shape
    d_a = ya4.shape[3]
    d_b = of4.shape[3]
    hd = d_b // n_heads
    kern = functools.partial(_outproj_kernel, n_heads=n_heads, hd=hd)

    def raster(c, col=0):
        return pl.BlockSpec((1, rb, wb, c), lambda b, r, w: (b, r, w, col))

    def colmajor(c):
        return pl.BlockSpec((1, wb, rb, c), lambda b, r, w: (b, w, r, 0))

    return pl.pallas_call(
        kern,
        grid=(bsz, n_r // rb, n_w // wb),
        in_specs=[raster(d_a), colmajor(d_b), colmajor(d_b), raster(d_b, gb_block), raster(d),
                  pl.BlockSpec((1, 1, d), lambda b, r, w: (b, 0, 2)),
                  pl.BlockSpec((1, hd), lambda b, r, w: (0, 0)),
                  pl.BlockSpec((1, d), lambda b, r, w: (0, 0)),
                  pl.BlockSpec((d_a + d_b, d), lambda b, r, w: (0, 0),
                               pipeline_mode=pl.Buffered(1))],
        out_specs=raster(d),
        out_shape=jax.ShapeDtypeStruct((bsz, n_r, n_w, d), F32),
        scratch_shapes=[pltpu.VMEM((rb * wb, d_a + d_b), BF16)],
        compiler_params=pltpu.CompilerParams(
            dimension_semantics=("parallel", "parallel", "parallel"),
            vmem_limit_bytes=VMEM_LIMIT),
        name="out_projection",
    )(ya4, of4, ob4, pa4, x4, mod3, hnw.reshape(1, hd), fnw.reshape(1, d), w_bf16)


def kernel(x, c, ctx, c_ctx, ada_w, ada_b, norm_w, w_in, conv_w, conv_b, lru_wr, lru_br, lru_wi,
           lru_bi, lru_lambda, hgrn_lb_logits, hgrn_norm_w, w_out, final_norm_w):
    bsz, t_len, d = x.shape
    tc_len = ctx.shape[1]
    assert ada_w.shape[0] == 1, "single-layer stack only"
    d_a = conv_w.shape[2]
    d_b = hgrn_lb_logits.shape[2]
    hd = hgrn_norm_w.shape[1]
    n_heads = d_b // hd
    n_blocks_a = lru_wr.shape[2]
    n_cols = w_in.shape[2]
    assert t_len == GRID_W * CHUNK and tc_len % CHUNK == 0
    assert d_a // n_blocks_a == LANES and hd == LANES
    assert (2 * d_a) % d_b == 0 and n_cols == 2 * d_a + 5 * d_b
    first_b_group = (2 * d_a) // d_b
    n_rows_grid = t_len // GRID_W

    n_rows = -(-(bsz + 1) // SUBLANES) * SUBLANES
    cc = jnp.zeros((n_rows, d), F32).at[:bsz].set(c).at[bsz].set(c_ctx)
    mod3 = _modulation(cc, ada_w[0], ada_b[0]).reshape(n_rows, 1, 3 * d)

    w_in_b = w_in[0]

    tm = 1024 if t_len % 1024 == 0 else t_len
    tn = 1024 if d_a % 1024 == 0 and d_b % 1024 == 0 else min(d_a, d_b)
    a_tiles = (2 * d_a) // tn
    b_tiles = (4 * d_b) // tn
    q_tiles = d_b // tn
    h_lat, h_col = _normalise(x.reshape(bsz, n_rows_grid, GRID_W, d), mod3, norm_w[0], 16)
    tm_h = 2 * tm if (bsz * t_len) % (2 * tm) == 0 else tm
    p_b = _projection_from_h(h_col.reshape(bsz * t_len, d), w_in_b, lambda j: j + a_tiles,
                             4 * d_b, tm_h, tn, (0, q_tiles))
    p_b = p_b.reshape(bsz, GRID_W, n_rows_grid, 4 * d_b)
    p_a = _projection_from_h(h_lat.reshape(bsz * t_len, d), w_in_b,
                             lambda j: jnp.where(j < a_tiles, j, j + b_tiles),
                             2 * d_a + d_b, tm_h, tn, (d_a // tn, (2 * d_a + d_b) // tn))
    xa_tiles = d_a // tn
    ctx_cols = d_a + 4 * d_b
    p_ctx = _in_projection(ctx.reshape(bsz * tc_len, d), mod3, norm_w[0], w_in_b,
                           lambda i: bsz, lambda j: jnp.where(j < xa_tiles, j, j + xa_tiles),
                           ctx_cols, bsz * tc_len, tn, (xa_tiles, xa_tiles + q_tiles))
    p_lat = p_a.reshape(bsz, t_len, 2 * d_a + d_b)
    p_ctx = p_ctx.reshape(bsz, tc_len, ctx_cols)

    def gate_w(dirn):
        return jnp.concatenate([lru_wr[0, dirn], lru_wi[0, dirn]], axis=-1).astype(BF16)

    def gate_b(dirn):
        return jnp.concatenate([lru_br[0, dirn].reshape(n_blocks_a, 1, LANES),
                                lru_bi[0, dirn].reshape(n_blocks_a, 1, LANES)], axis=-1)

    cb = 256 if d_a % 256 == 0 else LANES
    ya, w_out_b = _rglru(p_lat, p_ctx, conv_w[0], conv_b[0], gate_w(0), gate_b(0), gate_w(1),
                         gate_b(1), lru_lambda[0], w_out[0], d_a, cb, tt=256)

    assert d_a % d_b == 0
    states = _hgrn(p_ctx.reshape(bsz, tc_len // CHUNK, CHUNK, ctx_cols), hgrn_lb_logits, d_b,
                   n_heads, d_a // d_b, None, False)
    o_f, o_b = _hgrn(p_b, hgrn_lb_logits, d_b, n_heads, 0, states, True)

    grid4 = lambda z: z.reshape(bsz, n_rows_grid, GRID_W, z.shape[-1])
    out = _out_projection(grid4(ya), o_f, o_b, grid4(p_lat), first_b_group, grid4(x), mod3,
                          hgrn_norm_w[0], final_norm_w, w_out_b, n_heads, 16, 16)
    return out.reshape(bsz, t_len, d)
```

```python
import functools

import jax
import jax.numpy as jnp
from jax import lax
from jax.experimental import pallas as pl
from jax.experimental.pallas import tpu as pltpu

GRID_W = 64
CHUNK = 64
LRU_C = 8.0
EPS = 1e-6
CONV_PAD_L = 2
LANES = 128
SUBLANES = 8
EXP2_CLAMP = 115.0
NEG_LOG2E = -1.4426950408889634
VMEM_LIMIT = 56 * 1024 * 1024
PROLOGUE_ROWS = 16
SEG_PAD = 4
CHUNKS_PER_STEP = 4
TILES_PER_TRIP = 8

F32 = jnp.float32
BF16 = jnp.bfloat16


def _sigmoid(z):
    return 1.0 / (1.0 + jnp.exp2(z * NEG_LOG2E))


def _silu(z):
    return z * _sigmoid(z)


def _softplus(z):
    return jnp.maximum(z, 0.0) + jnp.log1p(jnp.exp(-jnp.abs(z)))


def _mod_kernel(c_ref, w_ref, b_ref, o_ref):
    s = _silu(c_ref[...])
    o_ref[...] = jnp.dot(s.astype(BF16), w_ref[...].astype(BF16),
                         preferred_element_type=F32) + b_ref[...]


def _modulation(cc, w, b):
    rows, d = cc.shape
    n = w.shape[1]
    tn = 512 if n % 512 == 0 else n
    return pl.pallas_call(
        _mod_kernel,
        grid=(n // tn,),
        in_specs=[pl.BlockSpec((rows, d), lambda j: (0, 0)),
                  pl.BlockSpec((d, tn), lambda j: (0, j)),
                  pl.BlockSpec((1, tn), lambda j: (0, j))],
        out_specs=pl.BlockSpec((rows, tn), lambda j: (0, j)),
        out_shape=jax.ShapeDtypeStruct((rows, n), F32),
        compiler_params=pltpu.CompilerParams(
            dimension_semantics=("arbitrary",), vmem_limit_bytes=VMEM_LIMIT),
        name="adaln_modulation",
    )(cc, w, b.reshape(1, n))


def _store_projection(o_ref, res, j, silu_tiles):
    lo, hi = silu_tiles
    z = res.astype(o_ref.dtype)
    if hi <= lo:
        o_ref[...] = z
        return
    in_range = jnp.logical_and(j >= lo, j < hi)
    o_ref[...] = jnp.where(in_range, z * (0.5 * jnp.tanh(0.5 * z) + 0.5), z)


def _inproj_kernel(x_ref, shift_ref, scale_ref, nw_ref, w_ref, o_ref, h_ref, *, silu_tiles):
    @pl.when(pl.program_id(1) == 0)
    def _():
        rows = min(PROLOGUE_ROWS, x_ref.shape[0])
        gain = nw_ref[...] * (1.0 + scale_ref[0])

        def slab(s, carry):
            sl = pl.ds(pl.multiple_of(s * rows, rows), rows)
            x = x_ref[sl, :]
            rs = lax.rsqrt(jnp.mean(x * x, axis=-1, keepdims=True) + EPS)
            h_ref[sl, :] = (x_ref[sl, :] * rs * gain + shift_ref[0]).astype(BF16)
            return carry

        lax.fori_loop(0, x_ref.shape[0] // rows, slab, 0, unroll=4)

    res = jnp.dot(h_ref[...], w_ref[...].astype(BF16), preferred_element_type=F32)
    _store_projection(o_ref, res, pl.program_id(1), silu_tiles)


def _in_projection(x2d, mod3, norm_w, w_bf16, row_of_tile, wcol_of_tile, n_out, tm, tn, silu_tiles):
    m, d = x2d.shape
    return pl.pallas_call(
        functools.partial(_inproj_kernel, silu_tiles=silu_tiles),
        grid=(m // tm, n_out // tn),
        in_specs=[pl.BlockSpec((tm, d), lambda i, j: (i, 0)),
                  pl.BlockSpec((1, 1, d), lambda i, j: (row_of_tile(i), 0, 0)),
                  pl.BlockSpec((1, 1, d), lambda i, j: (row_of_tile(i), 0, 1)),
                  pl.BlockSpec((1, d), lambda i, j: (0, 0)),
                  pl.BlockSpec((d, tn), lambda i, j: (0, wcol_of_tile(j)))],
        out_specs=pl.BlockSpec((tm, tn), lambda i, j: (i, j)),
        out_shape=jax.ShapeDtypeStruct((m, n_out), BF16),
        scratch_shapes=[pltpu.VMEM((tm, d), BF16)],
        compiler_params=pltpu.CompilerParams(
            dimension_semantics=("parallel", "arbitrary"),
            vmem_limit_bytes=VMEM_LIMIT),
        name="in_projection",
    )(x2d, mod3, mod3, norm_w.reshape(1, d), w_bf16)


def _matmul_kernel(h_ref, w_ref, o_ref, *, silu_tiles):
    res = jnp.dot(h_ref[...], w_ref[...].astype(BF16), preferred_element_type=F32)
    _store_projection(o_ref, res, pl.program_id(1), silu_tiles)


def _projection_from_h(h2d, w_f32, wcol_of_tile, n_out, tm, tn, silu_tiles):
    m, d = h2d.shape
    return pl.pallas_call(
        functools.partial(_matmul_kernel, silu_tiles=silu_tiles),
        grid=(m // tm, n_out // tn),
        in_specs=[pl.BlockSpec((tm, d), lambda i, j: (i, 0)),
                  pl.BlockSpec((d, tn), lambda i, j: (0, wcol_of_tile(j)))],
        out_specs=pl.BlockSpec((tm, tn), lambda i, j: (i, j)),
        out_shape=jax.ShapeDtypeStruct((m, n_out), BF16),
        compiler_params=pltpu.CompilerParams(
            dimension_semantics=("parallel", "arbitrary"),
            vmem_limit_bytes=VMEM_LIMIT),
        name="projection_from_h",
    )(h2d, w_f32)


def _normalise_kernel(x_ref, shift_ref, scale_ref, nw_ref, hr_ref, hc_ref, *, slab):
    _, n_r, n_w, d = x_ref.shape
    rr = 2 * SUBLANES
    lanes = [slice(s * slab, (s + 1) * slab) for s in range(d // slab)]

    def row_group(g, carry):
        rows = pl.ds(pl.multiple_of(g * rr, rr), rr)
        sq = jnp.zeros((rr, n_w, slab), F32)
        for sl in lanes:
            xs = x_ref[0, rows, :, sl]
            sq = sq + xs * xs
        rs = lax.rsqrt(jnp.sum(sq, axis=-1, keepdims=True) * (1.0 / d) + EPS)
        for sl in lanes:
            gain = nw_ref[:, sl] * (1.0 + scale_ref[0][:, sl])
            y = x_ref[0, rows, :, sl] * rs * gain + shift_ref[0][:, sl]
            hr_ref[0, rows, :, sl] = y.astype(BF16)
            hc_ref[0, :, rows, sl] = jnp.transpose(y, (1, 0, 2)).astype(BF16)
        return carry

    lax.fori_loop(0, n_r // rr, row_group, 0)


def _normalise(x4d, mod3, norm_w, wb):
    bsz, n_r, n_w, d = x4d.shape
    return pl.pallas_call(
        functools.partial(_normalise_kernel, slab=2 * LANES),
        grid=(bsz, n_w // wb),
        in_specs=[pl.BlockSpec((1, n_r, wb, d), lambda b, w: (b, 0, w, 0)),
                  pl.BlockSpec((1, 1, d), lambda b, w: (b, 0, 0)),
                  pl.BlockSpec((1, 1, d), lambda b, w: (b, 0, 1)),
                  pl.BlockSpec((1, d), lambda b, w: (0, 0))],
        out_specs=[pl.BlockSpec((1, n_r, wb, d), lambda b, w: (b, 0, w, 0)),
                   pl.BlockSpec((1, wb, n_r, d), lambda b, w: (b, w, 0, 0))],
        out_shape=[jax.ShapeDtypeStruct((bsz, n_r, n_w, d), BF16),
                   jax.ShapeDtypeStruct((bsz, n_w, n_r, d), BF16)],
        compiler_params=pltpu.CompilerParams(
            dimension_semantics=("parallel", "parallel"), vmem_limit_bytes=VMEM_LIMIT),
        name="normalise_rows",
    )(x4d, mod3, mod3, norm_w.reshape(1, d))


def _group_scan(a, b, reverse):
    row = lax.broadcasted_iota(jnp.int32, a.shape, 1)
    for k in (1, 2, 4):
        if reverse:
            a_sh = pltpu.roll(a, SUBLANES - k, axis=1)
            b_sh = pltpu.roll(b, SUBLANES - k, axis=1)
            m = row < SUBLANES - k
        else:
            a_sh = pltpu.roll(a, k, axis=1)
            b_sh = pltpu.roll(b, k, axis=1)
            m = row >= k
        b = jnp.where(m, a * b_sh + b, b)
        a = jnp.where(m, a * a_sh, a)
    return a, b


def _sqrt_unit(x):
    return jnp.where(x > 0.0, x * lax.rsqrt(x), 0.0)


def _segment_scan(a, b, carry, reverse, scr_ref):
    tt = a.shape[0]
    seg = tt // SUBLANES
    pitch = seg + SEG_PAD
    for s in range(SUBLANES):
        rows = pl.ds(pitch * s, seg, stride=1)
        scr_ref[0, rows, :] = a[seg * s:seg * (s + 1), :]
        scr_ref[1, rows, :] = b[seg * s:seg * (s + 1), :]
    hs, ps = [None] * seg, [None] * seg
    h = p = None
    for j in (range(seg - 1, -1, -1) if reverse else range(seg)):
        step = pl.ds(j, SUBLANES, stride=pitch)
        a_j = scr_ref[0, step, :]
        b_j = scr_ref[1, step, :]
        h = b_j if h is None else a_j * h + b_j
        p = a_j if p is None else a_j * p
        hs[j], ps[j] = h, p
    g3, e3 = _group_scan(p[None], h[None], reverse)
    end = e3[0] + g3[0] * carry
    row = lax.broadcasted_iota(jnp.int32, end.shape, 0)
    if reverse:
        enter = jnp.where(row == SUBLANES - 1, carry, pltpu.roll(end, SUBLANES - 1, axis=0))
        carry_out = end[0:1, :]
    else:
        enter = jnp.where(row == 0, carry, pltpu.roll(end, 1, axis=0))
        carry_out = end[SUBLANES - 1:SUBLANES, :]
    for j in range(seg):
        scr_ref[2, pl.ds(j, SUBLANES, stride=pitch), :] = hs[j] + ps[j] * enter
    h_time = [scr_ref[2, pl.ds(pitch * s, seg, stride=1), :] for s in range(SUBLANES)]
    return jnp.concatenate(h_time, axis=0), carry_out


def _lru_block(u, wg, bg, sp, carry, reverse, scr_ref):
    g = jnp.dot(u.astype(BF16), wg, preferred_element_type=F32) + bg
    r = _sigmoid(g[:, :LANES])
    i = _sigmoid(g[:, LANES:])
    a = jnp.exp2(r * sp)
    b = _sqrt_unit(1.0 - a * a) * (i * u)
    return _segment_scan(a, b, carry, reverse, scr_ref)


def _rglru_kernel(xa_ref, ga_ref, xc_ref, cw_ref, cb_ref, wgf_ref, bgf_ref, wgb_ref, bgb_ref,
                  lam_ref, wcast_ref, o_ref, wcast_out_ref, xf_ref, xcf_ref, u_ref, hf_ref, scr_ref,
                  *, tt):
    wcast_out_ref[...] = wcast_ref[...].astype(BF16)
    t_len = xa_ref.shape[1]
    tc_len = xc_ref.shape[1]
    nblk = xa_ref.shape[2] // LANES
    lanes = [slice(k * LANES, (k + 1) * LANES) for k in range(nblk)]
    zeros = jnp.zeros((SUBLANES, LANES), F32)
    for k, sl in enumerate(lanes):
        xf_ref[k, 0:SUBLANES, :] = zeros
        xf_ref[k, SUBLANES:SUBLANES + t_len, :] = xa_ref[0, :, sl].astype(F32)
        xf_ref[k, SUBLANES + t_len:, :] = zeros
        xcf_ref[k, 0:SUBLANES, :] = zeros
        xcf_ref[k, SUBLANES:SUBLANES + tc_len, :] = xc_ref[0, :, sl].astype(F32)
        xcf_ref[k, SUBLANES + tc_len:, :] = zeros

    sp_f = _softplus(-lam_ref[0:1, :]) * (LRU_C * NEG_LOG2E)
    sp_b = _softplus(-lam_ref[1:2, :]) * (LRU_C * NEG_LOG2E)

    def conv(src_ref, k, t0):
        u = cb_ref[:, lanes[k]]
        for tap in range(cw_ref.shape[0]):
            off = SUBLANES - CONV_PAD_L + tap
            if off % SUBLANES == 0:
                start = t0 + off
                rows = pl.ds(start if isinstance(start, int) else pl.multiple_of(start, SUBLANES), tt)
            else:
                rows = pl.ds(t0 + off, tt, stride=1)
            u = u + src_ref[k, rows, :] * cw_ref[tap:tap + 1, lanes[k]]
        return u

    def block(u, k, carry, reverse, slot):
        scr = scr_ref.at[k * TILES_PER_TRIP + slot]
        if reverse:
            return _lru_block(u, wgb_ref[k], bgb_ref[k], sp_b[:, lanes[k]], carry, True, scr)
        return _lru_block(u, wgf_ref[k], bgf_ref[k], sp_f[:, lanes[k]], carry, False, scr)

    n_lat = t_len // tt
    n_ctx = tc_len // tt
    assert n_lat % TILES_PER_TRIP == 0
    carry0 = tuple(jnp.zeros((1, LANES), F32) for _ in lanes)

    def tile_start(trip, slot, n_tiles, reverse):
        idx = trip * TILES_PER_TRIP + slot
        return pl.multiple_of((n_tiles - 1 - idx if reverse else idx) * tt, tt)

    def ctx_pass(reverse):
        carry = carry0
        for idx in range(n_ctx):
            t0 = (n_ctx - 1 - idx if reverse else idx) * tt
            carry = tuple(block(conv(xcf_ref, k, t0), k, carry[k], reverse, idx % TILES_PER_TRIP)[1]
                          for k in range(nblk))
        return carry

    def lat_f(trip, carry):
        for slot in range(TILES_PER_TRIP):
            t0 = tile_start(trip, slot, n_lat, False)
            rows = pl.ds(t0, tt)
            out = []
            for k, sl in enumerate(lanes):
                u = conv(xf_ref, k, t0)
                u_ref[rows, sl] = u
                h, c_out = block(u, k, carry[k], False, slot)
                hf_ref[rows, sl] = h
                out.append(c_out)
            carry = tuple(out)
        return carry

    lax.fori_loop(0, n_lat // TILES_PER_TRIP, lat_f, ctx_pass(False))

    def lat_b(trip, carry):
        for slot in range(TILES_PER_TRIP):
            rows = pl.ds(tile_start(trip, slot, n_lat, True), tt)
            out = []
            for k, sl in enumerate(lanes):
                h, c_out = block(u_ref[rows, sl], k, carry[k], True, slot)
                gate = ga_ref[0, rows, sl].astype(F32)
                o_ref[0, rows, sl] = ((hf_ref[rows, sl] + h) * gate).astype(o_ref.dtype)
                out.append(c_out)
            carry = tuple(out)
        return carry

    lax.fori_loop(0, n_lat // TILES_PER_TRIP, lat_b, ctx_pass(True))


def _rglru(p_lat, p_ctx, conv_w, conv_b, wg_f, bg_f, wg_b, bg_b, lam, w_cast, d_a, cb, tt):
    bsz, t_len, _ = p_lat.shape
    tc_len = p_ctx.shape[1]
    nblk = cb // LANES
    ncb = d_a // cb
    wrows, wcols = w_cast.shape
    assert wrows % (bsz * ncb) == 0
    wr = wrows // (bsz * ncb)
    wspec = pl.BlockSpec((wr, wcols), lambda b, c: (b * ncb + c, 0))
    kern = functools.partial(_rglru_kernel, tt=tt)
    return pl.pallas_call(
        kern,
        grid=(bsz, ncb),
        in_specs=[pl.BlockSpec((1, t_len, cb), lambda b, c: (b, 0, c)),
                  pl.BlockSpec((1, t_len, cb), lambda b, c: (b, 0, ncb + c)),
                  pl.BlockSpec((1, tc_len, cb), lambda b, c: (b, 0, c)),
                  pl.BlockSpec((conv_w.shape[0], cb), lambda b, c: (0, c)),
                  pl.BlockSpec((1, cb), lambda b, c: (0, c)),
                  pl.BlockSpec((nblk, LANES, 2 * LANES), lambda b, c: (c, 0, 0)),
                  pl.BlockSpec((nblk, 1, 2 * LANES), lambda b, c: (c, 0, 0)),
                  pl.BlockSpec((nblk, LANES, 2 * LANES), lambda b, c: (c, 0, 0)),
                  pl.BlockSpec((nblk, 1, 2 * LANES), lambda b, c: (c, 0, 0)),
                  pl.BlockSpec((2, cb), lambda b, c: (0, c)),
                  wspec],
        out_specs=[pl.BlockSpec((1, t_len, cb), lambda b, c: (b, 0, c)), wspec],
        out_shape=[jax.ShapeDtypeStruct((bsz, t_len, d_a), BF16),
                   jax.ShapeDtypeStruct((wrows, wcols), BF16)],
        scratch_shapes=[pltpu.VMEM((nblk, t_len + 2 * SUBLANES, LANES), F32),
                        pltpu.VMEM((nblk, tc_len + 2 * SUBLANES, LANES), F32),
                        pltpu.VMEM((t_len, cb), F32),
                        pltpu.VMEM((t_len, cb), F32),
                        pltpu.VMEM((nblk * TILES_PER_TRIP, 3, tt + SUBLANES * SEG_PAD, LANES),
                                   F32)],
        compiler_params=pltpu.CompilerParams(
            dimension_semantics=("parallel", "parallel"),
            vmem_limit_bytes=VMEM_LIMIT),
        name="rglru",
    )(p_lat, p_lat, p_ctx, conv_w, conv_b.reshape(1, d_a), wg_f, bg_f, wg_b, bg_b, lam, w_cast)


class _HgrnDir:
    def __init__(self, d, q_ref, f_ref, v_ref, lb, st_ref, o_ref, scratch, reverse, n_heads, hd,
                 sub):
        self.d, self.q_ref, self.f_ref, self.v_ref, self.lb = d, q_ref, f_ref, v_ref, lb
        self.sub = sub
        self.st_ref, self.o_ref, self.reverse, self.n_heads, self.hd = st_ref, o_ref, reverse, n_heads, hd
        (self.hl_ref, self.kk_ref, self.cum_ref, self.qd_ref, self.kdt_ref, self.qin_ref,
         self.dec_ref, self.dect_ref, self.sc_ref) = scratch
        self.c_len = q_ref.shape[2]
        r_i = lax.broadcasted_iota(jnp.int32, (self.c_len, self.c_len), 0)
        c_i = lax.broadcasted_iota(jnp.int32, (self.c_len, self.c_len), 1)
        self.tri = (r_i <= c_i) if reverse else (r_i >= c_i)

    def _slabs(self, width):
        total = self.n_heads * self.hd
        return [slice(s, s + width) for s in range(0, total, width)]


    def _gate_slab(self, sl):
        d, c_len = self.d, self.c_len
        lb = self.lb[:, sl]
        f = lb + (1.0 - lb) * _sigmoid(self.f_ref[0, self.sub, :, sl].astype(F32))
        logf = jnp.log2(f)
        self.kk_ref[d, :, sl] = (1.0 - f).astype(BF16)
        hi = logf.astype(BF16)
        self.hl_ref[d, 0:c_len, sl] = hi
        self.hl_ref[d, c_len:2 * c_len, sl] = (logf - hi.astype(F32)).astype(BF16)

    def _cumulate(self):
        tri_b = self.tri.astype(BF16)
        tri2 = jnp.concatenate([tri_b, tri_b], axis=1)
        self.cum_ref[self.d] = jnp.dot(tri2, self.hl_ref[self.d], preferred_element_type=F32)

    def _decay_slab(self, sl):
        d, c_len = self.d, self.c_len
        half = c_len // 2
        row_last = 0 if self.reverse else c_len - 1
        row_ref = half if self.reverse else half - 1
        cum = self.cum_ref[d, :, sl]
        last = cum[row_last:row_last + 1, :]
        ref = cum[row_ref:row_ref + 1, :]
        kk = self.kk_ref[d, :, sl]
        qs = self.q_ref[0, self.sub, :, sl]
        self.qd_ref[d, :, sl] = qs * jnp.exp2(jnp.minimum(cum - ref, EXP2_CLAMP)).astype(BF16)
        kd = kk * jnp.exp2(jnp.minimum(ref - cum, EXP2_CLAMP)).astype(BF16)
        self.qin_ref[d, :, sl] = qs * jnp.exp2(cum).astype(BF16)
        dec = jnp.exp2(last)
        fac = jnp.exp2(last - ref)
        for i in range((sl.stop - sl.start) // self.hd):
            h = sl.start // self.hd + i
            lanes = slice(i * self.hd, (i + 1) * self.hd)
            self.kdt_ref[d, h] = jnp.transpose(kd[:, lanes])
            self.dec_ref[d, h:h + 1, :] = dec[:, lanes]
            self.dec_ref[d, self.n_heads + h:self.n_heads + h + 1, :] = fac[:, lanes]

    def _decay_columns(self):
        self.dect_ref[self.d] = jnp.transpose(self.dec_ref[self.d])

    def _head(self, h):
        return slice(h * self.hd, (h + 1) * self.hd)

    def _scores(self, h):
        d, sl = self.d, self._head(h)
        s = jnp.dot(self.qd_ref[d, :, sl], self.kdt_ref[d, h], preferred_element_type=F32)
        self.sc_ref[d, h] = jnp.where(self.tri, s, 0.0).astype(BF16)

    def _output(self, h):
        d, sl = self.d, self._head(h)
        o = (jnp.dot(self.sc_ref[d, h], self.v_ref[0, self.sub, :, sl], preferred_element_type=F32)
             + jnp.dot(self.qin_ref[d, :, sl], self.st_ref[h].astype(BF16),
                       preferred_element_type=F32))
        self.o_ref[0, self.sub, :, sl] = o.astype(self.o_ref.dtype)

    def _update(self, h):
        d, sl = self.d, self._head(h)
        decay = jnp.broadcast_to(self.dect_ref[d, :, h:h + 1], (self.hd, self.hd))
        fac = self.dect_ref[d, :, self.n_heads + h:self.n_heads + h + 1].astype(BF16)
        kout_t = self.kdt_ref[d, h] * jnp.broadcast_to(fac, (self.hd, self.c_len))
        self.st_ref[h] = (self.st_ref[h] * decay
                          + jnp.dot(kout_t, self.v_ref[0, self.sub, :, sl],
                                    preferred_element_type=F32))

    def stages(self):
        heads = range(self.n_heads)
        slabs = self._slabs(2 * LANES)
        out = [[functools.partial(self._gate_slab, sl) for sl in slabs], [self._cumulate],
               [functools.partial(self._decay_slab, sl) for sl in slabs], [self._decay_columns]]
        if self.o_ref is not None:
            out += [[functools.partial(self._scores, h) for h in heads],
                    [functools.partial(self._output, h) for h in heads]]
        return out + [[functools.partial(self._update, h) for h in heads]]


def _lower_bounds(logits_ref, layer):
    out = []
    for d in range(2):
        rows = [logits_ref[d, l:l + 1, :] for l in range(logits_ref.shape[1])]
        m = functools.reduce(jnp.maximum, rows)
        e = [jnp.exp(r - m) for r in rows]
        out.append(sum(e[:layer + 1]) / sum(e))
    return out


def _hgrn_kernel(*refs, n_heads, hd, has_init, emit_o, emit_state):
    qf_ref, ff_ref, vf_ref, qb_ref, fb_ref, vb_ref, lg_ref = refs[:7]
    pos = 7
    if has_init:
        s0f_ref, s0b_ref = refs[pos:pos + 2]
        pos += 2
    if emit_o:
        of_ref, ob_ref = refs[pos:pos + 2]
        pos += 2
    if emit_state:
        sof_ref, sob_ref = refs[pos:pos + 2]
        pos += 2
    sf_ref, sb_ref = refs[pos:pos + 2]
    scratch = refs[pos + 2:]

    j = pl.program_id(1)
    dec_ref = scratch[6]
    assert 2 * n_heads <= dec_ref.shape[1]

    @pl.when(j == 0)
    def _():
        if has_init:
            sf_ref[...] = s0f_ref[0]
            sb_ref[...] = s0b_ref[0]
        else:
            sf_ref[...] = jnp.zeros_like(sf_ref)
            sb_ref[...] = jnp.zeros_like(sb_ref)
        dec_ref[...] = jnp.zeros_like(dec_ref)

    lb_f, lb_b = _lower_bounds(lg_ref, 0)
    n_sub = qf_ref.shape[1]
    for k in range(n_sub):
        dirs = [_HgrnDir(0, qf_ref, ff_ref, vf_ref, lb_f, sf_ref, of_ref if emit_o else None,
                         scratch, False, n_heads, hd, k),
                _HgrnDir(1, qb_ref, fb_ref, vb_ref, lb_b, sb_ref, ob_ref if emit_o else None,
                         scratch, True, n_heads, hd, n_sub - 1 - k)]
        for stage_f, stage_b in zip(dirs[0].stages(), dirs[1].stages()):
            for unit in stage_f + stage_b:
                unit()

    if emit_state:
        @pl.when(j == pl.num_programs(1) - 1)
        def _():
            sof_ref[0] = sf_ref[...]
            sob_ref[0] = sb_ref[...]


def _hgrn(p_view, logits, d_b, n_heads, first_group, init_states, emit_o):
    bsz, n_chunks = p_view.shape[:2]
    hd = d_b // n_heads
    has_init = init_states is not None
    emit_state = not emit_o

    cps = CHUNKS_PER_STEP if n_chunks % CHUNKS_PER_STEP == 0 else 1
    n_steps = n_chunks // cps

    def chunk(j, reverse):
        return n_steps - 1 - j if reverse else j

    def spec(group, reverse):
        def imap(b, j):
            return (b, chunk(j, reverse), 0, first_group + group)
        return pl.BlockSpec((1, cps, CHUNK, d_b), imap)

    in_specs = [spec(0, False), spec(1, False), spec(3, False),
                spec(0, True), spec(2, True), spec(3, True),
                pl.BlockSpec(logits.shape, lambda b, j: (0, 0, 0))]
    args = [p_view] * 6 + [logits]
    state_spec = pl.BlockSpec((1, n_heads, hd, hd), lambda b, j: (b, 0, 0, 0))
    state_shape = jax.ShapeDtypeStruct((bsz, n_heads, hd, hd), F32)
    if has_init:
        in_specs += [state_spec, state_spec]
        args += list(init_states)
    out_specs, out_shape = [], []
    if emit_o:
        o_shape = jax.ShapeDtypeStruct((bsz, n_chunks, CHUNK, d_b), BF16)
        out_specs += [pl.BlockSpec((1, cps, CHUNK, d_b), lambda b, j: (b, chunk(j, False), 0, 0)),
                      pl.BlockSpec((1, cps, CHUNK, d_b), lambda b, j: (b, chunk(j, True), 0, 0))]
        out_shape += [o_shape, o_shape]
    if emit_state:
        out_specs += [state_spec, state_spec]
        out_shape += [state_shape, state_shape]
    kern = functools.partial(_hgrn_kernel, n_heads=n_heads, hd=hd, has_init=has_init,
                             emit_o=emit_o, emit_state=emit_state)
    return pl.pallas_call(
        kern,
        grid=(bsz, n_steps),
        in_specs=in_specs,
        out_specs=out_specs,
        out_shape=out_shape,
        scratch_shapes=[pltpu.VMEM((n_heads, hd, hd), F32),
                        pltpu.VMEM((n_heads, hd, hd), F32),
                        pltpu.VMEM((2, 2 * CHUNK, d_b), BF16),
                        pltpu.VMEM((2, CHUNK, d_b), BF16),
                        pltpu.VMEM((2, CHUNK, d_b), F32),
                        pltpu.VMEM((2, CHUNK, d_b), BF16),
                        pltpu.VMEM((2, n_heads, hd, CHUNK), BF16),
                        pltpu.VMEM((2, CHUNK, d_b), BF16),
                        pltpu.VMEM((2, hd, hd), F32),
                        pltpu.VMEM((2, hd, hd), F32),
                        pltpu.VMEM((2, n_heads, CHUNK, CHUNK), BF16)],
        compiler_params=pltpu.CompilerParams(
            dimension_semantics=("parallel", "arbitrary"),
            vmem_limit_bytes=VMEM_LIMIT),
        name="hgrn2_latent" if emit_o else "hgrn2_context",
    )(*args)


def _outproj_kernel(ya_ref, of_ref, ob_ref, gb_ref, x_ref, gate_ref, hnw_ref, fnw_ref, w_ref,
                    o_ref, y_ref, *, n_heads, hd):
    _, rb, wb, d_a = ya_ref.shape
    rows = rb * wb
    d = x_ref.shape[3]
    y_ref[:, 0:d_a] = ya_ref[0].reshape(rows, d_a)
    hnw = hnw_ref[...]
    for h in range(n_heads):
        sl = slice(h * hd, (h + 1) * hd)
        o = of_ref[0, :, :, sl].astype(F32) + ob_ref[0, :, :, sl].astype(F32)
        ms = jnp.mean(o * o, axis=-1, keepdims=True)
        on = jnp.transpose(o * lax.rsqrt(ms + EPS) * hnw, (1, 0, 2))
        yb = on * gb_ref[0, :, :, sl].astype(F32)
        y_ref[:, d_a + h * hd:d_a + (h + 1) * hd] = yb.reshape(rows, hd).astype(BF16)
    acc = jnp.dot(y_ref[...], w_ref[...], preferred_element_type=F32)
    z = x_ref[0].reshape(rows, d) + gate_ref[0] * acc
    ms = jnp.mean(z * z, axis=-1, keepdims=True)
    o_ref[0] = (z * lax.rsqrt(ms + EPS) * fnw_ref[...]).reshape(rb, wb, d)


def _out_projection(ya4, of4, ob4, pa4, gb_block, x4, mod3, hnw, fnw, w_bf16, n_heads, rb, wb):
    bsz, n_r, n_w, d = x4.shape
    d_a = ya4.shape[3]
    d_b = of4.shape[3]
    hd = d_b // n_heads
    kern = functools.partial(_outproj_kernel, n_heads=n_heads, hd=hd)

    def raster(c, col=0):
        return pl.BlockSpec((1, rb, wb, c), lambda b, r, w: (b, r, w, col))

    def colmajor(c):
        return pl.BlockSpec((1, wb, rb, c), lambda b, r, w: (b, w, r, 0))

    return pl.pallas_call(
        kern,
        grid=(bsz, n_r // rb, n_w // wb),
        in_specs=[raster(d_a), colmajor(d_b), colmajor(d_b), raster(d_b, gb_block), raster(d),
                  pl.BlockSpec((1, 1, d), lambda b, r, w: (b, 0, 2)),
                  pl.BlockSpec((1, hd), lambda b, r, w: (0, 0)),
                  pl.BlockSpec((1, d), lambda b, r, w: (0, 0)),
                  pl.BlockSpec((d_a + d_b, d), lambda b, r, w: (0, 0),
                               pipeline_mode=pl.Buffered(1))],
        out_specs=raster(d),
        out_shape=jax.ShapeDtypeStruct((bsz, n_r, n_w, d), F32),
        scratch_shapes=[pltpu.VMEM((rb * wb, d_a + d_b), BF16)],
        compiler_params=pltpu.CompilerParams(
            dimension_semantics=("parallel", "parallel", "parallel"),
            vmem_limit_bytes=VMEM_LIMIT),
        name="out_projection",
    )(ya4, of4, ob4, pa4, x4, mod3, hnw.reshape(1, hd), fnw.reshape(1, d), w_bf16)


def kernel(x, c, ctx, c_ctx, ada_w, ada_b, norm_w, w_in, conv_w, conv_b, lru_wr, lru_br, lru_wi,
           lru_bi, lru_lambda, hgrn_lb_logits, hgrn_norm_w, w_out, final_norm_w):
    bsz, t_len, d = x.shape
    tc_len = ctx.shape[1]
    assert ada_w.shape[0] == 1, "single-layer stack only"
    d_a = conv_w.shape[2]
    d_b = hgrn_lb_logits.shape[2]
    hd = hgrn_norm_w.shape[1]
    n_heads = d_b // hd
    n_blocks_a = lru_wr.shape[2]
    n_cols = w_in.shape[2]
    assert t_len == GRID_W * CHUNK and tc_len % CHUNK == 0
    assert d_a // n_blocks_a == LANES and hd == LANES
    assert (2 * d_a) % d_b == 0 and n_cols == 2 * d_a + 5 * d_b
    first_b_group = (2 * d_a) // d_b
    n_rows_grid = t_len // GRID_W

    n_rows = -(-(bsz + 1) // SUBLANES) * SUBLANES
    cc = jnp.zeros((n_rows, d), F32).at[:bsz].set(c).at[bsz].set(c_ctx)
    mod3 = _modulation(cc, ada_w[0], ada_b[0]).reshape(n_rows, 1, 3 * d)

    w_in_b = w_in[0]

    tm = 1024 if t_len % 1024 == 0 else t_len
    tn = 1024 if d_a % 1024 == 0 and d_b % 1024 == 0 else min(d_a, d_b)
    a_tiles = (2 * d_a) // tn
    b_tiles = (4 * d_b) // tn
    q_tiles = d_b // tn
    h_lat, h_col = _normalise(x.reshape(bsz, n_rows_grid, GRID_W, d), mod3, norm_w[0], 16)
    tm_h = 2 * tm if (bsz * t_len) % (2 * tm) == 0 else tm
    p_b = _projection_from_h(h_col.reshape(bsz * t_len, d), w_in_b, lambda j: j + a_tiles,
                             4 * d_b, tm_h, tn, (0, q_tiles))
    p_b = p_b.reshape(bsz, GRID_W, n_rows_grid, 4 * d_b)
    p_a = _projection_from_h(h_lat.reshape(bsz * t_len, d), w_in_b,
                             lambda j: jnp.where(j < a_tiles, j, j + b_tiles),
                             2 * d_a + d_b, tm_h, tn, (d_a // tn, (2 * d_a + d_b) // tn))
    xa_tiles = d_a // tn
    ctx_cols = d_a + 4 * d_b
    p_ctx = _in_projection(ctx.reshape(bsz * tc_len, d), mod3, norm_w[0], w_in_b,
                           lambda i: bsz, lambda j: jnp.where(j < xa_tiles, j, j + xa_tiles),
                           ctx_cols, bsz * tc_len, tn, (xa_tiles, xa_tiles + q_tiles))
    p_lat = p_a.reshape(bsz, t_len, 2 * d_a + d_b)
    p_ctx = p_ctx.reshape(bsz, tc_len, ctx_cols)

    def gate_w(dirn):
        return jnp.concatenate([lru_wr[0, dirn], lru_wi[0, dirn]], axis=-1).astype(BF16)

    def gate_b(dirn):
        return jnp.concatenate([lru_br[0, dirn].reshape(n_blocks_a, 1, LANES),
                                lru_bi[0, dirn].reshape(n_blocks_a, 1, LANES)], axis=-1)

    cb = 256 if d_a % 256 == 0 else LANES
    ya, w_out_b = _rglru(p_lat, p_ctx, conv_w[0], conv_b[0], gate_w(0), gate_b(0), gate_w(1),
                         gate_b(1), lru_lambda[0], w_out[0], d_a, cb, tt=256)

    assert d_a % d_b == 0
    states = _hgrn(p_ctx.reshape(bsz, tc_len // CHUNK, CHUNK, ctx_cols), hgrn_lb_logits, d_b,
                   n_heads, d_a // d_b, None, False)
    o_f, o_b = _hgrn(p_b, hgrn_lb_logits, d_b, n_heads, 0, states, True)

    grid4 = lambda z: z.reshape(bsz, n_rows_grid, GRID_W, z.shape[-1])
    out = _out_projection(grid4(ya), o_f, o_b, grid4(p_lat), first_b_group, grid4(x), mod3,
                          hgrn_norm_w[0], final_norm_w, w_out_b, n_heads, 16, 16)
    return out.reshape(bsz, t_len, d)
```
